```python
import jax
import jax.numpy as jnp
from jax import lax
import numpy as np

D_MODEL = 1024
BATCH = 1
SEQ = 16384
DEPTH = 1

GRID_W = 64
CTX_LEN = 256
D_MIX = D_MODEL
D_RWKV = D_MIX // 2
RWKV_HEAD = 64
RWKV_HEADS = D_RWKV // RWKV_HEAD
D_POOL = D_MIX - D_RWKV
POOL_WINDOWS = (2, 4, 8, 16)
POOL_GROUPS = len(POOL_WINDOWS)
POOL_GW = D_POOL // POOL_GROUPS
DECAY_LORA = 64
AAA_LORA = 64
GATE_LORA = 128
RWKV_SIZES = (D_RWKV, D_RWKV, D_RWKV, DECAY_LORA, DECAY_LORA, AAA_LORA, AAA_LORA, GATE_LORA)
D_SHIFT = sum(RWKV_SIZES)
D_IN = D_SHIFT + D_POOL
N_GROUPS = 4
EXPERTS_PER_GROUP = 8
N_EXPERTS = N_GROUPS * EXPERTS_PER_GROUP
TOP_K = 2
D_EXPERT = 512
MOE_BLOCK = 128
NORM_EPS = 1e-6
GN_EPS = 64e-5
POS_THETA = 10000.0
DECAY_SCALE = 0.6065306597

kernel_name = 'hymba_rwkv7_pool_hmoe_dit_block'


def _rmsnorm(h, g):
    hf = h.astype(jnp.float32)
    hf = hf * lax.rsqrt(jnp.mean(hf * hf, axis=-1, keepdims=True) + NORM_EPS)
    return (hf * g.astype(jnp.float32)).astype(h.dtype)


def _modulate(h, shift, scale):
    return h * (1 + scale[:, None, :]) + shift[:, None, :]


def _grid_pos_embed(rows, d):
    quarter = d // 4
    freq = POS_THETA ** (-jnp.arange(quarter, dtype=jnp.float32) / quarter)
    row = jnp.repeat(jnp.arange(rows, dtype=jnp.float32), GRID_W)[:, None] * freq
    col = jnp.tile(jnp.arange(GRID_W, dtype=jnp.float32), rows)[:, None] * freq
    return jnp.concatenate([jnp.sin(row), jnp.cos(row), jnp.sin(col), jnp.cos(col)], axis=-1)


def _centred_shift(h, mu):
    zero = jnp.zeros_like(h[:, :1])
    prev = jnp.concatenate([zero, h[:, :-1]], axis=1)
    nxt = jnp.concatenate([h[:, 1:], zero], axis=1)
    return h + mu * (0.5 * (prev + nxt) - h)


def _rwkv_features(u, mu, w0_f, w2_f, w0_b, w2_b, a0_f, a2_f, a0_b, a2_b, g2, k_k, k_a):
    B, T, _ = u.shape
    s = _centred_shift(u[..., :D_SHIFT], mu).astype(jnp.float32)
    cuts = np.cumsum(RWKV_SIZES)[:-1].tolist()
    r, k, v, wl_f, wl_b, al_f, al_b, gl = jnp.split(s, cuts, axis=-1)

    def heads(t):
        return t.reshape(B, T, RWKV_HEADS, RWKV_HEAD)

    def decay(wl, w0, w2):
        return jnp.exp(-DECAY_SCALE * jax.nn.sigmoid(w0 + jnp.tanh(wl) @ w2))

    w_f = decay(wl_f, w0_f, w2_f)
    w_b = decay(wl_b, w0_b, w2_b)
    a_f = jax.nn.sigmoid(a0_f + al_f @ a2_f)
    a_b = jax.nn.sigmoid(a0_b + al_b @ a2_b)
    g = jax.nn.sigmoid(gl) @ g2
    kk = heads(k * k_k)
    kk = kk / jnp.maximum(jnp.sqrt(jnp.sum(kk * kk, axis=-1, keepdims=True)), 1e-12)
    k_f = k * (1 + (a_f - 1) * k_a)
    k_b = k * (1 + (a_b - 1) * k_a)
    return {'r': heads(r), 'v': heads(v), 'kk': kk, 'g': g,
            'k_f': heads(k_f), 'w_f': heads(w_f), 'a_f': heads(a_f),
            'k_b': heads(k_b), 'w_b': heads(w_b), 'a_b': heads(a_b)}


def _rwkv7_scan(s0, r, w, k, v, kk, a, reverse):
    xs = tuple(jnp.moveaxis(t.astype(jnp.float32), 1, 0) for t in (r, w, k, v, kk, a))

    def step(S, inp):
        r_t, w_t, k_t, v_t, kk_t, a_t = inp
        sa = jnp.einsum('bhij,bhj->bhi', S, -kk_t)
        S = (S * w_t[:, :, None, :] + sa[..., None] * (kk_t * a_t)[:, :, None, :]
             + v_t[..., None] * k_t[:, :, None, :])
        return S, jnp.einsum('bhij,bhj->bhi', S, r_t)

    s_final, ys = lax.scan(step, s0, xs, reverse=reverse)
    return s_final, jnp.moveaxis(ys, 0, 1)


def _rwkv_stream(f, s0_f, s0_b, r_k_f, r_k_b, gn_w, gn_b, readout):
    s_f, y_f = _rwkv7_scan(s0_f, f['r'], f['w_f'], f['k_f'], f['v'], f['kk'], f['a_f'], False)
    s_b, y_b = _rwkv7_scan(s0_b, f['r'], f['w_b'], f['k_b'], f['v'], f['kk'], f['a_b'], True)
    if not readout:
        return None, s_f, s_b
    B, T, H, N = y_f.shape
    y = y_f + y_b
    mean = jnp.mean(y, axis=-1, keepdims=True)
    var = jnp.mean(jnp.square(y - mean), axis=-1, keepdims=True)
    y = ((y - mean) * lax.rsqrt(var + GN_EPS)).reshape(B, T, D_RWKV) * gn_w + gn_b
    bonus = (jnp.sum(f['r'] * f['k_f'] * r_k_f.reshape(H, N), axis=-1, keepdims=True)
             + jnp.sum(f['r'] * f['k_b'] * r_k_b.reshape(H, N), axis=-1, keepdims=True)) * f['v']
    return (y + bonus.reshape(B, T, D_RWKV)) * f['g'], s_f, s_b


def _centred_window_mean(h, win):
    T = h.shape[1]
    cs = jnp.concatenate([jnp.zeros_like(h[:, :1]), jnp.cumsum(h, axis=1)], axis=1)
    t = jnp.arange(T)
    lo = jnp.clip(t - win // 2, 0, T)
    hi = jnp.clip(t - win // 2 + win, 0, T)
    return (cs[:, hi] - cs[:, lo]) / (hi - lo).astype(h.dtype)[None, :, None]


def _pool_mixer(p, w_pool, pool_scale):
    B, T, _ = p.shape
    grp = p.reshape(B, T, POOL_GROUPS, POOL_GW).astype(jnp.float32)
    pooled = jnp.stack([_centred_window_mean(grp[:, :, i], win) for i, win in enumerate(POOL_WINDOWS)],
                       axis=2) - grp
    mixed = jnp.einsum('btgc,gcd->btgd', pooled, w_pool)
    return mixed.reshape(B, T, D_POOL) * pool_scale


def _hier_moe(h, w_rg, b_rg, w_re, b_re, w_gate, w_up, w_down):
    B, T, D = h.shape
    xt = h.reshape(-1, D)
    n = xt.shape[0]
    grp_logits = (xt @ w_rg + b_rg).astype(jnp.float32)
    grp = jnp.argmax(grp_logits, axis=-1)
    p_grp = jnp.take_along_axis(jax.nn.softmax(grp_logits, axis=-1), grp[:, None], axis=1)
    exp_logits = (xt @ w_re + b_re).astype(jnp.float32).reshape(n, N_GROUPS, EXPERTS_PER_GROUP)
    sel = jnp.take_along_axis(exp_logits, grp[:, None, None], axis=1)[:, 0]
    top_val, top_idx = lax.top_k(sel, TOP_K)
    gates = p_grp * jax.nn.softmax(top_val, axis=-1)
    expert = (grp[:, None] * EXPERTS_PER_GROUP + top_idx).astype(jnp.int32)

    m = n * TOP_K
    flat_e = expert.reshape(-1)
    flat_tok = jnp.repeat(jnp.arange(n, dtype=jnp.int32), TOP_K)
    flat_g = gates.reshape(-1)
    order = jnp.argsort(flat_e)
    se, stok, sg = flat_e[order], flat_tok[order], flat_g[order]
    counts = jax.ops.segment_sum(jnp.ones((m,), jnp.int32), flat_e, num_segments=N_EXPERTS)
    starts = jnp.cumsum(counts) - counts
    pcounts = (counts + MOE_BLOCK - 1) // MOE_BLOCK * MOE_BLOCK
    pends = jnp.cumsum(pcounts)
    pstarts = pends - pcounts
    dest = pstarts[se] + (jnp.arange(m, dtype=jnp.int32) - starts[se])
    n_blocks = (m + N_EXPERTS * (MOE_BLOCK - 1) + MOE_BLOCK - 1) // MOE_BLOCK
    P = n_blocks * MOE_BLOCK
    buf_tok = jnp.full((P,), n, jnp.int32).at[dest].set(stok)
    buf_g = jnp.zeros((P,), jnp.float32).at[dest].set(sg)
    blk_expert = jnp.minimum(
        jnp.searchsorted(pends, jnp.arange(n_blocks, dtype=jnp.int32) * MOE_BLOCK, side='right'),
        N_EXPERTS - 1).astype(jnp.int32)
    x_pad = jnp.concatenate([xt, jnp.zeros((1, D), xt.dtype)], axis=0)

    def run_block(args):
        tok, e = args
        xb = x_pad[tok]
        hid = jax.nn.silu(xb @ w_gate[e]) * (xb @ w_up[e])
        return hid @ w_down[e]

    yb = lax.map(run_block, (buf_tok.reshape(n_blocks, MOE_BLOCK), blk_expert))
    yb = yb.reshape(P, D) * buf_g[:, None].astype(yb.dtype)
    y = jax.ops.segment_sum(yb, buf_tok, num_segments=n + 1)[:n]
    return y.reshape(B, T, D).astype(h.dtype)


def setup_inputs(seed: int = 0) -> dict:
    key = jax.random.key(seed)
    ks = jax.random.split(key, 40)
    L, D = DEPTH, D_MODEL

    def nrm(k, shape, scale):
        return jax.random.normal(k, shape, jnp.float32) * scale

    return {
        'x': nrm(ks[0], (BATCH, SEQ, D), 1.0),
        'c': nrm(ks[1], (BATCH, D), 1.0),
        'ctx': nrm(ks[2], (BATCH, CTX_LEN, D), 1.0),
        'c_ctx': nrm(ks[3], (D,), 1.0),
        'norm1_g': 1.0 + nrm(ks[4], (L, D), 0.02),
        'w_mod': nrm(ks[5], (L, D, 6 * D), 0.5 * D ** -0.5),
        'b_mod': nrm(ks[6], (L, 6 * D), 0.02),
        'w_in': nrm(ks[7], (L, D, D_IN), D ** -0.5),
        'mu_shift': jax.random.uniform(ks[8], (L, D_SHIFT), jnp.float32),
        'w0_f': nrm(ks[9], (L, D_RWKV), 0.5),
        'w2_f': nrm(ks[10], (L, DECAY_LORA, D_RWKV), DECAY_LORA ** -0.5),
        'w0_b': nrm(ks[11], (L, D_RWKV), 0.5),
        'w2_b': nrm(ks[12], (L, DECAY_LORA, D_RWKV), DECAY_LORA ** -0.5),
        'a0_f': nrm(ks[13], (L, D_RWKV), 0.5),
        'a2_f': nrm(ks[14], (L, AAA_LORA, D_RWKV), AAA_LORA ** -0.5),
        'a0_b': nrm(ks[15], (L, D_RWKV), 0.5),
        'a2_b': nrm(ks[16], (L, AAA_LORA, D_RWKV), AAA_LORA ** -0.5),
        'g2': nrm(ks[17], (L, GATE_LORA, D_RWKV), GATE_LORA ** -0.5),
        'k_k': 0.85 + nrm(ks[18], (L, D_RWKV), 0.02),
        'k_a': 1.0 + nrm(ks[19], (L, D_RWKV), 0.02),
        'r_k_f': nrm(ks[20], (L, D_RWKV), 0.1),
        'r_k_b': nrm(ks[21], (L, D_RWKV), 0.1),
        'gn_w': 1.0 + nrm(ks[22], (L, D_RWKV), 0.02),
        'gn_b': nrm(ks[23], (L, D_RWKV), 0.02),
        'w_pool': nrm(ks[24], (L, POOL_GROUPS, POOL_GW, POOL_GW), POOL_GW ** -0.5),
        'pool_scale': 1.0 + nrm(ks[25], (L, D_POOL), 0.02),
        'w_out': nrm(ks[26], (L, D_MIX, D), D_MIX ** -0.5),
        'norm2_g': 1.0 + nrm(ks[27], (L, D), 0.02),
        'w_router_grp': nrm(ks[28], (L, D, N_GROUPS), D ** -0.5),
        'b_router_grp': nrm(ks[29], (L, N_GROUPS), 0.01),
        'w_router_exp': nrm(ks[30], (L, D, N_EXPERTS), D ** -0.5),
        'b_router_exp': nrm(ks[31], (L, N_EXPERTS), 0.01),
        'w_gate': nrm(ks[32], (L, N_EXPERTS, D, D_EXPERT), D ** -0.5),
        'w_up': nrm(ks[33], (L, N_EXPERTS, D, D_EXPERT), D ** -0.5),
        'w_down': nrm(ks[34], (L, N_EXPERTS, D_EXPERT, D), D_EXPERT ** -0.5),
        'final_g': 1.0 + nrm(ks[35], (D,), 0.02),
    }


def reference(x, c, ctx, c_ctx, norm1_g, w_mod, b_mod, w_in, mu_shift, w0_f, w2_f, w0_b, w2_b,
              a0_f, a2_f, a0_b, a2_b, g2, k_k, k_a, r_k_f, r_k_b, gn_w, gn_b, w_pool, pool_scale,
              w_out, norm2_g, w_router_grp, b_router_grp, w_router_exp, b_router_exp,
              w_gate, w_up, w_down, final_g):
    B, S, D = x.shape
    rows = S // GRID_W
    x = x + _grid_pos_embed(rows, D)[None].astype(x.dtype)
    xc = ctx
    for l in range(DEPTH):
        need_ctx = l < DEPTH - 1
        mod_x = jax.nn.silu(c) @ w_mod[l] + b_mod[l]
        mod_c = jax.nn.silu(c_ctx)[None] @ w_mod[l] + b_mod[l]
        sh_a, sc_a, gt_a, sh_f, sc_f, gt_f = jnp.split(mod_x, 6, axis=-1)
        csh_a, csc_a, cgt_a, csh_f, csc_f, cgt_f = jnp.split(mod_c, 6, axis=-1)

        u_x = _modulate(_rmsnorm(x, norm1_g[l]), sh_a, sc_a) @ w_in[l]
        u_c = _modulate(_rmsnorm(xc, norm1_g[l]), csh_a, csc_a) @ w_in[l]
        rw_args = (mu_shift[l], w0_f[l], w2_f[l], w0_b[l], w2_b[l], a0_f[l], a2_f[l],
                   a0_b[l], a2_b[l], g2[l], k_k[l], k_a[l])
        f_c = _rwkv_features(u_c, *rw_args)
        f_x = _rwkv_features(u_x, *rw_args)
        s0 = jnp.zeros((xc.shape[0], RWKV_HEADS, RWKV_HEAD, RWKV_HEAD), jnp.float32)
        rw_c, s_ctx_f, s_ctx_b = _rwkv_stream(f_c, s0, s0, r_k_f[l], r_k_b[l], gn_w[l], gn_b[l], need_ctx)
        rw_x, _, _ = _rwkv_stream(f_x, s_ctx_f, s_ctx_b, r_k_f[l], r_k_b[l], gn_w[l], gn_b[l], True)
        pool_x = _pool_mixer(u_x[..., D_SHIFT:], w_pool[l], pool_scale[l])
        mix_x = jnp.concatenate([rw_x, pool_x], axis=-1) @ w_out[l]
        x = x + (gt_a[:, None, :] * mix_x).astype(x.dtype)
        if need_ctx:
            pool_c = _pool_mixer(u_c[..., D_SHIFT:], w_pool[l], pool_scale[l])
            mix_c = jnp.concatenate([rw_c, pool_c], axis=-1) @ w_out[l]
            xc = xc + (cgt_a[:, None, :] * mix_c).astype(xc.dtype)

        moe_args = (w_router_grp[l], b_router_grp[l], w_router_exp[l], b_router_exp[l],
                    w_gate[l], w_up[l], w_down[l])
        h_x = _modulate(_rmsnorm(x, norm2_g[l]), sh_f, sc_f)
        if need_ctx:
            h_c = _modulate(_rmsnorm(xc, norm2_g[l]), csh_f, csc_f)
            y = _hier_moe(jnp.concatenate([h_c, h_x], axis=1), *moe_args)
            n_ctx = xc.shape[1]
            xc = xc + (cgt_f[:, None, :] * y[:, :n_ctx]).astype(xc.dtype)
            x = x + (gt_f[:, None, :] * y[:, n_ctx:]).astype(x.dtype)
        else:
            x = x + (gt_f[:, None, :] * _hier_moe(h_x, *moe_args)).astype(x.dtype)
    return _rmsnorm(x, final_g)
```

```python
import functools

import jax
import jax.numpy as jnp
from jax import lax
from jax.experimental import pallas as pl
from jax.experimental.pallas import tpu as pltpu

F32 = jnp.float32
BF16 = jnp.bfloat16

GRID_W = 64
HEAD = 64
N_HEADS = 8
D_RWKV = HEAD * N_HEADS
D_POOL = 512
POOL_WINDOWS = (2, 4, 8, 16)
POOL_GW = 128
POOL_HALO = 8
D_SHIFT = 1920
N_GROUPS = 4
EXPERTS_PER_GROUP = 8
N_EXPERTS = 32
D_EXPERT = 512
NORM_EPS = 1e-6
GN_EPS = 64e-5
POS_THETA = 10000.0
DECAY_SCALE = 0.6065306597

CHUNK = 64
QUAD = 4 * HEAD
TM_PROJ = 512
TM_FEAT = 256
MOE_BM = 256
ROUTER_ROWS = 48
V7X_VMEM_LIMIT = 56 * 1024 * 1024


def _cparams(n_axes=1, vmem=V7X_VMEM_LIMIT):
    return pltpu.CompilerParams(dimension_semantics=("arbitrary",) * n_axes,
                                vmem_limit_bytes=vmem)


def _dot(a, b):
    return jnp.dot(a, b, preferred_element_type=F32)


def _dot_nt(a, b):
    return lax.dot_general(a, b, (((1,), (1,)), ((), ())), preferred_element_type=F32)


def _dot_tn(a, b):
    return lax.dot_general(a, b, (((0,), (0,)), ((), ())), preferred_element_type=F32)


def _split2(x):
    hi = x.astype(BF16)
    lo = (x - hi.astype(F32)).astype(BF16)
    return hi, lo


def _split3(x):
    hi = x.astype(BF16)
    r1 = x - hi.astype(F32)
    mid = r1.astype(BF16)
    lo = (r1 - mid.astype(F32)).astype(BF16)
    return hi, mid, lo


def _dot3(a, b):
    ah, al = _split2(a)
    bh, bl = _split2(b)
    return _dot(ah, bh) + (_dot(ah, bl) + _dot(al, bh))


def _head_ones():
    r = lax.broadcasted_iota(jnp.int32, (D_RWKV, D_RWKV), 0) >> 6
    c = lax.broadcasted_iota(jnp.int32, (D_RWKV, D_RWKV), 1) >> 6
    return (r == c).astype(BF16)


def _headsum(x, ones):
    hi, lo = _split2(x)
    return _dot(hi, ones) + _dot(lo, ones)


def _mod_body(cs_ref, w_ref, b_ref, o_ref):
    a = cs_ref[...]
    a = a * jax.nn.sigmoid(a)
    o_ref[...] = _dot3(a, w_ref[...]) + b_ref[...]


def _mod_call(cs, w_mod, b_mod):
    d, n = w_mod.shape
    tn = 512
    return pl.pallas_call(
        _mod_body,
        grid=(n // tn,),
        in_specs=[pl.BlockSpec((8, d), lambda j: (0, 0)),
                  pl.BlockSpec((d, tn), lambda j: (0, j)),
                  pl.BlockSpec((1, tn), lambda j: (0, j))],
        out_specs=pl.BlockSpec((8, tn), lambda j: (0, j)),
        out_shape=jax.ShapeDtypeStruct((8, n), F32),
        compiler_params=_cparams(),
        name="mod",
    )(cs, w_mod, b_mod)


def _pos_tile(rowtab_ref, coltab_ref):
    rt = rowtab_ref[...]
    rowpart = jnp.concatenate(
        [jnp.broadcast_to(rt[r:r + 1, :], (GRID_W, rt.shape[1])) for r in range(TM_PROJ // GRID_W)], axis=0)
    return jnp.concatenate([rowpart, coltab_ref[...]], axis=1)


def _inproj_body(x_ref, ctx_ref, rowtab_ref, coltab_ref, g1_ref, mod_ref, w_ref, us_ref, p_ref, *, n_x_tiles):
    d = x_ref.shape[1]
    is_ctx = pl.program_id(0) == n_x_tiles
    xin = jnp.where(is_ctx, ctx_ref[...], x_ref[...] + _pos_tile(rowtab_ref, coltab_ref))
    ms = jnp.mean(xin * xin, axis=-1, keepdims=True)
    xn = xin * lax.rsqrt(ms + NORM_EPS) * g1_ref[...]
    sh = jnp.where(is_ctx, mod_ref[1:2, 0:d], mod_ref[0:1, 0:d])
    sc = jnp.where(is_ctx, mod_ref[1:2, d:2 * d], mod_ref[0:1, d:2 * d])
    h = xn * (1.0 + sc) + sh
    u = _dot(h.astype(BF16), w_ref[...])
    us_ref[...] = u[:, :D_SHIFT]
    p_ref[...] = u[:, D_SHIFT:]


def _inproj_call(x2, ctx_pad, rowtab, coltab, g1, mod, w_in_bf):
    s, d = x2.shape
    nx = s // TM_PROJ
    d_in = w_in_bf.shape[1]
    rows = (nx + 1) * TM_PROJ
    xmap = lambda i: (jnp.minimum(i, nx - 1), 0)
    const = lambda i: (0, 0)
    return pl.pallas_call(
        functools.partial(_inproj_body, n_x_tiles=nx),
        grid=(nx + 1,),
        in_specs=[pl.BlockSpec((TM_PROJ, d), xmap),
                  pl.BlockSpec((TM_PROJ, d), const),
                  pl.BlockSpec((TM_PROJ // GRID_W, d // 2), xmap),
                  pl.BlockSpec((TM_PROJ, d // 2), const),
                  pl.BlockSpec((1, d), const),
                  pl.BlockSpec(mod.shape, const),
                  pl.BlockSpec((d, d_in), const)],
        out_specs=[pl.BlockSpec((TM_PROJ, D_SHIFT), lambda i: (i, 0)),
                   pl.BlockSpec((TM_PROJ, D_POOL), lambda i: (i, 0))],
        out_shape=[jax.ShapeDtypeStruct((rows, D_SHIFT), F32),
                   jax.ShapeDtypeStruct((rows, D_POOL), F32)],
        compiler_params=_cparams(),
        name="inproj",
    )(x2, ctx_pad, rowtab, coltab, g1, mod, w_in_bf)


def _feat_body(um_ref, up_ref, un_ref, mu_ref, par_ref, w2_ref, a2_ref, g2_ref,
               af_ref, rf_ref, btf_ref, ktf_ref, bhf_ref, khf_ref,
               ab_ref, rb_ref, btb_ref, ktb_ref, bhb_ref, khb_ref,
               v_ref, gamf_ref, gamb_ref, bonus_ref, g_ref, *, n_x_tiles):
    i = pl.program_id(0)
    tm = um_ref.shape[0]
    u = um_ref[...]
    prev_ok = jnp.logical_and(i != 0, i != n_x_tiles)
    next_ok = jnp.logical_and(i != n_x_tiles - 1, i != n_x_tiles)
    prow = jnp.where(prev_ok, up_ref[POOL_HALO - 1:POOL_HALO, :], 0.0)
    nrow = jnp.where(next_ok, un_ref[0:1, :], 0.0)
    rid = lax.broadcasted_iota(jnp.int32, (tm, 1), 0)
    up = jnp.where(rid == 0, prow, pltpu.roll(u, 1, 0))
    dn = jnp.where(rid == tm - 1, nrow, pltpu.roll(u, tm - 1, 0))
    s = u + mu_ref[...] * (0.5 * (up + dn) - u)

    w0f, w0b, a0f, a0b = par_ref[0:1, :], par_ref[1:2, :], par_ref[2:3, :], par_ref[3:4, :]
    k_k, k_a, rkf, rkb = par_ref[4:5, :], par_ref[5:6, :], par_ref[6:7, :], par_ref[7:8, :]

    r = s[:, 0:512]
    k = s[:, 512:1024]
    v = s[:, 1024:1536]
    zw = _dot(jnp.tanh(s[:, 1536:1664]).astype(BF16), w2_ref[...])
    za = _dot(s[:, 1664:1792].astype(BF16), a2_ref[...])
    g = _dot(jax.nn.sigmoid(s[:, 1792:1920]).astype(BF16), g2_ref[...])
    lw_f = -DECAY_SCALE * jax.nn.sigmoid(w0f + zw[:, :512])
    lw_b = -DECAY_SCALE * jax.nn.sigmoid(w0b + zw[:, 512:])
    a_f = jax.nn.sigmoid(a0f + za[:, :512])
    a_b = jax.nn.sigmoid(a0b + za[:, 512:])

    ones = _head_ones()
    kkr = k * k_k
    nrm = jnp.sqrt(_headsum(kkr * kkr, ones))
    kk = kkr / jnp.maximum(nrm, 1e-12)
    k_f = k * (1.0 + (a_f - 1.0) * k_a)
    k_b = k * (1.0 + (a_b - 1.0) * k_a)
    bonus_ref[...] = _headsum(r * (k_f * rkf + k_b * rkb), ones) * v
    g_ref[...] = g
    v_ref[...] = v.astype(BF16)

    rr = lax.broadcasted_iota(jnp.int32, (tm, tm), 0)
    cc = lax.broadcasted_iota(jnp.int32, (tm, tm), 1)
    same = (rr >> 6) == (cc >> 6)
    tri = jnp.concatenate([jnp.logical_and(same, cc < rr).astype(BF16),
                           jnp.logical_and(same, cc > rr).astype(BF16)], axis=0)
    lw = jnp.concatenate([lw_f, lw_b], axis=1)
    l1, l2, l3 = _split3(lw)
    sums = _dot(tri, l1) + (_dot(tri, l2) + _dot(tri, l3))
    before_f, after_f = sums[:tm, :512], sums[tm:, :512]
    before_b, after_b = sums[:tm, 512:], sums[tm:, 512:]

    def emit(excl, rest, lwd, a_d, k_d, a_ref, r_ref, bt_ref, kt_ref, bh_ref, kh_ref, gam_ref):
        cum = excl + lwd
        e_neg = jnp.exp(-cum)
        e_rest = jnp.exp(rest)
        kka = kk * a_d
        a_ref[...] = (-kk * jnp.exp(excl)).astype(BF16)
        r_ref[...] = (r * jnp.exp(cum)).astype(BF16)
        bt_ref[...] = (kka * e_neg).astype(BF16)
        kt_ref[...] = (k_d * e_neg).astype(BF16)
        bh_ref[...] = (kka * e_rest).astype(BF16)
        kh_ref[...] = (k_d * e_rest).astype(BF16)
        tot = cum + rest
        gam_ref[0] = jnp.exp(jnp.concatenate(
            [tot[c * CHUNK:c * CHUNK + 1, :] for c in range(tm // CHUNK)], axis=0))

    emit(before_f, after_f, lw_f, a_f, k_f, af_ref, rf_ref, btf_ref, ktf_ref, bhf_ref, khf_ref, gamf_ref)
    emit(after_b, before_b, lw_b, a_b, k_b, ab_ref, rb_ref, btb_ref, ktb_ref, bhb_ref, khb_ref, gamb_ref)


def _feat_call(u_shift, n_tok, n_x_tiles, mu, par, w2, a2, g2):
    tm = TM_FEAT
    nt = n_tok // tm
    hb = tm // POOL_HALO
    last_blk = u_shift.shape[0] // POOL_HALO - 1
    const = lambda i: (0, 0)
    tok = lambda i: (i, 0)
    seq_bf = jax.ShapeDtypeStruct((n_tok, D_RWKV), BF16)
    seq_f32 = jax.ShapeDtypeStruct((n_tok, D_RWKV), F32)
    gam = jax.ShapeDtypeStruct((nt, tm // CHUNK, D_RWKV), F32)
    seq_spec = pl.BlockSpec((tm, D_RWKV), tok)
    gam_spec = pl.BlockSpec((1, tm // CHUNK, D_RWKV), lambda i: (i, 0, 0))
    return pl.pallas_call(
        functools.partial(_feat_body, n_x_tiles=n_x_tiles),
        grid=(nt,),
        in_specs=[pl.BlockSpec((tm, D_SHIFT), tok),
                  pl.BlockSpec((POOL_HALO, D_SHIFT), lambda i: (jnp.maximum(i * hb - 1, 0), 0)),
                  pl.BlockSpec((POOL_HALO, D_SHIFT), lambda i: (jnp.minimum((i + 1) * hb, last_blk), 0)),
                  pl.BlockSpec((1, D_SHIFT), const),
                  pl.BlockSpec((8, D_RWKV), const),
                  pl.BlockSpec(w2.shape, const),
                  pl.BlockSpec(a2.shape, const),
                  pl.BlockSpec(g2.shape, const)],
        out_specs=[seq_spec] * 13 + [gam_spec, gam_spec, seq_spec, seq_spec],
        out_shape=[seq_bf] * 13 + [gam, gam, seq_f32, seq_f32],
        compiler_params=_cparams(),
        name="feat",
    )(u_shift, u_shift, u_shift, mu, par, w2, a2, g2)


def _blockdiag(x, head_of_lane):
    xb = x.astype(BF16)
    zero = jnp.zeros_like(xb)
    return jnp.concatenate([jnp.where(head_of_lane == h, xb, zero) for h in range(QUAD // HEAD)], axis=0)


def _unblock(x, head_of_lane):
    out = jnp.where(head_of_lane == 0, x[0:HEAD, :], 0.0)
    for h in range(1, QUAD // HEAD):
        out = jnp.where(head_of_lane == h, x[h * HEAD:(h + 1) * HEAD, :], out)
    return out


def _scan_chunk(a, r, bt, kt, bh, kh, v, gam, hst, reverse):
    lane = lax.broadcasted_iota(jnp.int32, (HEAD, QUAD), 1)
    head_of_lane = lane >> 6
    s_idx = lane & (HEAD - 1)
    t_idx = lax.broadcasted_iota(jnp.int32, (HEAD, QUAD), 0)
    strict = (s_idx > t_idx) if reverse else (s_idx < t_idx)
    incl = jnp.logical_or(strict, s_idx == t_idx)
    eye = s_idx == t_idx
    bd = functools.partial(_blockdiag, head_of_lane=head_of_lane)

    ll = _dot_nt(jnp.concatenate([a, r], axis=0), jnp.concatenate([bd(bt), bd(kt)], axis=0))
    lab = jnp.where(strict, ll[:HEAD, :QUAD], 0.0)
    lak = jnp.where(strict, ll[:HEAD, QUAD:], 0.0)
    mrb = jnp.where(incl, ll[HEAD:, :QUAD], 0.0)
    mrk = jnp.where(incl, ll[HEAD:, QUAD:], 0.0)

    def same_block(shift):
        return (s_idx >> shift) == (t_idx >> shift)

    ld = jnp.where(same_block(4), lab, 0.0)
    tm = jnp.where(eye, 1.0, 0.0) + ld
    xk = _dot(ld.astype(BF16), bd(ld))
    for _ in range(2):
        y = _dot(jnp.concatenate([tm, xk], axis=0).astype(BF16), bd(xk))
        tm = tm + y[:HEAD]
        xk = y[HEAD:]
    tm = tm + _dot(tm.astype(BF16), bd(xk))
    for shift in (4, 5):
        e = jnp.where(jnp.logical_and(same_block(shift + 1), jnp.logical_not(same_block(shift))), lab, 0.0)
        y = _dot(tm.astype(BF16), bd(e))
        tm = tm + _dot(y.astype(BF16), bd(tm))

    vv = _dot(jnp.concatenate([lak, mrk], axis=0).astype(BF16), bd(v))
    x1 = _dot(tm.astype(BF16), jnp.concatenate([bd(a), bd(vv[:HEAD])], axis=1))
    x2 = _dot(mrb.astype(BF16), jnp.concatenate([bd(x1[:, :QUAD]), bd(x1[:, QUAD:])], axis=1))
    qeff = r.astype(F32) + x2[:, :QUAD]
    yin = x2[:, QUAD:] + vv[HEAD:]
    rhs = jnp.concatenate(
        [x1.astype(BF16), jnp.concatenate([jnp.zeros((HEAD, QUAD), BF16), v], axis=1)], axis=0)
    pz = _dot_tn(jnp.concatenate([bh, kh], axis=0), rhs)
    pq = _unblock(pz[:, :QUAD], head_of_lane) + jnp.where(eye, gam, 0.0)
    zq = _unblock(pz[:, QUAD:], head_of_lane)
    qp = _dot(jnp.concatenate([qeff, pq], axis=0).astype(BF16), bd(hst))
    return qp[:HEAD] + yin, qp[HEAD:] + zq


def _scan_body(af, rf, btf, ktf, bhf, khf, vf, gamf, ab, rb, btb, ktb, bhb, khb, vb, gamb,
               yf_ref, yb_ref, hf_ref, hb_ref):
    @pl.when(pl.program_id(0) == 0)
    def _():
        hf_ref[...] = jnp.zeros_like(hf_ref)
        hb_ref[...] = jnp.zeros_like(hb_ref)

    for refs, y_ref, h_ref, reverse in (((af, rf, btf, ktf, bhf, khf, vf, gamf), yf_ref, hf_ref, False),
                                        ((ab, rb, btb, ktb, bhb, khb, vb, gamb), yb_ref, hb_ref, True)):
        for q in range(D_RWKV // QUAD):
            sl = slice(q * QUAD, (q + 1) * QUAD)
            ops = [ref[:, sl] for ref in refs[:7]]
            y, hn = _scan_chunk(*ops, refs[7][0, :, sl], h_ref[:, sl], reverse)
            y_ref[:, sl] = y
            h_ref[:, sl] = hn


def _scan_call(feats, gamf, gamb, n_x_chunks, n_c_chunks):
    af, rf, btf, ktf, bhf, khf, ab, rb, btb, ktb, bhb, khb, v = feats
    n = n_x_chunks + n_c_chunks
    fwd = lambda i: (jnp.where(i < n_c_chunks, n_x_chunks + i, i - n_c_chunks), 0)
    bwd = lambda i: (n - 1 - i, 0)
    fwd3 = lambda i: (jnp.where(i < n_c_chunks, n_x_chunks + i, i - n_c_chunks), 0, 0)
    bwd3 = lambda i: (n - 1 - i, 0, 0)
    blk = (CHUNK, D_RWKV)
    in_specs = ([pl.BlockSpec(blk, fwd)] * 7 + [pl.BlockSpec((1, 1, D_RWKV), fwd3)]
                + [pl.BlockSpec(blk, bwd)] * 7 + [pl.BlockSpec((1, 1, D_RWKV), bwd3)])
    y = jax.ShapeDtypeStruct((n * CHUNK, D_RWKV), F32)
    return pl.pallas_call(
        _scan_body,
        grid=(n,),
        in_specs=in_specs,
        out_specs=[pl.BlockSpec(blk, fwd), pl.BlockSpec(blk, bwd)],
        out_shape=[y, y],
        scratch_shapes=[pltpu.VMEM((HEAD, D_RWKV), F32), pltpu.VMEM((HEAD, D_RWKV), F32)],
        compiler_params=_cparams(),
        name="scan",
    )(af, rf, btf, ktf, bhf, khf, v, gamf, ab, rb, btb, ktb, bhb, khb, v, gamb)


def _mix_body(x_ref, rowtab_ref, coltab_ref, yf_ref, yb_ref, bonus_ref, g_ref, pm_ref, pp_ref, pn_ref,
              gn_ref, wpool_ref, pscale_ref, wout_ref, mod_ref, o_ref, slab_ref, *, n_x_tiles, seq_len):
    j = pl.program_id(0)
    tm, d = x_ref.shape
    hl = POOL_HALO
    slab_ref[0:hl, :] = jnp.where(j != 0, pp_ref[...], 0.0)
    slab_ref[hl:hl + tm, :] = pm_ref[...]
    slab_ref[hl + tm:2 * hl + tm, :] = jnp.where(j != n_x_tiles - 1, pn_ref[...], 0.0)
    tglob = j * tm + lax.broadcasted_iota(jnp.int32, (tm, 1), 0)
    pooled = []
    for gi, win in enumerate(POOL_WINDOWS):
        cols = slice(gi * POOL_GW, (gi + 1) * POOL_GW)
        acc = slab_ref[hl - win // 2:hl - win // 2 + tm, cols]
        for off in range(-win // 2 + 1, win // 2):
            acc = acc + slab_ref[hl + off:hl + off + tm, cols]
        lo = jnp.maximum(tglob - win // 2, 0)
        hi = jnp.minimum(tglob - win // 2 + win, seq_len)
        cnt = (hi - lo).astype(F32)
        pg = acc / cnt - slab_ref[hl:hl + tm, cols]
        pooled.append(_dot(pg.astype(BF16), wpool_ref[gi]))
    pool = jnp.concatenate(pooled, axis=1) * pscale_ref[...]

    ones = _head_ones()
    y = yf_ref[...] + yb_ref[...]
    yc = y - _headsum(y, ones) * (1.0 / HEAD)
    var = _headsum(yc * yc, ones) * (1.0 / HEAD)
    rw = (yc * lax.rsqrt(var + GN_EPS) * gn_ref[0:1, :] + gn_ref[1:2, :] + bonus_ref[...]) * g_ref[...]
    mix = _dot(jnp.concatenate([rw, pool], axis=1).astype(BF16), wout_ref[...])
    gt_a = mod_ref[0:1, 2 * d:3 * d]
    o_ref[...] = x_ref[...] + _pos_tile(rowtab_ref, coltab_ref) + gt_a * mix


def _mix_call(x2, rowtab, coltab, yf, yb, bonus, g, p, gn, w_pool_bf, pscale, w_out_bf, mod):
    s, d = x2.shape
    tm = TM_PROJ
    nx = s // tm
    hb = tm // POOL_HALO
    const = lambda j: (0, 0)
    tok = lambda j: (j, 0)
    seq_spec = pl.BlockSpec((tm, D_RWKV), tok)
    return pl.pallas_call(
        functools.partial(_mix_body, n_x_tiles=nx, seq_len=s),
        grid=(nx,),
        in_specs=[pl.BlockSpec((tm, d), tok),
                  pl.BlockSpec((tm // GRID_W, d // 2), tok),
                  pl.BlockSpec((tm, d // 2), const),
                  seq_spec, seq_spec, seq_spec, seq_spec, seq_spec,
                  pl.BlockSpec((POOL_HALO, D_POOL), lambda j: (jnp.maximum(j * hb - 1, 0), 0)),
                  pl.BlockSpec((POOL_HALO, D_POOL), lambda j: ((j + 1) * hb, 0)),
                  pl.BlockSpec((2, D_RWKV), const),
                  pl.BlockSpec(w_pool_bf.shape, lambda j: (0, 0, 0)),
                  pl.BlockSpec((1, D_POOL), const),
                  pl.BlockSpec(w_out_bf.shape, const),
                  pl.BlockSpec(mod.shape, const)],
        out_specs=pl.BlockSpec((tm, d), tok),
        out_shape=jax.ShapeDtypeStruct((s, d), F32),
        scratch_shapes=[pltpu.VMEM((tm + 2 * POOL_HALO, D_POOL), F32)],
        compiler_params=_cparams(),
        name="mix",
    )(x2, rowtab, coltab, yf, yb, bonus, g, p, p, p, gn, w_pool_bf, pscale, w_out_bf, mod)


def _pack_bf16_pairs(h):
    n = h.shape[1] // 2
    hi = pltpu.bitcast(h[:, :n].astype(BF16).astype(F32), jnp.uint32)
    lo = pltpu.bitcast(h[:, n:].astype(BF16).astype(F32), jnp.uint32)
    return hi | (lo >> 16)


def _unpack_bf16_pairs(pk):
    hi = pltpu.bitcast(pk & jnp.uint32(0xFFFF0000), F32).astype(BF16)
    lo = pltpu.bitcast(pk << 16, F32).astype(BF16)
    return jnp.concatenate([hi, lo], axis=1)


def _router_body(x_ref, g2_ref, mod_ref, wr_ref, br_ref, hpk_ref, eid_ref, gate_ref):
    tm, d = x_ref.shape
    x = x_ref[...]
    ms = jnp.mean(x * x, axis=-1, keepdims=True)
    xn = x * lax.rsqrt(ms + NORM_EPS) * g2_ref[...]
    h = xn * (1.0 + mod_ref[0:1, 4 * d:5 * d]) + mod_ref[0:1, 3 * d:4 * d]
    hpk_ref[...] = _pack_bf16_pairs(h)

    wh, wl = _split2(wr_ref[...])
    hh, hl = _split2(h)
    logits = _dot_nt(wh, hh) + (_dot_nt(wh, hl) + _dot_nt(wl, hh)) + br_ref[...]
    gl = logits[0:N_GROUPS, :]
    gmax = jnp.max(gl, axis=0, keepdims=True)
    gidx = lax.broadcasted_iota(jnp.int32, gl.shape, 0).astype(F32)
    grp = jnp.min(jnp.where(gl == gmax, gidx, float(N_GROUPS)), axis=0, keepdims=True)
    p_grp = 1.0 / jnp.sum(jnp.exp(gl - gmax), axis=0, keepdims=True)
    sel = jnp.zeros((EXPERTS_PER_GROUP, tm), F32)
    for gi in range(N_GROUPS):
        lo = N_GROUPS + gi * EXPERTS_PER_GROUP
        sel = jnp.where(grp == float(gi), logits[lo:lo + EXPERTS_PER_GROUP, :], sel)
    eidx = lax.broadcasted_iota(jnp.int32, sel.shape, 0).astype(F32)
    top1 = jnp.max(sel, axis=0, keepdims=True)
    i1 = jnp.min(jnp.where(sel == top1, eidx, float(EXPERTS_PER_GROUP)), axis=0, keepdims=True)
    sel2 = jnp.where(eidx == i1, -jnp.inf, sel)
    top2 = jnp.max(sel2, axis=0, keepdims=True)
    i2 = jnp.min(jnp.where(sel2 == top2, eidx, float(EXPERTS_PER_GROUP)), axis=0, keepdims=True)
    e2 = jnp.exp(top2 - top1)
    inv = 1.0 / (1.0 + e2)
    zf = jnp.zeros((6, tm), F32)
    gate_ref[...] = jnp.concatenate([p_grp * inv, p_grp * (e2 * inv), zf], axis=0)
    base = grp * float(EXPERTS_PER_GROUP)
    eid_ref[...] = jnp.concatenate([base + i1, base + i2, zf], axis=0).astype(jnp.int32)


def _router_call(x_mid, g2n, mod, wr, br):
    s, d = x_mid.shape
    tm = TM_PROJ
    const = lambda j: (0, 0)
    return pl.pallas_call(
        _router_body,
        grid=(s // tm,),
        in_specs=[pl.BlockSpec((tm, d), lambda j: (j, 0)),
                  pl.BlockSpec((1, d), const),
                  pl.BlockSpec(mod.shape, const),
                  pl.BlockSpec(wr.shape, const),
                  pl.BlockSpec(br.shape, const)],
        out_specs=[pl.BlockSpec((tm, d // 2), lambda j: (j, 0)),
                   pl.BlockSpec((8, tm), lambda j: (0, j)),
                   pl.BlockSpec((8, tm), lambda j: (0, j))],
        out_shape=[jax.ShapeDtypeStruct((s, d // 2), jnp.uint32),
                   jax.ShapeDtypeStruct((8, s), jnp.int32),
                   jax.ShapeDtypeStruct((8, s), F32)],
        compiler_params=_cparams(),
        name="router",
    )(x_mid, g2n, mod, wr, br)


def _row_copy(src, dst, sem):
    return pltpu.make_async_copy(src, dst, sem)


def _dispatch_body(dest_ref, h_ref, xs_in_ref, xs_ref, sem):
    del xs_in_ref
    tm = h_ref.shape[0]

    def issue(r, c):
        for k in range(2):
            d = dest_ref[0, 0, k * tm + r]
            _row_copy(h_ref.at[pl.ds(r, 1)], xs_ref.at[pl.ds(d, 1)], sem).start()
        return c

    lax.fori_loop(0, tm, issue, 0)

    def drain(r, c):
        for _ in range(2):
            _row_copy(h_ref.at[pl.ds(0, 1)], xs_ref.at[pl.ds(0, 1)], sem).wait()
        return c

    lax.fori_loop(0, tm, drain, 0)


def _dispatch_call(dest3, hpk, xs_zero):
    s, w = hpk.shape
    tm = TM_PROJ
    return pl.pallas_call(
        _dispatch_body,
        grid=(s // tm,),
        in_specs=[pl.BlockSpec((1, 1, 2 * tm), lambda j: (j, 0, 0), memory_space=pltpu.SMEM),
                  pl.BlockSpec((tm, w), lambda j: (j, 0)),
                  pl.BlockSpec(memory_space=pl.ANY)],
        out_specs=pl.BlockSpec(memory_space=pl.ANY),
        out_shape=jax.ShapeDtypeStruct(xs_zero.shape, xs_zero.dtype),
        scratch_shapes=[pltpu.SemaphoreType.DMA(())],
        input_output_aliases={2: 0},
        compiler_params=_cparams(),
        name="dispatch",
    )(dest3, hpk, xs_zero)


def _experts_body(blk_e_ref, n_used_ref, xs_ref, wg_ref, wu_ref, wd_ref, y_ref):
    del blk_e_ref

    @pl.when(pl.program_id(0) < n_used_ref[0])
    def _():
        xb = _unpack_bf16_pairs(xs_ref[...])
        gate = _dot(xb, wg_ref[0])
        upp = _dot(xb, wu_ref[0])
        hid = (gate * jax.nn.sigmoid(gate)) * upp
        y_ref[...] = _dot(hid.astype(BF16), wd_ref[0])

    @pl.when(pl.program_id(0) >= n_used_ref[0])
    def _():
        y_ref[...] = jnp.zeros_like(y_ref)


def _experts_call(blk_expert, n_used, xs, wg, wu, wd):
    p, w = xs.shape
    d = 2 * w
    nb = p // MOE_BM
    grid_spec = pltpu.PrefetchScalarGridSpec(
        num_scalar_prefetch=2,
        grid=(nb,),
        in_specs=[pl.BlockSpec((MOE_BM, w), lambda i, be, nu: (i, 0)),
                  pl.BlockSpec((1, d, D_EXPERT), lambda i, be, nu: (be[i], 0, 0)),
                  pl.BlockSpec((1, d, D_EXPERT), lambda i, be, nu: (be[i], 0, 0)),
                  pl.BlockSpec((1, D_EXPERT, d), lambda i, be, nu: (be[i], 0, 0))],
        out_specs=pl.BlockSpec((MOE_BM, d), lambda i, be, nu: (i, 0)),
    )
    return pl.pallas_call(
        _experts_body,
        grid_spec=grid_spec,
        out_shape=jax.ShapeDtypeStruct((p, d), F32),
        compiler_params=_cparams(),
        name="experts",
    )(blk_expert, n_used, xs, wg, wu, wd)


def _combine_body(dest_ref, x_ref, gate_ref, mod_ref, fg_ref, yb_ref, o_ref, buf_ref, sem):
    tm, d = x_ref.shape

    def issue(r, c):
        for k in range(2):
            src = dest_ref[0, 0, k * tm + r]
            _row_copy(yb_ref.at[pl.ds(src, 1)], buf_ref.at[k, pl.ds(r, 1)], sem).start()
        return c

    lax.fori_loop(0, tm, issue, 0)

    def drain(r, c):
        for k in range(2):
            _row_copy(yb_ref.at[pl.ds(0, 1)], buf_ref.at[k, pl.ds(0, 1)], sem).wait()
        return c

    lax.fori_loop(0, tm, drain, 0)
    y = gate_ref[:, 0:1] * buf_ref[0] + gate_ref[:, 1:2] * buf_ref[1]
    x = x_ref[...] + mod_ref[0:1, 5 * d:6 * d] * y
    ms = jnp.mean(x * x, axis=-1, keepdims=True)
    o_ref[...] = x * lax.rsqrt(ms + NORM_EPS) * fg_ref[...]


def _combine_call(dest3, x_mid, gcol, mod, fg, yb):
    s, d = x_mid.shape
    tm = TM_PROJ
    const = lambda j: (0, 0)
    return pl.pallas_call(
        _combine_body,
        grid=(s // tm,),
        in_specs=[pl.BlockSpec((1, 1, 2 * tm), lambda j: (j, 0, 0), memory_space=pltpu.SMEM),
                  pl.BlockSpec((tm, d), lambda j: (j, 0)),
                  pl.BlockSpec((tm, 2), lambda j: (j, 0)),
                  pl.BlockSpec(mod.shape, const),
                  pl.BlockSpec((1, d), const),
                  pl.BlockSpec(memory_space=pl.ANY)],
        out_specs=pl.BlockSpec((tm, d), lambda j: (j, 0)),
        out_shape=jax.ShapeDtypeStruct((s, d), F32),
        scratch_shapes=[pltpu.VMEM((2, tm, d), F32), pltpu.SemaphoreType.DMA(())],
        compiler_params=_cparams(),
        name="combine",
    )(dest3, x_mid, gcol, mod, fg, yb)


def _blockdiag2(wf, wb):
    z = jnp.zeros_like(wf)
    return jnp.concatenate([jnp.concatenate([wf, z], axis=1), jnp.concatenate([z, wb], axis=1)], axis=0)


def _routing_plan(eids, n_tok):
    m = 2 * n_tok
    flat_e = eids[:2].reshape(m)
    onehot = (flat_e[:, None] == jnp.arange(N_EXPERTS, dtype=jnp.int32)[None, :]).astype(jnp.int32)
    csum = jnp.cumsum(onehot, axis=0)
    counts = csum[-1]
    pcounts = (counts + MOE_BM - 1) // MOE_BM * MOE_BM
    pends = jnp.cumsum(pcounts)
    pstarts = pends - pcounts
    dest = jnp.sum(onehot * (csum - 1 + pstarts[None, :]), axis=1).astype(jnp.int32)
    n_blocks = (m + N_EXPERTS * (MOE_BM - 1) + MOE_BM - 1) // MOE_BM
    blk_start = jnp.arange(n_blocks, dtype=jnp.int32) * MOE_BM
    blk_expert = jnp.minimum(jnp.sum((blk_start[:, None] >= pends[None, :]).astype(jnp.int32), axis=1),
                             N_EXPERTS - 1).astype(jnp.int32)
    n_used = (pends[-1:] // MOE_BM).astype(jnp.int32)
    tm = TM_PROJ
    dest3 = dest.reshape(2, n_tok // tm, tm).transpose(1, 0, 2).reshape(n_tok // tm, 1, 2 * tm)
    return dest3, blk_expert, n_used, n_blocks


def kernel(x, c, ctx, c_ctx, norm1_g, w_mod, b_mod, w_in, mu_shift, w0_f, w2_f, w0_b, w2_b, a0_f, a2_f, a0_b, a2_b, g2, k_k, k_a, r_k_f, r_k_b, gn_w, gn_b, w_pool, pool_scale, w_out, norm2_g, w_router_grp, b_router_grp, w_router_exp, b_router_exp, w_gate, w_up, w_down, final_g):
    b, s, d = x.shape
    n_ctx = ctx.shape[1]
    assert b == 1 and c.shape[0] == 1 and w_mod.shape[0] == 1
    assert s % TM_PROJ == 0 and n_ctx % CHUNK == 0 and n_ctx == TM_FEAT and d == 2 * D_RWKV
    x2 = x[0]
    rows = s // GRID_W

    quarter = d // 4
    freq = POS_THETA ** (-jnp.arange(quarter, dtype=F32) / quarter)
    rarg = jnp.arange(rows, dtype=F32)[:, None] * freq
    carg = jnp.arange(GRID_W, dtype=F32)[:, None] * freq
    rowtab = jnp.concatenate([jnp.sin(rarg), jnp.cos(rarg)], axis=-1)
    coltab = jnp.tile(jnp.concatenate([jnp.sin(carg), jnp.cos(carg)], axis=-1), (TM_PROJ // GRID_W, 1))

    cs = jnp.concatenate([c, c_ctx[None, :], jnp.zeros((6, d), F32)], axis=0)
    mod = _mod_call(cs, w_mod[0], b_mod)

    ctx_pad = jnp.concatenate([ctx[0], jnp.zeros((TM_PROJ - n_ctx, d), F32)], axis=0)
    u_shift, p = _inproj_call(x2, ctx_pad, rowtab, coltab, norm1_g, mod, w_in[0].astype(BF16))

    par = jnp.concatenate([w0_f, w0_b, a0_f, a0_b, k_k, k_a, r_k_f, r_k_b], axis=0)
    n_tok = s + n_ctx
    outs = _feat_call(u_shift, n_tok, s // TM_FEAT, mu_shift,
                      par, _blockdiag2(w2_f[0], w2_b[0]).astype(BF16),
                      _blockdiag2(a2_f[0], a2_b[0]).astype(BF16), g2[0].astype(BF16))
    feats, gamf, gamb, bonus, g = outs[:13], outs[13], outs[14], outs[15], outs[16]
    n_chunks = n_tok // CHUNK
    yf, yb = _scan_call(feats, gamf.reshape(n_chunks, 1, D_RWKV), gamb.reshape(n_chunks, 1, D_RWKV),
                        s // CHUNK, n_ctx // CHUNK)

    gn = jnp.concatenate([gn_w, gn_b], axis=0)
    x_mid = _mix_call(x2, rowtab, coltab, yf, yb, bonus, g, p, gn, w_pool[0].astype(BF16), pool_scale,
                      w_out[0].astype(BF16), mod)

    wr = jnp.concatenate([w_router_grp[0].T, w_router_exp[0].T,
                          jnp.zeros((ROUTER_ROWS - N_GROUPS - N_EXPERTS, d), F32)], axis=0)
    br = jnp.concatenate([b_router_grp[0], b_router_exp[0],
                          jnp.zeros((ROUTER_ROWS - N_GROUPS - N_EXPERTS,), F32)])[:, None]
    hpk, eids, gates = _router_call(x_mid, norm2_g, mod, wr, br)

    dest3, blk_expert, n_used, n_blocks = _routing_plan(eids, s)
    xs = _dispatch_call(dest3, hpk, jnp.zeros((n_blocks * MOE_BM, d // 2), jnp.uint32))
    yexp = _experts_call(blk_expert, n_used, xs, w_gate[0].astype(BF16), w_up[0].astype(BF16),
                         w_down[0].astype(BF16))
    out = _combine_call(dest3, x_mid, gates[:2].T, mod, final_g[None, :], yexp)
    return out[None]
```

```python
import functools

import jax
import jax.numpy as jnp
from jax import lax
from jax.experimental import pallas as pl
from jax.experimental.pallas import tpu as pltpu

F32 = jnp.float32
BF16 = jnp.bfloat16

GRID_W = 64
HEAD = 64
N_HEADS = 8
D_RWKV = HEAD * N_HEADS
D_POOL = 512
POOL_WINDOWS = (2, 4, 8, 16)
POOL_GW = 128
POOL_HALO = 8
D_SHIFT = 1920
N_GROUPS = 4
EXPERTS_PER_GROUP = 8
N_EXPERTS = 32
D_EXPERT = 512
NORM_EPS = 1e-6
GN_EPS = 64e-5
POS_THETA = 10000.0
DECAY_SCALE = 0.6065306597

CHUNK = 64
QUAD = 4 * HEAD
SCAN_CPS = 4
TM_PROJ = 512
TM_FEAT = 256
MOE_BM = 256
ROUTER_ROWS = 48
V7X_VMEM_LIMIT = 56 * 1024 * 1024


def _cparams(n_axes=1, vmem=V7X_VMEM_LIMIT):
    return pltpu.CompilerParams(dimension_semantics=("arbitrary",) * n_axes,
                                vmem_limit_bytes=vmem)


def _dot(a, b):
    return jnp.dot(a, b, preferred_element_type=F32)


def _dot_nt(a, b):
    return lax.dot_general(a, b, (((1,), (1,)), ((), ())), preferred_element_type=F32)


def _dot_tn(a, b):
    return lax.dot_general(a, b, (((0,), (0,)), ((), ())), preferred_element_type=F32)


def _split2(x):
    hi = x.astype(BF16)
    lo = (x - hi.astype(F32)).astype(BF16)
    return hi, lo


def _split3(x):
    hi = x.astype(BF16)
    r1 = x - hi.astype(F32)
    mid = r1.astype(BF16)
    lo = (r1 - mid.astype(F32)).astype(BF16)
    return hi, mid, lo


def _dot3(a, b):
    ah, al = _split2(a)
    bh, bl = _split2(b)
    return _dot(ah, bh) + (_dot(ah, bl) + _dot(al, bh))


def _head_ones():
    r = lax.broadcasted_iota(jnp.int32, (D_RWKV, D_RWKV), 0) >> 6
    c = lax.broadcasted_iota(jnp.int32, (D_RWKV, D_RWKV), 1) >> 6
    return (r == c).astype(BF16)


def _headsum(x, ones):
    hi, lo = _split2(x)
    return _dot(hi, ones) + _dot(lo, ones)


def _mod_body(cs_ref, w_ref, b_ref, o_ref):
    a = cs_ref[...]
    a = a * jax.nn.sigmoid(a)
    o_ref[...] = _dot3(a, w_ref[...]) + b_ref[...]


def _mod_call(cs, w_mod, b_mod):
    d, n = w_mod.shape
    tn = 512
    return pl.pallas_call(
        _mod_body,
        grid=(n // tn,),
        in_specs=[pl.BlockSpec((8, d), lambda j: (0, 0)),
                  pl.BlockSpec((d, tn), lambda j: (0, j)),
                  pl.BlockSpec((1, tn), lambda j: (0, j))],
        out_specs=pl.BlockSpec((8, tn), lambda j: (0, j)),
        out_shape=jax.ShapeDtypeStruct((8, n), F32),
        compiler_params=_cparams(),
        name="mod",
    )(cs, w_mod, b_mod)


def _pos_tile(rowtab_ref, coltab_ref):
    rt = rowtab_ref[...]
    rowpart = jnp.concatenate(
        [jnp.broadcast_to(rt[r:r + 1, :], (GRID_W, rt.shape[1])) for r in range(TM_PROJ // GRID_W)], axis=0)
    return jnp.concatenate([rowpart, coltab_ref[...]], axis=1)


def _inproj_body(x_ref, ctx_ref, rowtab_ref, coltab_ref, g1_ref, mod_ref, w_ref, us_ref, p_ref, *, n_x_tiles):
    d = x_ref.shape[1]
    is_ctx = pl.program_id(0) == n_x_tiles
    xin = jnp.where(is_ctx, ctx_ref[...], x_ref[...] + _pos_tile(rowtab_ref, coltab_ref))
    ms = jnp.mean(xin * xin, axis=-1, keepdims=True)
    xn = xin * lax.rsqrt(ms + NORM_EPS) * g1_ref[...]
    sh = jnp.where(is_ctx, mod_ref[1:2, 0:d], mod_ref[0:1, 0:d])
    sc = jnp.where(is_ctx, mod_ref[1:2, d:2 * d], mod_ref[0:1, d:2 * d])
    h = xn * (1.0 + sc) + sh
    u = _dot(h.astype(BF16), w_ref[...])
    us_ref[...] = u[:, :D_SHIFT]
    p_ref[...] = u[:, D_SHIFT:]


def _inproj_call(x2, ctx_pad, rowtab, coltab, g1, mod, w_in_bf):
    s, d = x2.shape
    nx = s // TM_PROJ
    d_in = w_in_bf.shape[1]
    rows = (nx + 1) * TM_PROJ
    xmap = lambda i: (jnp.minimum(i, nx - 1), 0)
    const = lambda i: (0, 0)
    return pl.pallas_call(
        functools.partial(_inproj_body, n_x_tiles=nx),
        grid=(nx + 1,),
        in_specs=[pl.BlockSpec((TM_PROJ, d), xmap),
                  pl.BlockSpec((TM_PROJ, d), const),
                  pl.BlockSpec((TM_PROJ // GRID_W, d // 2), xmap),
                  pl.BlockSpec((TM_PROJ, d // 2), const),
                  pl.BlockSpec((1, d), const),
                  pl.BlockSpec(mod.shape, const),
                  pl.BlockSpec((d, d_in), const)],
        out_specs=[pl.BlockSpec((TM_PROJ, D_SHIFT), lambda i: (i, 0)),
                   pl.BlockSpec((TM_PROJ, D_POOL), lambda i: (i, 0))],
        out_shape=[jax.ShapeDtypeStruct((rows, D_SHIFT), F32),
                   jax.ShapeDtypeStruct((rows, D_POOL), F32)],
        compiler_params=_cparams(),
        name="inproj",
    )(x2, ctx_pad, rowtab, coltab, g1, mod, w_in_bf)


def _feat_body(um_ref, up_ref, un_ref, mu_ref, par_ref, w2_ref, a2_ref, g2_ref,
               af_ref, rf_ref, btf_ref, ktf_ref, bhf_ref, khf_ref,
               ab_ref, rb_ref, btb_ref, ktb_ref, bhb_ref, khb_ref,
               v_ref, gamf_ref, gamb_ref, bonus_ref, g_ref, *, n_x_tiles):
    i = pl.program_id(0)
    tm = um_ref.shape[0]
    u = um_ref[...]
    prev_ok = jnp.logical_and(i != 0, i != n_x_tiles)
    next_ok = jnp.logical_and(i != n_x_tiles - 1, i != n_x_tiles)
    prow = jnp.where(prev_ok, up_ref[POOL_HALO - 1:POOL_HALO, :], 0.0)
    nrow = jnp.where(next_ok, un_ref[0:1, :], 0.0)
    rid = lax.broadcasted_iota(jnp.int32, (tm, 1), 0)
    up = jnp.where(rid == 0, prow, pltpu.roll(u, 1, 0))
    dn = jnp.where(rid == tm - 1, nrow, pltpu.roll(u, tm - 1, 0))
    s = u + mu_ref[...] * (0.5 * (up + dn) - u)

    w0f, w0b, a0f, a0b = par_ref[0:1, :], par_ref[1:2, :], par_ref[2:3, :], par_ref[3:4, :]
    k_k, k_a, rkf, rkb = par_ref[4:5, :], par_ref[5:6, :], par_ref[6:7, :], par_ref[7:8, :]

    r = s[:, 0:512]
    k = s[:, 512:1024]
    v = s[:, 1024:1536]
    zw = _dot(jnp.tanh(s[:, 1536:1664]).astype(BF16), w2_ref[...])
    za = _dot(s[:, 1664:1792].astype(BF16), a2_ref[...])
    g = _dot(jax.nn.sigmoid(s[:, 1792:1920]).astype(BF16), g2_ref[...])
    lw_f = -DECAY_SCALE * jax.nn.sigmoid(w0f + zw[:, :512])
    lw_b = -DECAY_SCALE * jax.nn.sigmoid(w0b + zw[:, 512:])
    a_f = jax.nn.sigmoid(a0f + za[:, :512])
    a_b = jax.nn.sigmoid(a0b + za[:, 512:])

    ones = _head_ones()
    kkr = k * k_k
    nrm = jnp.sqrt(_headsum(kkr * kkr, ones))
    kk = kkr / jnp.maximum(nrm, 1e-12)
    k_f = k * (1.0 + (a_f - 1.0) * k_a)
    k_b = k * (1.0 + (a_b - 1.0) * k_a)
    bonus_ref[...] = _headsum(r * (k_f * rkf + k_b * rkb), ones) * v
    g_ref[...] = g
    v_ref[...] = v.astype(BF16)

    rr = lax.broadcasted_iota(jnp.int32, (tm, tm), 0)
    cc = lax.broadcasted_iota(jnp.int32, (tm, tm), 1)
    same = (rr >> 6) == (cc >> 6)
    tri = jnp.concatenate([jnp.logical_and(same, cc < rr).astype(BF16),
                           jnp.logical_and(same, cc > rr).astype(BF16)], axis=0)
    lw = jnp.concatenate([lw_f, lw_b], axis=1)
    l1, l2, l3 = _split3(lw)
    sums = _dot(tri, l1) + (_dot(tri, l2) + _dot(tri, l3))
    before_f, after_f = sums[:tm, :512], sums[tm:, :512]
    before_b, after_b = sums[:tm, 512:], sums[tm:, 512:]

    def emit(excl, rest, lwd, a_d, k_d, a_ref, r_ref, bt_ref, kt_ref, bh_ref, kh_ref, gam_ref):
        cum = excl + lwd
        e_neg = jnp.exp(-cum)
        e_rest = jnp.exp(rest)
        kka = kk * a_d
        a_ref[...] = (-kk * jnp.exp(excl)).astype(BF16)
        r_ref[...] = (r * jnp.exp(cum)).astype(BF16)
        bt_ref[...] = (kka * e_neg).astype(BF16)
        kt_ref[...] = (k_d * e_neg).astype(BF16)
        bh_ref[...] = (kka * e_rest).astype(BF16)
        kh_ref[...] = (k_d * e_rest).astype(BF16)
        tot = cum + rest
        gam_ref[0] = jnp.exp(jnp.concatenate(
            [tot[c * CHUNK:c * CHUNK + 1, :] for c in range(tm // CHUNK)], axis=0))

    emit(before_f, after_f, lw_f, a_f, k_f, af_ref, rf_ref, btf_ref, ktf_ref, bhf_ref, khf_ref, gamf_ref)
    emit(after_b, before_b, lw_b, a_b, k_b, ab_ref, rb_ref, btb_ref, ktb_ref, bhb_ref, khb_ref, gamb_ref)


def _feat_call(u_shift, n_tok, n_x_tiles, mu, par, w2, a2, g2):
    tm = TM_FEAT
    nt = n_tok // tm
    hb = tm // POOL_HALO
    last_blk = u_shift.shape[0] // POOL_HALO - 1
    const = lambda i: (0, 0)
    tok = lambda i: (i, 0)
    seq_bf = jax.ShapeDtypeStruct((n_tok, D_RWKV), BF16)
    seq_f32 = jax.ShapeDtypeStruct((n_tok, D_RWKV), F32)
    gam = jax.ShapeDtypeStruct((nt, tm // CHUNK, D_RWKV), F32)
    seq_spec = pl.BlockSpec((tm, D_RWKV), tok)
    gam_spec = pl.BlockSpec((1, tm // CHUNK, D_RWKV), lambda i: (i, 0, 0))
    return pl.pallas_call(
        functools.partial(_feat_body, n_x_tiles=n_x_tiles),
        grid=(nt,),
        in_specs=[pl.BlockSpec((tm, D_SHIFT), tok),
                  pl.BlockSpec((POOL_HALO, D_SHIFT), lambda i: (jnp.maximum(i * hb - 1, 0), 0)),
                  pl.BlockSpec((POOL_HALO, D_SHIFT), lambda i: (jnp.minimum((i + 1) * hb, last_blk), 0)),
                  pl.BlockSpec((1, D_SHIFT), const),
                  pl.BlockSpec((8, D_RWKV), const),
                  pl.BlockSpec(w2.shape, const),
                  pl.BlockSpec(a2.shape, const),
                  pl.BlockSpec(g2.shape, const)],
        out_specs=[seq_spec] * 13 + [gam_spec, gam_spec, seq_spec, seq_spec],
        out_shape=[seq_bf] * 13 + [gam, gam, seq_f32, seq_f32],
        compiler_params=_cparams(),
        name="feat",
    )(u_shift, u_shift, u_shift, mu, par, w2, a2, g2)


def _blockdiag(x, head_of_lane):
    xb = x.astype(BF16)
    zero = jnp.zeros_like(xb)
    return jnp.concatenate([jnp.where(head_of_lane == h, xb, zero) for h in range(QUAD // HEAD)], axis=0)


def _unblock(x, head_of_lane):
    out = jnp.where(head_of_lane == 0, x[0:HEAD, :], 0.0)
    for h in range(1, QUAD // HEAD):
        out = jnp.where(head_of_lane == h, x[h * HEAD:(h + 1) * HEAD, :], out)
    return out


def _scan_chunks(chains):
    lane = lax.broadcasted_iota(jnp.int32, (HEAD, QUAD), 1)
    head_of_lane = lane >> 6
    s_idx = lane & (HEAD - 1)
    t_idx = lax.broadcasted_iota(jnp.int32, (HEAD, QUAD), 0)
    eye = s_idx == t_idx
    lower = s_idx < t_idx
    upper = s_idx > t_idx
    bd = functools.partial(_blockdiag, head_of_lane=head_of_lane)

    def same_block(shift):
        return (s_idx >> shift) == (t_idx >> shift)

    def each(fn):
        return [fn(c) for c in chains]

    def stage_masks(c):
        ll = _dot_nt(jnp.concatenate([c["a"], c["r"]], axis=0),
                     jnp.concatenate([bd(c["bt"]), bd(c["kt"])], axis=0))
        strict = upper if c["reverse"] else lower
        incl = jnp.logical_or(strict, eye)
        c["lab"] = jnp.where(strict, ll[:HEAD, :QUAD], 0.0)
        c["lak"] = jnp.where(strict, ll[:HEAD, QUAD:], 0.0)
        c["mrb"] = jnp.where(incl, ll[HEAD:, :QUAD], 0.0)
        c["mrk"] = jnp.where(incl, ll[HEAD:, QUAD:], 0.0)

    each(stage_masks)

    def stage_square(c):
        ld = jnp.where(same_block(4), c["lab"], 0.0)
        c["tm"] = jnp.where(eye, 1.0, 0.0) + ld
        c["xk"] = _dot(ld.astype(BF16), bd(ld))

    def stage_double(c):
        y = _dot(jnp.concatenate([c["tm"], c["xk"]], axis=0).astype(BF16), bd(c["xk"]))
        c["tm"] = c["tm"] + y[:HEAD]
        c["xk"] = y[HEAD:]

    def stage_last_power(c):
        c["tm"] = c["tm"] + _dot(c["tm"].astype(BF16), bd(c["xk"]))

    each(stage_square)
    each(stage_double)
    each(stage_double)
    each(stage_last_power)
    each(lambda c: c.update(vv=_dot(jnp.concatenate([c["lak"], c["mrk"]], axis=0).astype(BF16), bd(c["v"]))))
    for shift in (4, 5):
        joined = jnp.logical_and(same_block(shift + 1), jnp.logical_not(same_block(shift)))
        each(lambda c: c.update(te=_dot(c["tm"].astype(BF16), bd(jnp.where(joined, c["lab"], 0.0)))))
        each(lambda c: c.update(tm=c["tm"] + _dot(c["te"].astype(BF16), bd(c["tm"]))))

    each(lambda c: c.update(x1=_dot(c["tm"].astype(BF16),
                                    jnp.concatenate([bd(c["a"]), bd(c["vv"][:HEAD])], axis=1))))
    each(lambda c: c.update(x2=_dot(c["mrb"].astype(BF16),
                                    jnp.concatenate([bd(c["x1"][:, :QUAD]), bd(c["x1"][:, QUAD:])], axis=1))))

    def stage_state_terms(c):
        rhs = jnp.concatenate(
            [c["x1"].astype(BF16), jnp.concatenate([jnp.zeros((HEAD, QUAD), BF16), c["v"]], axis=1)], axis=0)
        pz = _dot_tn(jnp.concatenate([c["bh"], c["kh"]], axis=0), rhs)
        c["pq"] = _unblock(pz[:, :QUAD], head_of_lane) + jnp.where(eye, c["gam"], 0.0)
        c["zq"] = _unblock(pz[:, QUAD:], head_of_lane)
        c["qeff"] = c["r"].astype(F32) + c["x2"][:, :QUAD]
        c["yin"] = c["x2"][:, QUAD:] + c["vv"][HEAD:]

    each(stage_state_terms)

    outs, state = [], {}
    for c in chains:
        hst = state.get(c["key"], c["hst"])
        qp = _dot(jnp.concatenate([c["qeff"], c["pq"]], axis=0).astype(BF16), bd(hst))
        state[c["key"]] = qp[HEAD:] + c["zq"]
        outs.append(qp[:HEAD] + c["yin"])
    return outs, state


def _scan_body(af, rf, btf, ktf, bhf, khf, vf, gamf, ab, rb, btb, ktb, bhb, khb, vb, gamb,
               yf_ref, yb_ref, hf_ref, hb_ref):
    @pl.when(pl.program_id(0) == 0)
    def _():
        hf_ref[...] = jnp.zeros_like(hf_ref)
        hb_ref[...] = jnp.zeros_like(hb_ref)

    names = ("a", "r", "bt", "kt", "bh", "kh", "v")
    chains, sinks, state_refs = [], [], {}
    for ci in range(SCAN_CPS):
        for refs, y_ref, h_ref, reverse in (((af, rf, btf, ktf, bhf, khf, vf, gamf), yf_ref, hf_ref, False),
                                            ((ab, rb, btb, ktb, bhb, khb, vb, gamb), yb_ref, hb_ref, True)):
            chunk = SCAN_CPS - 1 - ci if reverse else ci
            rows = slice(chunk * CHUNK, (chunk + 1) * CHUNK)
            for q in range(D_RWKV // QUAD):
                sl = slice(q * QUAD, (q + 1) * QUAD)
                chain = {n: ref[rows, sl] for n, ref in zip(names, refs[:7])}
                chain.update(gam=refs[7][chunk, :, sl], hst=h_ref[:, sl], reverse=reverse, key=(reverse, q))
                chains.append(chain)
                sinks.append((y_ref, rows, sl))
                state_refs[(reverse, q)] = (h_ref, sl)
    outs, state = _scan_chunks(chains)
    for y, (y_ref, rows, sl) in zip(outs, sinks):
        y_ref[rows, sl] = y
    for key, (h_ref, sl) in state_refs.items():
        h_ref[:, sl] = state[key]


def _scan_call(feats, gamf, gamb, n_x_chunks, n_c_chunks):
    af, rf, btf, ktf, bhf, khf, ab, rb, btb, ktb, bhb, khb, v = feats
    assert n_x_chunks % SCAN_CPS == 0 and n_c_chunks % SCAN_CPS == 0
    nxb, ncb = n_x_chunks // SCAN_CPS, n_c_chunks // SCAN_CPS
    n = nxb + ncb
    fwd = lambda i: (jnp.where(i < ncb, nxb + i, i - ncb), 0)
    bwd = lambda i: (n - 1 - i, 0)
    fwd3 = lambda i: (jnp.where(i < ncb, nxb + i, i - ncb), 0, 0)
    bwd3 = lambda i: (n - 1 - i, 0, 0)
    blk = (SCAN_CPS * CHUNK, D_RWKV)
    in_specs = ([pl.BlockSpec(blk, fwd)] * 7 + [pl.BlockSpec((SCAN_CPS, 1, D_RWKV), fwd3)]
                + [pl.BlockSpec(blk, bwd)] * 7 + [pl.BlockSpec((SCAN_CPS, 1, D_RWKV), bwd3)])
    y = jax.ShapeDtypeStruct((n * SCAN_CPS * CHUNK, D_RWKV), F32)
    return pl.pallas_call(
        _scan_body,
        grid=(n,),
        in_specs=in_specs,
        out_specs=[pl.BlockSpec(blk, fwd), pl.BlockSpec(blk, bwd)],
        out_shape=[y, y],
        scratch_shapes=[pltpu.VMEM((HEAD, D_RWKV), F32), pltpu.VMEM((HEAD, D_RWKV), F32)],
        compiler_params=_cparams(),
        name="scan",
    )(af, rf, btf, ktf, bhf, khf, v, gamf, ab, rb, btb, ktb, bhb, khb, v, gamb)


def _mix_body(x_ref, rowtab_ref, coltab_ref, yf_ref, yb_ref, bonus_ref, g_ref, pm_ref, pp_ref, pn_ref,
              gn_ref, wpool_ref, pscale_ref, wout_ref, mod_ref, o_ref, slab_ref, *, n_x_tiles, seq_len):
    j = pl.program_id(0)
    tm, d = x_ref.shape
    hl = POOL_HALO
    slab_ref[0:hl, :] = jnp.where(j != 0, pp_ref[...], 0.0)
    slab_ref[hl:hl + tm, :] = pm_ref[...]
    slab_ref[hl + tm:2 * hl + tm, :] = jnp.where(j != n_x_tiles - 1, pn_ref[...], 0.0)
    tglob = j * tm + lax.broadcasted_iota(jnp.int32, (tm, 1), 0)
    pooled = []
    for gi, win in enumerate(POOL_WINDOWS):
        cols = slice(gi * POOL_GW, (gi + 1) * POOL_GW)
        acc = slab_ref[hl - win // 2:hl - win // 2 + tm, cols]
        for off in range(-win // 2 + 1, win // 2):
            acc = acc + slab_ref[hl + off:hl + off + tm, cols]
        lo = jnp.maximum(tglob - win // 2, 0)
        hi = jnp.minimum(tglob - win // 2 + win, seq_len)
        cnt = (hi - lo).astype(F32)
        pg = acc / cnt - slab_ref[hl:hl + tm, cols]
        pooled.append(_dot(pg.astype(BF16), wpool_ref[gi]))
    pool = jnp.concatenate(pooled, axis=1) * pscale_ref[...]

    ones = _head_ones()
    y = yf_ref[...] + yb_ref[...]
    yc = y - _headsum(y, ones) * (1.0 / HEAD)
    var = _headsum(yc * yc, ones) * (1.0 / HEAD)
    rw = (yc * lax.rsqrt(var + GN_EPS) * gn_ref[0:1, :] + gn_ref[1:2, :] + bonus_ref[...]) * g_ref[...]
    mix = _dot(jnp.concatenate([rw, pool], axis=1).astype(BF16), wout_ref[...])
    gt_a = mod_ref[0:1, 2 * d:3 * d]
    o_ref[...] = x_ref[...] + _pos_tile(rowtab_ref, coltab_ref) + gt_a * mix


def _mix_call(x2, rowtab, coltab, yf, yb, bonus, g, p, gn, w_pool_bf, pscale, w_out_bf, mod):
    s, d = x2.shape
    tm = TM_PROJ
    nx = s // tm
    hb = tm // POOL_HALO
    const = lambda j: (0, 0)
    tok = lambda j: (j, 0)
    seq_spec = pl.BlockSpec((tm, D_RWKV), tok)
    return pl.pallas_call(
        functools.partial(_mix_body, n_x_tiles=nx, seq_len=s),
        grid=(nx,),
        in_specs=[pl.BlockSpec((tm, d), tok),
                  pl.BlockSpec((tm // GRID_W, d // 2), tok),
                  pl.BlockSpec((tm, d // 2), const),
                  seq_spec, seq_spec, seq_spec, seq_spec, seq_spec,
                  pl.BlockSpec((POOL_HALO, D_POOL), lambda j: (jnp.maximum(j * hb - 1, 0), 0)),
                  pl.BlockSpec((POOL_HALO, D_POOL), lambda j: ((j + 1) * hb, 0)),
                  pl.BlockSpec((2, D_RWKV), const),
                  pl.BlockSpec(w_pool_bf.shape, lambda j: (0, 0, 0)),
                  pl.BlockSpec((1, D_POOL), const),
                  pl.BlockSpec(w_out_bf.shape, const),
                  pl.BlockSpec(mod.shape, const)],
        out_specs=pl.BlockSpec((tm, d), tok),
        out_shape=jax.ShapeDtypeStruct((s, d), F32),
        scratch_shapes=[pltpu.VMEM((tm + 2 * POOL_HALO, D_POOL), F32)],
        compiler_params=_cparams(),
        name="mix",
    )(x2, rowtab, coltab, yf, yb, bonus, g, p, p, p, gn, w_pool_bf, pscale, w_out_bf, mod)


def _pack_bf16_pairs(h):
    n = h.shape[1] // 2
    hi = pltpu.bitcast(h[:, :n].astype(BF16).astype(F32), jnp.uint32)
    lo = pltpu.bitcast(h[:, n:].astype(BF16).astype(F32), jnp.uint32)
    return hi | (lo >> 16)


def _unpack_bf16_pairs(pk):
    hi = pltpu.bitcast(pk & jnp.uint32(0xFFFF0000), F32).astype(BF16)
    lo = pltpu.bitcast(pk << 16, F32).astype(BF16)
    return jnp.concatenate([hi, lo], axis=1)


def _router_body(x_ref, g2_ref, mod_ref, wr_ref, br_ref, hpk_ref, eid_ref, gate_ref):
    tm, d = x_ref.shape
    x = x_ref[...]
    ms = jnp.mean(x * x, axis=-1, keepdims=True)
    xn = x * lax.rsqrt(ms + NORM_EPS) * g2_ref[...]
    h = xn * (1.0 + mod_ref[0:1, 4 * d:5 * d]) + mod_ref[0:1, 3 * d:4 * d]
    hpk_ref[...] = _pack_bf16_pairs(h)

    wh, wl = _split2(wr_ref[...])
    hh, hl = _split2(h)
    logits = _dot_nt(wh, hh) + (_dot_nt(wh, hl) + _dot_nt(wl, hh)) + br_ref[...]
    gl = logits[0:N_GROUPS, :]
    gmax = jnp.max(gl, axis=0, keepdims=True)
    gidx = lax.broadcasted_iota(jnp.int32, gl.shape, 0).astype(F32)
    grp = jnp.min(jnp.where(gl == gmax, gidx, float(N_GROUPS)), axis=0, keepdims=True)
    p_grp = 1.0 / jnp.sum(jnp.exp(gl - gmax), axis=0, keepdims=True)
    sel = jnp.zeros((EXPERTS_PER_GROUP, tm), F32)
    for gi in range(N_GROUPS):
        lo = N_GROUPS + gi * EXPERTS_PER_GROUP
        sel = jnp.where(grp == float(gi), logits[lo:lo + EXPERTS_PER_GROUP, :], sel)
    eidx = lax.broadcasted_iota(jnp.int32, sel.shape, 0).astype(F32)
    top1 = jnp.max(sel, axis=0, keepdims=True)
    i1 = jnp.min(jnp.where(sel == top1, eidx, float(EXPERTS_PER_GROUP)), axis=0, keepdims=True)
    sel2 = jnp.where(eidx == i1, -jnp.inf, sel)
    top2 = jnp.max(sel2, axis=0, keepdims=True)
    i2 = jnp.min(jnp.where(sel2 == top2, eidx, float(EXPERTS_PER_GROUP)), axis=0, keepdims=True)
    e2 = jnp.exp(top2 - top1)
    inv = 1.0 / (1.0 + e2)
    zf = jnp.zeros((6, tm), F32)
    gate_ref[...] = jnp.concatenate([p_grp * inv, p_grp * (e2 * inv), zf], axis=0)
    base = grp * float(EXPERTS_PER_GROUP)
    eid_ref[...] = jnp.concatenate([base + i1, base + i2, zf], axis=0).astype(jnp.int32)


def _router_call(x_mid, g2n, mod, wr, br):
    s, d = x_mid.shape
    tm = TM_PROJ
    const = lambda j: (0, 0)
    return pl.pallas_call(
        _router_body,
        grid=(s // tm,),
        in_specs=[pl.BlockSpec((tm, d), lambda j: (j, 0)),
                  pl.BlockSpec((1, d), const),
                  pl.BlockSpec(mod.shape, const),
                  pl.BlockSpec(wr.shape, const),
                  pl.BlockSpec(br.shape, const)],
        out_specs=[pl.BlockSpec((tm, d // 2), lambda j: (j, 0)),
                   pl.BlockSpec((8, tm), lambda j: (0, j)),
                   pl.BlockSpec((8, tm), lambda j: (0, j))],
        out_shape=[jax.ShapeDtypeStruct((s, d // 2), jnp.uint32),
                   jax.ShapeDtypeStruct((8, s), jnp.int32),
                   jax.ShapeDtypeStruct((8, s), F32)],
        compiler_params=_cparams(),
        name="router",
    )(x_mid, g2n, mod, wr, br)


def _row_copy(src, dst, sem):
    return pltpu.make_async_copy(src, dst, sem)


def _dispatch_body(dest_ref, h_ref, xs_in_ref, xs_ref, sem):
    del xs_in_ref
    tm = h_ref.shape[0]

    def issue(r, c):
        for k in range(2):
            d = dest_ref[0, 0, k * tm + r]
            _row_copy(h_ref.at[pl.ds(r, 1)], xs_ref.at[pl.ds(d, 1)], sem).start()
        return c

    lax.fori_loop(0, tm, issue, 0)

    def drain(r, c):
        for _ in range(2):
            _row_copy(h_ref.at[pl.ds(0, 1)], xs_ref.at[pl.ds(0, 1)], sem).wait()
        return c

    lax.fori_loop(0, tm, drain, 0)


def _dispatch_call(dest3, hpk, xs_zero):
    s, w = hpk.shape
    tm = TM_PROJ
    return pl.pallas_call(
        _dispatch_body,
        grid=(s // tm,),
        in_specs=[pl.BlockSpec((1, 1, 2 * tm), lambda j: (j, 0, 0), memory_space=pltpu.SMEM),
                  pl.BlockSpec((tm, w), lambda j: (j, 0)),
                  pl.BlockSpec(memory_space=pl.ANY)],
        out_specs=pl.BlockSpec(memory_space=pl.ANY),
        out_shape=jax.ShapeDtypeStruct(xs_zero.shape, xs_zero.dtype),
        scratch_shapes=[pltpu.SemaphoreType.DMA(())],
        input_output_aliases={2: 0},
        compiler_params=_cparams(),
        name="dispatch",
    )(dest3, hpk, xs_zero)


def _experts_body(blk_e_ref, n_used_ref, xs_ref, wg_ref, wu_ref, wd_ref, y_ref):
    del blk_e_ref

    @pl.when(pl.program_id(0) < n_used_ref[0])
    def _():
        xb = _unpack_bf16_pairs(xs_ref[...])
        gate = _dot(xb, wg_ref[0])
        upp = _dot(xb, wu_ref[0])
        hid = (gate * jax.nn.sigmoid(gate)) * upp
        y_ref[...] = _dot(hid.astype(BF16), wd_ref[0])

    @pl.when(pl.program_id(0) >= n_used_ref[0])
    def _():
        y_ref[...] = jnp.zeros_like(y_ref)


def _experts_call(blk_expert, n_used, xs, wg, wu, wd):
    p, w = xs.shape
    d = 2 * w
    nb = p // MOE_BM
    grid_spec = pltpu.PrefetchScalarGridSpec(
        num_scalar_prefetch=2,
        grid=(nb,),
        in_specs=[pl.BlockSpec((MOE_BM, w), lambda i, be, nu: (i, 0)),
                  pl.BlockSpec((1, d, D_EXPERT), lambda i, be, nu: (be[i], 0, 0)),
                  pl.BlockSpec((1, d, D_EXPERT), lambda i, be, nu: (be[i], 0, 0)),
                  pl.BlockSpec((1, D_EXPERT, d), lambda i, be, nu: (be[i], 0, 0))],
        out_specs=pl.BlockSpec((MOE_BM, d), lambda i, be, nu: (i, 0)),
    )
    return pl.pallas_call(
        _experts_body,
        grid_spec=grid_spec,
        out_shape=jax.ShapeDtypeStruct((p, d), F32),
        compiler_params=_cparams(),
        name="experts",
    )(blk_expert, n_used, xs, wg, wu, wd)


def _combine_body(dest_ref, x_ref, gate_ref, mod_ref, fg_ref, yb_ref, o_ref, buf_ref, sem):
    tm, d = x_ref.shape

    def issue(r, c):
        for k in range(2):
            src = dest_ref[0, 0, k * tm + r]
            _row_copy(yb_ref.at[pl.ds(src, 1)], buf_ref.at[k, pl.ds(r, 1)], sem).start()
        return c

    lax.fori_loop(0, tm, issue, 0)

    def drain(r, c):
        for k in range(2):
            _row_copy(yb_ref.at[pl.ds(0, 1)], buf_ref.at[k, pl.ds(0, 1)], sem).wait()
        return c

    lax.fori_loop(0, tm, drain, 0)
    y = gate_ref[:, 0:1] * buf_ref[0] + gate_ref[:, 1:2] * buf_ref[1]
    x = x_ref[...] + mod_ref[0:1, 5 * d:6 * d] * y
    ms = jnp.mean(x * x, axis=-1, keepdims=True)
    o_ref[...] = x * lax.rsqrt(ms + NORM_EPS) * fg_ref[...]


def _combine_call(dest3, x_mid, gcol, mod, fg, yb):
    s, d = x_mid.shape
    tm = TM_PROJ
    const = lambda j: (0, 0)
    return pl.pallas_call(
        _combine_body,
        grid=(s // tm,),
        in_specs=[pl.BlockSpec((1, 1, 2 * tm), lambda j: (j, 0, 0), memory_space=pltpu.SMEM),
                  pl.BlockSpec((tm, d), lambda j: (j, 0)),
                  pl.BlockSpec((tm, 2), lambda j: (j, 0)),
                  pl.BlockSpec(mod.shape, const),
                  pl.BlockSpec((1, d), const),
                  pl.BlockSpec(memory_space=pl.ANY)],
        out_specs=pl.BlockSpec((tm, d), lambda j: (j, 0)),
        out_shape=jax.ShapeDtypeStruct((s, d), F32),
        scratch_shapes=[pltpu.VMEM((2, tm, d), F32), pltpu.SemaphoreType.DMA(())],
        compiler_params=_cparams(),
        name="combine",
    )(dest3, x_mid, gcol, mod, fg, yb)


def _blockdiag2(wf, wb):
    z = jnp.zeros_like(wf)
    return jnp.concatenate([jnp.concatenate([wf, z], axis=1), jnp.concatenate([z, wb], axis=1)], axis=0)


def _routing_plan(eids, n_tok):
    m = 2 * n_tok
    flat_e = eids[:2].reshape(m)
    onehot = (flat_e[:, None] == jnp.arange(N_EXPERTS, dtype=jnp.int32)[None, :]).astype(jnp.int32)
    csum = jnp.cumsum(onehot, axis=0)
    counts = csum[-1]
    pcounts = (counts + MOE_BM - 1) // MOE_BM * MOE_BM
    pends = jnp.cumsum(pcounts)
    pstarts = pends - pcounts
    dest = jnp.sum(onehot * (csum - 1 + pstarts[None, :]), axis=1).astype(jnp.int32)
    n_blocks = (m + N_EXPERTS * (MOE_BM - 1) + MOE_BM - 1) // MOE_BM
    blk_start = jnp.arange(n_blocks, dtype=jnp.int32) * MOE_BM
    blk_expert = jnp.minimum(jnp.sum((blk_start[:, None] >= pends[None, :]).astype(jnp.int32), axis=1),
                             N_EXPERTS - 1).astype(jnp.int32)
    n_used = (pends[-1:] // MOE_BM).astype(jnp.int32)
    tm = TM_PROJ
    dest3 = dest.reshape(2, n_tok // tm, tm).transpose(1, 0, 2).reshape(n_tok // tm, 1, 2 * tm)
    return dest3, blk_expert, n_used, n_blocks


def kernel(x, c, ctx, c_ctx, norm1_g, w_mod, b_mod, w_in, mu_shift, w0_f, w2_f, w0_b, w2_b, a0_f, a2_f, a0_b, a2_b, g2, k_k, k_a, r_k_f, r_k_b, gn_w, gn_b, w_pool, pool_scale, w_out, norm2_g, w_router_grp, b_router_grp, w_router_exp, b_router_exp, w_gate, w_up, w_down, final_g):
    b, s, d = x.shape
    n_ctx = ctx.shape[1]
    assert b == 1 and c.shape[0] == 1 and w_mod.shape[0] == 1
    assert s % TM_PROJ == 0 and n_ctx % CHUNK == 0 and n_ctx == TM_FEAT and d == 2 * D_RWKV
    x2 = x[0]
    rows = s // GRID_W

    quarter = d // 4
    freq = POS_THETA ** (-jnp.arange(quarter, dtype=F32) / quarter)
    rarg = jnp.arange(rows, dtype=F32)[:, None] * freq
    carg = jnp.arange(GRID_W, dtype=F32)[:, None] * freq
    rowtab = jnp.concatenate([jnp.sin(rarg), jnp.cos(rarg)], axis=-1)
    coltab = jnp.tile(jnp.concatenate([jnp.sin(carg), jnp.cos(carg)], axis=-1), (TM_PROJ // GRID_W, 1))

    cs = jnp.concatenate([c, c_ctx[None, :], jnp.zeros((6, d), F32)], axis=0)
    mod = _mod_call(cs, w_mod[0], b_mod)

    ctx_pad = jnp.concatenate([ctx[0], jnp.zeros((TM_PROJ - n_ctx, d), F32)], axis=0)
    u_shift, p = _inproj_call(x2, ctx_pad, rowtab, coltab, norm1_g, mod, w_in[0].astype(BF16))

    par = jnp.concatenate([w0_f, w0_b, a0_f, a0_b, k_k, k_a, r_k_f, r_k_b], axis=0)
    n_tok = s + n_ctx
    outs = _feat_call(u_shift, n_tok, s // TM_FEAT, mu_shift,
                      par, _blockdiag2(w2_f[0], w2_b[0]).astype(BF16),
                      _blockdiag2(a2_f[0], a2_b[0]).astype(BF16), g2[0].astype(BF16))
    feats, gamf, gamb, bonus, g = outs[:13], outs[13], outs[14], outs[15], outs[16]
    n_chunks = n_tok // CHUNK
    yf, yb = _scan_call(feats, gamf.reshape(n_chunks, 1, D_RWKV), gamb.reshape(n_chunks, 1, D_RWKV),
                        s // CHUNK, n_ctx // CHUNK)

    gn = jnp.concatenate([gn_w, gn_b], axis=0)
    x_mid = _mix_call(x2, rowtab, coltab, yf, yb, bonus, g, p, gn, w_pool[0].astype(BF16), pool_scale,
                      w_out[0].astype(BF16), mod)

    wr = jnp.concatenate([w_router_grp[0].T, w_router_exp[0].T,
                          jnp.zeros((ROUTER_ROWS - N_GROUPS - N_EXPERTS, d), F32)], axis=0)
    br = jnp.concatenate([b_router_grp[0], b_router_exp[0],
                          jnp.zeros((ROUTER_ROWS - N_GROUPS - N_EXPERTS,), F32)])[:, None]
    hpk, eids, gates = _router_call(x_mid, norm2_g, mod, wr, br)

    dest3, blk_expert, n_used, n_blocks = _routing_plan(eids, s)
    xs = _dispatch_call(dest3, hpk, jnp.zeros((n_blocks * MOE_BM, d // 2), jnp.uint32))
    yexp = _experts_call(blk_expert, n_used, xs, w_gate[0].astype(BF16), w_up[0].astype(BF16),
                         w_down[0].astype(BF16))
    out = _combine_call(dest3, x_mid, gates[:2].T, mod, final_g[None, :], yexp)
    return out[None]
```

```python
import functools

import jax
import jax.numpy as jnp
from jax import lax
from jax.experimental import pallas as pl
from jax.experimental.pallas import tpu as pltpu

F32 = jnp.float32
BF16 = jnp.bfloat16

GRID_W = 64
SUBLANES, LANES = 8, 128
HEAD = 64
N_HEADS = 8
D_RWKV = HEAD * N_HEADS
D_POOL = 512
POOL_WINDOWS = (2, 4, 8, 16)
POOL_GW = 128
POOL_HALO = 8
D_SHIFT = 1920
N_GROUPS = 4
EXPERTS_PER_GROUP = 8
N_EXPERTS = 32
D_EXPERT = 512
NORM_EPS = 1e-6
GN_EPS = 64e-5
POS_THETA = 10000.0
DECAY_SCALE = 0.6065306597

CHUNK = 64
QUAD = 4 * HEAD
SCAN_CPS = 4
TM_PROJ = 512
TM_FEAT = 256
MOE_BM = 256
ROUTER_ROWS = 48
V7X_VMEM_LIMIT = 56 * 1024 * 1024


def _cparams(n_axes=1, vmem=V7X_VMEM_LIMIT):
    return pltpu.CompilerParams(dimension_semantics=("arbitrary",) * n_axes,
                                vmem_limit_bytes=vmem)


def _dot(a, b):
    return jnp.dot(a, b, preferred_element_type=F32)


def _dot_nt(a, b):
    return lax.dot_general(a, b, (((1,), (1,)), ((), ())), preferred_element_type=F32)


def _dot_tn(a, b):
    return lax.dot_general(a, b, (((0,), (0,)), ((), ())), preferred_element_type=F32)


def _split2(x):
    hi = x.astype(BF16)
    lo = (x - hi.astype(F32)).astype(BF16)
    return hi, lo


def _split3(x):
    hi = x.astype(BF16)
    r1 = x - hi.astype(F32)
    mid = r1.astype(BF16)
    lo = (r1 - mid.astype(F32)).astype(BF16)
    return hi, mid, lo


def _dot3(a, b):
    ah, al = _split2(a)
    bh, bl = _split2(b)
    return _dot(ah, bh) + (_dot(ah, bl) + _dot(al, bh))


def _head_ones():
    r = lax.broadcasted_iota(jnp.int32, (D_RWKV, D_RWKV), 0) >> 6
    c = lax.broadcasted_iota(jnp.int32, (D_RWKV, D_RWKV), 1) >> 6
    return (r == c).astype(BF16)


def _headsum(x, ones):
    hi, lo = _split2(x)
    return _dot(hi, ones) + _dot(lo, ones)


def _mod_body(cs_ref, w_ref, b_ref, o_ref):
    a = cs_ref[...]
    a = a * jax.nn.sigmoid(a)
    o_ref[...] = _dot3(a, w_ref[...]) + b_ref[...]


def _mod_call(cs, w_mod, b_mod):
    d, n = w_mod.shape
    tn = 512
    return pl.pallas_call(
        _mod_body,
        grid=(n // tn,),
        in_specs=[pl.BlockSpec((8, d), lambda j: (0, 0)),
                  pl.BlockSpec((d, tn), lambda j: (0, j)),
                  pl.BlockSpec((1, tn), lambda j: (0, j))],
        out_specs=pl.BlockSpec((8, tn), lambda j: (0, j)),
        out_shape=jax.ShapeDtypeStruct((8, n), F32),
        compiler_params=_cparams(),
        name="mod",
    )(cs, w_mod, b_mod)


def _pos_tile(rowtab_ref, coltab_ref):
    rt = rowtab_ref[...]
    rowpart = jnp.concatenate(
        [jnp.broadcast_to(rt[r:r + 1, :], (GRID_W, rt.shape[1])) for r in range(TM_PROJ // GRID_W)], axis=0)
    return jnp.concatenate([rowpart, coltab_ref[...]], axis=1)


def _inproj_body(x_ref, ctx_ref, rowtab_ref, coltab_ref, g1_ref, mod_ref, w_ref, us_ref, p_ref, *, n_x_tiles):
    d = x_ref.shape[1]
    is_ctx = pl.program_id(0) == n_x_tiles
    xin = jnp.where(is_ctx, ctx_ref[...], x_ref[...] + _pos_tile(rowtab_ref, coltab_ref))
    ms = jnp.mean(xin * xin, axis=-1, keepdims=True)
    xn = xin * lax.rsqrt(ms + NORM_EPS) * g1_ref[...]
    sh = jnp.where(is_ctx, mod_ref[1:2, 0:d], mod_ref[0:1, 0:d])
    sc = jnp.where(is_ctx, mod_ref[1:2, d:2 * d], mod_ref[0:1, d:2 * d])
    h = xn * (1.0 + sc) + sh
    u = _dot(h.astype(BF16), w_ref[...])
    us_ref[...] = u[:, :D_SHIFT]
    p_ref[...] = u[:, D_SHIFT:]


def _inproj_call(x2, ctx_pad, rowtab, coltab, g1, mod, w_in_bf):
    s, d = x2.shape
    nx = s // TM_PROJ
    d_in = w_in_bf.shape[1]
    rows = (nx + 1) * TM_PROJ
    xmap = lambda i: (jnp.minimum(i, nx - 1), 0)
    const = lambda i: (0, 0)
    return pl.pallas_call(
        functools.partial(_inproj_body, n_x_tiles=nx),
        grid=(nx + 1,),
        in_specs=[pl.BlockSpec((TM_PROJ, d), xmap),
                  pl.BlockSpec((TM_PROJ, d), const),
                  pl.BlockSpec((TM_PROJ // GRID_W, d // 2), xmap),
                  pl.BlockSpec((TM_PROJ, d // 2), const),
                  pl.BlockSpec((1, d), const),
                  pl.BlockSpec(mod.shape, const),
                  pl.BlockSpec((d, d_in), const)],
        out_specs=[pl.BlockSpec((TM_PROJ, D_SHIFT), lambda i: (i, 0)),
                   pl.BlockSpec((TM_PROJ, D_POOL), lambda i: (i, 0))],
        out_shape=[jax.ShapeDtypeStruct((rows, D_SHIFT), F32),
                   jax.ShapeDtypeStruct((rows, D_POOL), F32)],
        compiler_params=_cparams(),
        name="inproj",
    )(x2, ctx_pad, rowtab, coltab, g1, mod, w_in_bf)


def _feat_body(um_ref, up_ref, un_ref, mu_ref, par_ref, w2_ref, a2_ref, g2_ref,
               af_ref, rf_ref, btf_ref, ktf_ref, bhf_ref, khf_ref,
               ab_ref, rb_ref, btb_ref, ktb_ref, bhb_ref, khb_ref,
               v_ref, gamf_ref, gamb_ref, bonus_ref, g_ref, *, n_x_tiles):
    i = pl.program_id(0)
    tm = um_ref.shape[0]
    u = um_ref[...]
    prev_ok = jnp.logical_and(i != 0, i != n_x_tiles)
    next_ok = jnp.logical_and(i != n_x_tiles - 1, i != n_x_tiles)
    prow = jnp.where(prev_ok, up_ref[POOL_HALO - 1:POOL_HALO, :], 0.0)
    nrow = jnp.where(next_ok, un_ref[0:1, :], 0.0)
    rid = lax.broadcasted_iota(jnp.int32, (tm, 1), 0)
    up = jnp.where(rid == 0, prow, pltpu.roll(u, 1, 0))
    dn = jnp.where(rid == tm - 1, nrow, pltpu.roll(u, tm - 1, 0))
    s = u + mu_ref[...] * (0.5 * (up + dn) - u)

    w0f, w0b, a0f, a0b = par_ref[0:1, :], par_ref[1:2, :], par_ref[2:3, :], par_ref[3:4, :]
    k_k, k_a, rkf, rkb = par_ref[4:5, :], par_ref[5:6, :], par_ref[6:7, :], par_ref[7:8, :]

    r = s[:, 0:512]
    k = s[:, 512:1024]
    v = s[:, 1024:1536]
    zw = _dot(jnp.tanh(s[:, 1536:1664]).astype(BF16), w2_ref[...])
    za = _dot(s[:, 1664:1792].astype(BF16), a2_ref[...])
    g = _dot(jax.nn.sigmoid(s[:, 1792:1920]).astype(BF16), g2_ref[...])
    lw_f = -DECAY_SCALE * jax.nn.sigmoid(w0f + zw[:, :512])
    lw_b = -DECAY_SCALE * jax.nn.sigmoid(w0b + zw[:, 512:])
    a_f = jax.nn.sigmoid(a0f + za[:, :512])
    a_b = jax.nn.sigmoid(a0b + za[:, 512:])

    ones = _head_ones()
    kkr = k * k_k
    nrm = jnp.sqrt(_headsum(kkr * kkr, ones))
    kk = kkr / jnp.maximum(nrm, 1e-12)
    k_f = k * (1.0 + (a_f - 1.0) * k_a)
    k_b = k * (1.0 + (a_b - 1.0) * k_a)
    bonus_ref[...] = _headsum(r * (k_f * rkf + k_b * rkb), ones) * v
    g_ref[...] = g
    v_ref[...] = v.astype(BF16)

    rr = lax.broadcasted_iota(jnp.int32, (tm, tm), 0)
    cc = lax.broadcasted_iota(jnp.int32, (tm, tm), 1)
    same = (rr >> 6) == (cc >> 6)
    tri = jnp.concatenate([jnp.logical_and(same, cc < rr).astype(BF16),
                           jnp.logical_and(same, cc > rr).astype(BF16)], axis=0)
    lw = jnp.concatenate([lw_f, lw_b], axis=1)
    l1, l2, l3 = _split3(lw)
    sums = _dot(tri, l1) + (_dot(tri, l2) + _dot(tri, l3))
    before_f, after_f = sums[:tm, :512], sums[tm:, :512]
    before_b, after_b = sums[:tm, 512:], sums[tm:, 512:]

    def emit(excl, rest, lwd, a_d, k_d, a_ref, r_ref, bt_ref, kt_ref, bh_ref, kh_ref, gam_ref):
        cum = excl + lwd
        e_neg = jnp.exp(-cum)
        e_rest = jnp.exp(rest)
        kka = kk * a_d
        a_ref[...] = (-kk * jnp.exp(excl)).astype(BF16)
        r_ref[...] = (r * jnp.exp(cum)).astype(BF16)
        bt_ref[...] = (kka * e_neg).astype(BF16)
        kt_ref[...] = (k_d * e_neg).astype(BF16)
        bh_ref[...] = (kka * e_rest).astype(BF16)
        kh_ref[...] = (k_d * e_rest).astype(BF16)
        tot = cum + rest
        gam_ref[0] = jnp.exp(jnp.concatenate(
            [tot[c * CHUNK:c * CHUNK + 1, :] for c in range(tm // CHUNK)], axis=0))

    emit(before_f, after_f, lw_f, a_f, k_f, af_ref, rf_ref, btf_ref, ktf_ref, bhf_ref, khf_ref, gamf_ref)
    emit(after_b, before_b, lw_b, a_b, k_b, ab_ref, rb_ref, btb_ref, ktb_ref, bhb_ref, khb_ref, gamb_ref)


def _feat_call(u_shift, n_tok, n_x_tiles, mu, par, w2, a2, g2):
    tm = TM_FEAT
    nt = n_tok // tm
    hb = tm // POOL_HALO
    last_blk = u_shift.shape[0] // POOL_HALO - 1
    const = lambda i: (0, 0)
    tok = lambda i: (i, 0)
    seq_bf = jax.ShapeDtypeStruct((n_tok, D_RWKV), BF16)
    seq_f32 = jax.ShapeDtypeStruct((n_tok, D_RWKV), F32)
    gam = jax.ShapeDtypeStruct((nt, tm // CHUNK, D_RWKV), F32)
    seq_spec = pl.BlockSpec((tm, D_RWKV), tok)
    gam_spec = pl.BlockSpec((1, tm // CHUNK, D_RWKV), lambda i: (i, 0, 0))
    return pl.pallas_call(
        functools.partial(_feat_body, n_x_tiles=n_x_tiles),
        grid=(nt,),
        in_specs=[pl.BlockSpec((tm, D_SHIFT), tok),
                  pl.BlockSpec((POOL_HALO, D_SHIFT), lambda i: (jnp.maximum(i * hb - 1, 0), 0)),
                  pl.BlockSpec((POOL_HALO, D_SHIFT), lambda i: (jnp.minimum((i + 1) * hb, last_blk), 0)),
                  pl.BlockSpec((1, D_SHIFT), const),
                  pl.BlockSpec((8, D_RWKV), const),
                  pl.BlockSpec(w2.shape, const),
                  pl.BlockSpec(a2.shape, const),
                  pl.BlockSpec(g2.shape, const)],
        out_specs=[seq_spec] * 13 + [gam_spec, gam_spec, seq_spec, seq_spec],
        out_shape=[seq_bf] * 13 + [gam, gam, seq_f32, seq_f32],
        compiler_params=_cparams(),
        name="feat",
    )(u_shift, u_shift, u_shift, mu, par, w2, a2, g2)


def _blockdiag(x, head_of_lane):
    xb = x.astype(BF16)
    zero = jnp.zeros_like(xb)
    return jnp.concatenate([jnp.where(head_of_lane == h, xb, zero) for h in range(QUAD // HEAD)], axis=0)


def _unblock(x, head_of_lane):
    out = jnp.where(head_of_lane == 0, x[0:HEAD, :], 0.0)
    for h in range(1, QUAD // HEAD):
        out = jnp.where(head_of_lane == h, x[h * HEAD:(h + 1) * HEAD, :], out)
    return out


def _scan_chunks(chains):
    lane = lax.broadcasted_iota(jnp.int32, (HEAD, QUAD), 1)
    head_of_lane = lane >> 6
    s_idx = lane & (HEAD - 1)
    t_idx = lax.broadcasted_iota(jnp.int32, (HEAD, QUAD), 0)
    eye = s_idx == t_idx
    lower = s_idx < t_idx
    upper = s_idx > t_idx
    bd = functools.partial(_blockdiag, head_of_lane=head_of_lane)

    def same_block(shift):
        return (s_idx >> shift) == (t_idx >> shift)

    def each(fn):
        return [fn(c) for c in chains]

    def stage_masks(c):
        ll = _dot_nt(jnp.concatenate([c["a"], c["r"]], axis=0),
                     jnp.concatenate([bd(c["bt"]), bd(c["kt"])], axis=0))
        strict = upper if c["reverse"] else lower
        incl = jnp.logical_or(strict, eye)
        c["lab"] = jnp.where(strict, ll[:HEAD, :QUAD], 0.0)
        c["lak"] = jnp.where(strict, ll[:HEAD, QUAD:], 0.0)
        c["mrb"] = jnp.where(incl, ll[HEAD:, :QUAD], 0.0)
        c["mrk"] = jnp.where(incl, ll[HEAD:, QUAD:], 0.0)

    each(stage_masks)

    def stage_square(c):
        ld = jnp.where(same_block(4), c["lab"], 0.0)
        c["tm"] = jnp.where(eye, 1.0, 0.0) + ld
        c["xk"] = _dot(ld.astype(BF16), bd(ld))

    def stage_double(c):
        y = _dot(jnp.concatenate([c["tm"], c["xk"]], axis=0).astype(BF16), bd(c["xk"]))
        c["tm"] = c["tm"] + y[:HEAD]
        c["xk"] = y[HEAD:]

    def stage_last_power(c):
        c["tm"] = c["tm"] + _dot(c["tm"].astype(BF16), bd(c["xk"]))

    each(stage_square)
    each(stage_double)
    each(stage_double)
    each(stage_last_power)
    each(lambda c: c.update(vv=_dot(jnp.concatenate([c["lak"], c["mrk"]], axis=0).astype(BF16), bd(c["v"]))))
    for shift in (4, 5):
        joined = jnp.logical_and(same_block(shift + 1), jnp.logical_not(same_block(shift)))
        each(lambda c: c.update(te=_dot(c["tm"].astype(BF16), bd(jnp.where(joined, c["lab"], 0.0)))))
        each(lambda c: c.update(tm=c["tm"] + _dot(c["te"].astype(BF16), bd(c["tm"]))))

    each(lambda c: c.update(x1=_dot(c["tm"].astype(BF16),
                                    jnp.concatenate([bd(c["a"]), bd(c["vv"][:HEAD])], axis=1))))
    each(lambda c: c.update(x2=_dot(c["mrb"].astype(BF16),
                                    jnp.concatenate([bd(c["x1"][:, :QUAD]), bd(c["x1"][:, QUAD:])], axis=1))))

    def stage_state_terms(c):
        rhs = jnp.concatenate(
            [c["x1"].astype(BF16), jnp.concatenate([jnp.zeros((HEAD, QUAD), BF16), c["v"]], axis=1)], axis=0)
        pz = _dot_tn(jnp.concatenate([c["bh"], c["kh"]], axis=0), rhs)
        c["pq"] = _unblock(pz[:, :QUAD], head_of_lane) + jnp.where(eye, c["gam"], 0.0)
        c["zq"] = _unblock(pz[:, QUAD:], head_of_lane)
        c["qeff"] = c["r"].astype(F32) + c["x2"][:, :QUAD]
        c["yin"] = c["x2"][:, QUAD:] + c["vv"][HEAD:]

    each(stage_state_terms)

    outs, state = [], {}
    for c in chains:
        hst = state.get(c["key"], c["hst"])
        qp = _dot(jnp.concatenate([c["qeff"], c["pq"]], axis=0).astype(BF16), bd(hst))
        state[c["key"]] = qp[HEAD:] + c["zq"]
        outs.append(qp[:HEAD] + c["yin"])
    return outs, state


def _scan_body(af, rf, btf, ktf, bhf, khf, vf, gamf, ab, rb, btb, ktb, bhb, khb, vb, gamb,
               yf_ref, yb_ref, hf_ref, hb_ref):
    @pl.when(pl.program_id(0) == 0)
    def _():
        hf_ref[...] = jnp.zeros_like(hf_ref)
        hb_ref[...] = jnp.zeros_like(hb_ref)

    names = ("a", "r", "bt", "kt", "bh", "kh", "v")
    chains, sinks, state_refs = [], [], {}
    for ci in range(SCAN_CPS):
        for refs, y_ref, h_ref, reverse in (((af, rf, btf, ktf, bhf, khf, vf, gamf), yf_ref, hf_ref, False),
                                            ((ab, rb, btb, ktb, bhb, khb, vb, gamb), yb_ref, hb_ref, True)):
            chunk = SCAN_CPS - 1 - ci if reverse else ci
            rows = slice(chunk * CHUNK, (chunk + 1) * CHUNK)
            for q in range(D_RWKV // QUAD):
                sl = slice(q * QUAD, (q + 1) * QUAD)
                chain = {n: ref[rows, sl] for n, ref in zip(names, refs[:7])}
                chain.update(gam=refs[7][chunk, :, sl], hst=h_ref[:, sl], reverse=reverse, key=(reverse, q))
                chains.append(chain)
                sinks.append((y_ref, rows, sl))
                state_refs[(reverse, q)] = (h_ref, sl)
    outs, state = _scan_chunks(chains)
    for y, (y_ref, rows, sl) in zip(outs, sinks):
        y_ref[rows, sl] = y
    for key, (h_ref, sl) in state_refs.items():
        h_ref[:, sl] = state[key]


def _scan_call(feats, gamf, gamb, n_x_chunks, n_c_chunks):
    af, rf, btf, ktf, bhf, khf, ab, rb, btb, ktb, bhb, khb, v = feats
    assert n_x_chunks % SCAN_CPS == 0 and n_c_chunks % SCAN_CPS == 0
    nxb, ncb = n_x_chunks // SCAN_CPS, n_c_chunks // SCAN_CPS
    n = nxb + ncb
    fwd = lambda i: (jnp.where(i < ncb, nxb + i, i - ncb), 0)
    bwd = lambda i: (n - 1 - i, 0)
    fwd3 = lambda i: (jnp.where(i < ncb, nxb + i, i - ncb), 0, 0)
    bwd3 = lambda i: (n - 1 - i, 0, 0)
    blk = (SCAN_CPS * CHUNK, D_RWKV)
    in_specs = ([pl.BlockSpec(blk, fwd)] * 7 + [pl.BlockSpec((SCAN_CPS, 1, D_RWKV), fwd3)]
                + [pl.BlockSpec(blk, bwd)] * 7 + [pl.BlockSpec((SCAN_CPS, 1, D_RWKV), bwd3)])
    y = jax.ShapeDtypeStruct((n * SCAN_CPS * CHUNK, D_RWKV), F32)
    return pl.pallas_call(
        _scan_body,
        grid=(n,),
        in_specs=in_specs,
        out_specs=[pl.BlockSpec(blk, fwd), pl.BlockSpec(blk, bwd)],
        out_shape=[y, y],
        scratch_shapes=[pltpu.VMEM((HEAD, D_RWKV), F32), pltpu.VMEM((HEAD, D_RWKV), F32)],
        compiler_params=_cparams(),
        name="scan",
    )(af, rf, btf, ktf, bhf, khf, v, gamf, ab, rb, btb, ktb, bhb, khb, v, gamb)


def _mix_body(x_ref, rowtab_ref, coltab_ref, yf_ref, yb_ref, bonus_ref, g_ref, pm_ref, pp_ref, pn_ref,
              gn_ref, wpool_ref, pscale_ref, wout_ref, mod_ref, o_ref, slab_ref, *, n_x_tiles, seq_len):
    j = pl.program_id(0)
    tm, d = x_ref.shape
    hl = POOL_HALO
    slab_ref[0:hl, :] = jnp.where(j != 0, pp_ref[...], 0.0)
    slab_ref[hl:hl + tm, :] = pm_ref[...]
    slab_ref[hl + tm:2 * hl + tm, :] = jnp.where(j != n_x_tiles - 1, pn_ref[...], 0.0)
    tglob = j * tm + lax.broadcasted_iota(jnp.int32, (tm, 1), 0)
    pooled = []
    for gi, win in enumerate(POOL_WINDOWS):
        cols = slice(gi * POOL_GW, (gi + 1) * POOL_GW)
        acc = slab_ref[hl - win // 2:hl - win // 2 + tm, cols]
        for off in range(-win // 2 + 1, win // 2):
            acc = acc + slab_ref[hl + off:hl + off + tm, cols]
        lo = jnp.maximum(tglob - win // 2, 0)
        hi = jnp.minimum(tglob - win // 2 + win, seq_len)
        cnt = (hi - lo).astype(F32)
        pg = acc / cnt - slab_ref[hl:hl + tm, cols]
        pooled.append(_dot(pg.astype(BF16), wpool_ref[gi]))
    pool = jnp.concatenate(pooled, axis=1) * pscale_ref[...]

    ones = _head_ones()
    y = yf_ref[...] + yb_ref[...]
    yc = y - _headsum(y, ones) * (1.0 / HEAD)
    var = _headsum(yc * yc, ones) * (1.0 / HEAD)
    rw = (yc * lax.rsqrt(var + GN_EPS) * gn_ref[0:1, :] + gn_ref[1:2, :] + bonus_ref[...]) * g_ref[...]
    mix = _dot(jnp.concatenate([rw, pool], axis=1).astype(BF16), wout_ref[...])
    gt_a = mod_ref[0:1, 2 * d:3 * d]
    o_ref[...] = x_ref[...] + _pos_tile(rowtab_ref, coltab_ref) + gt_a * mix


def _mix_call(x2, rowtab, coltab, yf, yb, bonus, g, p, gn, w_pool_bf, pscale, w_out_bf, mod):
    s, d = x2.shape
    tm = TM_PROJ
    nx = s // tm
    hb = tm // POOL_HALO
    const = lambda j: (0, 0)
    tok = lambda j: (j, 0)
    seq_spec = pl.BlockSpec((tm, D_RWKV), tok)
    return pl.pallas_call(
        functools.partial(_mix_body, n_x_tiles=nx, seq_len=s),
        grid=(nx,),
        in_specs=[pl.BlockSpec((tm, d), tok),
                  pl.BlockSpec((tm // GRID_W, d // 2), tok),
                  pl.BlockSpec((tm, d // 2), const),
                  seq_spec, seq_spec, seq_spec, seq_spec, seq_spec,
                  pl.BlockSpec((POOL_HALO, D_POOL), lambda j: (jnp.maximum(j * hb - 1, 0), 0)),
                  pl.BlockSpec((POOL_HALO, D_POOL), lambda j: ((j + 1) * hb, 0)),
                  pl.BlockSpec((2, D_RWKV), const),
                  pl.BlockSpec(w_pool_bf.shape, lambda j: (0, 0, 0)),
                  pl.BlockSpec((1, D_POOL), const),
                  pl.BlockSpec(w_out_bf.shape, const),
                  pl.BlockSpec(mod.shape, const)],
        out_specs=pl.BlockSpec((tm, d), tok),
        out_shape=jax.ShapeDtypeStruct((s, d), F32),
        scratch_shapes=[pltpu.VMEM((tm + 2 * POOL_HALO, D_POOL), F32)],
        compiler_params=_cparams(),
        name="mix",
    )(x2, rowtab, coltab, yf, yb, bonus, g, p, p, p, gn, w_pool_bf, pscale, w_out_bf, mod)


def _to_token_tiles(ref, x):
    tm = x.shape[0]
    for j in range(x.shape[1] // LANES):
        ref[pl.ds(j, tm, stride=SUBLANES), :] = x[:, j * LANES:(j + 1) * LANES]


def _from_token_tiles(ref, tm):
    return jnp.concatenate([ref[pl.ds(j, tm, stride=SUBLANES), :] for j in range(SUBLANES)], axis=1)


def _router_body(x_ref, g2_ref, mod_ref, wr_ref, br_ref, h3_ref, eid_ref, gate_ref):
    tm, d = x_ref.shape
    x = x_ref[...]
    ms = jnp.mean(x * x, axis=-1, keepdims=True)
    xn = x * lax.rsqrt(ms + NORM_EPS) * g2_ref[...]
    h = xn * (1.0 + mod_ref[0:1, 4 * d:5 * d]) + mod_ref[0:1, 3 * d:4 * d]
    _to_token_tiles(h3_ref, h)

    wh, wl = _split2(wr_ref[...])
    hh, hl = _split2(h)
    logits = _dot_nt(wh, hh) + (_dot_nt(wh, hl) + _dot_nt(wl, hh)) + br_ref[...]
    gl = logits[0:N_GROUPS, :]
    gmax = jnp.max(gl, axis=0, keepdims=True)
    gidx = lax.broadcasted_iota(jnp.int32, gl.shape, 0).astype(F32)
    grp = jnp.min(jnp.where(gl == gmax, gidx, float(N_GROUPS)), axis=0, keepdims=True)
    p_grp = 1.0 / jnp.sum(jnp.exp(gl - gmax), axis=0, keepdims=True)
    sel = jnp.zeros((EXPERTS_PER_GROUP, tm), F32)
    for gi in range(N_GROUPS):
        lo = N_GROUPS + gi * EXPERTS_PER_GROUP
        sel = jnp.where(grp == float(gi), logits[lo:lo + EXPERTS_PER_GROUP, :], sel)
    eidx = lax.broadcasted_iota(jnp.int32, sel.shape, 0).astype(F32)
    top1 = jnp.max(sel, axis=0, keepdims=True)
    i1 = jnp.min(jnp.where(sel == top1, eidx, float(EXPERTS_PER_GROUP)), axis=0, keepdims=True)
    sel2 = jnp.where(eidx == i1, -jnp.inf, sel)
    top2 = jnp.max(sel2, axis=0, keepdims=True)
    i2 = jnp.min(jnp.where(sel2 == top2, eidx, float(EXPERTS_PER_GROUP)), axis=0, keepdims=True)
    e2 = jnp.exp(top2 - top1)
    inv = 1.0 / (1.0 + e2)
    zf = jnp.zeros((6, tm), F32)
    gate_ref[...] = jnp.concatenate([p_grp * inv, p_grp * (e2 * inv), zf], axis=0)
    base = grp * float(EXPERTS_PER_GROUP)
    eid_ref[...] = jnp.concatenate([base + i1, base + i2, zf], axis=0).astype(jnp.int32)


def _router_call(x_mid, g2n, mod, wr, br):
    s, d = x_mid.shape
    tm = TM_PROJ
    const = lambda j: (0, 0)
    return pl.pallas_call(
        _router_body,
        grid=(s // tm,),
        in_specs=[pl.BlockSpec((tm, d), lambda j: (j, 0)),
                  pl.BlockSpec((1, d), const),
                  pl.BlockSpec(mod.shape, const),
                  pl.BlockSpec(wr.shape, const),
                  pl.BlockSpec(br.shape, const)],
        out_specs=[pl.BlockSpec((tm * SUBLANES, LANES), lambda j: (j, 0)),
                   pl.BlockSpec((8, tm), lambda j: (0, j)),
                   pl.BlockSpec((8, tm), lambda j: (0, j))],
        out_shape=[jax.ShapeDtypeStruct((s * SUBLANES, LANES), F32),
                   jax.ShapeDtypeStruct((8, s), jnp.int32),
                   jax.ShapeDtypeStruct((8, s), F32)],
        compiler_params=_cparams(),
        name="router",
    )(x_mid, g2n, mod, wr, br)


def _tile_copy(src_ref, src_row, dst_ref, dst_row, sem):
    return pltpu.make_async_copy(src_ref.at[pl.ds(pl.multiple_of(src_row, SUBLANES), SUBLANES)],
                                 dst_ref.at[pl.ds(pl.multiple_of(dst_row, SUBLANES), SUBLANES)], sem)


def _experts_body(blk_e_ref, n_used_ref, n_prev_ref, src_ref, src_next_ref, dst_prev_ref, h3_ref,
                  wg_ref, wu_ref, wd_ref, y3_ref, xbuf0_ref, xbuf1_ref, obuf0_ref, obuf1_ref, gsem_ref, ssem_ref):
    del blk_e_ref
    i = pl.program_id(0)
    n_used = n_used_ref[0]
    n_prev = n_prev_ref[i]
    xbufs = (xbuf0_ref, xbuf1_ref)
    obufs = (obuf0_ref, obuf1_ref)

    def gather(idx_ref, slot, start):
        for r in range(MOE_BM):
            cp = _tile_copy(h3_ref, idx_ref[0, 0, r] if start else 0, xbufs[slot], r * SUBLANES, gsem_ref.at[slot])
            cp.start() if start else cp.wait()

    def scatter_row(slot, start, r):
        cp = _tile_copy(obufs[slot], r * SUBLANES, y3_ref, dst_prev_ref[0, 0, r] if start else 0, ssem_ref.at[0])
        cp.start() if start else cp.wait()

    def scatter(slot, start):
        @pl.when(n_prev == MOE_BM)
        def _():
            for r in range(MOE_BM):
                scatter_row(slot, start, r)

        @pl.when(n_prev < MOE_BM)
        def _():
            def body(r, c):
                scatter_row(slot, start, r)
                return c

            lax.fori_loop(0, n_prev, body, 0)

    @pl.when(i == 0)
    def _():
        gather(src_ref, 0, True)

    for slot in (0, 1):
        @pl.when(jnp.logical_and(i < n_used, (i & 1) == slot))
        def _():
            gather(src_ref, slot, False)
            gather(src_next_ref, 1 - slot, True)
            scatter(1 - slot, True)
            xb = _from_token_tiles(xbufs[slot], MOE_BM).astype(BF16)
            gate = _dot(xb, wg_ref[0].astype(BF16))
            upp = _dot(xb, wu_ref[0].astype(BF16))
            hid = (gate * jax.nn.sigmoid(gate)) * upp
            _to_token_tiles(obufs[slot], _dot(hid.astype(BF16), wd_ref[0].astype(BF16)))
            scatter(1 - slot, False)

        @pl.when(jnp.logical_and(i == n_used, (i & 1) == slot))
        def _():
            gather(src_ref, slot, False)
            scatter(1 - slot, True)
            scatter(1 - slot, False)


def _experts_call(blk_expert, n_used, n_prev, src3, dst_prev3, h3, wg, wu, wd, n_rows_out):
    nb = src3.shape[0]
    d = wg.shape[1]
    last = nb - 1
    cur = lambda i, be, nu, npv: (jnp.minimum(i, last), 0, 0)
    nxt = lambda i, be, nu, npv: (jnp.minimum(i + 1, last), 0, 0)
    wmap = lambda i, be, nu, npv: (be[jnp.minimum(i, last)], 0, 0)
    idx_blk = (1, 1, MOE_BM)
    buf = pltpu.VMEM((MOE_BM * SUBLANES, LANES), F32)
    grid_spec = pltpu.PrefetchScalarGridSpec(
        num_scalar_prefetch=3,
        grid=(nb + 1,),
        in_specs=[pl.BlockSpec(idx_blk, cur, memory_space=pltpu.SMEM),
                  pl.BlockSpec(idx_blk, nxt, memory_space=pltpu.SMEM),
                  pl.BlockSpec(idx_blk, lambda i, be, nu, npv: (i, 0, 0), memory_space=pltpu.SMEM),
                  pl.BlockSpec(memory_space=pl.ANY),
                  pl.BlockSpec((1, d, D_EXPERT), wmap),
                  pl.BlockSpec((1, d, D_EXPERT), wmap),
                  pl.BlockSpec((1, D_EXPERT, d), wmap)],
        out_specs=pl.BlockSpec(memory_space=pl.ANY),
        scratch_shapes=[buf, buf, buf, buf, pltpu.SemaphoreType.DMA((2,)), pltpu.SemaphoreType.DMA((1,))],
    )
    return pl.pallas_call(
        _experts_body,
        grid_spec=grid_spec,
        out_shape=jax.ShapeDtypeStruct((n_rows_out * SUBLANES, LANES), F32),
        compiler_params=_cparams(),
        name="experts",
    )(blk_expert, n_used, n_prev, src3, src3, dst_prev3, h3, wg, wu, wd)


def _combine_body(x_ref, y0_ref, y1_ref, gate_ref, mod_ref, fg_ref, o_ref):
    tm, d = x_ref.shape
    y = gate_ref[:, 0:1] * _from_token_tiles(y0_ref, tm) + gate_ref[:, 1:2] * _from_token_tiles(y1_ref, tm)
    x = x_ref[...] + mod_ref[0:1, 5 * d:6 * d] * y
    ms = jnp.mean(x * x, axis=-1, keepdims=True)
    o_ref[...] = x * lax.rsqrt(ms + NORM_EPS) * fg_ref[...]


def _combine_call(x_mid, y3, gcol, mod, fg):
    s, d = x_mid.shape
    tm = TM_PROJ
    nt = s // tm
    const = lambda j: (0, 0)
    return pl.pallas_call(
        _combine_body,
        grid=(nt,),
        in_specs=[pl.BlockSpec((tm, d), lambda j: (j, 0)),
                  pl.BlockSpec((tm * SUBLANES, LANES), lambda j: (j, 0)),
                  pl.BlockSpec((tm * SUBLANES, LANES), lambda j: (j + nt, 0)),
                  pl.BlockSpec((tm, 2), lambda j: (j, 0)),
                  pl.BlockSpec(mod.shape, const),
                  pl.BlockSpec((1, d), const)],
        out_specs=pl.BlockSpec((tm, d), lambda j: (j, 0)),
        out_shape=jax.ShapeDtypeStruct((s, d), F32),
        compiler_params=_cparams(),
        name="combine",
    )(x_mid, y3, y3, gcol, mod, fg)


def _blockdiag2(wf, wb):
    z = jnp.zeros_like(wf)
    return jnp.concatenate([jnp.concatenate([wf, z], axis=1), jnp.concatenate([z, wb], axis=1)], axis=0)


def _routing_plan(eids, n_tok):
    m = 2 * n_tok
    experts = jnp.arange(N_EXPERTS, dtype=jnp.int32)
    flat_e = eids[:2].reshape(m)
    order = jnp.argsort(flat_e, stable=True).astype(jnp.int32)
    counts = jnp.sum((flat_e[:, None] == experts[None, :]).astype(jnp.int32), axis=0)
    starts = jnp.cumsum(counts) - counts
    pcounts = (counts + MOE_BM - 1) // MOE_BM * MOE_BM
    pends = jnp.cumsum(pcounts)
    pstarts = pends - pcounts
    n_blocks = (m + N_EXPERTS * (MOE_BM - 1) + MOE_BM - 1) // MOE_BM
    slot = jnp.arange(n_blocks * MOE_BM, dtype=jnp.int32)
    slot_e = jnp.minimum(jnp.sum((slot[:, None] >= pends[None, :]).astype(jnp.int32), axis=1), N_EXPERTS - 1)
    sel = (slot_e[:, None] == experts[None, :]).astype(jnp.int32)
    local = slot - jnp.sum(sel * pstarts[None, :], axis=1)
    valid = local < jnp.sum(sel * counts[None, :], axis=1)
    pair = order[jnp.clip(jnp.sum(sel * starts[None, :], axis=1) + local, 0, m - 1)]
    src_tok = jnp.where(valid, jnp.where(pair >= n_tok, pair - n_tok, pair), 0)
    dst_row = jnp.where(valid, pair, 0)
    blk_expert = slot_e[::MOE_BM]
    n_used = (pends[-1:] // MOE_BM).astype(jnp.int32)
    shape3 = (n_blocks, 1, MOE_BM)
    n_valid = jnp.sum(valid.reshape(n_blocks, MOE_BM).astype(jnp.int32), axis=1)
    n_prev = jnp.concatenate([jnp.zeros((1,), jnp.int32), n_valid])
    dst_prev = jnp.concatenate([jnp.zeros((MOE_BM,), jnp.int32), dst_row])
    return ((src_tok * SUBLANES).reshape(shape3), (dst_prev * SUBLANES).reshape(n_blocks + 1, 1, MOE_BM),
            blk_expert, n_used, n_prev)


def kernel(x, c, ctx, c_ctx, norm1_g, w_mod, b_mod, w_in, mu_shift, w0_f, w2_f, w0_b, w2_b, a0_f, a2_f, a0_b, a2_b, g2, k_k, k_a, r_k_f, r_k_b, gn_w, gn_b, w_pool, pool_scale, w_out, norm2_g, w_router_grp, b_router_grp, w_router_exp, b_router_exp, w_gate, w_up, w_down, final_g):
    b, s, d = x.shape
    n_ctx = ctx.shape[1]
    assert b == 1 and c.shape[0] == 1 and w_mod.shape[0] == 1
    assert s % TM_PROJ == 0 and n_ctx % CHUNK == 0 and n_ctx == TM_FEAT and d == 2 * D_RWKV
    x2 = x[0]
    rows = s // GRID_W

    quarter = d // 4
    freq = POS_THETA ** (-jnp.arange(quarter, dtype=F32) / quarter)
    rarg = jnp.arange(rows, dtype=F32)[:, None] * freq
    carg = jnp.arange(GRID_W, dtype=F32)[:, None] * freq
    rowtab = jnp.concatenate([jnp.sin(rarg), jnp.cos(rarg)], axis=-1)
    coltab = jnp.tile(jnp.concatenate([jnp.sin(carg), jnp.cos(carg)], axis=-1), (TM_PROJ // GRID_W, 1))

    cs = jnp.concatenate([c, c_ctx[None, :], jnp.zeros((6, d), F32)], axis=0)
    mod = _mod_call(cs, w_mod[0], b_mod)

    ctx_pad = jnp.concatenate([ctx[0], jnp.zeros((TM_PROJ - n_ctx, d), F32)], axis=0)
    u_shift, p = _inproj_call(x2, ctx_pad, rowtab, coltab, norm1_g, mod, w_in[0].astype(BF16))

    par = jnp.concatenate([w0_f, w0_b, a0_f, a0_b, k_k, k_a, r_k_f, r_k_b], axis=0)
    n_tok = s + n_ctx
    outs = _feat_call(u_shift, n_tok, s // TM_FEAT, mu_shift,
                      par, _blockdiag2(w2_f[0], w2_b[0]).astype(BF16),
                      _blockdiag2(a2_f[0], a2_b[0]).astype(BF16), g2[0].astype(BF16))
    feats, gamf, gamb, bonus, g = outs[:13], outs[13], outs[14], outs[15], outs[16]
    n_chunks = n_tok // CHUNK
    yf, yb = _scan_call(feats, gamf.reshape(n_chunks, 1, D_RWKV), gamb.reshape(n_chunks, 1, D_RWKV),
                        s // CHUNK, n_ctx // CHUNK)

    gn = jnp.concatenate([gn_w, gn_b], axis=0)
    x_mid = _mix_call(x2, rowtab, coltab, yf, yb, bonus, g, p, gn, w_pool[0].astype(BF16), pool_scale,
                      w_out[0].astype(BF16), mod)

    wr = jnp.concatenate([w_router_grp[0].T, w_router_exp[0].T,
                          jnp.zeros((ROUTER_ROWS - N_GROUPS - N_EXPERTS, d), F32)], axis=0)
    br = jnp.concatenate([b_router_grp[0], b_router_exp[0],
                          jnp.zeros((ROUTER_ROWS - N_GROUPS - N_EXPERTS,), F32)])[:, None]
    h3, eids, gates = _router_call(x_mid, norm2_g, mod, wr, br)

    src3, dst_prev3, blk_expert, n_used, n_prev = _routing_plan(eids, s)
    y3 = _experts_call(blk_expert, n_used, n_prev, src3, dst_prev3, h3, w_gate[0], w_up[0], w_down[0], 2 * s)
    out = _combine_call(x_mid, y3, gates[:2].T, mod, final_g[None, :])
    return out[None]
```

```python
import functools

import jax
import jax.numpy as jnp
from jax import lax
from jax.experimental import pallas as pl
from jax.experimental.pallas import tpu as pltpu

F32 = jnp.float32
BF16 = jnp.bfloat16

GRID_W = 64
SUBLANES, LANES = 8, 128
HEAD = 64
N_HEADS = 8
D_RWKV = HEAD * N_HEADS
D_POOL = 512
POOL_WINDOWS = (2, 4, 8, 16)
POOL_GW = 128
POOL_HALO = 8
D_SHIFT = 1920
N_GROUPS = 4
EXPERTS_PER_GROUP = 8
N_EXPERTS = 32
D_EXPERT = 512
NORM_EPS = 1e-6
GN_EPS = 64e-5
POS_THETA = 10000.0
DECAY_SCALE = 0.6065306597

CHUNK = 64
QUAD = 4 * HEAD
SCAN_CPS = 4
TM_PROJ = 512
TM_FEAT = 256
MOE_BM = 256
ROUTER_ROWS = 48
V7X_VMEM_LIMIT = 56 * 1024 * 1024


def _cparams(n_axes=1, vmem=V7X_VMEM_LIMIT):
    return pltpu.CompilerParams(dimension_semantics=("arbitrary",) * n_axes,
                                vmem_limit_bytes=vmem)


def _dot(a, b):
    return jnp.dot(a, b, preferred_element_type=F32)


def _dot_nt(a, b):
    return lax.dot_general(a, b, (((1,), (1,)), ((), ())), preferred_element_type=F32)


def _dot_tn(a, b):
    return lax.dot_general(a, b, (((0,), (0,)), ((), ())), preferred_element_type=F32)


def _split2(x):
    hi = x.astype(BF16)
    lo = (x - hi.astype(F32)).astype(BF16)
    return hi, lo


def _split3(x):
    hi = x.astype(BF16)
    r1 = x - hi.astype(F32)
    mid = r1.astype(BF16)
    lo = (r1 - mid.astype(F32)).astype(BF16)
    return hi, mid, lo


def _dot3(a, b):
    ah, al = _split2(a)
    bh, bl = _split2(b)
    return _dot(ah, bh) + (_dot(ah, bl) + _dot(al, bh))


def _head_ones():
    r = lax.broadcasted_iota(jnp.int32, (D_RWKV, D_RWKV), 0) >> 6
    c = lax.broadcasted_iota(jnp.int32, (D_RWKV, D_RWKV), 1) >> 6
    return (r == c).astype(BF16)


def _headsum(x, ones):
    hi, lo = _split2(x)
    return _dot(hi, ones) + _dot(lo, ones)


def _mod_body(cs_ref, w_ref, b_ref, o_ref):
    a = cs_ref[...]
    a = a * jax.nn.sigmoid(a)
    o_ref[...] = _dot3(a, w_ref[...]) + b_ref[...]


def _mod_call(cs, w_mod, b_mod):
    d, n = w_mod.shape
    tn = 512
    return pl.pallas_call(
        _mod_body,
        grid=(n // tn,),
        in_specs=[pl.BlockSpec((8, d), lambda j: (0, 0)),
                  pl.BlockSpec((d, tn), lambda j: (0, j)),
                  pl.BlockSpec((1, tn), lambda j: (0, j))],
        out_specs=pl.BlockSpec((8, tn), lambda j: (0, j)),
        out_shape=jax.ShapeDtypeStruct((8, n), F32),
        compiler_params=_cparams(),
        name="mod",
    )(cs, w_mod, b_mod)


def _pos_tile(rowtab_ref, coltab_ref):
    rt = rowtab_ref[...]
    rowpart = jnp.concatenate(
        [jnp.broadcast_to(rt[r:r + 1, :], (GRID_W, rt.shape[1])) for r in range(TM_PROJ // GRID_W)], axis=0)
    return jnp.concatenate([rowpart, coltab_ref[...]], axis=1)


def _inproj_body(x_ref, ctx_ref, rowtab_ref, coltab_ref, g1_ref, mod_ref, w_ref, us_ref, p_ref, *, n_x_tiles):
    d = x_ref.shape[1]
    is_ctx = pl.program_id(0) == n_x_tiles
    xin = jnp.where(is_ctx, ctx_ref[...], x_ref[...] + _pos_tile(rowtab_ref, coltab_ref))
    ms = jnp.mean(xin * xin, axis=-1, keepdims=True)
    xn = xin * lax.rsqrt(ms + NORM_EPS) * g1_ref[...]
    sh = jnp.where(is_ctx, mod_ref[1:2, 0:d], mod_ref[0:1, 0:d])
    sc = jnp.where(is_ctx, mod_ref[1:2, d:2 * d], mod_ref[0:1, d:2 * d])
    h = xn * (1.0 + sc) + sh
    u = _dot(h.astype(BF16), w_ref[...])
    us_ref[...] = u[:, :D_SHIFT]
    p_ref[...] = u[:, D_SHIFT:]


def _inproj_call(x2, ctx_pad, rowtab, coltab, g1, mod, w_in_bf):
    s, d = x2.shape
    nx = s // TM_PROJ
    d_in = w_in_bf.shape[1]
    rows = (nx + 1) * TM_PROJ
    xmap = lambda i: (jnp.minimum(i, nx - 1), 0)
    const = lambda i: (0, 0)
    return pl.pallas_call(
        functools.partial(_inproj_body, n_x_tiles=nx),
        grid=(nx + 1,),
        in_specs=[pl.BlockSpec((TM_PROJ, d), xmap),
                  pl.BlockSpec((TM_PROJ, d), const),
                  pl.BlockSpec((TM_PROJ // GRID_W, d // 2), xmap),
                  pl.BlockSpec((TM_PROJ, d // 2), const),
                  pl.BlockSpec((1, d), const),
                  pl.BlockSpec(mod.shape, const),
                  pl.BlockSpec((d, d_in), const)],
        out_specs=[pl.BlockSpec((TM_PROJ, D_SHIFT), lambda i: (i, 0)),
                   pl.BlockSpec((TM_PROJ, D_POOL), lambda i: (i, 0))],
        out_shape=[jax.ShapeDtypeStruct((rows, D_SHIFT), F32),
                   jax.ShapeDtypeStruct((rows, D_POOL), F32)],
        compiler_params=_cparams(),
        name="inproj",
    )(x2, ctx_pad, rowtab, coltab, g1, mod, w_in_bf)


def _feat_body(um_ref, up_ref, un_ref, mu_ref, par_ref, w2_ref, a2_ref, g2_ref,
               af_ref, rf_ref, btf_ref, ktf_ref, bhf_ref, khf_ref,
               ab_ref, rb_ref, btb_ref, ktb_ref, bhb_ref, khb_ref,
               v_ref, gamf_ref, gamb_ref, bonus_ref, g_ref, *, n_x_tiles):
    i = pl.program_id(0)
    tm = um_ref.shape[0]
    u = um_ref[...]
    prev_ok = jnp.logical_and(i != 0, i != n_x_tiles)
    next_ok = jnp.logical_and(i != n_x_tiles - 1, i != n_x_tiles)
    prow = jnp.where(prev_ok, up_ref[POOL_HALO - 1:POOL_HALO, :], 0.0)
    nrow = jnp.where(next_ok, un_ref[0:1, :], 0.0)
    rid = lax.broadcasted_iota(jnp.int32, (tm, 1), 0)
    up = jnp.where(rid == 0, prow, pltpu.roll(u, 1, 0))
    dn = jnp.where(rid == tm - 1, nrow, pltpu.roll(u, tm - 1, 0))
    s = u + mu_ref[...] * (0.5 * (up + dn) - u)

    w0f, w0b, a0f, a0b = par_ref[0:1, :], par_ref[1:2, :], par_ref[2:3, :], par_ref[3:4, :]
    k_k, k_a, rkf, rkb = par_ref[4:5, :], par_ref[5:6, :], par_ref[6:7, :], par_ref[7:8, :]

    r = s[:, 0:512]
    k = s[:, 512:1024]
    v = s[:, 1024:1536]
    zw = _dot(jnp.tanh(s[:, 1536:1664]).astype(BF16), w2_ref[...])
    za = _dot(s[:, 1664:1792].astype(BF16), a2_ref[...])
    g = _dot(jax.nn.sigmoid(s[:, 1792:1920]).astype(BF16), g2_ref[...])
    lw_f = -DECAY_SCALE * jax.nn.sigmoid(w0f + zw[:, :512])
    lw_b = -DECAY_SCALE * jax.nn.sigmoid(w0b + zw[:, 512:])
    a_f = jax.nn.sigmoid(a0f + za[:, :512])
    a_b = jax.nn.sigmoid(a0b + za[:, 512:])

    ones = _head_ones()
    kkr = k * k_k
    nrm = jnp.sqrt(_headsum(kkr * kkr, ones))
    kk = kkr / jnp.maximum(nrm, 1e-12)
    k_f = k * (1.0 + (a_f - 1.0) * k_a)
    k_b = k * (1.0 + (a_b - 1.0) * k_a)
    bonus_ref[...] = _headsum(r * (k_f * rkf + k_b * rkb), ones) * v
    g_ref[...] = g
    v_ref[...] = v.astype(BF16)

    rr = lax.broadcasted_iota(jnp.int32, (tm, tm), 0)
    cc = lax.broadcasted_iota(jnp.int32, (tm, tm), 1)
    same = (rr >> 6) == (cc >> 6)
    tri = jnp.concatenate([jnp.logical_and(same, cc < rr).astype(BF16),
                           jnp.logical_and(same, cc > rr).astype(BF16)], axis=0)
    lw = jnp.concatenate([lw_f, lw_b], axis=1)
    l1, l2, l3 = _split3(lw)
    sums = _dot(tri, l1) + (_dot(tri, l2) + _dot(tri, l3))
    before_f, after_f = sums[:tm, :512], sums[tm:, :512]
    before_b, after_b = sums[:tm, 512:], sums[tm:, 512:]

    def emit(excl, rest, lwd, a_d, k_d, a_ref, r_ref, bt_ref, kt_ref, bh_ref, kh_ref, gam_ref):
        cum = excl + lwd
        e_neg = jnp.exp(-cum)
        e_rest = jnp.exp(rest)
        kka = kk * a_d
        a_ref[...] = (-kk * jnp.exp(excl)).astype(BF16)
        r_ref[...] = (r * jnp.exp(cum)).astype(BF16)
        bt_ref[...] = (kka * e_neg).astype(BF16)
        kt_ref[...] = (k_d * e_neg).astype(BF16)
        bh_ref[...] = (kka * e_rest).astype(BF16)
        kh_ref[...] = (k_d * e_rest).astype(BF16)
        tot = cum + rest
        gam_ref[0] = jnp.exp(jnp.concatenate(
            [tot[c * CHUNK:c * CHUNK + 1, :] for c in range(tm // CHUNK)], axis=0))

    emit(before_f, after_f, lw_f, a_f, k_f, af_ref, rf_ref, btf_ref, ktf_ref, bhf_ref, khf_ref, gamf_ref)
    emit(after_b, before_b, lw_b, a_b, k_b, ab_ref, rb_ref, btb_ref, ktb_ref, bhb_ref, khb_ref, gamb_ref)


def _feat_call(u_shift, n_tok, n_x_tiles, mu, par, w2, a2, g2):
    tm = TM_FEAT
    nt = n_tok // tm
    hb = tm // POOL_HALO
    last_blk = u_shift.shape[0] // POOL_HALO - 1
    const = lambda i: (0, 0)
    tok = lambda i: (i, 0)
    seq_bf = jax.ShapeDtypeStruct((n_tok, D_RWKV), BF16)
    seq_f32 = jax.ShapeDtypeStruct((n_tok, D_RWKV), F32)
    gam = jax.ShapeDtypeStruct((nt, tm // CHUNK, D_RWKV), F32)
    seq_spec = pl.BlockSpec((tm, D_RWKV), tok)
    gam_spec = pl.BlockSpec((1, tm // CHUNK, D_RWKV), lambda i: (i, 0, 0))
    return pl.pallas_call(
        functools.partial(_feat_body, n_x_tiles=n_x_tiles),
        grid=(nt,),
        in_specs=[pl.BlockSpec((tm, D_SHIFT), tok),
                  pl.BlockSpec((POOL_HALO, D_SHIFT), lambda i: (jnp.maximum(i * hb - 1, 0), 0)),
                  pl.BlockSpec((POOL_HALO, D_SHIFT), lambda i: (jnp.minimum((i + 1) * hb, last_blk), 0)),
                  pl.BlockSpec((1, D_SHIFT), const),
                  pl.BlockSpec((8, D_RWKV), const),
                  pl.BlockSpec(w2.shape, const),
                  pl.BlockSpec(a2.shape, const),
                  pl.BlockSpec(g2.shape, const)],
        out_specs=[seq_spec] * 13 + [gam_spec, gam_spec, seq_spec, seq_spec],
        out_shape=[seq_bf] * 13 + [gam, gam, seq_f32, seq_f32],
        compiler_params=_cparams(),
        name="feat",
    )(u_shift, u_shift, u_shift, mu, par, w2, a2, g2)


def _blockdiag(x, head_of_lane):
    xb = x.astype(BF16)
    zero = jnp.zeros_like(xb)
    return jnp.concatenate([jnp.where(head_of_lane == h, xb, zero) for h in range(QUAD // HEAD)], axis=0)


def _unblock(x, head_of_lane):
    out = jnp.where(head_of_lane == 0, x[0:HEAD, :], 0.0)
    for h in range(1, QUAD // HEAD):
        out = jnp.where(head_of_lane == h, x[h * HEAD:(h + 1) * HEAD, :], out)
    return out


def _scan_chunks(chains):
    lane = lax.broadcasted_iota(jnp.int32, (HEAD, QUAD), 1)
    head_of_lane = lane >> 6
    s_idx = lane & (HEAD - 1)
    t_idx = lax.broadcasted_iota(jnp.int32, (HEAD, QUAD), 0)
    eye = s_idx == t_idx
    lower = s_idx < t_idx
    upper = s_idx > t_idx
    bd = functools.partial(_blockdiag, head_of_lane=head_of_lane)

    def same_block(shift):
        return (s_idx >> shift) == (t_idx >> shift)

    def each(fn):
        return [fn(c) for c in chains]

    def stage_masks(c):
        ll = _dot_nt(jnp.concatenate([c["a"], c["r"]], axis=0),
                     jnp.concatenate([bd(c["bt"]), bd(c["kt"])], axis=0))
        strict = upper if c["reverse"] else lower
        incl = jnp.logical_or(strict, eye)
        c["lab"] = jnp.where(strict, ll[:HEAD, :QUAD], 0.0)
        c["lak"] = jnp.where(strict, ll[:HEAD, QUAD:], 0.0)
        c["mrb"] = jnp.where(incl, ll[HEAD:, :QUAD], 0.0)
        c["mrk"] = jnp.where(incl, ll[HEAD:, QUAD:], 0.0)

    each(stage_masks)

    def stage_square(c):
        ld = jnp.where(same_block(4), c["lab"], 0.0)
        c["tm"] = jnp.where(eye, 1.0, 0.0) + ld
        c["xk"] = _dot(ld.astype(BF16), bd(ld))

    def stage_double(c):
        y = _dot(jnp.concatenate([c["tm"], c["xk"]], axis=0).astype(BF16), bd(c["xk"]))
        c["tm"] = c["tm"] + y[:HEAD]
        c["xk"] = y[HEAD:]

    def stage_last_power(c):
        c["tm"] = c["tm"] + _dot(c["tm"].astype(BF16), bd(c["xk"]))

    each(stage_square)
    each(stage_double)
    each(stage_double)
    each(stage_last_power)
    each(lambda c: c.update(vv=_dot(jnp.concatenate([c["lak"], c["mrk"]], axis=0).astype(BF16), bd(c["v"]))))
    for shift in (4, 5):
        joined = jnp.logical_and(same_block(shift + 1), jnp.logical_not(same_block(shift)))
        each(lambda c: c.update(te=_dot(c["tm"].astype(BF16), bd(jnp.where(joined, c["lab"], 0.0)))))
        each(lambda c: c.update(tm=c["tm"] + _dot(c["te"].astype(BF16), bd(c["tm"]))))

    each(lambda c: c.update(x1=_dot(c["tm"].astype(BF16),
                                    jnp.concatenate([bd(c["a"]), bd(c["vv"][:HEAD])], axis=1))))
    each(lambda c: c.update(x2=_dot(c["mrb"].astype(BF16),
                                    jnp.concatenate([bd(c["x1"][:, :QUAD]), bd(c["x1"][:, QUAD:])], axis=1))))

    def stage_state_terms(c):
        rhs = jnp.concatenate(
            [c["x1"].astype(BF16), jnp.concatenate([jnp.zeros((HEAD, QUAD), BF16), c["v"]], axis=1)], axis=0)
        pz = _dot_tn(jnp.concatenate([c["bh"], c["kh"]], axis=0), rhs)
        c["pq"] = _unblock(pz[:, :QUAD], head_of_lane) + jnp.where(eye, c["gam"], 0.0)
        c["zq"] = _unblock(pz[:, QUAD:], head_of_lane)
        c["qeff"] = c["r"].astype(F32) + c["x2"][:, :QUAD]
        c["yin"] = c["x2"][:, QUAD:] + c["vv"][HEAD:]

    each(stage_state_terms)

    outs, state = [], {}
    for c in chains:
        hst = state.get(c["key"], c["hst"])
        qp = _dot(jnp.concatenate([c["qeff"], c["pq"]], axis=0).astype(BF16), bd(hst))
        state[c["key"]] = qp[HEAD:] + c["zq"]
        outs.append(qp[:HEAD] + c["yin"])
    return outs, state


def _scan_body(af, rf, btf, ktf, bhf, khf, vf, gamf, ab, rb, btb, ktb, bhb, khb, vb, gamb,
               yf_ref, yb_ref, hf_ref, hb_ref):
    @pl.when(pl.program_id(0) == 0)
    def _():
        hf_ref[...] = jnp.zeros_like(hf_ref)
        hb_ref[...] = jnp.zeros_like(hb_ref)

    names = ("a", "r", "bt", "kt", "bh", "kh", "v")
    chains, sinks, state_refs = [], [], {}
    for ci in range(SCAN_CPS):
        for refs, y_ref, h_ref, reverse in (((af, rf, btf, ktf, bhf, khf, vf, gamf), yf_ref, hf_ref, False),
                                            ((ab, rb, btb, ktb, bhb, khb, vb, gamb), yb_ref, hb_ref, True)):
            chunk = SCAN_CPS - 1 - ci if reverse else ci
            rows = slice(chunk * CHUNK, (chunk + 1) * CHUNK)
            for q in range(D_RWKV // QUAD):
                sl = slice(q * QUAD, (q + 1) * QUAD)
                chain = {n: ref[rows, sl] for n, ref in zip(names, refs[:7])}
                chain.update(gam=refs[7][chunk, :, sl], hst=h_ref[:, sl], reverse=reverse, key=(reverse, q))
                chains.append(chain)
                sinks.append((y_ref, rows, sl))
                state_refs[(reverse, q)] = (h_ref, sl)
    outs, state = _scan_chunks(chains)
    for y, (y_ref, rows, sl) in zip(outs, sinks):
        y_ref[rows, sl] = y
    for key, (h_ref, sl) in state_refs.items():
        h_ref[:, sl] = state[key]


def _scan_call(feats, gamf, gamb, n_x_chunks, n_c_chunks):
    af, rf, btf, ktf, bhf, khf, ab, rb, btb, ktb, bhb, khb, v = feats
    assert n_x_chunks % SCAN_CPS == 0 and n_c_chunks % SCAN_CPS == 0
    nxb, ncb = n_x_chunks // SCAN_CPS, n_c_chunks // SCAN_CPS
    n = nxb + ncb
    fwd = lambda i: (jnp.where(i < ncb, nxb + i, i - ncb), 0)
    bwd = lambda i: (n - 1 - i, 0)
    fwd3 = lambda i: (jnp.where(i < ncb, nxb + i, i - ncb), 0, 0)
    bwd3 = lambda i: (n - 1 - i, 0, 0)
    blk = (SCAN_CPS * CHUNK, D_RWKV)
    in_specs = ([pl.BlockSpec(blk, fwd)] * 7 + [pl.BlockSpec((SCAN_CPS, 1, D_RWKV), fwd3)]
                + [pl.BlockSpec(blk, bwd)] * 7 + [pl.BlockSpec((SCAN_CPS, 1, D_RWKV), bwd3)])
    y = jax.ShapeDtypeStruct((n * SCAN_CPS * CHUNK, D_RWKV), F32)
    return pl.pallas_call(
        _scan_body,
        grid=(n,),
        in_specs=in_specs,
        out_specs=[pl.BlockSpec(blk, fwd), pl.BlockSpec(blk, bwd)],
        out_shape=[y, y],
        scratch_shapes=[pltpu.VMEM((HEAD, D_RWKV), F32), pltpu.VMEM((HEAD, D_RWKV), F32)],
        compiler_params=_cparams(),
        name="scan",
    )(af, rf, btf, ktf, bhf, khf, v, gamf, ab, rb, btb, ktb, bhb, khb, v, gamb)


def _mix_body(x_ref, rowtab_ref, coltab_ref, yf_ref, yb_ref, bonus_ref, g_ref, pm_ref, pp_ref, pn_ref,
              gn_ref, wpool_ref, pscale_ref, wout_ref, mod_ref, o_ref, slab_ref, *, n_x_tiles, seq_len):
    j = pl.program_id(0)
    tm, d = x_ref.shape
    hl = POOL_HALO
    slab_ref[0:hl, :] = jnp.where(j != 0, pp_ref[...], 0.0)
    slab_ref[hl:hl + tm, :] = pm_ref[...]
    slab_ref[hl + tm:2 * hl + tm, :] = jnp.where(j != n_x_tiles - 1, pn_ref[...], 0.0)
    tglob = j * tm + lax.broadcasted_iota(jnp.int32, (tm, 1), 0)
    pooled = []
    for gi, win in enumerate(POOL_WINDOWS):
        cols = slice(gi * POOL_GW, (gi + 1) * POOL_GW)
        acc = slab_ref[hl - win // 2:hl - win // 2 + tm, cols]
        for off in range(-win // 2 + 1, win // 2):
            acc = acc + slab_ref[hl + off:hl + off + tm, cols]
        lo = jnp.maximum(tglob - win // 2, 0)
        hi = jnp.minimum(tglob - win // 2 + win, seq_len)
        cnt = (hi - lo).astype(F32)
        pg = acc / cnt - slab_ref[hl:hl + tm, cols]
        pooled.append(_dot(pg.astype(BF16), wpool_ref[gi]))
    pool = jnp.concatenate(pooled, axis=1) * pscale_ref[...]

    ones = _head_ones()
    y = yf_ref[...] + yb_ref[...]
    yc = y - _headsum(y, ones) * (1.0 / HEAD)
    var = _headsum(yc * yc, ones) * (1.0 / HEAD)
    rw = (yc * lax.rsqrt(var + GN_EPS) * gn_ref[0:1, :] + gn_ref[1:2, :] + bonus_ref[...]) * g_ref[...]
    mix = _dot(jnp.concatenate([rw, pool], axis=1).astype(BF16), wout_ref[...])
    gt_a = mod_ref[0:1, 2 * d:3 * d]
    o_ref[...] = x_ref[...] + _pos_tile(rowtab_ref, coltab_ref) + gt_a * mix


def _mix_call(x2, rowtab, coltab, yf, yb, bonus, g, p, gn, w_pool_bf, pscale, w_out_bf, mod):
    s, d = x2.shape
    tm = TM_PROJ
    nx = s // tm
    hb = tm // POOL_HALO
    const = lambda j: (0, 0)
    tok = lambda j: (j, 0)
    seq_spec = pl.BlockSpec((tm, D_RWKV), tok)
    return pl.pallas_call(
        functools.partial(_mix_body, n_x_tiles=nx, seq_len=s),
        grid=(nx,),
        in_specs=[pl.BlockSpec((tm, d), tok),
                  pl.BlockSpec((tm // GRID_W, d // 2), tok),
                  pl.BlockSpec((tm, d // 2), const),
                  seq_spec, seq_spec, seq_spec, seq_spec, seq_spec,
                  pl.BlockSpec((POOL_HALO, D_POOL), lambda j: (jnp.maximum(j * hb - 1, 0), 0)),
                  pl.BlockSpec((POOL_HALO, D_POOL), lambda j: ((j + 1) * hb, 0)),
                  pl.BlockSpec((2, D_RWKV), const),
                  pl.BlockSpec(w_pool_bf.shape, lambda j: (0, 0, 0)),
                  pl.BlockSpec((1, D_POOL), const),
                  pl.BlockSpec(w_out_bf.shape, const),
                  pl.BlockSpec(mod.shape, const)],
        out_specs=pl.BlockSpec((tm, d), tok),
        out_shape=jax.ShapeDtypeStruct((s, d), F32),
        scratch_shapes=[pltpu.VMEM((tm + 2 * POOL_HALO, D_POOL), F32)],
        compiler_params=_cparams(),
        name="mix",
    )(x2, rowtab, coltab, yf, yb, bonus, g, p, p, p, gn, w_pool_bf, pscale, w_out_bf, mod)


def _to_token_tiles(ref, x):
    tm = x.shape[0]
    for j in range(x.shape[1] // LANES):
        ref[pl.ds(j, tm, stride=SUBLANES), :] = x[:, j * LANES:(j + 1) * LANES]


def _from_token_tiles(ref, tm):
    return jnp.concatenate([ref[pl.ds(j, tm, stride=SUBLANES), :] for j in range(SUBLANES)], axis=1)


def _router_body(x_ref, g2_ref, mod_ref, wr_ref, br_ref, h3_ref, eid_ref, gate_ref):
    tm, d = x_ref.shape
    x = x_ref[...]
    ms = jnp.mean(x * x, axis=-1, keepdims=True)
    xn = x * lax.rsqrt(ms + NORM_EPS) * g2_ref[...]
    h = xn * (1.0 + mod_ref[0:1, 4 * d:5 * d]) + mod_ref[0:1, 3 * d:4 * d]
    _to_token_tiles(h3_ref, h)

    wh, wl = _split2(wr_ref[...])
    hh, hl = _split2(h)
    logits = _dot_nt(wh, hh) + (_dot_nt(wh, hl) + _dot_nt(wl, hh)) + br_ref[...]
    gl = logits[0:N_GROUPS, :]
    gmax = jnp.max(gl, axis=0, keepdims=True)
    gidx = lax.broadcasted_iota(jnp.int32, gl.shape, 0).astype(F32)
    grp = jnp.min(jnp.where(gl == gmax, gidx, float(N_GROUPS)), axis=0, keepdims=True)
    p_grp = 1.0 / jnp.sum(jnp.exp(gl - gmax), axis=0, keepdims=True)
    sel = jnp.zeros((EXPERTS_PER_GROUP, tm), F32)
    for gi in range(N_GROUPS):
        lo = N_GROUPS + gi * EXPERTS_PER_GROUP
        sel = jnp.where(grp == float(gi), logits[lo:lo + EXPERTS_PER_GROUP, :], sel)
    eidx = lax.broadcasted_iota(jnp.int32, sel.shape, 0).astype(F32)
    top1 = jnp.max(sel, axis=0, keepdims=True)
    i1 = jnp.min(jnp.where(sel == top1, eidx, float(EXPERTS_PER_GROUP)), axis=0, keepdims=True)
    sel2 = jnp.where(eidx == i1, -jnp.inf, sel)
    top2 = jnp.max(sel2, axis=0, keepdims=True)
    i2 = jnp.min(jnp.where(sel2 == top2, eidx, float(EXPERTS_PER_GROUP)), axis=0, keepdims=True)
    e2 = jnp.exp(top2 - top1)
    inv = 1.0 / (1.0 + e2)
    zf = jnp.zeros((6, tm), F32)
    gate_ref[...] = jnp.concatenate([p_grp * inv, p_grp * (e2 * inv), zf], axis=0)
    base = grp * float(EXPERTS_PER_GROUP)
    eid_ref[...] = jnp.concatenate([base + i1, base + i2, zf], axis=0).astype(jnp.int32)


def _router_call(x_mid, g2n, mod, wr, br):
    s, d = x_mid.shape
    tm = TM_PROJ
    const = lambda j: (0, 0)
    return pl.pallas_call(
        _router_body,
        grid=(s // tm,),
        in_specs=[pl.BlockSpec((tm, d), lambda j: (j, 0)),
                  pl.BlockSpec((1, d), const),
                  pl.BlockSpec(mod.shape, const),
                  pl.BlockSpec(wr.shape, const),
                  pl.BlockSpec(br.shape, const)],
        out_specs=[pl.BlockSpec((tm * SUBLANES, LANES), lambda j: (j, 0)),
                   pl.BlockSpec((8, tm), lambda j: (0, j)),
                   pl.BlockSpec((8, tm), lambda j: (0, j))],
        out_shape=[jax.ShapeDtypeStruct((s * SUBLANES, LANES), F32),
                   jax.ShapeDtypeStruct((8, s), jnp.int32),
                   jax.ShapeDtypeStruct((8, s), F32)],
        compiler_params=_cparams(),
        name="router",
    )(x_mid, g2n, mod, wr, br)


def _tile_copy(src_ref, src_row, dst_ref, dst_row, sem):
    return pltpu.make_async_copy(src_ref.at[pl.ds(pl.multiple_of(src_row, SUBLANES), SUBLANES)],
                                 dst_ref.at[pl.ds(pl.multiple_of(dst_row, SUBLANES), SUBLANES)], sem)


def _experts_body(blk_e_ref, n_used_ref, n_prev_ref, src_ref, src_next_ref, dst_prev_ref, h3_ref,
                  wg_ref, wu_ref, wd_ref, y3_ref, xbuf0_ref, xbuf1_ref, obuf0_ref, obuf1_ref, gsem_ref, ssem_ref):
    del blk_e_ref
    i = pl.program_id(0)
    n_used = n_used_ref[0]
    n_prev = n_prev_ref[i]
    xbufs = (xbuf0_ref, xbuf1_ref)
    obufs = (obuf0_ref, obuf1_ref)

    def gather(idx_ref, slot, start):
        for r in range(MOE_BM):
            cp = _tile_copy(h3_ref, idx_ref[0, 0, r] if start else 0, xbufs[slot], r * SUBLANES, gsem_ref.at[slot])
            cp.start(priority=r % 2) if start else cp.wait()

    def scatter_row(slot, start, r):
        cp = _tile_copy(obufs[slot], r * SUBLANES, y3_ref, dst_prev_ref[0, 0, r] if start else 0, ssem_ref.at[0])
        cp.start(priority=r % 2 if isinstance(r, int) else 0) if start else cp.wait()

    def scatter(slot, start):
        @pl.when(n_prev == MOE_BM)
        def _():
            for r in range(MOE_BM):
                scatter_row(slot, start, r)

        @pl.when(n_prev < MOE_BM)
        def _():
            def body(r, c):
                scatter_row(slot, start, r)
                return c

            lax.fori_loop(0, n_prev, body, 0)

    @pl.when(i == 0)
    def _():
        gather(src_ref, 0, True)

    for slot in (0, 1):
        @pl.when(jnp.logical_and(i < n_used, (i & 1) == slot))
        def _():
            gather(src_ref, slot, False)
            gather(src_next_ref, 1 - slot, True)
            scatter(1 - slot, True)
            xb = _from_token_tiles(xbufs[slot], MOE_BM).astype(BF16)
            gate = _dot(xb, wg_ref[0].astype(BF16))
            upp = _dot(xb, wu_ref[0].astype(BF16))
            hid = (gate * jax.nn.sigmoid(gate)) * upp
            _to_token_tiles(obufs[slot], _dot(hid.astype(BF16), wd_ref[0].astype(BF16)))
            scatter(1 - slot, False)

        @pl.when(jnp.logical_and(i == n_used, (i & 1) == slot))
        def _():
            gather(src_ref, slot, False)
            scatter(1 - slot, True)
            scatter(1 - slot, False)


def _experts_call(blk_expert, n_used, n_prev, src3, dst_prev3, h3, wg, wu, wd, n_rows_out):
    nb = src3.shape[0]
    d = wg.shape[1]
    last = nb - 1
    cur = lambda i, be, nu, npv: (jnp.minimum(i, last), 0, 0)
    nxt = lambda i, be, nu, npv: (jnp.minimum(i + 1, last), 0, 0)
    wmap = lambda i, be, nu, npv: (be[jnp.minimum(i, last)], 0, 0)
    idx_blk = (1, 1, MOE_BM)
    buf = pltpu.VMEM((MOE_BM * SUBLANES, LANES), F32)
    grid_spec = pltpu.PrefetchScalarGridSpec(
        num_scalar_prefetch=3,
        grid=(nb + 1,),
        in_specs=[pl.BlockSpec(idx_blk, cur, memory_space=pltpu.SMEM),
                  pl.BlockSpec(idx_blk, nxt, memory_space=pltpu.SMEM),
                  pl.BlockSpec(idx_blk, lambda i, be, nu, npv: (i, 0, 0), memory_space=pltpu.SMEM),
                  pl.BlockSpec(memory_space=pl.ANY),
                  pl.BlockSpec((1, d, D_EXPERT), wmap),
                  pl.BlockSpec((1, d, D_EXPERT), wmap),
                  pl.BlockSpec((1, D_EXPERT, d), wmap)],
        out_specs=pl.BlockSpec(memory_space=pl.ANY),
        scratch_shapes=[buf, buf, buf, buf, pltpu.SemaphoreType.DMA((2,)), pltpu.SemaphoreType.DMA((1,))],
    )
    return pl.pallas_call(
        _experts_body,
        grid_spec=grid_spec,
        out_shape=jax.ShapeDtypeStruct((n_rows_out * SUBLANES, LANES), F32),
        compiler_params=_cparams(),
        name="experts",
    )(blk_expert, n_used, n_prev, src3, src3, dst_prev3, h3, wg, wu, wd)


def _combine_body(x_ref, y0_ref, y1_ref, gate_ref, mod_ref, fg_ref, o_ref):
    tm, d = x_ref.shape
    y = gate_ref[:, 0:1] * _from_token_tiles(y0_ref, tm) + gate_ref[:, 1:2] * _from_token_tiles(y1_ref, tm)
    x = x_ref[...] + mod_ref[0:1, 5 * d:6 * d] * y
    ms = jnp.mean(x * x, axis=-1, keepdims=True)
    o_ref[...] = x * lax.rsqrt(ms + NORM_EPS) * fg_ref[...]


def _combine_call(x_mid, y3, gcol, mod, fg):
    s, d = x_mid.shape
    tm = TM_PROJ
    nt = s // tm
    const = lambda j: (0, 0)
    return pl.pallas_call(
        _combine_body,
        grid=(nt,),
        in_specs=[pl.BlockSpec((tm, d), lambda j: (j, 0)),
                  pl.BlockSpec((tm * SUBLANES, LANES), lambda j: (j, 0)),
                  pl.BlockSpec((tm * SUBLANES, LANES), lambda j: (j + nt, 0)),
                  pl.BlockSpec((tm, 2), lambda j: (j, 0)),
                  pl.BlockSpec(mod.shape, const),
                  pl.BlockSpec((1, d), const)],
        out_specs=pl.BlockSpec((tm, d), lambda j: (j, 0)),
        out_shape=jax.ShapeDtypeStruct((s, d), F32),
        compiler_params=_cparams(),
        name="combine",
    )(x_mid, y3, y3, gcol, mod, fg)


def _blockdiag2(wf, wb):
    z = jnp.zeros_like(wf)
    return jnp.concatenate([jnp.concatenate([wf, z], axis=1), jnp.concatenate([z, wb], axis=1)], axis=0)


def _routing_plan(eids, n_tok):
    m = 2 * n_tok
    experts = jnp.arange(N_EXPERTS, dtype=jnp.int32)
    flat_e = eids[:2].reshape(m)
    order = jnp.argsort(flat_e, stable=True).astype(jnp.int32)
    counts = jnp.sum((flat_e[:, None] == experts[None, :]).astype(jnp.int32), axis=0)
    starts = jnp.cumsum(counts) - counts
    pcounts = (counts + MOE_BM - 1) // MOE_BM * MOE_BM
    pends = jnp.cumsum(pcounts)
    pstarts = pends - pcounts
    n_blocks = (m + N_EXPERTS * (MOE_BM - 1) + MOE_BM - 1) // MOE_BM
    slot = jnp.arange(n_blocks * MOE_BM, dtype=jnp.int32)
    slot_e = jnp.minimum(jnp.sum((slot[:, None] >= pends[None, :]).astype(jnp.int32), axis=1), N_EXPERTS - 1)
    sel = (slot_e[:, None] == experts[None, :]).astype(jnp.int32)
    local = slot - jnp.sum(sel * pstarts[None, :], axis=1)
    valid = local < jnp.sum(sel * counts[None, :], axis=1)
    pair = order[jnp.clip(jnp.sum(sel * starts[None, :], axis=1) + local, 0, m - 1)]
    src_tok = jnp.where(valid, jnp.where(pair >= n_tok, pair - n_tok, pair), 0)
    dst_row = jnp.where(valid, pair, 0)
    blk_expert = slot_e[::MOE_BM]
    n_used = (pends[-1:] // MOE_BM).astype(jnp.int32)
    shape3 = (n_blocks, 1, MOE_BM)
    n_valid = jnp.sum(valid.reshape(n_blocks, MOE_BM).astype(jnp.int32), axis=1)
    n_prev = jnp.concatenate([jnp.zeros((1,), jnp.int32), n_valid])
    dst_prev = jnp.concatenate([jnp.zeros((MOE_BM,), jnp.int32), dst_row])
    return ((src_tok * SUBLANES).reshape(shape3), (dst_prev * SUBLANES).reshape(n_blocks + 1, 1, MOE_BM),
            blk_expert, n_used, n_prev)


def kernel(x, c, ctx, c_ctx, norm1_g, w_mod, b_mod, w_in, mu_shift, w0_f, w2_f, w0_b, w2_b, a0_f, a2_f, a0_b, a2_b, g2, k_k, k_a, r_k_f, r_k_b, gn_w, gn_b, w_pool, pool_scale, w_out, norm2_g, w_router_grp, b_router_grp, w_router_exp, b_router_exp, w_gate, w_up, w_down, final_g):
    b, s, d = x.shape
    n_ctx = ctx.shape[1]
    assert b == 1 and c.shape[0] == 1 and w_mod.shape[0] == 1
    assert s % TM_PROJ == 0 and n_ctx % CHUNK == 0 and n_ctx == TM_FEAT and d == 2 * D_RWKV
    x2 = x[0]
    rows = s // GRID_W

    quarter = d // 4
    freq = POS_THETA ** (-jnp.arange(quarter, dtype=F32) / quarter)
    rarg = jnp.arange(rows, dtype=F32)[:, None] * freq
    carg = jnp.arange(GRID_W, dtype=F32)[:, None] * freq
    rowtab = jnp.concatenate([jnp.sin(rarg), jnp.cos(rarg)], axis=-1)
    coltab = jnp.tile(jnp.concatenate([jnp.sin(carg), jnp.cos(carg)], axis=-1), (TM_PROJ // GRID_W, 1))

    cs = jnp.concatenate([c, c_ctx[None, :], jnp.zeros((6, d), F32)], axis=0)
    mod = _mod_call(cs, w_mod[0], b_mod)

    ctx_pad = jnp.concatenate([ctx[0], jnp.zeros((TM_PROJ - n_ctx, d), F32)], axis=0)
    u_shift, p = _inproj_call(x2, ctx_pad, rowtab, coltab, norm1_g, mod, w_in[0].astype(BF16))

    par = jnp.concatenate([w0_f, w0_b, a0_f, a0_b, k_k, k_a, r_k_f, r_k_b], axis=0)
    n_tok = s + n_ctx
    outs = _feat_call(u_shift, n_tok, s // TM_FEAT, mu_shift,
                      par, _blockdiag2(w2_f[0], w2_b[0]).astype(BF16),
                      _blockdiag2(a2_f[0], a2_b[0]).astype(BF16), g2[0].astype(BF16))
    feats, gamf, gamb, bonus, g = outs[:13], outs[13], outs[14], outs[15], outs[16]
    n_chunks = n_tok // CHUNK
    yf, yb = _scan_call(feats, gamf.reshape(n_chunks, 1, D_RWKV), gamb.reshape(n_chunks, 1, D_RWKV),
                        s // CHUNK, n_ctx // CHUNK)

    gn = jnp.concatenate([gn_w, gn_b], axis=0)
    x_mid = _mix_call(x2, rowtab, coltab, yf, yb, bonus, g, p, gn, w_pool[0].astype(BF16), pool_scale,
                      w_out[0].astype(BF16), mod)

    wr = jnp.concatenate([w_router_grp[0].T, w_router_exp[0].T,
                          jnp.zeros((ROUTER_ROWS - N_GROUPS - N_EXPERTS, d), F32)], axis=0)
    br = jnp.concatenate([b_router_grp[0], b_router_exp[0],
                          jnp.zeros((ROUTER_ROWS - N_GROUPS - N_EXPERTS,), F32)])[:, None]
    h3, eids, gates = _router_call(x_mid, norm2_g, mod, wr, br)

    src3, dst_prev3, blk_expert, n_used, n_prev = _routing_plan(eids, s)
    y3 = _experts_call(blk_expert, n_used, n_prev, src3, dst_prev3, h3, w_gate[0], w_up[0], w_down[0], 2 * s)
    out = _combine_call(x_mid, y3, gates[:2].T, mod, final_g[None, :])
    return out[None]
```

```python
import functools

import jax
import jax.numpy as jnp
from jax import lax
from jax.experimental import pallas as pl
from jax.experimental.pallas import tpu as pltpu
from jax.experimental.pallas import tpu_sc as plsc

F32 = jnp.float32
BF16 = jnp.bfloat16

GRID_W = 64
SUBLANES, LANES = 8, 128
HEAD = 64
N_HEADS = 8
D_RWKV = HEAD * N_HEADS
D_POOL = 512
POOL_WINDOWS = (2, 4, 8, 16)
POOL_GW = 128
POOL_HALO = 8
D_SHIFT = 1920
N_GROUPS = 4
EXPERTS_PER_GROUP = 8
N_EXPERTS = 32
D_EXPERT = 512
NORM_EPS = 1e-6
GN_EPS = 64e-5
POS_THETA = 10000.0
DECAY_SCALE = 0.6065306597

CHUNK = 64
QUAD = 4 * HEAD
SCAN_CPS = 4
TM_PROJ = 512
TM_FEAT = 256
MOE_BM = 256
ROUTER_ROWS = 48
V7X_VMEM_LIMIT = 56 * 1024 * 1024
V7X_SC_CORES, V7X_SC_SUBCORES = 2, 16
SC_GATHER_ROWS = 32


def _cparams(n_axes=1, vmem=V7X_VMEM_LIMIT):
    return pltpu.CompilerParams(dimension_semantics=("arbitrary",) * n_axes,
                                vmem_limit_bytes=vmem)


def _dot(a, b):
    return jnp.dot(a, b, preferred_element_type=F32)


def _dot_nt(a, b):
    return lax.dot_general(a, b, (((1,), (1,)), ((), ())), preferred_element_type=F32)


def _dot_tn(a, b):
    return lax.dot_general(a, b, (((0,), (0,)), ((), ())), preferred_element_type=F32)


def _split2(x):
    hi = x.astype(BF16)
    lo = (x - hi.astype(F32)).astype(BF16)
    return hi, lo


def _split3(x):
    hi = x.astype(BF16)
    r1 = x - hi.astype(F32)
    mid = r1.astype(BF16)
    lo = (r1 - mid.astype(F32)).astype(BF16)
    return hi, mid, lo


def _dot3(a, b):
    ah, al = _split2(a)
    bh, bl = _split2(b)
    return _dot(ah, bh) + (_dot(ah, bl) + _dot(al, bh))


def _head_ones():
    r = lax.broadcasted_iota(jnp.int32, (D_RWKV, D_RWKV), 0) >> 6
    c = lax.broadcasted_iota(jnp.int32, (D_RWKV, D_RWKV), 1) >> 6
    return (r == c).astype(BF16)


def _headsum(x, ones):
    hi, lo = _split2(x)
    return _dot(hi, ones) + _dot(lo, ones)


def _mod_body(cs_ref, w_ref, b_ref, o_ref):
    a = cs_ref[...]
    a = a * jax.nn.sigmoid(a)
    o_ref[...] = _dot3(a, w_ref[...]) + b_ref[...]


def _mod_call(cs, w_mod, b_mod):
    d, n = w_mod.shape
    tn = 512
    return pl.pallas_call(
        _mod_body,
        grid=(n // tn,),
        in_specs=[pl.BlockSpec((8, d), lambda j: (0, 0)),
                  pl.BlockSpec((d, tn), lambda j: (0, j)),
                  pl.BlockSpec((1, tn), lambda j: (0, j))],
        out_specs=pl.BlockSpec((8, tn), lambda j: (0, j)),
        out_shape=jax.ShapeDtypeStruct((8, n), F32),
        compiler_params=_cparams(),
        name="mod",
    )(cs, w_mod, b_mod)


def _pos_tile(rowtab_ref, coltab_ref):
    rt = rowtab_ref[...]
    rowpart = jnp.concatenate(
        [jnp.broadcast_to(rt[r:r + 1, :], (GRID_W, rt.shape[1])) for r in range(TM_PROJ // GRID_W)], axis=0)
    return jnp.concatenate([rowpart, coltab_ref[...]], axis=1)


def _inproj_body(x_ref, ctx_ref, rowtab_ref, coltab_ref, g1_ref, mod_ref, w_ref, us_ref, p_ref, *, n_x_tiles):
    d = x_ref.shape[1]
    is_ctx = pl.program_id(0) == n_x_tiles
    xin = jnp.where(is_ctx, ctx_ref[...], x_ref[...] + _pos_tile(rowtab_ref, coltab_ref))
    ms = jnp.mean(xin * xin, axis=-1, keepdims=True)
    xn = xin * lax.rsqrt(ms + NORM_EPS) * g1_ref[...]
    sh = jnp.where(is_ctx, mod_ref[1:2, 0:d], mod_ref[0:1, 0:d])
    sc = jnp.where(is_ctx, mod_ref[1:2, d:2 * d], mod_ref[0:1, d:2 * d])
    h = xn * (1.0 + sc) + sh
    u = _dot(h.astype(BF16), w_ref[...])
    us_ref[...] = u[:, :D_SHIFT]
    p_ref[...] = u[:, D_SHIFT:]


def _inproj_call(x2, ctx_pad, rowtab, coltab, g1, mod, w_in_bf):
    s, d = x2.shape
    nx = s // TM_PROJ
    d_in = w_in_bf.shape[1]
    rows = (nx + 1) * TM_PROJ
    xmap = lambda i: (jnp.minimum(i, nx - 1), 0)
    const = lambda i: (0, 0)
    return pl.pallas_call(
        functools.partial(_inproj_body, n_x_tiles=nx),
        grid=(nx + 1,),
        in_specs=[pl.BlockSpec((TM_PROJ, d), xmap),
                  pl.BlockSpec((TM_PROJ, d), const),
                  pl.BlockSpec((TM_PROJ // GRID_W, d // 2), xmap),
                  pl.BlockSpec((TM_PROJ, d // 2), const),
                  pl.BlockSpec((1, d), const),
                  pl.BlockSpec(mod.shape, const),
                  pl.BlockSpec((d, d_in), const)],
        out_specs=[pl.BlockSpec((TM_PROJ, D_SHIFT), lambda i: (i, 0)),
                   pl.BlockSpec((TM_PROJ, D_POOL), lambda i: (i, 0))],
        out_shape=[jax.ShapeDtypeStruct((rows, D_SHIFT), F32),
                   jax.ShapeDtypeStruct((rows, D_POOL), F32)],
        compiler_params=_cparams(),
        name="inproj",
    )(x2, ctx_pad, rowtab, coltab, g1, mod, w_in_bf)


def _feat_body(um_ref, up_ref, un_ref, mu_ref, par_ref, w2_ref, a2_ref, g2_ref,
               af_ref, rf_ref, btf_ref, ktf_ref, bhf_ref, khf_ref,
               ab_ref, rb_ref, btb_ref, ktb_ref, bhb_ref, khb_ref,
               v_ref, gamf_ref, gamb_ref, bonus_ref, g_ref, *, n_x_tiles):
    i = pl.program_id(0)
    tm = um_ref.shape[0]
    u = um_ref[...]
    prev_ok = jnp.logical_and(i != 0, i != n_x_tiles)
    next_ok = jnp.logical_and(i != n_x_tiles - 1, i != n_x_tiles)
    prow = jnp.where(prev_ok, up_ref[POOL_HALO - 1:POOL_HALO, :], 0.0)
    nrow = jnp.where(next_ok, un_ref[0:1, :], 0.0)
    rid = lax.broadcasted_iota(jnp.int32, (tm, 1), 0)
    up = jnp.where(rid == 0, prow, pltpu.roll(u, 1, 0))
    dn = jnp.where(rid == tm - 1, nrow, pltpu.roll(u, tm - 1, 0))
    s = u + mu_ref[...] * (0.5 * (up + dn) - u)

    w0f, w0b, a0f, a0b = par_ref[0:1, :], par_ref[1:2, :], par_ref[2:3, :], par_ref[3:4, :]
    k_k, k_a, rkf, rkb = par_ref[4:5, :], par_ref[5:6, :], par_ref[6:7, :], par_ref[7:8, :]

    r = s[:, 0:512]
    k = s[:, 512:1024]
    v = s[:, 1024:1536]
    zw = _dot(jnp.tanh(s[:, 1536:1664]).astype(BF16), w2_ref[...])
    za = _dot(s[:, 1664:1792].astype(BF16), a2_ref[...])
    g = _dot(jax.nn.sigmoid(s[:, 1792:1920]).astype(BF16), g2_ref[...])
    lw_f = -DECAY_SCALE * jax.nn.sigmoid(w0f + zw[:, :512])
    lw_b = -DECAY_SCALE * jax.nn.sigmoid(w0b + zw[:, 512:])
    a_f = jax.nn.sigmoid(a0f + za[:, :512])
    a_b = jax.nn.sigmoid(a0b + za[:, 512:])

    ones = _head_ones()
    kkr = k * k_k
    nrm = jnp.sqrt(_headsum(kkr * kkr, ones))
    kk = kkr / jnp.maximum(nrm, 1e-12)
    k_f = k * (1.0 + (a_f - 1.0) * k_a)
    k_b = k * (1.0 + (a_b - 1.0) * k_a)
    bonus_ref[...] = _headsum(r * (k_f * rkf + k_b * rkb), ones) * v
    g_ref[...] = g
    v_ref[...] = v.astype(BF16)

    rr = lax.broadcasted_iota(jnp.int32, (tm, tm), 0)
    cc = lax.broadcasted_iota(jnp.int32, (tm, tm), 1)
    same = (rr >> 6) == (cc >> 6)
    tri = jnp.concatenate([jnp.logical_and(same, cc < rr).astype(BF16),
                           jnp.logical_and(same, cc > rr).astype(BF16)], axis=0)
    lw = jnp.concatenate([lw_f, lw_b], axis=1)
    l1, l2, l3 = _split3(lw)
    sums = _dot(tri, l1) + (_dot(tri, l2) + _dot(tri, l3))
    before_f, after_f = sums[:tm, :512], sums[tm:, :512]
    before_b, after_b = sums[:tm, 512:], sums[tm:, 512:]

    def emit(excl, rest, lwd, a_d, k_d, a_ref, r_ref, bt_ref, kt_ref, bh_ref, kh_ref, gam_ref):
        cum = excl + lwd
        e_neg = jnp.exp(-cum)
        e_rest = jnp.exp(rest)
        kka = kk * a_d
        a_ref[...] = (-kk * jnp.exp(excl)).astype(BF16)
        r_ref[...] = (r * jnp.exp(cum)).astype(BF16)
        bt_ref[...] = (kka * e_neg).astype(BF16)
        kt_ref[...] = (k_d * e_neg).astype(BF16)
        bh_ref[...] = (kka * e_rest).astype(BF16)
        kh_ref[...] = (k_d * e_rest).astype(BF16)
        tot = cum + rest
        gam_ref[0] = jnp.exp(jnp.concatenate(
            [tot[c * CHUNK:c * CHUNK + 1, :] for c in range(tm // CHUNK)], axis=0))

    emit(before_f, after_f, lw_f, a_f, k_f, af_ref, rf_ref, btf_ref, ktf_ref, bhf_ref, khf_ref, gamf_ref)
    emit(after_b, before_b, lw_b, a_b, k_b, ab_ref, rb_ref, btb_ref, ktb_ref, bhb_ref, khb_ref, gamb_ref)


def _feat_call(u_shift, n_tok, n_x_tiles, mu, par, w2, a2, g2):
    tm = TM_FEAT
    nt = n_tok // tm
    hb = tm // POOL_HALO
    last_blk = u_shift.shape[0] // POOL_HALO - 1
    const = lambda i: (0, 0)
    tok = lambda i: (i, 0)
    seq_bf = jax.ShapeDtypeStruct((n_tok, D_RWKV), BF16)
    seq_f32 = jax.ShapeDtypeStruct((n_tok, D_RWKV), F32)
    gam = jax.ShapeDtypeStruct((nt, tm // CHUNK, D_RWKV), F32)
    seq_spec = pl.BlockSpec((tm, D_RWKV), tok)
    gam_spec = pl.BlockSpec((1, tm // CHUNK, D_RWKV), lambda i: (i, 0, 0))
    return pl.pallas_call(
        functools.partial(_feat_body, n_x_tiles=n_x_tiles),
        grid=(nt,),
        in_specs=[pl.BlockSpec((tm, D_SHIFT), tok),
                  pl.BlockSpec((POOL_HALO, D_SHIFT), lambda i: (jnp.maximum(i * hb - 1, 0), 0)),
                  pl.BlockSpec((POOL_HALO, D_SHIFT), lambda i: (jnp.minimum((i + 1) * hb, last_blk), 0)),
                  pl.BlockSpec((1, D_SHIFT), const),
                  pl.BlockSpec((8, D_RWKV), const),
                  pl.BlockSpec(w2.shape, const),
                  pl.BlockSpec(a2.shape, const),
                  pl.BlockSpec(g2.shape, const)],
        out_specs=[seq_spec] * 13 + [gam_spec, gam_spec, seq_spec, seq_spec],
        out_shape=[seq_bf] * 13 + [gam, gam, seq_f32, seq_f32],
        compiler_params=_cparams(),
        name="feat",
    )(u_shift, u_shift, u_shift, mu, par, w2, a2, g2)


def _blockdiag(x, head_of_lane):
    xb = x.astype(BF16)
    zero = jnp.zeros_like(xb)
    return jnp.concatenate([jnp.where(head_of_lane == h, xb, zero) for h in range(QUAD // HEAD)], axis=0)


def _unblock(x, head_of_lane):
    out = jnp.where(head_of_lane == 0, x[0:HEAD, :], 0.0)
    for h in range(1, QUAD // HEAD):
        out = jnp.where(head_of_lane == h, x[h * HEAD:(h + 1) * HEAD, :], out)
    return out


def _scan_chunks(chains):
    lane = lax.broadcasted_iota(jnp.int32, (HEAD, QUAD), 1)
    head_of_lane = lane >> 6
    s_idx = lane & (HEAD - 1)
    t_idx = lax.broadcasted_iota(jnp.int32, (HEAD, QUAD), 0)
    eye = s_idx == t_idx
    lower = s_idx < t_idx
    upper = s_idx > t_idx
    bd = functools.partial(_blockdiag, head_of_lane=head_of_lane)

    def same_block(shift):
        return (s_idx >> shift) == (t_idx >> shift)

    def each(fn):
        return [fn(c) for c in chains]

    def stage_masks(c):
        ll = _dot_nt(jnp.concatenate([c["a"], c["r"]], axis=0),
                     jnp.concatenate([bd(c["bt"]), bd(c["kt"])], axis=0))
        strict = upper if c["reverse"] else lower
        incl = jnp.logical_or(strict, eye)
        c["lab"] = jnp.where(strict, ll[:HEAD, :QUAD], 0.0)
        c["lak"] = jnp.where(strict, ll[:HEAD, QUAD:], 0.0)
        c["mrb"] = jnp.where(incl, ll[HEAD:, :QUAD], 0.0)
        c["mrk"] = jnp.where(incl, ll[HEAD:, QUAD:], 0.0)

    each(stage_masks)

    def stage_square(c):
        ld = jnp.where(same_block(4), c["lab"], 0.0)
        c["tm"] = jnp.where(eye, 1.0, 0.0) + ld
        c["xk"] = _dot(ld.astype(BF16), bd(ld))

    def stage_double(c):
        y = _dot(jnp.concatenate([c["tm"], c["xk"]], axis=0).astype(BF16), bd(c["xk"]))
        c["tm"] = c["tm"] + y[:HEAD]
        c["xk"] = y[HEAD:]

    def stage_last_power(c):
        c["tm"] = c["tm"] + _dot(c["tm"].astype(BF16), bd(c["xk"]))

    each(stage_square)
    each(stage_double)
    each(stage_double)
    each(stage_last_power)
    each(lambda c: c.update(vv=_dot(jnp.concatenate([c["lak"], c["mrk"]], axis=0).astype(BF16), bd(c["v"]))))
    for shift in (4, 5):
        joined = jnp.logical_and(same_block(shift + 1), jnp.logical_not(same_block(shift)))
        each(lambda c: c.update(te=_dot(c["tm"].astype(BF16), bd(jnp.where(joined, c["lab"], 0.0)))))
        each(lambda c: c.update(tm=c["tm"] + _dot(c["te"].astype(BF16), bd(c["tm"]))))

    each(lambda c: c.update(x1=_dot(c["tm"].astype(BF16),
                                    jnp.concatenate([bd(c["a"]), bd(c["vv"][:HEAD])], axis=1))))
    each(lambda c: c.update(x2=_dot(c["mrb"].astype(BF16),
                                    jnp.concatenate([bd(c["x1"][:, :QUAD]), bd(c["x1"][:, QUAD:])], axis=1))))

    def stage_state_terms(c):
        rhs = jnp.concatenate(
            [c["x1"].astype(BF16), jnp.concatenate([jnp.zeros((HEAD, QUAD), BF16), c["v"]], axis=1)], axis=0)
        pz = _dot_tn(jnp.concatenate([c["bh"], c["kh"]], axis=0), rhs)
        c["pq"] = _unblock(pz[:, :QUAD], head_of_lane) + jnp.where(eye, c["gam"], 0.0)
        c["zq"] = _unblock(pz[:, QUAD:], head_of_lane)
        c["qeff"] = c["r"].astype(F32) + c["x2"][:, :QUAD]
        c["yin"] = c["x2"][:, QUAD:] + c["vv"][HEAD:]

    each(stage_state_terms)

    outs, state = [], {}
    for c in chains:
        hst = state.get(c["key"], c["hst"])
        qp = _dot(jnp.concatenate([c["qeff"], c["pq"]], axis=0).astype(BF16), bd(hst))
        state[c["key"]] = qp[HEAD:] + c["zq"]
        outs.append(qp[:HEAD] + c["yin"])
    return outs, state


def _scan_body(af, rf, btf, ktf, bhf, khf, vf, gamf, ab, rb, btb, ktb, bhb, khb, vb, gamb,
               yf_ref, yb_ref, hf_ref, hb_ref):
    @pl.when(pl.program_id(0) == 0)
    def _():
        hf_ref[...] = jnp.zeros_like(hf_ref)
        hb_ref[...] = jnp.zeros_like(hb_ref)

    names = ("a", "r", "bt", "kt", "bh", "kh", "v")
    chains, sinks, state_refs = [], [], {}
    for ci in range(SCAN_CPS):
        for refs, y_ref, h_ref, reverse in (((af, rf, btf, ktf, bhf, khf, vf, gamf), yf_ref, hf_ref, False),
                                            ((ab, rb, btb, ktb, bhb, khb, vb, gamb), yb_ref, hb_ref, True)):
            chunk = SCAN_CPS - 1 - ci if reverse else ci
            rows = slice(chunk * CHUNK, (chunk + 1) * CHUNK)
            for q in range(D_RWKV // QUAD):
                sl = slice(q * QUAD, (q + 1) * QUAD)
                chain = {n: ref[rows, sl] for n, ref in zip(names, refs[:7])}
                chain.update(gam=refs[7][chunk, :, sl], hst=h_ref[:, sl], reverse=reverse, key=(reverse, q))
                chains.append(chain)
                sinks.append((y_ref, rows, sl))
                state_refs[(reverse, q)] = (h_ref, sl)
    outs, state = _scan_chunks(chains)
    for y, (y_ref, rows, sl) in zip(outs, sinks):
        y_ref[rows, sl] = y
    for key, (h_ref, sl) in state_refs.items():
        h_ref[:, sl] = state[key]


def _scan_call(feats, gamf, gamb, n_x_chunks, n_c_chunks):
    af, rf, btf, ktf, bhf, khf, ab, rb, btb, ktb, bhb, khb, v = feats
    assert n_x_chunks % SCAN_CPS == 0 and n_c_chunks % SCAN_CPS == 0
    nxb, ncb = n_x_chunks // SCAN_CPS, n_c_chunks // SCAN_CPS
    n = nxb + ncb
    fwd = lambda i: (jnp.where(i < ncb, nxb + i, i - ncb), 0)
    bwd = lambda i: (n - 1 - i, 0)
    fwd3 = lambda i: (jnp.where(i < ncb, nxb + i, i - ncb), 0, 0)
    bwd3 = lambda i: (n - 1 - i, 0, 0)
    blk = (SCAN_CPS * CHUNK, D_RWKV)
    in_specs = ([pl.BlockSpec(blk, fwd)] * 7 + [pl.BlockSpec((SCAN_CPS, 1, D_RWKV), fwd3)]
                + [pl.BlockSpec(blk, bwd)] * 7 + [pl.BlockSpec((SCAN_CPS, 1, D_RWKV), bwd3)])
    y = jax.ShapeDtypeStruct((n * SCAN_CPS * CHUNK, D_RWKV), F32)
    return pl.pallas_call(
        _scan_body,
        grid=(n,),
        in_specs=in_specs,
        out_specs=[pl.BlockSpec(blk, fwd), pl.BlockSpec(blk, bwd)],
        out_shape=[y, y],
        scratch_shapes=[pltpu.VMEM((HEAD, D_RWKV), F32), pltpu.VMEM((HEAD, D_RWKV), F32)],
        compiler_params=_cparams(),
        name="scan",
    )(af, rf, btf, ktf, bhf, khf, v, gamf, ab, rb, btb, ktb, bhb, khb, v, gamb)


def _mix_body(x_ref, rowtab_ref, coltab_ref, yf_ref, yb_ref, bonus_ref, g_ref, pm_ref, pp_ref, pn_ref,
              gn_ref, wpool_ref, pscale_ref, wout_ref, mod_ref, o_ref, slab_ref, *, n_x_tiles, seq_len):
    j = pl.program_id(0)
    tm, d = x_ref.shape
    hl = POOL_HALO
    slab_ref[0:hl, :] = jnp.where(j != 0, pp_ref[...], 0.0)
    slab_ref[hl:hl + tm, :] = pm_ref[...]
    slab_ref[hl + tm:2 * hl + tm, :] = jnp.where(j != n_x_tiles - 1, pn_ref[...], 0.0)
    tglob = j * tm + lax.broadcasted_iota(jnp.int32, (tm, 1), 0)
    pooled = []
    for gi, win in enumerate(POOL_WINDOWS):
        cols = slice(gi * POOL_GW, (gi + 1) * POOL_GW)
        acc = slab_ref[hl - win // 2:hl - win // 2 + tm, cols]
        for off in range(-win // 2 + 1, win // 2):
            acc = acc + slab_ref[hl + off:hl + off + tm, cols]
        lo = jnp.maximum(tglob - win // 2, 0)
        hi = jnp.minimum(tglob - win // 2 + win, seq_len)
        cnt = (hi - lo).astype(F32)
        pg = acc / cnt - slab_ref[hl:hl + tm, cols]
        pooled.append(_dot(pg.astype(BF16), wpool_ref[gi]))
    pool = jnp.concatenate(pooled, axis=1) * pscale_ref[...]

    ones = _head_ones()
    y = yf_ref[...] + yb_ref[...]
    yc = y - _headsum(y, ones) * (1.0 / HEAD)
    var = _headsum(yc * yc, ones) * (1.0 / HEAD)
    rw = (yc * lax.rsqrt(var + GN_EPS) * gn_ref[0:1, :] + gn_ref[1:2, :] + bonus_ref[...]) * g_ref[...]
    mix = _dot(jnp.concatenate([rw, pool], axis=1).astype(BF16), wout_ref[...])
    gt_a = mod_ref[0:1, 2 * d:3 * d]
    o_ref[...] = x_ref[...] + _pos_tile(rowtab_ref, coltab_ref) + gt_a * mix


def _mix_call(x2, rowtab, coltab, yf, yb, bonus, g, p, gn, w_pool_bf, pscale, w_out_bf, mod):
    s, d = x2.shape
    tm = TM_PROJ
    nx = s // tm
    hb = tm // POOL_HALO
    const = lambda j: (0, 0)
    tok = lambda j: (j, 0)
    seq_spec = pl.BlockSpec((tm, D_RWKV), tok)
    return pl.pallas_call(
        functools.partial(_mix_body, n_x_tiles=nx, seq_len=s),
        grid=(nx,),
        in_specs=[pl.BlockSpec((tm, d), tok),
                  pl.BlockSpec((tm // GRID_W, d // 2), tok),
                  pl.BlockSpec((tm, d // 2), const),
                  seq_spec, seq_spec, seq_spec, seq_spec, seq_spec,
                  pl.BlockSpec((POOL_HALO, D_POOL), lambda j: (jnp.maximum(j * hb - 1, 0), 0)),
                  pl.BlockSpec((POOL_HALO, D_POOL), lambda j: ((j + 1) * hb, 0)),
                  pl.BlockSpec((2, D_RWKV), const),
                  pl.BlockSpec(w_pool_bf.shape, lambda j: (0, 0, 0)),
                  pl.BlockSpec((1, D_POOL), const),
                  pl.BlockSpec(w_out_bf.shape, const),
                  pl.BlockSpec(mod.shape, const)],
        out_specs=pl.BlockSpec((tm, d), tok),
        out_shape=jax.ShapeDtypeStruct((s, d), F32),
        scratch_shapes=[pltpu.VMEM((tm + 2 * POOL_HALO, D_POOL), F32)],
        compiler_params=_cparams(),
        name="mix",
    )(x2, rowtab, coltab, yf, yb, bonus, g, p, p, p, gn, w_pool_bf, pscale, w_out_bf, mod)


def _to_token_tiles(ref, x):
    tm = x.shape[0]
    for j in range(x.shape[1] // LANES):
        ref[pl.ds(j, tm, stride=SUBLANES), :] = x[:, j * LANES:(j + 1) * LANES]


def _from_token_tiles(ref, tm):
    return jnp.concatenate([ref[pl.ds(j, tm, stride=SUBLANES), :] for j in range(SUBLANES)], axis=1)


def _router_body(x_ref, g2_ref, mod_ref, wr_ref, br_ref, h3_ref, eid_ref, gate_ref):
    tm, d = x_ref.shape
    x = x_ref[...]
    ms = jnp.mean(x * x, axis=-1, keepdims=True)
    xn = x * lax.rsqrt(ms + NORM_EPS) * g2_ref[...]
    h = xn * (1.0 + mod_ref[0:1, 4 * d:5 * d]) + mod_ref[0:1, 3 * d:4 * d]
    _to_token_tiles(h3_ref, h)

    wh, wl = _split2(wr_ref[...])
    hh, hl = _split2(h)
    logits = _dot_nt(wh, hh) + (_dot_nt(wh, hl) + _dot_nt(wl, hh)) + br_ref[...]
    gl = logits[0:N_GROUPS, :]
    gmax = jnp.max(gl, axis=0, keepdims=True)
    gidx = lax.broadcasted_iota(jnp.int32, gl.shape, 0).astype(F32)
    grp = jnp.min(jnp.where(gl == gmax, gidx, float(N_GROUPS)), axis=0, keepdims=True)
    p_grp = 1.0 / jnp.sum(jnp.exp(gl - gmax), axis=0, keepdims=True)
    sel = jnp.zeros((EXPERTS_PER_GROUP, tm), F32)
    for gi in range(N_GROUPS):
        lo = N_GROUPS + gi * EXPERTS_PER_GROUP
        sel = jnp.where(grp == float(gi), logits[lo:lo + EXPERTS_PER_GROUP, :], sel)
    eidx = lax.broadcasted_iota(jnp.int32, sel.shape, 0).astype(F32)
    top1 = jnp.max(sel, axis=0, keepdims=True)
    i1 = jnp.min(jnp.where(sel == top1, eidx, float(EXPERTS_PER_GROUP)), axis=0, keepdims=True)
    sel2 = jnp.where(eidx == i1, -jnp.inf, sel)
    top2 = jnp.max(sel2, axis=0, keepdims=True)
    i2 = jnp.min(jnp.where(sel2 == top2, eidx, float(EXPERTS_PER_GROUP)), axis=0, keepdims=True)
    e2 = jnp.exp(top2 - top1)
    inv = 1.0 / (1.0 + e2)
    zf = jnp.zeros((6, tm), F32)
    gate_ref[...] = jnp.concatenate([p_grp * inv, p_grp * (e2 * inv), zf], axis=0)
    base = grp * float(EXPERTS_PER_GROUP)
    eid_ref[...] = jnp.concatenate([base + i1, base + i2, zf], axis=0).astype(jnp.int32)


def _router_call(x_mid, g2n, mod, wr, br):
    s, d = x_mid.shape
    tm = TM_PROJ
    const = lambda j: (0, 0)
    return pl.pallas_call(
        _router_body,
        grid=(s // tm,),
        in_specs=[pl.BlockSpec((tm, d), lambda j: (j, 0)),
                  pl.BlockSpec((1, d), const),
                  pl.BlockSpec(mod.shape, const),
                  pl.BlockSpec(wr.shape, const),
                  pl.BlockSpec(br.shape, const)],
        out_specs=[pl.BlockSpec((tm * SUBLANES, LANES), lambda j: (j, 0)),
                   pl.BlockSpec((8, tm), lambda j: (0, j)),
                   pl.BlockSpec((8, tm), lambda j: (0, j))],
        out_shape=[jax.ShapeDtypeStruct((s * SUBLANES, LANES), F32),
                   jax.ShapeDtypeStruct((8, s), jnp.int32),
                   jax.ShapeDtypeStruct((8, s), F32)],
        compiler_params=_cparams(),
        name="router",
    )(x_mid, g2n, mod, wr, br)


def _sc_gather_tiles(table3, idx):
    n_out = idx.shape[0]
    n_workers = V7X_SC_CORES * V7X_SC_SUBCORES
    per_worker = n_out // n_workers
    n_chunks = per_worker // SC_GATHER_ROWS
    assert per_worker * n_workers == n_out and n_chunks * SC_GATHER_ROWS == per_worker
    mesh = plsc.VectorSubcoreMesh(core_axis_name="c", subcore_axis_name="s",
                                  num_cores=V7X_SC_CORES, num_subcores=V7X_SC_SUBCORES)

    def body(table_hbm, idx_hbm, out_hbm, idx_v, rows_v, sem):
        worker = lax.axis_index("s") * V7X_SC_CORES + lax.axis_index("c")
        base = worker * per_worker
        pltpu.sync_copy(idx_hbm.at[pl.ds(base, per_worker)], idx_v)

        @pl.loop(0, n_chunks)
        def _(j):
            off = pl.multiple_of(j * SC_GATHER_ROWS, SC_GATHER_ROWS)
            pltpu.async_copy(table_hbm.at[idx_v.at[pl.ds(off, SC_GATHER_ROWS)]], rows_v, sem).wait()
            pltpu.sync_copy(rows_v, out_hbm.at[pl.ds(base + off, SC_GATHER_ROWS)])

    return pl.kernel(
        body,
        out_type=jax.ShapeDtypeStruct((n_out,) + table3.shape[1:], table3.dtype),
        mesh=mesh,
        scratch_types=[pltpu.VMEM((per_worker,), jnp.int32),
                       pltpu.VMEM((SC_GATHER_ROWS,) + table3.shape[1:], table3.dtype),
                       pltpu.SemaphoreType.DMA],
        name="sc_gather",
    )(table3, idx)


def _experts_body(blk_e_ref, n_used_ref, xs_ref, wg_ref, wu_ref, wd_ref, y_ref):
    del blk_e_ref

    @pl.when(pl.program_id(0) < n_used_ref[0])
    def _():
        xb = _from_token_tiles(xs_ref, MOE_BM).astype(BF16)
        gate = _dot(xb, wg_ref[0].astype(BF16))
        upp = _dot(xb, wu_ref[0].astype(BF16))
        hid = (gate * jax.nn.sigmoid(gate)) * upp
        _to_token_tiles(y_ref, _dot(hid.astype(BF16), wd_ref[0].astype(BF16)))

    @pl.when(pl.program_id(0) >= n_used_ref[0])
    def _():
        y_ref[...] = jnp.zeros_like(y_ref)


def _experts_call(blk_expert, n_used, xs3, wg, wu, wd):
    nb = blk_expert.shape[0]
    d = wg.shape[1]
    rows = MOE_BM * SUBLANES
    wmap = lambda i, be, nu: (be[i], 0, 0)
    grid_spec = pltpu.PrefetchScalarGridSpec(
        num_scalar_prefetch=2,
        grid=(nb,),
        in_specs=[pl.BlockSpec((rows, LANES), lambda i, be, nu: (i, 0)),
                  pl.BlockSpec((1, d, D_EXPERT), wmap),
                  pl.BlockSpec((1, d, D_EXPERT), wmap),
                  pl.BlockSpec((1, D_EXPERT, d), wmap)],
        out_specs=pl.BlockSpec((rows, LANES), lambda i, be, nu: (i, 0)),
    )
    return pl.pallas_call(
        _experts_body,
        grid_spec=grid_spec,
        out_shape=jax.ShapeDtypeStruct((nb * rows, LANES), F32),
        compiler_params=_cparams(),
        name="experts",
    )(blk_expert, n_used, xs3, wg, wu, wd)


def _combine_body(x_ref, y0_ref, y1_ref, gate_ref, mod_ref, fg_ref, o_ref):
    tm, d = x_ref.shape
    y = gate_ref[:, 0:1] * _from_token_tiles(y0_ref, tm) + gate_ref[:, 1:2] * _from_token_tiles(y1_ref, tm)
    x = x_ref[...] + mod_ref[0:1, 5 * d:6 * d] * y
    ms = jnp.mean(x * x, axis=-1, keepdims=True)
    o_ref[...] = x * lax.rsqrt(ms + NORM_EPS) * fg_ref[...]


def _combine_call(x_mid, y3, gcol, mod, fg):
    s, d = x_mid.shape
    tm = TM_PROJ
    nt = s // tm
    const = lambda j: (0, 0)
    return pl.pallas_call(
        _combine_body,
        grid=(nt,),
        in_specs=[pl.BlockSpec((tm, d), lambda j: (j, 0)),
                  pl.BlockSpec((tm * SUBLANES, LANES), lambda j: (j, 0)),
                  pl.BlockSpec((tm * SUBLANES, LANES), lambda j: (j + nt, 0)),
                  pl.BlockSpec((tm, 2), lambda j: (j, 0)),
                  pl.BlockSpec(mod.shape, const),
                  pl.BlockSpec((1, d), const)],
        out_specs=pl.BlockSpec((tm, d), lambda j: (j, 0)),
        out_shape=jax.ShapeDtypeStruct((s, d), F32),
        compiler_params=_cparams(),
        name="combine",
    )(x_mid, y3, y3, gcol, mod, fg)


def _blockdiag2(wf, wb):
    z = jnp.zeros_like(wf)
    return jnp.concatenate([jnp.concatenate([wf, z], axis=1), jnp.concatenate([z, wb], axis=1)], axis=0)


def _routing_plan(eids, n_tok):
    m = 2 * n_tok
    experts = jnp.arange(N_EXPERTS, dtype=jnp.int32)
    flat_e = eids[:2].reshape(m)
    order = jnp.argsort(flat_e, stable=True).astype(jnp.int32)
    onehot = (flat_e[:, None] == experts[None, :]).astype(jnp.int32)
    csum = jnp.cumsum(onehot, axis=0)
    counts = csum[-1]
    starts = jnp.cumsum(counts) - counts
    pcounts = (counts + MOE_BM - 1) // MOE_BM * MOE_BM
    pends = jnp.cumsum(pcounts)
    pstarts = pends - pcounts
    n_blocks = (m + N_EXPERTS * (MOE_BM - 1) + MOE_BM - 1) // MOE_BM
    slot = jnp.arange(n_blocks * MOE_BM, dtype=jnp.int32)
    slot_e = jnp.minimum(jnp.sum((slot[:, None] >= pends[None, :]).astype(jnp.int32), axis=1), N_EXPERTS - 1)
    sel = (slot_e[:, None] == experts[None, :]).astype(jnp.int32)
    local = slot - jnp.sum(sel * pstarts[None, :], axis=1)
    valid = local < jnp.sum(sel * counts[None, :], axis=1)
    pair = order[jnp.clip(jnp.sum(sel * starts[None, :], axis=1) + local, 0, m - 1)]
    src_tok = jnp.where(valid, jnp.where(pair >= n_tok, pair - n_tok, pair), 0)
    dest = jnp.sum(onehot * (csum - 1 + pstarts[None, :]), axis=1).astype(jnp.int32)
    blk_expert = slot_e[::MOE_BM]
    n_used = (pends[-1:] // MOE_BM).astype(jnp.int32)
    return src_tok, dest, blk_expert, n_used


def kernel(x, c, ctx, c_ctx, norm1_g, w_mod, b_mod, w_in, mu_shift, w0_f, w2_f, w0_b, w2_b, a0_f, a2_f, a0_b, a2_b, g2, k_k, k_a, r_k_f, r_k_b, gn_w, gn_b, w_pool, pool_scale, w_out, norm2_g, w_router_grp, b_router_grp, w_router_exp, b_router_exp, w_gate, w_up, w_down, final_g):
    b, s, d = x.shape
    n_ctx = ctx.shape[1]
    assert b == 1 and c.shape[0] == 1 and w_mod.shape[0] == 1
    assert s % TM_PROJ == 0 and n_ctx % CHUNK == 0 and n_ctx == TM_FEAT and d == 2 * D_RWKV
    x2 = x[0]
    rows = s // GRID_W

    quarter = d // 4
    freq = POS_THETA ** (-jnp.arange(quarter, dtype=F32) / quarter)
    rarg = jnp.arange(rows, dtype=F32)[:, None] * freq
    carg = jnp.arange(GRID_W, dtype=F32)[:, None] * freq
    rowtab = jnp.concatenate([jnp.sin(rarg), jnp.cos(rarg)], axis=-1)
    coltab = jnp.tile(jnp.concatenate([jnp.sin(carg), jnp.cos(carg)], axis=-1), (TM_PROJ // GRID_W, 1))

    cs = jnp.concatenate([c, c_ctx[None, :], jnp.zeros((6, d), F32)], axis=0)
    mod = _mod_call(cs, w_mod[0], b_mod)

    ctx_pad = jnp.concatenate([ctx[0], jnp.zeros((TM_PROJ - n_ctx, d), F32)], axis=0)
    u_shift, p = _inproj_call(x2, ctx_pad, rowtab, coltab, norm1_g, mod, w_in[0].astype(BF16))

    par = jnp.concatenate([w0_f, w0_b, a0_f, a0_b, k_k, k_a, r_k_f, r_k_b], axis=0)
    n_tok = s + n_ctx
    outs = _feat_call(u_shift, n_tok, s // TM_FEAT, mu_shift,
                      par, _blockdiag2(w2_f[0], w2_b[0]).astype(BF16),
                      _blockdiag2(a2_f[0], a2_b[0]).astype(BF16), g2[0].astype(BF16))
    feats, gamf, gamb, bonus, g = outs[:13], outs[13], outs[14], outs[15], outs[16]
    n_chunks = n_tok // CHUNK
    yf, yb = _scan_call(feats, gamf.reshape(n_chunks, 1, D_RWKV), gamb.reshape(n_chunks, 1, D_RWKV),
                        s // CHUNK, n_ctx // CHUNK)

    gn = jnp.concatenate([gn_w, gn_b], axis=0)
    x_mid = _mix_call(x2, rowtab, coltab, yf, yb, bonus, g, p, gn, w_pool[0].astype(BF16), pool_scale,
                      w_out[0].astype(BF16), mod)

    wr = jnp.concatenate([w_router_grp[0].T, w_router_exp[0].T,
                          jnp.zeros((ROUTER_ROWS - N_GROUPS - N_EXPERTS, d), F32)], axis=0)
    br = jnp.concatenate([b_router_grp[0], b_router_exp[0],
                          jnp.zeros((ROUTER_ROWS - N_GROUPS - N_EXPERTS,), F32)])[:, None]
    h3, eids, gates = _router_call(x_mid, norm2_g, mod, wr, br)

    src_tok, dest, blk_expert, n_used = _routing_plan(eids, s)
    tile = (SUBLANES, LANES)
    xs3 = _sc_gather_tiles(h3.reshape((s,) + tile), src_tok)
    yexp = _experts_call(blk_expert, n_used, xs3.reshape(-1, LANES), w_gate[0], w_up[0], w_down[0])
    y3 = _sc_gather_tiles(yexp.reshape((-1,) + tile), dest).reshape(-1, LANES)
    out = _combine_call(x_mid, y3, gates[:2].T, mod, final_g[None, :])
    return out[None]
```

```python
import functools

import jax
import jax.numpy as jnp
from jax import lax
from jax.experimental import pallas as pl
from jax.experimental.pallas import tpu as pltpu
from jax.experimental.pallas import tpu_sc as plsc

F32 = jnp.float32
BF16 = jnp.bfloat16

GRID_W = 64
SUBLANES, LANES = 8, 128
HEAD = 64
N_HEADS = 8
D_RWKV = HEAD * N_HEADS
D_POOL = 512
POOL_WINDOWS = (2, 4, 8, 16)
POOL_GW = 128
POOL_HALO = 8
D_SHIFT = 1920
N_GROUPS = 4
EXPERTS_PER_GROUP = 8
N_EXPERTS = 32
D_EXPERT = 512
NORM_EPS = 1e-6
GN_EPS = 64e-5
POS_THETA = 10000.0
DECAY_SCALE = 0.6065306597

CHUNK = 64
QUAD = 4 * HEAD
SCAN_CPS = 4
TM_PROJ = 512
TM_FEAT = 256
MOE_BM = 256
ROUTER_ROWS = 48
V7X_VMEM_LIMIT = 56 * 1024 * 1024
V7X_SC_CORES, V7X_SC_SUBCORES = 2, 16
SC_GATHER_ROWS = 32


def _cparams(n_axes=1, vmem=V7X_VMEM_LIMIT):
    return pltpu.CompilerParams(dimension_semantics=("arbitrary",) * n_axes,
                                vmem_limit_bytes=vmem)


def _dot(a, b):
    return jnp.dot(a, b, preferred_element_type=F32)


def _dot_nt(a, b):
    return lax.dot_general(a, b, (((1,), (1,)), ((), ())), preferred_element_type=F32)


def _dot_tn(a, b):
    return lax.dot_general(a, b, (((0,), (0,)), ((), ())), preferred_element_type=F32)


def _split2(x):
    hi = x.astype(BF16)
    lo = (x - hi.astype(F32)).astype(BF16)
    return hi, lo


def _split3(x):
    hi = x.astype(BF16)
    r1 = x - hi.astype(F32)
    mid = r1.astype(BF16)
    lo = (r1 - mid.astype(F32)).astype(BF16)
    return hi, mid, lo


def _dot3(a, b):
    ah, al = _split2(a)
    bh, bl = _split2(b)
    return _dot(ah, bh) + (_dot(ah, bl) + _dot(al, bh))


def _head_ones():
    r = lax.broadcasted_iota(jnp.int32, (D_RWKV, D_RWKV), 0) >> 6
    c = lax.broadcasted_iota(jnp.int32, (D_RWKV, D_RWKV), 1) >> 6
    return (r == c).astype(BF16)


def _headsum(x, ones):
    hi, lo = _split2(x)
    return _dot(hi, ones) + _dot(lo, ones)


def _mod_body(cs_ref, w_ref, b_ref, o_ref):
    a = cs_ref[...]
    a = a * jax.nn.sigmoid(a)
    o_ref[...] = _dot3(a, w_ref[...]) + b_ref[...]


def _mod_call(cs, w_mod, b_mod):
    d, n = w_mod.shape
    tn = 512
    return pl.pallas_call(
        _mod_body,
        grid=(n // tn,),
        in_specs=[pl.BlockSpec((8, d), lambda j: (0, 0)),
                  pl.BlockSpec((d, tn), lambda j: (0, j)),
                  pl.BlockSpec((1, tn), lambda j: (0, j))],
        out_specs=pl.BlockSpec((8, tn), lambda j: (0, j)),
        out_shape=jax.ShapeDtypeStruct((8, n), F32),
        compiler_params=_cparams(),
        name="mod",
    )(cs, w_mod, b_mod)


def _pos_tile(rowtab_ref, coltab_ref):
    rt = rowtab_ref[...]
    rowpart = jnp.concatenate(
        [jnp.broadcast_to(rt[r:r + 1, :], (GRID_W, rt.shape[1])) for r in range(TM_PROJ // GRID_W)], axis=0)
    return jnp.concatenate([rowpart, coltab_ref[...]], axis=1)


def _inproj_body(x_ref, ctx_ref, rowtab_ref, coltab_ref, g1_ref, mod_ref, w_ref, us_ref, p_ref, *, n_x_tiles):
    d = x_ref.shape[1]
    is_ctx = pl.program_id(0) == n_x_tiles
    xin = jnp.where(is_ctx, ctx_ref[...], x_ref[...] + _pos_tile(rowtab_ref, coltab_ref))
    ms = jnp.mean(xin * xin, axis=-1, keepdims=True)
    xn = xin * lax.rsqrt(ms + NORM_EPS) * g1_ref[...]
    sh = jnp.where(is_ctx, mod_ref[1:2, 0:d], mod_ref[0:1, 0:d])
    sc = jnp.where(is_ctx, mod_ref[1:2, d:2 * d], mod_ref[0:1, d:2 * d])
    h = xn * (1.0 + sc) + sh
    u = _dot(h.astype(BF16), w_ref[...])
    us_ref[...] = u[:, :D_SHIFT]
    p_ref[...] = u[:, D_SHIFT:]


def _inproj_call(x2, ctx_pad, rowtab, coltab, g1, mod, w_in_bf):
    s, d = x2.shape
    nx = s // TM_PROJ
    d_in = w_in_bf.shape[1]
    rows = (nx + 1) * TM_PROJ
    xmap = lambda i: (jnp.minimum(i, nx - 1), 0)
    const = lambda i: (0, 0)
    return pl.pallas_call(
        functools.partial(_inproj_body, n_x_tiles=nx),
        grid=(nx + 1,),
        in_specs=[pl.BlockSpec((TM_PROJ, d), xmap),
                  pl.BlockSpec((TM_PROJ, d), const),
                  pl.BlockSpec((TM_PROJ // GRID_W, d // 2), xmap),
                  pl.BlockSpec((TM_PROJ, d // 2), const),
                  pl.BlockSpec((1, d), const),
                  pl.BlockSpec(mod.shape, const),
                  pl.BlockSpec((d, d_in), const)],
        out_specs=[pl.BlockSpec((TM_PROJ, D_SHIFT), lambda i: (i, 0)),
                   pl.BlockSpec((TM_PROJ, D_POOL), lambda i: (i, 0))],
        out_shape=[jax.ShapeDtypeStruct((rows, D_SHIFT), F32),
                   jax.ShapeDtypeStruct((rows, D_POOL), F32)],
        compiler_params=_cparams(),
        name="inproj",
    )(x2, ctx_pad, rowtab, coltab, g1, mod, w_in_bf)


def _feat_body(um_ref, up_ref, un_ref, mu_ref, par_ref, w2_ref, a2_ref, g2_ref,
               af_ref, rf_ref, btf_ref, ktf_ref, bhf_ref, khf_ref,
               ab_ref, rb_ref, btb_ref, ktb_ref, bhb_ref, khb_ref,
               v_ref, gamf_ref, gamb_ref, bonus_ref, g_ref, *, n_x_tiles):
    i = pl.program_id(0)
    tm = um_ref.shape[0]
    u = um_ref[...]
    prev_ok = jnp.logical_and(i != 0, i != n_x_tiles)
    next_ok = jnp.logical_and(i != n_x_tiles - 1, i != n_x_tiles)
    prow = jnp.where(prev_ok, up_ref[POOL_HALO - 1:POOL_HALO, :], 0.0)
    nrow = jnp.where(next_ok, un_ref[0:1, :], 0.0)
    rid = lax.broadcasted_iota(jnp.int32, (tm, 1), 0)
    up = jnp.where(rid == 0, prow, pltpu.roll(u, 1, 0))
    dn = jnp.where(rid == tm - 1, nrow, pltpu.roll(u, tm - 1, 0))
    s = u + mu_ref[...] * (0.5 * (up + dn) - u)

    w0f, w0b, a0f, a0b = par_ref[0:1, :], par_ref[1:2, :], par_ref[2:3, :], par_ref[3:4, :]
    k_k, k_a, rkf, rkb = par_ref[4:5, :], par_ref[5:6, :], par_ref[6:7, :], par_ref[7:8, :]

    r = s[:, 0:512]
    k = s[:, 512:1024]
    v = s[:, 1024:1536]
    zw = _dot(jnp.tanh(s[:, 1536:1664]).astype(BF16), w2_ref[...])
    za = _dot(s[:, 1664:1792].astype(BF16), a2_ref[...])
    g = _dot(jax.nn.sigmoid(s[:, 1792:1920]).astype(BF16), g2_ref[...])
    lw_f = -DECAY_SCALE * jax.nn.sigmoid(w0f + zw[:, :512])
    lw_b = -DECAY_SCALE * jax.nn.sigmoid(w0b + zw[:, 512:])
    a_f = jax.nn.sigmoid(a0f + za[:, :512])
    a_b = jax.nn.sigmoid(a0b + za[:, 512:])

    ones = _head_ones()
    kkr = k * k_k
    nrm = jnp.sqrt(_headsum(kkr * kkr, ones))
    kk = kkr / jnp.maximum(nrm, 1e-12)
    k_f = k * (1.0 + (a_f - 1.0) * k_a)
    k_b = k * (1.0 + (a_b - 1.0) * k_a)
    bonus_ref[...] = _headsum(r * (k_f * rkf + k_b * rkb), ones) * v
    g_ref[...] = g
    v_ref[...] = v.astype(BF16)

    rr = lax.broadcasted_iota(jnp.int32, (tm, tm), 0)
    cc = lax.broadcasted_iota(jnp.int32, (tm, tm), 1)
    same = (rr >> 6) == (cc >> 6)
    tri = jnp.concatenate([jnp.logical_and(same, cc < rr).astype(BF16),
                           jnp.logical_and(same, cc > rr).astype(BF16)], axis=0)
    lw = jnp.concatenate([lw_f, lw_b], axis=1)
    l1, l2, l3 = _split3(lw)
    sums = _dot(tri, l1) + (_dot(tri, l2) + _dot(tri, l3))
    before_f, after_f = sums[:tm, :512], sums[tm:, :512]
    before_b, after_b = sums[:tm, 512:], sums[tm:, 512:]

    def emit(excl, rest, lwd, a_d, k_d, a_ref, r_ref, bt_ref, kt_ref, bh_ref, kh_ref, gam_ref):
        cum = excl + lwd
        e_neg = jnp.exp(-cum)
        e_rest = jnp.exp(rest)
        kka = kk * a_d
        a_ref[...] = (-kk * jnp.exp(excl)).astype(BF16)
        r_ref[...] = (r * jnp.exp(cum)).astype(BF16)
        bt_ref[...] = (kka * e_neg).astype(BF16)
        kt_ref[...] = (k_d * e_neg).astype(BF16)
        bh_ref[...] = (kka * e_rest).astype(BF16)
        kh_ref[...] = (k_d * e_rest).astype(BF16)
        tot = cum + rest
        gam_ref[0] = jnp.exp(jnp.concatenate(
            [tot[c * CHUNK:c * CHUNK + 1, :] for c in range(tm // CHUNK)], axis=0))

    emit(before_f, after_f, lw_f, a_f, k_f, af_ref, rf_ref, btf_ref, ktf_ref, bhf_ref, khf_ref, gamf_ref)
    emit(after_b, before_b, lw_b, a_b, k_b, ab_ref, rb_ref, btb_ref, ktb_ref, bhb_ref, khb_ref, gamb_ref)


def _feat_call(u_shift, n_tok, n_x_tiles, mu, par, w2, a2, g2):
    tm = TM_FEAT
    nt = n_tok // tm
    hb = tm // POOL_HALO
    last_blk = u_shift.shape[0] // POOL_HALO - 1
    const = lambda i: (0, 0)
    tok = lambda i: (i, 0)
    seq_bf = jax.ShapeDtypeStruct((n_tok, D_RWKV), BF16)
    seq_f32 = jax.ShapeDtypeStruct((n_tok, D_RWKV), F32)
    gam = jax.ShapeDtypeStruct((nt, tm // CHUNK, D_RWKV), F32)
    seq_spec = pl.BlockSpec((tm, D_RWKV), tok)
    gam_spec = pl.BlockSpec((1, tm // CHUNK, D_RWKV), lambda i: (i, 0, 0))
    return pl.pallas_call(
        functools.partial(_feat_body, n_x_tiles=n_x_tiles),
        grid=(nt,),
        in_specs=[pl.BlockSpec((tm, D_SHIFT), tok),
                  pl.BlockSpec((POOL_HALO, D_SHIFT), lambda i: (jnp.maximum(i * hb - 1, 0), 0)),
                  pl.BlockSpec((POOL_HALO, D_SHIFT), lambda i: (jnp.minimum((i + 1) * hb, last_blk), 0)),
                  pl.BlockSpec((1, D_SHIFT), const),
                  pl.BlockSpec((8, D_RWKV), const),
                  pl.BlockSpec(w2.shape, const),
                  pl.BlockSpec(a2.shape, const),
                  pl.BlockSpec(g2.shape, const)],
        out_specs=[seq_spec] * 13 + [gam_spec, gam_spec, seq_spec, seq_spec],
        out_shape=[seq_bf] * 13 + [gam, gam, seq_f32, seq_f32],
        compiler_params=_cparams(),
        name="feat",
    )(u_shift, u_shift, u_shift, mu, par, w2, a2, g2)


def _blockdiag(x, head_of_lane):
    xb = x.astype(BF16)
    zero = jnp.zeros_like(xb)
    return jnp.concatenate([jnp.where(head_of_lane == h, xb, zero) for h in range(QUAD // HEAD)], axis=0)


def _unblock(x, head_of_lane):
    out = jnp.where(head_of_lane == 0, x[0:HEAD, :], 0.0)
    for h in range(1, QUAD // HEAD):
        out = jnp.where(head_of_lane == h, x[h * HEAD:(h + 1) * HEAD, :], out)
    return out


def _scan_chunks(chains):
    lane = lax.broadcasted_iota(jnp.int32, (HEAD, QUAD), 1)
    head_of_lane = lane >> 6
    s_idx = lane & (HEAD - 1)
    t_idx = lax.broadcasted_iota(jnp.int32, (HEAD, QUAD), 0)
    eye = s_idx == t_idx
    lower = s_idx < t_idx
    upper = s_idx > t_idx
    bd = functools.partial(_blockdiag, head_of_lane=head_of_lane)

    def same_block(shift):
        return (s_idx >> shift) == (t_idx >> shift)

    def each(fn):
        return [fn(c) for c in chains]

    def stage_masks(c):
        ll = _dot_nt(jnp.concatenate([c["a"], c["r"]], axis=0),
                     jnp.concatenate([bd(c["bt"]), bd(c["kt"])], axis=0))
        strict = upper if c["reverse"] else lower
        incl = jnp.logical_or(strict, eye)
        c["lab"] = jnp.where(strict, ll[:HEAD, :QUAD], 0.0)
        c["lak"] = jnp.where(strict, ll[:HEAD, QUAD:], 0.0)
        c["mrb"] = jnp.where(incl, ll[HEAD:, :QUAD], 0.0)
        c["mrk"] = jnp.where(incl, ll[HEAD:, QUAD:], 0.0)

    each(stage_masks)

    def stage_square(c):
        ld = jnp.where(same_block(4), c["lab"], 0.0)
        c["tm"] = jnp.where(eye, 1.0, 0.0) + ld
        c["xk"] = _dot(ld.astype(BF16), bd(ld))

    def stage_double(c):
        y = _dot(jnp.concatenate([c["tm"], c["xk"]], axis=0).astype(BF16), bd(c["xk"]))
        c["tm"] = c["tm"] + y[:HEAD]
        c["xk"] = y[HEAD:]

    def stage_last_power(c):
        c["tm"] = c["tm"] + _dot(c["tm"].astype(BF16), bd(c["xk"]))

    each(stage_square)
    each(stage_double)
    each(stage_double)
    each(stage_last_power)
    each(lambda c: c.update(vv=_dot(jnp.concatenate([c["lak"], c["mrk"]], axis=0).astype(BF16), bd(c["v"]))))
    for shift in (4, 5):
        joined = jnp.logical_and(same_block(shift + 1), jnp.logical_not(same_block(shift)))
        each(lambda c: c.update(te=_dot(c["tm"].astype(BF16), bd(jnp.where(joined, c["lab"], 0.0)))))
        each(lambda c: c.update(tm=c["tm"] + _dot(c["te"].astype(BF16), bd(c["tm"]))))

    each(lambda c: c.update(x1=_dot(c["tm"].astype(BF16),
                                    jnp.concatenate([bd(c["a"]), bd(c["vv"][:HEAD])], axis=1))))
    each(lambda c: c.update(x2=_dot(c["mrb"].astype(BF16),
                                    jnp.concatenate([bd(c["x1"][:, :QUAD]), bd(c["x1"][:, QUAD:])], axis=1))))

    def stage_state_terms(c):
        rhs = jnp.concatenate(
            [c["x1"].astype(BF16), jnp.concatenate([jnp.zeros((HEAD, QUAD), BF16), c["v"]], axis=1)], axis=0)
        pz = _dot_tn(jnp.concatenate([c["bh"], c["kh"]], axis=0), rhs)
        c["pq"] = _unblock(pz[:, :QUAD], head_of_lane) + jnp.where(eye, c["gam"], 0.0)
        c["zq"] = _unblock(pz[:, QUAD:], head_of_lane)
        c["qeff"] = c["r"].astype(F32) + c["x2"][:, :QUAD]
        c["yin"] = c["x2"][:, QUAD:] + c["vv"][HEAD:]

    each(stage_state_terms)

    outs, state = [], {}
    for c in chains:
        hst = state.get(c["key"], c["hst"])
        qp = _dot(jnp.concatenate([c["qeff"], c["pq"]], axis=0).astype(BF16), bd(hst))
        state[c["key"]] = qp[HEAD:] + c["zq"]
        outs.append(qp[:HEAD] + c["yin"])
    return outs, state


def _scan_body(af, rf, btf, ktf, bhf, khf, vf, gamf, ab, rb, btb, ktb, bhb, khb, vb, gamb,
               yf_ref, yb_ref, hf_ref, hb_ref):
    @pl.when(pl.program_id(0) == 0)
    def _():
        hf_ref[...] = jnp.zeros_like(hf_ref)
        hb_ref[...] = jnp.zeros_like(hb_ref)

    names = ("a", "r", "bt", "kt", "bh", "kh", "v")
    chains, sinks, state_refs = [], [], {}
    for ci in range(SCAN_CPS):
        for refs, y_ref, h_ref, reverse in (((af, rf, btf, ktf, bhf, khf, vf, gamf), yf_ref, hf_ref, False),
                                            ((ab, rb, btb, ktb, bhb, khb, vb, gamb), yb_ref, hb_ref, True)):
            chunk = SCAN_CPS - 1 - ci if reverse else ci
            rows = slice(chunk * CHUNK, (chunk + 1) * CHUNK)
            for q in range(D_RWKV // QUAD):
                sl = slice(q * QUAD, (q + 1) * QUAD)
                chain = {n: ref[rows, sl] for n, ref in zip(names, refs[:7])}
                chain.update(gam=refs[7][chunk, :, sl], hst=h_ref[:, sl], reverse=reverse, key=(reverse, q))
                chains.append(chain)
                sinks.append((y_ref, rows, sl))
                state_refs[(reverse, q)] = (h_ref, sl)
    outs, state = _scan_chunks(chains)
    for y, (y_ref, rows, sl) in zip(outs, sinks):
        y_ref[rows, sl] = y
    for key, (h_ref, sl) in state_refs.items():
        h_ref[:, sl] = state[key]


def _scan_call(feats, gamf, gamb, n_x_chunks, n_c_chunks):
    af, rf, btf, ktf, bhf, khf, ab, rb, btb, ktb, bhb, khb, v = feats
    assert n_x_chunks % SCAN_CPS == 0 and n_c_chunks % SCAN_CPS == 0
    nxb, ncb = n_x_chunks // SCAN_CPS, n_c_chunks // SCAN_CPS
    n = nxb + ncb
    fwd = lambda i: (jnp.where(i < ncb, nxb + i, i - ncb), 0)
    bwd = lambda i: (n - 1 - i, 0)
    fwd3 = lambda i: (jnp.where(i < ncb, nxb + i, i - ncb), 0, 0)
    bwd3 = lambda i: (n - 1 - i, 0, 0)
    blk = (SCAN_CPS * CHUNK, D_RWKV)
    in_specs = ([pl.BlockSpec(blk, fwd)] * 7 + [pl.BlockSpec((SCAN_CPS, 1, D_RWKV), fwd3)]
                + [pl.BlockSpec(blk, bwd)] * 7 + [pl.BlockSpec((SCAN_CPS, 1, D_RWKV), bwd3)])
    y = jax.ShapeDtypeStruct((n * SCAN_CPS * CHUNK, D_RWKV), F32)
    return pl.pallas_call(
        _scan_body,
        grid=(n,),
        in_specs=in_specs,
        out_specs=[pl.BlockSpec(blk, fwd), pl.BlockSpec(blk, bwd)],
        out_shape=[y, y],
        scratch_shapes=[pltpu.VMEM((HEAD, D_RWKV), F32), pltpu.VMEM((HEAD, D_RWKV), F32)],
        compiler_params=_cparams(),
        name="scan",
    )(af, rf, btf, ktf, bhf, khf, v, gamf, ab, rb, btb, ktb, bhb, khb, v, gamb)


def _mix_body(x_ref, rowtab_ref, coltab_ref, yf_ref, yb_ref, bonus_ref, g_ref, pm_ref, pp_ref, pn_ref,
              gn_ref, wpool_ref, pscale_ref, wout_ref, mod_ref, o_ref, slab_ref, *, n_x_tiles, seq_len):
    j = pl.program_id(0)
    tm, d = x_ref.shape
    hl = POOL_HALO
    slab_ref[0:hl, :] = jnp.where(j != 0, pp_ref[...], 0.0)
    slab_ref[hl:hl + tm, :] = pm_ref[...]
    slab_ref[hl + tm:2 * hl + tm, :] = jnp.where(j != n_x_tiles - 1, pn_ref[...], 0.0)
    tglob = j * tm + lax.broadcasted_iota(jnp.int32, (tm, 1), 0)
    pooled = []
    for gi, win in enumerate(POOL_WINDOWS):
        cols = slice(gi * POOL_GW, (gi + 1) * POOL_GW)
        acc = slab_ref[hl - win // 2:hl - win // 2 + tm, cols]
        for off in range(-win // 2 + 1, win // 2):
            acc = acc + slab_ref[hl + off:hl + off + tm, cols]
        lo = jnp.maximum(tglob - win // 2, 0)
        hi = jnp.minimum(tglob - win // 2 + win, seq_len)
        cnt = (hi - lo).astype(F32)
        pg = acc / cnt - slab_ref[hl:hl + tm, cols]
        pooled.append(_dot(pg.astype(BF16), wpool_ref[gi]))
    pool = jnp.concatenate(pooled, axis=1) * pscale_ref[...]

    ones = _head_ones()
    y = yf_ref[...] + yb_ref[...]
    yc = y - _headsum(y, ones) * (1.0 / HEAD)
    var = _headsum(yc * yc, ones) * (1.0 / HEAD)
    rw = (yc * lax.rsqrt(var + GN_EPS) * gn_ref[0:1, :] + gn_ref[1:2, :] + bonus_ref[...]) * g_ref[...]
    mix = _dot(jnp.concatenate([rw, pool], axis=1).astype(BF16), wout_ref[...])
    gt_a = mod_ref[0:1, 2 * d:3 * d]
    o_ref[...] = x_ref[...] + _pos_tile(rowtab_ref, coltab_ref) + gt_a * mix


def _mix_call(x2, rowtab, coltab, yf, yb, bonus, g, p, gn, w_pool_bf, pscale, w_out_bf, mod):
    s, d = x2.shape
    tm = TM_PROJ
    nx = s // tm
    hb = tm // POOL_HALO
    const = lambda j: (0, 0)
    tok = lambda j: (j, 0)
    seq_spec = pl.BlockSpec((tm, D_RWKV), tok)
    return pl.pallas_call(
        functools.partial(_mix_body, n_x_tiles=nx, seq_len=s),
        grid=(nx,),
        in_specs=[pl.BlockSpec((tm, d), tok),
                  pl.BlockSpec((tm // GRID_W, d // 2), tok),
                  pl.BlockSpec((tm, d // 2), const),
                  seq_spec, seq_spec, seq_spec, seq_spec, seq_spec,
                  pl.BlockSpec((POOL_HALO, D_POOL), lambda j: (jnp.maximum(j * hb - 1, 0), 0)),
                  pl.BlockSpec((POOL_HALO, D_POOL), lambda j: ((j + 1) * hb, 0)),
                  pl.BlockSpec((2, D_RWKV), const),
                  pl.BlockSpec(w_pool_bf.shape, lambda j: (0, 0, 0)),
                  pl.BlockSpec((1, D_POOL), const),
                  pl.BlockSpec(w_out_bf.shape, const),
                  pl.BlockSpec(mod.shape, const)],
        out_specs=pl.BlockSpec((tm, d), tok),
        out_shape=jax.ShapeDtypeStruct((s, d), F32),
        scratch_shapes=[pltpu.VMEM((tm + 2 * POOL_HALO, D_POOL), F32)],
        compiler_params=_cparams(),
        name="mix",
    )(x2, rowtab, coltab, yf, yb, bonus, g, p, p, p, gn, w_pool_bf, pscale, w_out_bf, mod)


def _to_token_tiles(ref, x):
    tm = x.shape[0]
    for j in range(x.shape[1] // LANES):
        ref[pl.ds(j, tm, stride=SUBLANES), :] = x[:, j * LANES:(j + 1) * LANES]


def _from_token_tiles(ref, tm):
    return jnp.concatenate([ref[pl.ds(j, tm, stride=SUBLANES), :] for j in range(SUBLANES)], axis=1)


def _router_body(x_ref, g2_ref, mod_ref, wr_ref, br_ref, h3_ref, eid_ref, gate_ref):
    tm, d = x_ref.shape
    x = x_ref[...]
    ms = jnp.mean(x * x, axis=-1, keepdims=True)
    xn = x * lax.rsqrt(ms + NORM_EPS) * g2_ref[...]
    h = xn * (1.0 + mod_ref[0:1, 4 * d:5 * d]) + mod_ref[0:1, 3 * d:4 * d]
    _to_token_tiles(h3_ref, h)

    wh, wl = _split2(wr_ref[...])
    hh, hl = _split2(h)
    logits = _dot_nt(wh, hh) + (_dot_nt(wh, hl) + _dot_nt(wl, hh)) + br_ref[...]
    gl = logits[0:N_GROUPS, :]
    gmax = jnp.max(gl, axis=0, keepdims=True)
    gidx = lax.broadcasted_iota(jnp.int32, gl.shape, 0).astype(F32)
    grp = jnp.min(jnp.where(gl == gmax, gidx, float(N_GROUPS)), axis=0, keepdims=True)
    p_grp = 1.0 / jnp.sum(jnp.exp(gl - gmax), axis=0, keepdims=True)
    sel = jnp.zeros((EXPERTS_PER_GROUP, tm), F32)
    for gi in range(N_GROUPS):
        lo = N_GROUPS + gi * EXPERTS_PER_GROUP
        sel = jnp.where(grp == float(gi), logits[lo:lo + EXPERTS_PER_GROUP, :], sel)
    eidx = lax.broadcasted_iota(jnp.int32, sel.shape, 0).astype(F32)
    top1 = jnp.max(sel, axis=0, keepdims=True)
    i1 = jnp.min(jnp.where(sel == top1, eidx, float(EXPERTS_PER_GROUP)), axis=0, keepdims=True)
    sel2 = jnp.where(eidx == i1, -jnp.inf, sel)
    top2 = jnp.max(sel2, axis=0, keepdims=True)
    i2 = jnp.min(jnp.where(sel2 == top2, eidx, float(EXPERTS_PER_GROUP)), axis=0, keepdims=True)
    e2 = jnp.exp(top2 - top1)
    inv = 1.0 / (1.0 + e2)
    zf = jnp.zeros((6, tm), F32)
    gate_ref[...] = jnp.concatenate([p_grp * inv, p_grp * (e2 * inv), zf], axis=0)
    base = grp * float(EXPERTS_PER_GROUP)
    eid_ref[...] = jnp.concatenate([base + i1, base + i2, zf], axis=0).astype(jnp.int32)


def _router_call(x_mid, g2n, mod, wr, br):
    s, d = x_mid.shape
    tm = TM_PROJ
    const = lambda j: (0, 0)
    return pl.pallas_call(
        _router_body,
        grid=(s // tm,),
        in_specs=[pl.BlockSpec((tm, d), lambda j: (j, 0)),
                  pl.BlockSpec((1, d), const),
                  pl.BlockSpec(mod.shape, const),
                  pl.BlockSpec(wr.shape, const),
                  pl.BlockSpec(br.shape, const)],
        out_specs=[pl.BlockSpec((tm * SUBLANES, LANES), lambda j: (j, 0)),
                   pl.BlockSpec((8, tm), lambda j: (0, j)),
                   pl.BlockSpec((8, tm), lambda j: (0, j))],
        out_shape=[jax.ShapeDtypeStruct((s * SUBLANES, LANES), F32),
                   jax.ShapeDtypeStruct((8, s), jnp.int32),
                   jax.ShapeDtypeStruct((8, s), F32)],
        compiler_params=_cparams(),
        name="router",
    )(x_mid, g2n, mod, wr, br)


def _sc_gather_tiles(table3, idx):
    n_out = idx.shape[0]
    n_workers = V7X_SC_CORES * V7X_SC_SUBCORES
    per_worker = n_out // n_workers
    n_chunks = per_worker // SC_GATHER_ROWS
    assert per_worker * n_workers == n_out and n_chunks * SC_GATHER_ROWS == per_worker
    mesh = plsc.VectorSubcoreMesh(core_axis_name="c", subcore_axis_name="s",
                                  num_cores=V7X_SC_CORES, num_subcores=V7X_SC_SUBCORES)

    def body(table_hbm, idx_hbm, out_hbm, idx_v, rows_v, sem):
        worker = lax.axis_index("s") * V7X_SC_CORES + lax.axis_index("c")
        base = worker * per_worker
        pltpu.sync_copy(idx_hbm.at[pl.ds(base, per_worker)], idx_v)

        @pl.loop(0, n_chunks)
        def _(j):
            off = pl.multiple_of(j * SC_GATHER_ROWS, SC_GATHER_ROWS)
            pltpu.async_copy(table_hbm.at[idx_v.at[pl.ds(off, SC_GATHER_ROWS)]], rows_v, sem).wait()
            pltpu.sync_copy(rows_v, out_hbm.at[pl.ds(base + off, SC_GATHER_ROWS)])

    return pl.kernel(
        body,
        out_type=jax.ShapeDtypeStruct((n_out,) + table3.shape[1:], table3.dtype),
        mesh=mesh,
        scratch_types=[pltpu.VMEM((per_worker,), jnp.int32),
                       pltpu.VMEM((SC_GATHER_ROWS,) + table3.shape[1:], table3.dtype),
                       pltpu.SemaphoreType.DMA],
        name="sc_gather",
    )(table3, idx)


def _experts_body(blk_e_ref, n_used_ref, src_ref, src_next_ref, h3_ref, wg_ref, wu_ref, wd_ref, y_ref,
                  xbuf0_ref, xbuf1_ref, gsem_ref):
    del blk_e_ref
    i = pl.program_id(0)
    n_used = n_used_ref[0]
    xbufs = (xbuf0_ref, xbuf1_ref)

    def gather(idx_ref, slot, start):
        for r in range(MOE_BM):
            src_row = pl.multiple_of(idx_ref[0, 0, r], SUBLANES) if start else 0
            cp = pltpu.make_async_copy(h3_ref.at[pl.ds(src_row, SUBLANES)],
                                       xbufs[slot].at[pl.ds(r * SUBLANES, SUBLANES)], gsem_ref.at[slot])
            cp.start() if start else cp.wait()

    @pl.when(i == 0)
    def _():
        gather(src_ref, 0, True)

    for slot in (0, 1):
        @pl.when(jnp.logical_and(i < n_used, (i & 1) == slot))
        def _():
            gather(src_ref, slot, False)
            gather(src_next_ref, 1 - slot, True)
            xb = _from_token_tiles(xbufs[slot], MOE_BM).astype(BF16)
            gate = _dot(xb, wg_ref[0].astype(BF16))
            upp = _dot(xb, wu_ref[0].astype(BF16))
            hid = (gate * jax.nn.sigmoid(gate)) * upp
            _to_token_tiles(y_ref, _dot(hid.astype(BF16), wd_ref[0].astype(BF16)))

        @pl.when(jnp.logical_and(i == n_used, (i & 1) == slot))
        def _():
            gather(src_ref, slot, False)

    @pl.when(i >= n_used)
    def _():
        y_ref[...] = jnp.zeros_like(y_ref)


def _experts_call(blk_expert, n_used, src3, h3, wg, wu, wd):
    nb = src3.shape[0]
    d = wg.shape[1]
    rows = MOE_BM * SUBLANES
    last = nb - 1
    cur = lambda i, be, nu: (jnp.minimum(i, last), 0, 0)
    nxt = lambda i, be, nu: (jnp.minimum(i + 1, last), 0, 0)
    wmap = lambda i, be, nu: (be[jnp.minimum(i, last)], 0, 0)
    idx_blk = (1, 1, MOE_BM)
    buf = pltpu.VMEM((rows, LANES), F32)
    grid_spec = pltpu.PrefetchScalarGridSpec(
        num_scalar_prefetch=2,
        grid=(nb + 1,),
        in_specs=[pl.BlockSpec(idx_blk, cur, memory_space=pltpu.SMEM),
                  pl.BlockSpec(idx_blk, nxt, memory_space=pltpu.SMEM),
                  pl.BlockSpec(memory_space=pl.ANY),
                  pl.BlockSpec((1, d, D_EXPERT), wmap),
                  pl.BlockSpec((1, d, D_EXPERT), wmap),
                  pl.BlockSpec((1, D_EXPERT, d), wmap)],
        out_specs=pl.BlockSpec((rows, LANES), lambda i, be, nu: (i, 0)),
        scratch_shapes=[buf, buf, pltpu.SemaphoreType.DMA((2,))],
    )
    return pl.pallas_call(
        _experts_body,
        grid_spec=grid_spec,
        out_shape=jax.ShapeDtypeStruct(((nb + 1) * rows, LANES), F32),
        compiler_params=_cparams(),
        name="experts",
    )(blk_expert, n_used, src3, src3, h3, wg, wu, wd)


def _combine_body(x_ref, y0_ref, y1_ref, gate_ref, mod_ref, fg_ref, o_ref):
    tm, d = x_ref.shape
    y = gate_ref[:, 0:1] * _from_token_tiles(y0_ref, tm) + gate_ref[:, 1:2] * _from_token_tiles(y1_ref, tm)
    x = x_ref[...] + mod_ref[0:1, 5 * d:6 * d] * y
    ms = jnp.mean(x * x, axis=-1, keepdims=True)
    o_ref[...] = x * lax.rsqrt(ms + NORM_EPS) * fg_ref[...]


def _combine_call(x_mid, y3, gcol, mod, fg):
    s, d = x_mid.shape
    tm = TM_PROJ
    nt = s // tm
    const = lambda j: (0, 0)
    return pl.pallas_call(
        _combine_body,
        grid=(nt,),
        in_specs=[pl.BlockSpec((tm, d), lambda j: (j, 0)),
                  pl.BlockSpec((tm * SUBLANES, LANES), lambda j: (j, 0)),
                  pl.BlockSpec((tm * SUBLANES, LANES), lambda j: (j + nt, 0)),
                  pl.BlockSpec((tm, 2), lambda j: (j, 0)),
                  pl.BlockSpec(mod.shape, const),
                  pl.BlockSpec((1, d), const)],
        out_specs=pl.BlockSpec((tm, d), lambda j: (j, 0)),
        out_shape=jax.ShapeDtypeStruct((s, d), F32),
        compiler_params=_cparams(),
        name="combine",
    )(x_mid, y3, y3, gcol, mod, fg)


def _blockdiag2(wf, wb):
    z = jnp.zeros_like(wf)
    return jnp.concatenate([jnp.concatenate([wf, z], axis=1), jnp.concatenate([z, wb], axis=1)], axis=0)


def _routing_plan(eids, n_tok):
    m = 2 * n_tok
    experts = jnp.arange(N_EXPERTS, dtype=jnp.int32)
    flat_e = eids[:2].reshape(m)
    order = jnp.argsort(flat_e, stable=True).astype(jnp.int32)
    onehot = (flat_e[:, None] == experts[None, :]).astype(jnp.int32)
    csum = jnp.cumsum(onehot, axis=0)
    counts = csum[-1]
    starts = jnp.cumsum(counts) - counts
    pcounts = (counts + MOE_BM - 1) // MOE_BM * MOE_BM
    pends = jnp.cumsum(pcounts)
    pstarts = pends - pcounts
    n_blocks = (m + N_EXPERTS * (MOE_BM - 1) + MOE_BM - 1) // MOE_BM
    slot = jnp.arange(n_blocks * MOE_BM, dtype=jnp.int32)
    slot_e = jnp.minimum(jnp.sum((slot[:, None] >= pends[None, :]).astype(jnp.int32), axis=1), N_EXPERTS - 1)
    sel = (slot_e[:, None] == experts[None, :]).astype(jnp.int32)
    local = slot - jnp.sum(sel * pstarts[None, :], axis=1)
    valid = local < jnp.sum(sel * counts[None, :], axis=1)
    pair = order[jnp.clip(jnp.sum(sel * starts[None, :], axis=1) + local, 0, m - 1)]
    src_tok = jnp.where(valid, jnp.where(pair >= n_tok, pair - n_tok, pair), 0)
    dest = jnp.sum(onehot * (csum - 1 + pstarts[None, :]), axis=1).astype(jnp.int32)
    blk_expert = slot_e[::MOE_BM]
    n_used = (pends[-1:] // MOE_BM).astype(jnp.int32)
    return src_tok, dest, blk_expert, n_used


def kernel(x, c, ctx, c_ctx, norm1_g, w_mod, b_mod, w_in, mu_shift, w0_f, w2_f, w0_b, w2_b, a0_f, a2_f, a0_b, a2_b, g2, k_k, k_a, r_k_f, r_k_b, gn_w, gn_b, w_pool, pool_scale, w_out, norm2_g, w_router_grp, b_router_grp, w_router_exp, b_router_exp, w_gate, w_up, w_down, final_g):
    b, s, d = x.shape
    n_ctx = ctx.shape[1]
    assert b == 1 and c.shape[0] == 1 and w_mod.shape[0] == 1
    assert s % TM_PROJ == 0 and n_ctx % CHUNK == 0 and n_ctx == TM_FEAT and d == 2 * D_RWKV
    x2 = x[0]
    rows = s // GRID_W

    quarter = d // 4
    freq = POS_THETA ** (-jnp.arange(quarter, dtype=F32) / quarter)
    rarg = jnp.arange(rows, dtype=F32)[:, None] * freq
    carg = jnp.arange(GRID_W, dtype=F32)[:, None] * freq
    rowtab = jnp.concatenate([jnp.sin(rarg), jnp.cos(rarg)], axis=-1)
    coltab = jnp.tile(jnp.concatenate([jnp.sin(carg), jnp.cos(carg)], axis=-1), (TM_PROJ // GRID_W, 1))

    cs = jnp.concatenate([c, c_ctx[None, :], jnp.zeros((6, d), F32)], axis=0)
    mod = _mod_call(cs, w_mod[0], b_mod)

    ctx_pad = jnp.concatenate([ctx[0], jnp.zeros((TM_PROJ - n_ctx, d), F32)], axis=0)
    u_shift, p = _inproj_call(x2, ctx_pad, rowtab, coltab, norm1_g, mod, w_in[0].astype(BF16))

    par = jnp.concatenate([w0_f, w0_b, a0_f, a0_b, k_k, k_a, r_k_f, r_k_b], axis=0)
    n_tok = s + n_ctx
    outs = _feat_call(u_shift, n_tok, s // TM_FEAT, mu_shift,
                      par, _blockdiag2(w2_f[0], w2_b[0]).astype(BF16),
                      _blockdiag2(a2_f[0], a2_b[0]).astype(BF16), g2[0].astype(BF16))
    feats, gamf, gamb, bonus, g = outs[:13], outs[13], outs[14], outs[15], outs[16]
    n_chunks = n_tok // CHUNK
    yf, yb = _scan_call(feats, gamf.reshape(n_chunks, 1, D_RWKV), gamb.reshape(n_chunks, 1, D_RWKV),
                        s // CHUNK, n_ctx // CHUNK)

    gn = jnp.concatenate([gn_w, gn_b], axis=0)
    x_mid = _mix_call(x2, rowtab, coltab, yf, yb, bonus, g, p, gn, w_pool[0].astype(BF16), pool_scale,
                      w_out[0].astype(BF16), mod)

    wr = jnp.concatenate([w_router_grp[0].T, w_router_exp[0].T,
                          jnp.zeros((ROUTER_ROWS - N_GROUPS - N_EXPERTS, d), F32)], axis=0)
    br = jnp.concatenate([b_router_grp[0], b_router_exp[0],
                          jnp.zeros((ROUTER_ROWS - N_GROUPS - N_EXPERTS,), F32)])[:, None]
    h3, eids, gates = _router_call(x_mid, norm2_g, mod, wr, br)

    src_tok, dest, blk_expert, n_used = _routing_plan(eids, s)
    src3 = (src_tok * SUBLANES).reshape(-1, 1, MOE_BM)
    yexp = _experts_call(blk_expert, n_used, src3, h3, w_gate[0], w_up[0], w_down[0])
    y3 = _sc_gather_tiles(yexp.reshape(-1, SUBLANES, LANES), dest).reshape(-1, LANES)
    out = _combine_call(x_mid, y3, gates[:2].T, mod, final_g[None, :])
    return out[None]
```

```python
import functools

import jax
import jax.numpy as jnp
from jax import lax
from jax.experimental import pallas as pl
from jax.experimental.pallas import tpu as pltpu

F32 = jnp.float32
BF16 = jnp.bfloat16

GRID_W = 64
SUBLANES, LANES = 8, 128
HEAD = 64
N_HEADS = 8
D_RWKV = HEAD * N_HEADS
D_POOL = 512
POOL_WINDOWS = (2, 4, 8, 16)
POOL_GW = 128
POOL_HALO = 8
D_SHIFT = 1920
N_GROUPS = 4
EXPERTS_PER_GROUP = 8
N_EXPERTS = 32
D_EXPERT = 512
NORM_EPS = 1e-6
GN_EPS = 64e-5
POS_THETA = 10000.0
DECAY_SCALE = 0.6065306597

CHUNK = 64
QUAD = 4 * HEAD
SCAN_CPS = 4
TM_PROJ = 512
TM_FEAT = 256
MOE_BM = 256
ROUTER_ROWS = 48
V7X_VMEM_LIMIT = 56 * 1024 * 1024


def _cparams(n_axes=1, vmem=V7X_VMEM_LIMIT):
    return pltpu.CompilerParams(dimension_semantics=("arbitrary",) * n_axes,
                                vmem_limit_bytes=vmem)


def _dot(a, b):
    return jnp.dot(a, b, preferred_element_type=F32)


def _dot_nt(a, b):
    return lax.dot_general(a, b, (((1,), (1,)), ((), ())), preferred_element_type=F32)


def _dot_tn(a, b):
    return lax.dot_general(a, b, (((0,), (0,)), ((), ())), preferred_element_type=F32)


def _split2(x):
    hi = x.astype(BF16)
    lo = (x - hi.astype(F32)).astype(BF16)
    return hi, lo


def _split3(x):
    hi = x.astype(BF16)
    r1 = x - hi.astype(F32)
    mid = r1.astype(BF16)
    lo = (r1 - mid.astype(F32)).astype(BF16)
    return hi, mid, lo


def _dot3(a, b):
    ah, al = _split2(a)
    bh, bl = _split2(b)
    return _dot(ah, bh) + (_dot(ah, bl) + _dot(al, bh))


def _sigmoid(x):
    return 0.5 * jnp.tanh(0.5 * x) + 0.5


def _head_ones():
    r = lax.broadcasted_iota(jnp.int32, (D_RWKV, D_RWKV), 0) >> 6
    c = lax.broadcasted_iota(jnp.int32, (D_RWKV, D_RWKV), 1) >> 6
    return (r == c).astype(BF16)


def _headsum(x, ones):
    hi, lo = _split2(x)
    return _dot(hi, ones) + _dot(lo, ones)


def _mod_body(cs_ref, w_ref, b_ref, o_ref):
    a = cs_ref[...]
    a = a * _sigmoid(a)
    o_ref[...] = _dot3(a, w_ref[...]) + b_ref[...]


def _mod_call(cs, w_mod, b_mod):
    d, n = w_mod.shape
    tn = 512
    return pl.pallas_call(
        _mod_body,
        grid=(n // tn,),
        in_specs=[pl.BlockSpec((8, d), lambda j: (0, 0)),
                  pl.BlockSpec((d, tn), lambda j: (0, j)),
                  pl.BlockSpec((1, tn), lambda j: (0, j))],
        out_specs=pl.BlockSpec((8, tn), lambda j: (0, j)),
        out_shape=jax.ShapeDtypeStruct((8, n), F32),
        compiler_params=_cparams(),
        name="mod",
    )(cs, w_mod, b_mod)


def _pos_tile(rowtab_ref, coltab_ref):
    rt = rowtab_ref[...]
    rowpart = jnp.concatenate(
        [jnp.broadcast_to(rt[r:r + 1, :], (GRID_W, rt.shape[1])) for r in range(TM_PROJ // GRID_W)], axis=0)
    return jnp.concatenate([rowpart, coltab_ref[...]], axis=1)


def _inproj_body(x_ref, ctx_ref, rowtab_ref, coltab_ref, g1_ref, mod_ref, w_ref, us_ref, p_ref, *, n_x_tiles):
    d = x_ref.shape[1]
    is_ctx = pl.program_id(0) == n_x_tiles
    xin = jnp.where(is_ctx, ctx_ref[...], x_ref[...] + _pos_tile(rowtab_ref, coltab_ref))
    ms = jnp.mean(xin * xin, axis=-1, keepdims=True)
    xn = xin * lax.rsqrt(ms + NORM_EPS) * g1_ref[...]
    sh = jnp.where(is_ctx, mod_ref[1:2, 0:d], mod_ref[0:1, 0:d])
    sc = jnp.where(is_ctx, mod_ref[1:2, d:2 * d], mod_ref[0:1, d:2 * d])
    h = xn * (1.0 + sc) + sh
    u = _dot(h.astype(BF16), w_ref[...])
    us_ref[...] = u[:, :D_SHIFT]
    p_ref[...] = u[:, D_SHIFT:]


def _inproj_call(x2, ctx_pad, rowtab, coltab, g1, mod, w_in_bf):
    s, d = x2.shape
    nx = s // TM_PROJ
    d_in = w_in_bf.shape[1]
    rows = (nx + 1) * TM_PROJ
    xmap = lambda i: (jnp.minimum(i, nx - 1), 0)
    const = lambda i: (0, 0)
    return pl.pallas_call(
        functools.partial(_inproj_body, n_x_tiles=nx),
        grid=(nx + 1,),
        in_specs=[pl.BlockSpec((TM_PROJ, d), xmap),
                  pl.BlockSpec((TM_PROJ, d), const),
                  pl.BlockSpec((TM_PROJ // GRID_W, d // 2), xmap),
                  pl.BlockSpec((TM_PROJ, d // 2), const),
                  pl.BlockSpec((1, d), const),
                  pl.BlockSpec(mod.shape, const),
                  pl.BlockSpec((d, d_in), const)],
        out_specs=[pl.BlockSpec((TM_PROJ, D_SHIFT), lambda i: (i, 0)),
                   pl.BlockSpec((TM_PROJ, D_POOL), lambda i: (i, 0))],
        out_shape=[jax.ShapeDtypeStruct((rows, D_SHIFT), F32),
                   jax.ShapeDtypeStruct((rows, D_POOL), F32)],
        compiler_params=_cparams(),
        name="inproj",
    )(x2, ctx_pad, rowtab, coltab, g1, mod, w_in_bf)


def _feat_body(um_ref, up_ref, un_ref, mu_ref, par_ref, w2_ref, a2_ref, g2_ref,
               af_ref, rf_ref, btf_ref, ktf_ref, ab_ref, rb_ref, btb_ref, ktb_ref,
               v_ref, gamf_ref, gamb_ref, bonus_ref, g_ref, *, n_x_tiles):
    i = pl.program_id(0)
    tm = um_ref.shape[0]
    u = um_ref[...]
    prev_ok = jnp.logical_and(i != 0, i != n_x_tiles)
    next_ok = jnp.logical_and(i != n_x_tiles - 1, i != n_x_tiles)
    prow = jnp.where(prev_ok, up_ref[POOL_HALO - 1:POOL_HALO, :], 0.0)
    nrow = jnp.where(next_ok, un_ref[0:1, :], 0.0)
    rid = lax.broadcasted_iota(jnp.int32, (tm, 1), 0)
    up = jnp.where(rid == 0, prow, pltpu.roll(u, 1, 0))
    dn = jnp.where(rid == tm - 1, nrow, pltpu.roll(u, tm - 1, 0))
    mu = mu_ref[...]
    s = (1.0 - mu) * u + (0.5 * mu) * (up + dn)

    w0f, w0b, a0f, a0b = par_ref[0:1, :], par_ref[1:2, :], par_ref[2:3, :], par_ref[3:4, :]
    k_k, k_a, rkf, rkb = par_ref[4:5, :], par_ref[5:6, :], par_ref[6:7, :], par_ref[7:8, :]

    r = s[:, 0:512]
    k = s[:, 512:1024]
    v = s[:, 1024:1536]
    zw = _dot(jnp.tanh(s[:, 1536:1664]).astype(BF16), w2_ref[...])
    za = _dot(s[:, 1664:1792].astype(BF16), a2_ref[...])
    g = _dot(_sigmoid(s[:, 1792:1920]).astype(BF16), g2_ref[...])
    lw_f = -DECAY_SCALE * _sigmoid(w0f + zw[:, :512])
    lw_b = -DECAY_SCALE * _sigmoid(w0b + zw[:, 512:])
    a_f = _sigmoid(a0f + za[:, :512])
    a_b = _sigmoid(a0b + za[:, 512:])

    ones = _head_ones()
    kkr = k * k_k
    kk = kkr * lax.rsqrt(jnp.maximum(_headsum(kkr * kkr, ones), 1e-24))
    k_f = k * (1.0 + (a_f - 1.0) * k_a)
    k_b = k * (1.0 + (a_b - 1.0) * k_a)
    bonus_ref[...] = _headsum(r * (k_f * rkf + k_b * rkb), ones) * v
    g_ref[...] = g
    v_ref[...] = v.astype(BF16)

    rr = lax.broadcasted_iota(jnp.int32, (tm, tm), 0)
    cc = lax.broadcasted_iota(jnp.int32, (tm, tm), 1)
    same = (rr >> 6) == (cc >> 6)

    def visited_before(mask, lwd):
        m = jnp.logical_and(same, mask).astype(BF16)
        l1, l2, l3 = _split3(lwd)
        return _dot(m, l1) + (_dot(m, l2) + _dot(m, l3))

    def emit(excl, lwd, a_d, k_d, last_row, a_ref, r_ref, bt_ref, kt_ref, gam_ref):
        cum = excl + lwd
        gam_ref[0] = jnp.exp(jnp.concatenate(
            [cum[c * CHUNK + last_row:c * CHUNK + last_row + 1, :] for c in range(tm // CHUNK)], axis=0))
        e_neg = jnp.exp(-cum)
        a_ref[...] = (-kk * jnp.exp(excl)).astype(BF16)
        r_ref[...] = (r * jnp.exp(cum)).astype(BF16)
        bt_ref[...] = (kk * a_d * e_neg).astype(BF16)
        kt_ref[...] = (k_d * e_neg).astype(BF16)

    emit(visited_before(cc < rr, lw_f), lw_f, a_f, k_f, CHUNK - 1, af_ref, rf_ref, btf_ref, ktf_ref, gamf_ref)
    emit(visited_before(cc > rr, lw_b), lw_b, a_b, k_b, 0, ab_ref, rb_ref, btb_ref, ktb_ref, gamb_ref)


def _feat_call(u_shift, n_tok, n_x_tiles, mu, par, w2, a2, g2):
    tm = TM_FEAT
    nt = n_tok // tm
    hb = tm // POOL_HALO
    last_blk = u_shift.shape[0] // POOL_HALO - 1
    const = lambda i: (0, 0)
    tok = lambda i: (i, 0)
    seq_bf = jax.ShapeDtypeStruct((n_tok, D_RWKV), BF16)
    seq_f32 = jax.ShapeDtypeStruct((n_tok, D_RWKV), F32)
    gam = jax.ShapeDtypeStruct((nt, tm // CHUNK, D_RWKV), F32)
    seq_spec = pl.BlockSpec((tm, D_RWKV), tok)
    gam_spec = pl.BlockSpec((1, tm // CHUNK, D_RWKV), lambda i: (i, 0, 0))
    return pl.pallas_call(
        functools.partial(_feat_body, n_x_tiles=n_x_tiles),
        grid=(nt,),
        in_specs=[pl.BlockSpec((tm, D_SHIFT), tok),
                  pl.BlockSpec((POOL_HALO, D_SHIFT), lambda i: (jnp.maximum(i * hb - 1, 0), 0)),
                  pl.BlockSpec((POOL_HALO, D_SHIFT), lambda i: (jnp.minimum((i + 1) * hb, last_blk), 0)),
                  pl.BlockSpec((1, D_SHIFT), const),
                  pl.BlockSpec((8, D_RWKV), const),
                  pl.BlockSpec(w2.shape, const),
                  pl.BlockSpec(a2.shape, const),
                  pl.BlockSpec(g2.shape, const)],
        out_specs=[seq_spec] * 9 + [gam_spec, gam_spec, seq_spec, seq_spec],
        out_shape=[seq_bf] * 9 + [gam, gam, seq_f32, seq_f32],
        compiler_params=_cparams(),
        name="feat",
    )(u_shift, u_shift, u_shift, mu, par, w2, a2, g2)


def _blockdiag(x, head_of_lane):
    xb = x.astype(BF16)
    zero = jnp.zeros_like(xb)
    return jnp.concatenate([jnp.where(head_of_lane == h, xb, zero) for h in range(QUAD // HEAD)], axis=0)


def _unblock(x, head_of_lane):
    out = jnp.where(head_of_lane == 0, x[0:HEAD, :], 0.0)
    for h in range(1, QUAD // HEAD):
        out = jnp.where(head_of_lane == h, x[h * HEAD:(h + 1) * HEAD, :], out)
    return out


def _scan_chunks(chains):
    lane = lax.broadcasted_iota(jnp.int32, (HEAD, QUAD), 1)
    head_of_lane = lane >> 6
    s_idx = lane & (HEAD - 1)
    t_idx = lax.broadcasted_iota(jnp.int32, (HEAD, QUAD), 0)
    eye = s_idx == t_idx
    lower = s_idx < t_idx
    upper = s_idx > t_idx
    bd = functools.partial(_blockdiag, head_of_lane=head_of_lane)

    def same_block(shift):
        return (s_idx >> shift) == (t_idx >> shift)

    def each(fn):
        return [fn(c) for c in chains]

    def stage_masks(c):
        ll = _dot_nt(jnp.concatenate([c["a"], c["r"]], axis=0),
                     jnp.concatenate([bd(c["bt"]), bd(c["kt"])], axis=0))
        strict = upper if c["reverse"] else lower
        incl = jnp.logical_or(strict, eye)
        c["lab"] = jnp.where(strict, ll[:HEAD, :QUAD], 0.0)
        c["lak"] = jnp.where(strict, ll[:HEAD, QUAD:], 0.0)
        c["mrb"] = jnp.where(incl, ll[HEAD:, :QUAD], 0.0)
        c["mrk"] = jnp.where(incl, ll[HEAD:, QUAD:], 0.0)

    each(stage_masks)

    def stage_square(c):
        ld = jnp.where(same_block(4), c["lab"], 0.0)
        c["tm"] = jnp.where(eye, 1.0, 0.0) + ld
        c["xk"] = _dot(ld.astype(BF16), bd(ld))

    def stage_double(c):
        y = _dot(jnp.concatenate([c["tm"], c["xk"]], axis=0).astype(BF16), bd(c["xk"]))
        c["tm"] = c["tm"] + y[:HEAD]
        c["xk"] = y[HEAD:]

    def stage_last_power(c):
        c["tm"] = c["tm"] + _dot(c["tm"].astype(BF16), bd(c["xk"]))

    each(stage_square)
    each(stage_double)
    each(stage_double)
    each(stage_last_power)
    each(lambda c: c.update(vv=_dot(jnp.concatenate([c["lak"], c["mrk"]], axis=0).astype(BF16), bd(c["v"]))))
    for shift in (4, 5):
        joined = jnp.logical_and(same_block(shift + 1), jnp.logical_not(same_block(shift)))
        each(lambda c: c.update(te=_dot(c["tm"].astype(BF16), bd(jnp.where(joined, c["lab"], 0.0)))))
        each(lambda c: c.update(tm=c["tm"] + _dot(c["te"].astype(BF16), bd(c["tm"]))))

    each(lambda c: c.update(x1=_dot(c["tm"].astype(BF16),
                                    jnp.concatenate([bd(c["a"]), bd(c["vv"][:HEAD])], axis=1))))
    each(lambda c: c.update(x2=_dot(c["mrb"].astype(BF16),
                                    jnp.concatenate([bd(c["x1"][:, :QUAD]), bd(c["x1"][:, QUAD:])], axis=1))))

    def stage_state_terms(c):
        rhs = jnp.concatenate(
            [c["x1"].astype(BF16), jnp.concatenate([jnp.zeros((HEAD, QUAD), BF16), c["v"]], axis=1)], axis=0)
        hat = (jnp.concatenate([c["bt"], c["kt"]], axis=0).astype(F32) * c["gam"]).astype(BF16)
        pz = _dot_tn(hat, rhs)
        c["pq"] = _unblock(pz[:, :QUAD], head_of_lane) + jnp.where(eye, c["gam"], 0.0)
        c["zq"] = _unblock(pz[:, QUAD:], head_of_lane)
        c["qeff"] = c["r"].astype(F32) + c["x2"][:, :QUAD]
        c["yin"] = c["x2"][:, QUAD:] + c["vv"][HEAD:]

    each(stage_state_terms)

    outs, state = [], {}
    for c in chains:
        hst = state.get(c["key"], c["hst"])
        qp = _dot(jnp.concatenate([c["qeff"], c["pq"]], axis=0).astype(BF16), bd(hst))
        state[c["key"]] = qp[HEAD:] + c["zq"]
        outs.append(qp[:HEAD] + c["yin"])
    return outs, state


def _scan_body(af, rf, btf, ktf, vf, gamf, ab, rb, btb, ktb, vb, gamb, yf_ref, yb_ref, hf_ref, hb_ref):
    @pl.when(pl.program_id(0) == 0)
    def _():
        hf_ref[...] = jnp.zeros_like(hf_ref)
        hb_ref[...] = jnp.zeros_like(hb_ref)

    names = ("a", "r", "bt", "kt", "v")
    chains, sinks, state_refs = [], [], {}
    for ci in range(SCAN_CPS):
        for refs, y_ref, h_ref, reverse in (((af, rf, btf, ktf, vf, gamf), yf_ref, hf_ref, False),
                                            ((ab, rb, btb, ktb, vb, gamb), yb_ref, hb_ref, True)):
            chunk = SCAN_CPS - 1 - ci if reverse else ci
            rows = slice(chunk * CHUNK, (chunk + 1) * CHUNK)
            for q in range(D_RWKV // QUAD):
                sl = slice(q * QUAD, (q + 1) * QUAD)
                chain = {n: ref[rows, sl] for n, ref in zip(names, refs[:5])}
                chain.update(gam=refs[5][chunk, :, sl], hst=h_ref[:, sl], reverse=reverse, key=(reverse, q))
                chains.append(chain)
                sinks.append((y_ref, rows, sl))
                state_refs[(reverse, q)] = (h_ref, sl)
    outs, state = _scan_chunks(chains)
    for y, (y_ref, rows, sl) in zip(outs, sinks):
        y_ref[rows, sl] = y
    for key, (h_ref, sl) in state_refs.items():
        h_ref[:, sl] = state[key]


def _scan_call(feats, gamf, gamb, n_x_chunks, n_c_chunks):
    af, rf, btf, ktf, ab, rb, btb, ktb, v = feats
    assert n_x_chunks % SCAN_CPS == 0 and n_c_chunks % SCAN_CPS == 0
    nxb, ncb = n_x_chunks // SCAN_CPS, n_c_chunks // SCAN_CPS
    n = nxb + ncb
    fwd = lambda i: (jnp.where(i < ncb, nxb + i, i - ncb), 0)
    bwd = lambda i: (n - 1 - i, 0)
    fwd3 = lambda i: (jnp.where(i < ncb, nxb + i, i - ncb), 0, 0)
    bwd3 = lambda i: (n - 1 - i, 0, 0)
    blk = (SCAN_CPS * CHUNK, D_RWKV)
    in_specs = ([pl.BlockSpec(blk, fwd)] * 5 + [pl.BlockSpec((SCAN_CPS, 1, D_RWKV), fwd3)]
                + [pl.BlockSpec(blk, bwd)] * 5 + [pl.BlockSpec((SCAN_CPS, 1, D_RWKV), bwd3)])
    y = jax.ShapeDtypeStruct((n * SCAN_CPS * CHUNK, D_RWKV), F32)
    return pl.pallas_call(
        _scan_body,
        grid=(n,),
        in_specs=in_specs,
        out_specs=[pl.BlockSpec(blk, fwd), pl.BlockSpec(blk, bwd)],
        out_shape=[y, y],
        scratch_shapes=[pltpu.VMEM((HEAD, D_RWKV), F32), pltpu.VMEM((HEAD, D_RWKV), F32)],
        compiler_params=_cparams(),
        name="scan",
    )(af, rf, btf, ktf, v, gamf, ab, rb, btb, ktb, v, gamb)


def _mix_body(x_ref, rowtab_ref, coltab_ref, yf_ref, yb_ref, bonus_ref, g_ref, pm_ref, pp_ref, pn_ref,
              gn_ref, wpool_ref, pscale_ref, wout_ref, mod_ref, o_ref, slab_ref, *, n_x_tiles, seq_len):
    j = pl.program_id(0)
    tm, d = x_ref.shape
    hl = POOL_HALO
    slab_ref[0:hl, :] = jnp.where(j != 0, pp_ref[...], 0.0)
    slab_ref[hl:hl + tm, :] = pm_ref[...]
    slab_ref[hl + tm:2 * hl + tm, :] = jnp.where(j != n_x_tiles - 1, pn_ref[...], 0.0)
    tglob = j * tm + lax.broadcasted_iota(jnp.int32, (tm, 1), 0)
    pooled = []
    for gi, win in enumerate(POOL_WINDOWS):
        cols = slice(gi * POOL_GW, (gi + 1) * POOL_GW)
        acc = slab_ref[hl - win // 2:hl - win // 2 + tm, cols]
        for off in range(-win // 2 + 1, win // 2):
            acc = acc + slab_ref[hl + off:hl + off + tm, cols]
        lo = jnp.maximum(tglob - win // 2, 0)
        hi = jnp.minimum(tglob - win // 2 + win, seq_len)
        cnt = (hi - lo).astype(F32)
        pg = acc / cnt - slab_ref[hl:hl + tm, cols]
        pooled.append(_dot(pg.astype(BF16), wpool_ref[gi]))
    pool = jnp.concatenate(pooled, axis=1) * pscale_ref[...]

    ones = _head_ones()
    y = yf_ref[...] + yb_ref[...]
    yc = y - _headsum(y, ones) * (1.0 / HEAD)
    var = _headsum(yc * yc, ones) * (1.0 / HEAD)
    rw = (yc * lax.rsqrt(var + GN_EPS) * gn_ref[0:1, :] + gn_ref[1:2, :] + bonus_ref[...]) * g_ref[...]
    mix = _dot(jnp.concatenate([rw, pool], axis=1).astype(BF16), wout_ref[...])
    gt_a = mod_ref[0:1, 2 * d:3 * d]
    o_ref[...] = x_ref[...] + _pos_tile(rowtab_ref, coltab_ref) + gt_a * mix


def _mix_call(x2, rowtab, coltab, yf, yb, bonus, g, p, gn, w_pool_bf, pscale, w_out_bf, mod):
    s, d = x2.shape
    tm = TM_PROJ
    nx = s // tm
    hb = tm // POOL_HALO
    const = lambda j: (0, 0)
    tok = lambda j: (j, 0)
    seq_spec = pl.BlockSpec((tm, D_RWKV), tok)
    return pl.pallas_call(
        functools.partial(_mix_body, n_x_tiles=nx, seq_len=s),
        grid=(nx,),
        in_specs=[pl.BlockSpec((tm, d), tok),
                  pl.BlockSpec((tm // GRID_W, d // 2), tok),
                  pl.BlockSpec((tm, d // 2), const),
                  seq_spec, seq_spec, seq_spec, seq_spec, seq_spec,
                  pl.BlockSpec((POOL_HALO, D_POOL), lambda j: (jnp.maximum(j * hb - 1, 0), 0)),
                  pl.BlockSpec((POOL_HALO, D_POOL), lambda j: ((j + 1) * hb, 0)),
                  pl.BlockSpec((2, D_RWKV), const),
                  pl.BlockSpec(w_pool_bf.shape, lambda j: (0, 0, 0)),
                  pl.BlockSpec((1, D_POOL), const),
                  pl.BlockSpec(w_out_bf.shape, const),
                  pl.BlockSpec(mod.shape, const)],
        out_specs=pl.BlockSpec((tm, d), tok),
        out_shape=jax.ShapeDtypeStruct((s, d), F32),
        scratch_shapes=[pltpu.VMEM((tm + 2 * POOL_HALO, D_POOL), F32)],
        compiler_params=_cparams(),
        name="mix",
    )(x2, rowtab, coltab, yf, yb, bonus, g, p, p, p, gn, w_pool_bf, pscale, w_out_bf, mod)


def _to_token_tiles(ref, x):
    tm = x.shape[0]
    for j in range(x.shape[1] // LANES):
        ref[pl.ds(j, tm, stride=SUBLANES), :] = x[:, j * LANES:(j + 1) * LANES]


def _from_token_tiles(ref, tm):
    return jnp.concatenate([ref[pl.ds(j, tm, stride=SUBLANES), :] for j in range(SUBLANES)], axis=1)


def _router_body(x_ref, g2_ref, mod_ref, wr_ref, br_ref, h3_ref, eid_ref, gate_ref):
    tm, d = x_ref.shape
    x = x_ref[...]
    ms = jnp.mean(x * x, axis=-1, keepdims=True)
    xn = x * lax.rsqrt(ms + NORM_EPS) * g2_ref[...]
    h = xn * (1.0 + mod_ref[0:1, 4 * d:5 * d]) + mod_ref[0:1, 3 * d:4 * d]
    _to_token_tiles(h3_ref, h)

    wh, wl = _split2(wr_ref[...])
    hh, hl = _split2(h)
    logits = _dot_nt(wh, hh) + (_dot_nt(wh, hl) + _dot_nt(wl, hh)) + br_ref[...]
    gl = logits[0:N_GROUPS, :]
    gmax = jnp.max(gl, axis=0, keepdims=True)
    gidx = lax.broadcasted_iota(jnp.int32, gl.shape, 0).astype(F32)
    grp = jnp.min(jnp.where(gl == gmax, gidx, float(N_GROUPS)), axis=0, keepdims=True)
    p_grp = 1.0 / jnp.sum(jnp.exp(gl - gmax), axis=0, keepdims=True)
    sel = jnp.zeros((EXPERTS_PER_GROUP, tm), F32)
    for gi in range(N_GROUPS):
        lo = N_GROUPS + gi * EXPERTS_PER_GROUP
        sel = jnp.where(grp == float(gi), logits[lo:lo + EXPERTS_PER_GROUP, :], sel)
    eidx = lax.broadcasted_iota(jnp.int32, sel.shape, 0).astype(F32)
    top1 = jnp.max(sel, axis=0, keepdims=True)
    i1 = jnp.min(jnp.where(sel == top1, eidx, float(EXPERTS_PER_GROUP)), axis=0, keepdims=True)
    sel2 = jnp.where(eidx == i1, -jnp.inf, sel)
    top2 = jnp.max(sel2, axis=0, keepdims=True)
    i2 = jnp.min(jnp.where(sel2 == top2, eidx, float(EXPERTS_PER_GROUP)), axis=0, keepdims=True)
    e2 = jnp.exp(top2 - top1)
    inv = 1.0 / (1.0 + e2)
    zf = jnp.zeros((6, tm), F32)
    gate_ref[...] = jnp.concatenate([p_grp * inv, p_grp * (e2 * inv), zf], axis=0)
    base = grp * float(EXPERTS_PER_GROUP)
    eid_ref[...] = jnp.concatenate([base + i1, base + i2, zf], axis=0).astype(jnp.int32)


def _router_call(x_mid, g2n, mod, wr, br):
    s, d = x_mid.shape
    tm = TM_PROJ
    const = lambda j: (0, 0)
    return pl.pallas_call(
        _router_body,
        grid=(s // tm,),
        in_specs=[pl.BlockSpec((tm, d), lambda j: (j, 0)),
                  pl.BlockSpec((1, d), const),
                  pl.BlockSpec(mod.shape, const),
                  pl.BlockSpec(wr.shape, const),
                  pl.BlockSpec(br.shape, const)],
        out_specs=[pl.BlockSpec((tm * SUBLANES, LANES), lambda j: (j, 0)),
                   pl.BlockSpec((8, tm), lambda j: (0, j)),
                   pl.BlockSpec((8, tm), lambda j: (0, j))],
        out_shape=[jax.ShapeDtypeStruct((s * SUBLANES, LANES), F32),
                   jax.ShapeDtypeStruct((8, s), jnp.int32),
                   jax.ShapeDtypeStruct((8, s), F32)],
        compiler_params=_cparams(),
        name="router",
    )(x_mid, g2n, mod, wr, br)


def _tile_copy(src_ref, src_row, dst_ref, dst_row, sem):
    return pltpu.make_async_copy(src_ref.at[pl.ds(pl.multiple_of(src_row, SUBLANES), SUBLANES)],
                                 dst_ref.at[pl.ds(pl.multiple_of(dst_row, SUBLANES), SUBLANES)], sem)


def _experts_body(blk_e_ref, n_used_ref, n_prev_ref, src_ref, src_next_ref, dst_prev_ref, h3_ref,
                  wg_ref, wu_ref, wd_ref, y3_ref, xbuf0_ref, xbuf1_ref, obuf0_ref, obuf1_ref, gsem_ref, ssem_ref):
    del blk_e_ref
    i = pl.program_id(0)
    n_used = n_used_ref[0]
    n_prev = n_prev_ref[i]
    xbufs = (xbuf0_ref, xbuf1_ref)
    obufs = (obuf0_ref, obuf1_ref)

    def gather(idx_ref, slot, start):
        for r in range(MOE_BM):
            cp = _tile_copy(h3_ref, idx_ref[0, 0, r] if start else 0, xbufs[slot], r * SUBLANES, gsem_ref.at[slot])
            cp.start() if start else cp.wait()

    def scatter_row(slot, start, r):
        cp = _tile_copy(obufs[slot], r * SUBLANES, y3_ref, dst_prev_ref[0, 0, r] if start else 0, ssem_ref.at[0])
        cp.start() if start else cp.wait()

    def scatter(slot, start):
        @pl.when(n_prev == MOE_BM)
        def _():
            for r in range(MOE_BM):
                scatter_row(slot, start, r)

        @pl.when(n_prev < MOE_BM)
        def _():
            def body(r, c):
                scatter_row(slot, start, r)
                return c

            lax.fori_loop(0, n_prev, body, 0)

    @pl.when(i == 0)
    def _():
        gather(src_ref, 0, True)

    for slot in (0, 1):
        @pl.when(jnp.logical_and(i < n_used, (i & 1) == slot))
        def _():
            gather(src_ref, slot, False)
            gather(src_next_ref, 1 - slot, True)
            scatter(1 - slot, True)
            xb = _from_token_tiles(xbufs[slot], MOE_BM).astype(BF16)
            gate = _dot(xb, wg_ref[0].astype(BF16))
            upp = _dot(xb, wu_ref[0].astype(BF16))
            hid = (gate * _sigmoid(gate)) * upp
            _to_token_tiles(obufs[slot], _dot(hid.astype(BF16), wd_ref[0].astype(BF16)))
            scatter(1 - slot, False)

        @pl.when(jnp.logical_and(i == n_used, (i & 1) == slot))
        def _():
            gather(src_ref, slot, False)
            scatter(1 - slot, True)
            scatter(1 - slot, False)


def _experts_call(blk_expert, n_used, n_prev, src3, dst_prev3, h3, wg, wu, wd, n_rows_out):
    nb = src3.shape[0]
    d = wg.shape[1]
    last = nb - 1
    cur = lambda i, be, nu, npv: (jnp.minimum(i, last), 0, 0)
    nxt = lambda i, be, nu, npv: (jnp.minimum(i + 1, last), 0, 0)
    wmap = lambda i, be, nu, npv: (be[jnp.minimum(i, last)], 0, 0)
    idx_blk = (1, 1, MOE_BM)
    buf = pltpu.VMEM((MOE_BM * SUBLANES, LANES), F32)
    grid_spec = pltpu.PrefetchScalarGridSpec(
        num_scalar_prefetch=3,
        grid=(nb + 1,),
        in_specs=[pl.BlockSpec(idx_blk, cur, memory_space=pltpu.SMEM),
                  pl.BlockSpec(idx_blk, nxt, memory_space=pltpu.SMEM),
                  pl.BlockSpec(idx_blk, lambda i, be, nu, npv: (i, 0, 0), memory_space=pltpu.SMEM),
                  pl.BlockSpec(memory_space=pl.ANY),
                  pl.BlockSpec((1, d, D_EXPERT), wmap),
                  pl.BlockSpec((1, d, D_EXPERT), wmap),
                  pl.BlockSpec((1, D_EXPERT, d), wmap)],
        out_specs=pl.BlockSpec(memory_space=pl.ANY),
        scratch_shapes=[buf, buf, buf, buf, pltpu.SemaphoreType.DMA((2,)), pltpu.SemaphoreType.DMA((1,))],
    )
    return pl.pallas_call(
        _experts_body,
        grid_spec=grid_spec,
        out_shape=jax.ShapeDtypeStruct((n_rows_out * SUBLANES, LANES), F32),
        compiler_params=_cparams(),
        name="experts",
    )(blk_expert, n_used, n_prev, src3, src3, dst_prev3, h3, wg, wu, wd)


def _combine_body(x_ref, y0_ref, y1_ref, gate_ref, mod_ref, fg_ref, o_ref):
    tm, d = x_ref.shape
    y = gate_ref[:, 0:1] * _from_token_tiles(y0_ref, tm) + gate_ref[:, 1:2] * _from_token_tiles(y1_ref, tm)
    x = x_ref[...] + mod_ref[0:1, 5 * d:6 * d] * y
    ms = jnp.mean(x * x, axis=-1, keepdims=True)
    o_ref[...] = x * lax.rsqrt(ms + NORM_EPS) * fg_ref[...]


def _combine_call(x_mid, y3, gcol, mod, fg):
    s, d = x_mid.shape
    tm = TM_PROJ
    nt = s // tm
    const = lambda j: (0, 0)
    return pl.pallas_call(
        _combine_body,
        grid=(nt,),
        in_specs=[pl.BlockSpec((tm, d), lambda j: (j, 0)),
                  pl.BlockSpec((tm * SUBLANES, LANES), lambda j: (j, 0)),
                  pl.BlockSpec((tm * SUBLANES, LANES), lambda j: (j + nt, 0)),
                  pl.BlockSpec((tm, 2), lambda j: (j, 0)),
                  pl.BlockSpec(mod.shape, const),
                  pl.BlockSpec((1, d), const)],
        out_specs=pl.BlockSpec((tm, d), lambda j: (j, 0)),
        out_shape=jax.ShapeDtypeStruct((s, d), F32),
        compiler_params=_cparams(),
        name="combine",
    )(x_mid, y3, y3, gcol, mod, fg)


def _blockdiag2(wf, wb):
    z = jnp.zeros_like(wf)
    return jnp.concatenate([jnp.concatenate([wf, z], axis=1), jnp.concatenate([z, wb], axis=1)], axis=0)


def _routing_plan(eids, n_tok):
    m = 2 * n_tok
    experts = jnp.arange(N_EXPERTS, dtype=jnp.int32)
    flat_e = eids[:2].reshape(m)
    order = jnp.argsort(flat_e, stable=True).astype(jnp.int32)
    counts = jnp.sum((flat_e[:, None] == experts[None, :]).astype(jnp.int32), axis=0)
    starts = jnp.cumsum(counts) - counts
    pcounts = (counts + MOE_BM - 1) // MOE_BM * MOE_BM
    pends = jnp.cumsum(pcounts)
    pstarts = pends - pcounts
    n_blocks = (m + N_EXPERTS * (MOE_BM - 1) + MOE_BM - 1) // MOE_BM
    slot = jnp.arange(n_blocks * MOE_BM, dtype=jnp.int32)
    slot_e = jnp.minimum(jnp.sum((slot[:, None] >= pends[None, :]).astype(jnp.int32), axis=1), N_EXPERTS - 1)
    sel = (slot_e[:, None] == experts[None, :]).astype(jnp.int32)
    local = slot - jnp.sum(sel * pstarts[None, :], axis=1)
    valid = local < jnp.sum(sel * counts[None, :], axis=1)
    pair = order[jnp.clip(jnp.sum(sel * starts[None, :], axis=1) + local, 0, m - 1)]
    src_tok = jnp.where(valid, jnp.where(pair >= n_tok, pair - n_tok, pair), 0)
    dst_row = jnp.where(valid, pair, 0)
    blk_expert = slot_e[::MOE_BM]
    n_used = (pends[-1:] // MOE_BM).astype(jnp.int32)
    shape3 = (n_blocks, 1, MOE_BM)
    n_valid = jnp.sum(valid.reshape(n_blocks, MOE_BM).astype(jnp.int32), axis=1)
    n_prev = jnp.concatenate([jnp.zeros((1,), jnp.int32), n_valid])
    dst_prev = jnp.concatenate([jnp.zeros((MOE_BM,), jnp.int32), dst_row])
    return ((src_tok * SUBLANES).reshape(shape3), (dst_prev * SUBLANES).reshape(n_blocks + 1, 1, MOE_BM),
            blk_expert, n_used, n_prev)


def kernel(x, c, ctx, c_ctx, norm1_g, w_mod, b_mod, w_in, mu_shift, w0_f, w2_f, w0_b, w2_b, a0_f, a2_f, a0_b, a2_b, g2, k_k, k_a, r_k_f, r_k_b, gn_w, gn_b, w_pool, pool_scale, w_out, norm2_g, w_router_grp, b_router_grp, w_router_exp, b_router_exp, w_gate, w_up, w_down, final_g):
    b, s, d = x.shape
    n_ctx = ctx.shape[1]
    assert b == 1 and c.shape[0] == 1 and w_mod.shape[0] == 1
    assert s % TM_PROJ == 0 and n_ctx % CHUNK == 0 and n_ctx == TM_FEAT and d == 2 * D_RWKV
    x2 = x[0]
    rows = s // GRID_W

    quarter = d // 4
    freq = POS_THETA ** (-jnp.arange(quarter, dtype=F32) / quarter)
    rarg = jnp.arange(rows, dtype=F32)[:, None] * freq
    carg = jnp.arange(GRID_W, dtype=F32)[:, None] * freq
    rowtab = jnp.concatenate([jnp.sin(rarg), jnp.cos(rarg)], axis=-1)
    coltab = jnp.tile(jnp.concatenate([jnp.sin(carg), jnp.cos(carg)], axis=-1), (TM_PROJ // GRID_W, 1))

    cs = jnp.concatenate([c, c_ctx[None, :], jnp.zeros((6, d), F32)], axis=0)
    mod = _mod_call(cs, w_mod[0], b_mod)

    ctx_pad = jnp.concatenate([ctx[0], jnp.zeros((TM_PROJ - n_ctx, d), F32)], axis=0)
    u_shift, p = _inproj_call(x2, ctx_pad, rowtab, coltab, norm1_g, mod, w_in[0].astype(BF16))

    par = jnp.concatenate([w0_f, w0_b, a0_f, a0_b, k_k, k_a, r_k_f, r_k_b], axis=0)
    n_tok = s + n_ctx
    outs = _feat_call(u_shift, n_tok, s // TM_FEAT, mu_shift,
                      par, _blockdiag2(w2_f[0], w2_b[0]).astype(BF16),
                      _blockdiag2(a2_f[0], a2_b[0]).astype(BF16), g2[0].astype(BF16))
    feats, gamf, gamb, bonus, g = outs[:9], outs[9], outs[10], outs[11], outs[12]
    n_chunks = n_tok // CHUNK
    yf, yb = _scan_call(feats, gamf.reshape(n_chunks, 1, D_RWKV), gamb.reshape(n_chunks, 1, D_RWKV),
                        s // CHUNK, n_ctx // CHUNK)

    gn = jnp.concatenate([gn_w, gn_b], axis=0)
    x_mid = _mix_call(x2, rowtab, coltab, yf, yb, bonus, g, p, gn, w_pool[0].astype(BF16), pool_scale,
                      w_out[0].astype(BF16), mod)

    wr = jnp.concatenate([w_router_grp[0].T, w_router_exp[0].T,
                          jnp.zeros((ROUTER_ROWS - N_GROUPS - N_EXPERTS, d), F32)], axis=0)
    br = jnp.concatenate([b_router_grp[0], b_router_exp[0],
                          jnp.zeros((ROUTER_ROWS - N_GROUPS - N_EXPERTS,), F32)])[:, None]
    h3, eids, gates = _router_call(x_mid, norm2_g, mod, wr, br)

    src3, dst_prev3, blk_expert, n_used, n_prev = _routing_plan(eids, s)
    y3 = _experts_call(blk_expert, n_used, n_prev, src3, dst_prev3, h3, w_gate[0], w_up[0], w_down[0], 2 * s)
    out = _combine_call(x_mid, y3, gates[:2].T, mod, final_g[None, :])
    return out[None]
```

```python
import functools

import jax
import jax.numpy as jnp
from jax import lax
from jax.experimental import pallas as pl
from jax.experimental.pallas import tpu as pltpu

F32 = jnp.float32
BF16 = jnp.bfloat16

GRID_W = 64
SUBLANES, LANES = 8, 128
HEAD = 64
N_HEADS = 8
D_RWKV = HEAD * N_HEADS
D_POOL = 512
POOL_WINDOWS = (2, 4, 8, 16)
POOL_GW = 128
POOL_HALO = 8
D_SHIFT = 1920
N_GROUPS = 4
EXPERTS_PER_GROUP = 8
N_EXPERTS = 32
D_EXPERT = 512
NORM_EPS = 1e-6
GN_EPS = 64e-5
POS_THETA = 10000.0
DECAY_SCALE = 0.6065306597

CHUNK = 64
QUAD = 4 * HEAD
SCAN_CPS = 4
TM_PROJ = 512
TM_FEAT = 256
FEAT_SHIFT_BLOCKS = 3
MOE_BM = 256
ROUTER_ROWS = 48
V7X_VMEM_LIMIT = 56 * 1024 * 1024


def _cparams(n_axes=1, vmem=V7X_VMEM_LIMIT):
    return pltpu.CompilerParams(dimension_semantics=("arbitrary",) * n_axes,
                                vmem_limit_bytes=vmem)


def _dot(a, b):
    return jnp.dot(a, b, preferred_element_type=F32)


def _dot_nt(a, b):
    return lax.dot_general(a, b, (((1,), (1,)), ((), ())), preferred_element_type=F32)


def _dot_tn(a, b):
    return lax.dot_general(a, b, (((0,), (0,)), ((), ())), preferred_element_type=F32)


def _split2(x):
    hi = x.astype(BF16)
    lo = (x - hi.astype(F32)).astype(BF16)
    return hi, lo


def _dot3(a, b):
    ah, al = _split2(a)
    bh, bl = _split2(b)
    return _dot(ah, bh) + (_dot(ah, bl) + _dot(al, bh))


def _sigmoid(x):
    return 0.5 * jnp.tanh(0.5 * x) + 0.5


def _head_ones():
    r = lax.broadcasted_iota(jnp.int32, (D_RWKV, D_RWKV), 0) >> 6
    c = lax.broadcasted_iota(jnp.int32, (D_RWKV, D_RWKV), 1) >> 6
    return (r == c).astype(BF16)


def _headsum(x, ones):
    hi, lo = _split2(x)
    return _dot(hi, ones) + _dot(lo, ones)


def _mod_body(cs_ref, w_ref, b_ref, o_ref):
    a = cs_ref[...]
    a = a * _sigmoid(a)
    o_ref[...] = _dot3(a, w_ref[...]) + b_ref[...]


def _mod_call(cs, w_mod, b_mod):
    d, n = w_mod.shape
    tn = 512
    return pl.pallas_call(
        _mod_body,
        grid=(n // tn,),
        in_specs=[pl.BlockSpec((8, d), lambda j: (0, 0)),
                  pl.BlockSpec((d, tn), lambda j: (0, j)),
                  pl.BlockSpec((1, tn), lambda j: (0, j))],
        out_specs=pl.BlockSpec((8, tn), lambda j: (0, j)),
        out_shape=jax.ShapeDtypeStruct((8, n), F32),
        compiler_params=_cparams(),
        name="mod",
    )(cs, w_mod, b_mod)


def _pos_tile(rowtab_ref, coltab_ref):
    rt = rowtab_ref[...]
    rowpart = jnp.concatenate(
        [jnp.broadcast_to(rt[r:r + 1, :], (GRID_W, rt.shape[1])) for r in range(TM_PROJ // GRID_W)], axis=0)
    return jnp.concatenate([rowpart, coltab_ref[...]], axis=1)


def _features(u, prow, nrow, mu, par_ref, w2_ref, a2_ref, g2_ref, rows, chunk0, outs, consts):
    (af_ref, rf_ref, btf_ref, ktf_ref, ab_ref, rb_ref, btb_ref, ktb_ref, v_ref,
     gamf_ref, gamb_ref, bonus_ref, g_ref) = outs
    tm = u.shape[0]
    rid = lax.broadcasted_iota(jnp.int32, (SUBLANES, 1), 0)

    def shifted(lo, hi):
        uc, m = u[:, lo:hi], mu[:, lo:hi]
        up = pltpu.roll(uc, 1, 0)
        up = jnp.concatenate([jnp.where(rid == 0, prow[:, lo:hi], up[:SUBLANES]), up[SUBLANES:]], axis=0)
        dn = pltpu.roll(uc, tm - 1, 0)
        dn = jnp.concatenate(
            [dn[:-SUBLANES], jnp.where(rid == SUBLANES - 1, nrow[:, lo:hi], dn[-SUBLANES:])], axis=0)
        return (1.0 - m) * uc + (0.5 * m) * (up + dn)

    w0f, w0b, a0f, a0b = par_ref[0:1, :], par_ref[1:2, :], par_ref[2:3, :], par_ref[3:4, :]
    k_k, k_a, rkf, rkb = par_ref[4:5, :], par_ref[5:6, :], par_ref[6:7, :], par_ref[7:8, :]

    r = shifted(0, 512)
    yield
    k = shifted(512, 1024)
    yield
    v = shifted(1024, 1536)
    v_ref[rows, :] = v.astype(BF16)
    yield
    lora = shifted(1536, D_SHIFT)
    zw = _dot(jnp.tanh(lora[:, 0:128]).astype(BF16), w2_ref[...])
    za = _dot(lora[:, 128:256].astype(BF16), a2_ref[...])
    g_ref[rows, :] = _dot(_sigmoid(lora[:, 256:384]).astype(BF16), g2_ref[...])
    yield
    lw_f = -DECAY_SCALE * _sigmoid(w0f + zw[:, :512])
    lw_b = -DECAY_SCALE * _sigmoid(w0b + zw[:, 512:])
    a_f = _sigmoid(a0f + za[:, :512])
    a_b = _sigmoid(a0b + za[:, 512:])
    yield

    ones, earlier, later = consts
    kkr = k * k_k
    kk = kkr * lax.rsqrt(jnp.maximum(_headsum(kkr * kkr, ones), 1e-24))
    yield
    k_f = k * (1.0 + (a_f - 1.0) * k_a)
    k_b = k * (1.0 + (a_b - 1.0) * k_a)
    bonus_ref[rows, :] = _headsum(r * (k_f * rkf + k_b * rkb), ones) * v
    yield

    n_ch = tm // CHUNK

    def visited_before(m, lwd):
        l1, l2 = _split2(lwd)
        return _dot(m, l1) + _dot(m, l2)

    def emit(excl, lwd, a_d, k_d, last_row, a_ref, r_ref, bt_ref, kt_ref, gam_ref):
        cum = excl + lwd
        gam_ref[0, chunk0:chunk0 + n_ch, :] = jnp.exp(jnp.concatenate(
            [cum[c * CHUNK + last_row:c * CHUNK + last_row + 1, :] for c in range(n_ch)], axis=0))
        a_ref[rows, :] = (-kk * jnp.exp(excl)).astype(BF16)
        r_ref[rows, :] = (r * jnp.exp(cum)).astype(BF16)
        yield
        e_neg = jnp.exp(-cum)
        bt_ref[rows, :] = (kk * a_d * e_neg).astype(BF16)
        kt_ref[rows, :] = (k_d * e_neg).astype(BF16)
        yield

    excl_f = visited_before(earlier, lw_f)
    yield
    yield from emit(excl_f, lw_f, a_f, k_f, CHUNK - 1, af_ref, rf_ref, btf_ref, ktf_ref, gamf_ref)
    excl_b = visited_before(later, lw_b)
    yield
    yield from emit(excl_b, lw_b, a_b, k_b, 0, ab_ref, rb_ref, btb_ref, ktb_ref, gamb_ref)


def _front_body(x_ref, ctx_ref, rowtab_ref, coltab_ref, g1_ref, mod_ref, w_ref, mu_ref, par_ref, w2_ref, a2_ref,
                g2_ref, p_ref, af_ref, rf_ref, btf_ref, ktf_ref, ab_ref, rb_ref, btb_ref, ktb_ref, v_ref,
                gamf_ref, gamb_ref, bonus_ref, g_ref, u_ref, unew_ref, carry_ref, *, n_x_tiles):
    i = pl.program_id(0)
    d = x_ref.shape[1]
    d_in = w_ref.shape[1]
    half = TM_FEAT
    last = carry_ref.shape[0] - 1

    @pl.when(i == 0)
    def _():
        u_ref[...] = jnp.zeros_like(u_ref)
        carry_ref[...] = jnp.zeros_like(carry_ref)

    is_ctx = i >= n_x_tiles
    xin = jnp.where(is_ctx, ctx_ref[...], x_ref[...] + _pos_tile(rowtab_ref, coltab_ref))
    ms = jnp.mean(xin * xin, axis=-1, keepdims=True)
    xn = xin * lax.rsqrt(ms + NORM_EPS) * g1_ref[...]
    sh = jnp.where(is_ctx, mod_ref[1:2, 0:d], mod_ref[0:1, 0:d])
    sc = jnp.where(is_ctx, mod_ref[1:2, d:2 * d], mod_ref[0:1, d:2 * d])
    hb = (xn * (1.0 + sc) + sh).astype(BF16)
    pending = [(slice(r0, r0 + half), lo, min(lo + QUAD, d_in))
               for lo in range(0, d_in, QUAD) for r0 in range(0, hb.shape[0], half)]

    def project_next():
        if not pending:
            return
        rs, lo, hi = pending.pop(0)
        uc = _dot(hb[rs, :], w_ref[:, lo:hi])
        if lo < D_SHIFT:
            unew_ref[rs, lo:min(hi, D_SHIFT)] = uc[:, :min(hi, D_SHIFT) - lo]
        if hi > D_SHIFT:
            p_ref[rs, max(lo, D_SHIFT) - D_SHIFT:hi - D_SHIFT] = uc[:, max(lo, D_SHIFT) - lo:]

    outs = (af_ref, rf_ref, btf_ref, ktf_ref, ab_ref, rb_ref, btb_ref, ktb_ref, v_ref,
            gamf_ref, gamb_ref, bonus_ref, g_ref)
    mu = mu_ref[...]
    ctx_block = 2 * n_x_tiles
    blocks = ((2 * i - 3, carry_ref[SUBLANES:, :], carry_ref[SUBLANES - 1:SUBLANES, :], u_ref[0:1, :]),
              (2 * i - 2, u_ref[0:half, :], carry_ref[last:last + 1, :], u_ref[half:half + 1, :]))
    rr = lax.broadcasted_iota(jnp.int32, (half, half), 0)
    cc = lax.broadcasted_iota(jnp.int32, (half, half), 1)
    same = (rr >> 6) == (cc >> 6)
    consts = (_head_ones(), jnp.logical_and(same, cc < rr).astype(BF16),
              jnp.logical_and(same, cc > rr).astype(BF16))
    project_next()
    for hi, (blk, ub, prow, nrow) in enumerate(blocks):
        prev_ok = jnp.logical_and(blk != 0, blk != ctx_block)
        next_ok = jnp.logical_and(blk != ctx_block - 1, blk != ctx_block)
        for _ in _features(ub, jnp.where(prev_ok, prow, 0.0), jnp.where(next_ok, nrow, 0.0), mu, par_ref, w2_ref,
                           a2_ref, g2_ref, slice(hi * half, (hi + 1) * half), hi * (half // CHUNK), outs, consts):
            project_next()
    while pending:
        project_next()
    carry_ref[...] = u_ref[half - SUBLANES:, :]
    u_ref[...] = unew_ref[...]


def _front_call(x2, ctx_pad, rowtab, coltab, g1, mod, w_in_bf, mu, par, w2, a2, g2):
    s, d = x2.shape
    tm = TM_PROJ
    nx = s // tm
    d_in = w_in_bf.shape[1]
    n_steps = nx + 2
    xmap = lambda i: (jnp.minimum(i, nx - 1), 0)
    const = lambda i: (0, 0)
    step = lambda i: (i, 0)
    seq_bf = jax.ShapeDtypeStruct((n_steps * tm, D_RWKV), BF16)
    seq_f32 = jax.ShapeDtypeStruct((n_steps * tm, D_RWKV), F32)
    gam = jax.ShapeDtypeStruct((n_steps, tm // CHUNK, D_RWKV), F32)
    seq_spec = pl.BlockSpec((tm, D_RWKV), step)
    gam_spec = pl.BlockSpec((1, tm // CHUNK, D_RWKV), lambda i: (i, 0, 0))
    return pl.pallas_call(
        functools.partial(_front_body, n_x_tiles=nx),
        grid=(n_steps,),
        in_specs=[pl.BlockSpec((tm, d), xmap),
                  pl.BlockSpec((tm, d), const),
                  pl.BlockSpec((tm // GRID_W, d // 2), xmap),
                  pl.BlockSpec((tm, d // 2), const),
                  pl.BlockSpec((1, d), const),
                  pl.BlockSpec(mod.shape, const),
                  pl.BlockSpec((d, d_in), const),
                  pl.BlockSpec((1, D_SHIFT), const),
                  pl.BlockSpec((8, D_RWKV), const),
                  pl.BlockSpec(w2.shape, const),
                  pl.BlockSpec(a2.shape, const),
                  pl.BlockSpec(g2.shape, const)],
        out_specs=([pl.BlockSpec((tm, D_POOL), lambda i: (jnp.minimum(i, nx), 0))] + [seq_spec] * 9
                   + [gam_spec, gam_spec, seq_spec, seq_spec]),
        out_shape=([jax.ShapeDtypeStruct(((nx + 1) * tm, D_POOL), F32)] + [seq_bf] * 9
                   + [gam, gam, seq_f32, seq_f32]),
        scratch_shapes=[pltpu.VMEM((tm, D_SHIFT), F32), pltpu.VMEM((tm, D_SHIFT), F32),
                        pltpu.VMEM((TM_FEAT + SUBLANES, D_SHIFT), F32)],
        compiler_params=_cparams(),
        name="front",
    )(x2, ctx_pad, rowtab, coltab, g1, mod, w_in_bf, mu, par, w2, a2, g2)


def _blockdiag(x, head_of_lane):
    xb = x.astype(BF16)
    zero = jnp.zeros_like(xb)
    return jnp.concatenate([jnp.where(head_of_lane == h, xb, zero) for h in range(QUAD // HEAD)], axis=0)


def _unblock(x, head_of_lane):
    out = jnp.where(head_of_lane == 0, x[0:HEAD, :], 0.0)
    for h in range(1, QUAD // HEAD):
        out = jnp.where(head_of_lane == h, x[h * HEAD:(h + 1) * HEAD, :], out)
    return out


def _scan_chunks(chains):
    lane = lax.broadcasted_iota(jnp.int32, (HEAD, QUAD), 1)
    head_of_lane = lane >> 6
    s_idx = lane & (HEAD - 1)
    t_idx = lax.broadcasted_iota(jnp.int32, (HEAD, QUAD), 0)
    eye = s_idx == t_idx
    lower = s_idx < t_idx
    upper = s_idx > t_idx
    bd = functools.partial(_blockdiag, head_of_lane=head_of_lane)

    def same_block(shift):
        return (s_idx >> shift) == (t_idx >> shift)

    def each(fn):
        return [fn(c) for c in chains]

    def stage_masks(c):
        ll = _dot_nt(jnp.concatenate([c["a"], c["r"]], axis=0),
                     jnp.concatenate([bd(c["bt"]), bd(c["kt"])], axis=0))
        strict = upper if c["reverse"] else lower
        incl = jnp.logical_or(strict, eye)
        c["lab"] = jnp.where(strict, ll[:HEAD, :QUAD], 0.0)
        c["lak"] = jnp.where(strict, ll[:HEAD, QUAD:], 0.0)
        c["mrb"] = jnp.where(incl, ll[HEAD:, :QUAD], 0.0)
        c["mrk"] = jnp.where(incl, ll[HEAD:, QUAD:], 0.0)

    each(stage_masks)

    def stage_square(c):
        ld = jnp.where(same_block(4), c["lab"], 0.0)
        c["tm"] = jnp.where(eye, 1.0, 0.0) + ld
        c["xk"] = _dot(ld.astype(BF16), bd(ld))

    def stage_double(c):
        y = _dot(jnp.concatenate([c["tm"], c["xk"]], axis=0).astype(BF16), bd(c["xk"]))
        c["tm"] = c["tm"] + y[:HEAD]
        c["xk"] = y[HEAD:]

    def stage_last_power(c):
        c["tm"] = c["tm"] + _dot(c["tm"].astype(BF16), bd(c["xk"]))

    each(stage_square)
    each(stage_double)
    each(stage_double)
    each(stage_last_power)
    each(lambda c: c.update(vv=_dot(jnp.concatenate([c["lak"], c["mrk"]], axis=0).astype(BF16), bd(c["v"]))))
    for shift in (4, 5):
        joined = jnp.logical_and(same_block(shift + 1), jnp.logical_not(same_block(shift)))
        each(lambda c: c.update(te=_dot(c["tm"].astype(BF16), bd(jnp.where(joined, c["lab"], 0.0)))))
        each(lambda c: c.update(tm=c["tm"] + _dot(c["te"].astype(BF16), bd(c["tm"]))))

    each(lambda c: c.update(x1=_dot(c["tm"].astype(BF16),
                                    jnp.concatenate([bd(c["a"]), bd(c["vv"][:HEAD])], axis=1))))
    each(lambda c: c.update(x2=_dot(c["mrb"].astype(BF16),
                                    jnp.concatenate([bd(c["x1"][:, :QUAD]), bd(c["x1"][:, QUAD:])], axis=1))))

    def stage_state_terms(c):
        rhs = jnp.concatenate(
            [c["x1"].astype(BF16), jnp.concatenate([jnp.zeros((HEAD, QUAD), BF16), c["v"]], axis=1)], axis=0)
        hat = (jnp.concatenate([c["bt"], c["kt"]], axis=0).astype(F32) * c["gam"]).astype(BF16)
        pz = _dot_tn(hat, rhs)
        c["pq"] = _unblock(pz[:, :QUAD], head_of_lane) + jnp.where(eye, c["gam"], 0.0)
        c["zq"] = _unblock(pz[:, QUAD:], head_of_lane)
        c["qeff"] = c["r"].astype(F32) + c["x2"][:, :QUAD]
        c["yin"] = c["x2"][:, QUAD:] + c["vv"][HEAD:]

    each(stage_state_terms)

    outs, state = [], {}
    for c in chains:
        hst = state.get(c["key"], c["hst"])
        qp = _dot(jnp.concatenate([c["qeff"], c["pq"]], axis=0).astype(BF16), bd(hst))
        state[c["key"]] = qp[HEAD:] + c["zq"]
        outs.append(qp[:HEAD] + c["yin"])
    return outs, state


def _scan_body(af, rf, btf, ktf, vf, gamf, ab, rb, btb, ktb, vb, gamb, yf_ref, yb_ref, hf_ref, hb_ref):
    @pl.when(pl.program_id(0) == 0)
    def _():
        hf_ref[...] = jnp.zeros_like(hf_ref)
        hb_ref[...] = jnp.zeros_like(hb_ref)

    names = ("a", "r", "bt", "kt", "v")
    chains, sinks, state_refs = [], [], {}
    for ci in range(SCAN_CPS):
        for refs, y_ref, h_ref, reverse in (((af, rf, btf, ktf, vf, gamf), yf_ref, hf_ref, False),
                                            ((ab, rb, btb, ktb, vb, gamb), yb_ref, hb_ref, True)):
            chunk = SCAN_CPS - 1 - ci if reverse else ci
            rows = slice(chunk * CHUNK, (chunk + 1) * CHUNK)
            for q in range(D_RWKV // QUAD):
                sl = slice(q * QUAD, (q + 1) * QUAD)
                chain = {n: ref[rows, sl] for n, ref in zip(names, refs[:5])}
                chain.update(gam=refs[5][chunk, :, sl], hst=h_ref[:, sl], reverse=reverse, key=(reverse, q))
                chains.append(chain)
                sinks.append((y_ref, rows, sl))
                state_refs[(reverse, q)] = (h_ref, sl)
    outs, state = _scan_chunks(chains)
    for y, (y_ref, rows, sl) in zip(outs, sinks):
        y_ref[rows, sl] = y
    for key, (h_ref, sl) in state_refs.items():
        h_ref[:, sl] = state[key]


def _scan_call(feats, gamf, gamb, n_x_chunks, n_c_chunks):
    af, rf, btf, ktf, ab, rb, btb, ktb, v = feats
    assert n_x_chunks % SCAN_CPS == 0 and n_c_chunks % SCAN_CPS == 0
    nxb, ncb = n_x_chunks // SCAN_CPS, n_c_chunks // SCAN_CPS
    n = nxb + ncb
    assert SCAN_CPS * CHUNK == TM_FEAT
    fwd_blk = lambda i: jnp.where(i < ncb, nxb + i, i - ncb)
    bwd_blk = lambda i: n - 1 - i
    fwd_y = lambda i: (fwd_blk(i), 0)
    bwd_y = lambda i: (bwd_blk(i), 0)
    fwd = lambda i: (fwd_blk(i) + FEAT_SHIFT_BLOCKS, 0)
    bwd = lambda i: (bwd_blk(i) + FEAT_SHIFT_BLOCKS, 0)
    fwd3 = lambda i: (fwd_blk(i) + FEAT_SHIFT_BLOCKS, 0, 0)
    bwd3 = lambda i: (bwd_blk(i) + FEAT_SHIFT_BLOCKS, 0, 0)
    blk = (SCAN_CPS * CHUNK, D_RWKV)
    in_specs = ([pl.BlockSpec(blk, fwd)] * 5 + [pl.BlockSpec((SCAN_CPS, 1, D_RWKV), fwd3)]
                + [pl.BlockSpec(blk, bwd)] * 5 + [pl.BlockSpec((SCAN_CPS, 1, D_RWKV), bwd3)])
    y = jax.ShapeDtypeStruct((n * SCAN_CPS * CHUNK, D_RWKV), F32)
    return pl.pallas_call(
        _scan_body,
        grid=(n,),
        in_specs=in_specs,
        out_specs=[pl.BlockSpec(blk, fwd_y), pl.BlockSpec(blk, bwd_y)],
        out_shape=[y, y],
        scratch_shapes=[pltpu.VMEM((HEAD, D_RWKV), F32), pltpu.VMEM((HEAD, D_RWKV), F32)],
        compiler_params=_cparams(),
        name="scan",
    )(af, rf, btf, ktf, v, gamf, ab, rb, btb, ktb, v, gamb)


def _mix_body(x_ref, rowtab_ref, coltab_ref, yf_ref, yb_ref, bonus0_ref, bonus1_ref, g0_ref, g1_ref,
              pm_ref, pp_ref, pn_ref,
              gn_ref, wpool_ref, pscale_ref, wout_ref, mod_ref, o_ref, slab_ref, *, n_x_tiles, seq_len):
    j = pl.program_id(0)
    tm, d = x_ref.shape
    hl = POOL_HALO
    slab_ref[0:hl, :] = jnp.where(j != 0, pp_ref[...], 0.0)
    slab_ref[hl:hl + tm, :] = pm_ref[...]
    slab_ref[hl + tm:2 * hl + tm, :] = jnp.where(j != n_x_tiles - 1, pn_ref[...], 0.0)
    tglob = j * tm + lax.broadcasted_iota(jnp.int32, (tm, 1), 0)
    pooled = []
    for gi, win in enumerate(POOL_WINDOWS):
        cols = slice(gi * POOL_GW, (gi + 1) * POOL_GW)
        acc = slab_ref[hl - win // 2:hl - win // 2 + tm, cols]
        for off in range(-win // 2 + 1, win // 2):
            acc = acc + slab_ref[hl + off:hl + off + tm, cols]
        lo = jnp.maximum(tglob - win // 2, 0)
        hi = jnp.minimum(tglob - win // 2 + win, seq_len)
        cnt = (hi - lo).astype(F32)
        pg = acc / cnt - slab_ref[hl:hl + tm, cols]
        pooled.append(_dot(pg.astype(BF16), wpool_ref[gi]))
    pool = jnp.concatenate(pooled, axis=1) * pscale_ref[...]

    ones = _head_ones()
    y = yf_ref[...] + yb_ref[...]
    yc = y - _headsum(y, ones) * (1.0 / HEAD)
    var = _headsum(yc * yc, ones) * (1.0 / HEAD)
    bonus = jnp.concatenate([bonus0_ref[...], bonus1_ref[...]], axis=0)
    gate = jnp.concatenate([g0_ref[...], g1_ref[...]], axis=0)
    rw = (yc * lax.rsqrt(var + GN_EPS) * gn_ref[0:1, :] + gn_ref[1:2, :] + bonus) * gate
    mix = _dot(jnp.concatenate([rw, pool], axis=1).astype(BF16), wout_ref[...])
    gt_a = mod_ref[0:1, 2 * d:3 * d]
    o_ref[...] = x_ref[...] + _pos_tile(rowtab_ref, coltab_ref) + gt_a * mix


def _mix_call(x2, rowtab, coltab, yf, yb, bonus, g, p, gn, w_pool_bf, pscale, w_out_bf, mod):
    s, d = x2.shape
    tm = TM_PROJ
    nx = s // tm
    hb = tm // POOL_HALO
    const = lambda j: (0, 0)
    tok = lambda j: (j, 0)
    seq_spec = pl.BlockSpec((tm, D_RWKV), tok)
    assert tm == 2 * TM_FEAT
    feat_lo = pl.BlockSpec((TM_FEAT, D_RWKV), lambda j: (2 * j + FEAT_SHIFT_BLOCKS, 0))
    feat_hi = pl.BlockSpec((TM_FEAT, D_RWKV), lambda j: (2 * j + FEAT_SHIFT_BLOCKS + 1, 0))
    return pl.pallas_call(
        functools.partial(_mix_body, n_x_tiles=nx, seq_len=s),
        grid=(nx,),
        in_specs=[pl.BlockSpec((tm, d), tok),
                  pl.BlockSpec((tm // GRID_W, d // 2), tok),
                  pl.BlockSpec((tm, d // 2), const),
                  seq_spec, seq_spec, feat_lo, feat_hi, feat_lo, feat_hi, seq_spec,
                  pl.BlockSpec((POOL_HALO, D_POOL), lambda j: (jnp.maximum(j * hb - 1, 0), 0)),
                  pl.BlockSpec((POOL_HALO, D_POOL), lambda j: ((j + 1) * hb, 0)),
                  pl.BlockSpec((2, D_RWKV), const),
                  pl.BlockSpec(w_pool_bf.shape, lambda j: (0, 0, 0)),
                  pl.BlockSpec((1, D_POOL), const),
                  pl.BlockSpec(w_out_bf.shape, const),
                  pl.BlockSpec(mod.shape, const)],
        out_specs=pl.BlockSpec((tm, d), tok),
        out_shape=jax.ShapeDtypeStruct((s, d), F32),
        scratch_shapes=[pltpu.VMEM((tm + 2 * POOL_HALO, D_POOL), F32)],
        compiler_params=_cparams(),
        name="mix",
    )(x2, rowtab, coltab, yf, yb, bonus, bonus, g, g, p, p, p, gn, w_pool_bf, pscale, w_out_bf, mod)


def _to_token_tiles(ref, x):
    tm = x.shape[0]
    for j in range(x.shape[1] // LANES):
        ref[pl.ds(j, tm, stride=SUBLANES), :] = x[:, j * LANES:(j + 1) * LANES]


def _from_token_tiles(ref, tm):
    return jnp.concatenate([ref[pl.ds(j, tm, stride=SUBLANES), :] for j in range(SUBLANES)], axis=1)


def _router_body(x_ref, g2_ref, mod_ref, wr_ref, br_ref, h3_ref, eid_ref, gate_ref):
    tm, d = x_ref.shape
    x = x_ref[...]
    ms = jnp.mean(x * x, axis=-1, keepdims=True)
    xn = x * lax.rsqrt(ms + NORM_EPS) * g2_ref[...]
    h = xn * (1.0 + mod_ref[0:1, 4 * d:5 * d]) + mod_ref[0:1, 3 * d:4 * d]
    _to_token_tiles(h3_ref, h)

    wh, wl = _split2(wr_ref[...])
    hh, hl = _split2(h)
    logits = _dot_nt(wh, hh) + (_dot_nt(wh, hl) + _dot_nt(wl, hh)) + br_ref[...]
    gl = logits[0:N_GROUPS, :]
    gmax = jnp.max(gl, axis=0, keepdims=True)
    gidx = lax.broadcasted_iota(jnp.int32, gl.shape, 0).astype(F32)
    grp = jnp.min(jnp.where(gl == gmax, gidx, float(N_GROUPS)), axis=0, keepdims=True)
    p_grp = 1.0 / jnp.sum(jnp.exp(gl - gmax), axis=0, keepdims=True)
    sel = jnp.zeros((EXPERTS_PER_GROUP, tm), F32)
    for gi in range(N_GROUPS):
        lo = N_GROUPS + gi * EXPERTS_PER_GROUP
        sel = jnp.where(grp == float(gi), logits[lo:lo + EXPERTS_PER_GROUP, :], sel)
    eidx = lax.broadcasted_iota(jnp.int32, sel.shape, 0).astype(F32)
    top1 = jnp.max(sel, axis=0, keepdims=True)
    i1 = jnp.min(jnp.where(sel == top1, eidx, float(EXPERTS_PER_GROUP)), axis=0, keepdims=True)
    sel2 = jnp.where(eidx == i1, -jnp.inf, sel)
    top2 = jnp.max(sel2, axis=0, keepdims=True)
    i2 = jnp.min(jnp.where(sel2 == top2, eidx, float(EXPERTS_PER_GROUP)), axis=0, keepdims=True)
    e2 = jnp.exp(top2 - top1)
    inv = 1.0 / (1.0 + e2)
    zf = jnp.zeros((6, tm), F32)
    gate_ref[...] = jnp.concatenate([p_grp * inv, p_grp * (e2 * inv), zf], axis=0)
    base = grp * float(EXPERTS_PER_GROUP)
    eid_ref[...] = jnp.concatenate([base + i1, base + i2, zf], axis=0).astype(jnp.int32)


def _router_call(x_mid, g2n, mod, wr, br):
    s, d = x_mid.shape
    tm = TM_PROJ
    const = lambda j: (0, 0)
    return pl.pallas_call(
        _router_body,
        grid=(s // tm,),
        in_specs=[pl.BlockSpec((tm, d), lambda j: (j, 0)),
                  pl.BlockSpec((1, d), const),
                  pl.BlockSpec(mod.shape, const),
                  pl.BlockSpec(wr.shape, const),
                  pl.BlockSpec(br.shape, const)],
        out_specs=[pl.BlockSpec((tm * SUBLANES, LANES), lambda j: (j, 0)),
                   pl.BlockSpec((8, tm), lambda j: (0, j)),
                   pl.BlockSpec((8, tm), lambda j: (0, j))],
        out_shape=[jax.ShapeDtypeStruct((s * SUBLANES, LANES), F32),
                   jax.ShapeDtypeStruct((8, s), jnp.int32),
                   jax.ShapeDtypeStruct((8, s), F32)],
        compiler_params=_cparams(),
        name="router",
    )(x_mid, g2n, mod, wr, br)


def _tile_copy(src_ref, src_row, dst_ref, dst_row, sem):
    return pltpu.make_async_copy(src_ref.at[pl.ds(pl.multiple_of(src_row, SUBLANES), SUBLANES)],
                                 dst_ref.at[pl.ds(pl.multiple_of(dst_row, SUBLANES), SUBLANES)], sem)


def _experts_body(blk_e_ref, n_used_ref, n_prev_ref, src_ref, src_next_ref, dst_prev_ref, h3_ref,
                  wg_ref, wu_ref, wd_ref, y3_ref, xbuf0_ref, xbuf1_ref, obuf0_ref, obuf1_ref, gsem_ref, ssem_ref):
    del blk_e_ref
    i = pl.program_id(0)
    n_used = n_used_ref[0]
    n_prev = n_prev_ref[i]
    xbufs = (xbuf0_ref, xbuf1_ref)
    obufs = (obuf0_ref, obuf1_ref)

    def gather(idx_ref, slot, start):
        for r in range(MOE_BM):
            cp = _tile_copy(h3_ref, idx_ref[0, 0, r] if start else 0, xbufs[slot], r * SUBLANES, gsem_ref.at[slot])
            cp.start() if start else cp.wait()

    def scatter_row(slot, start, r):
        cp = _tile_copy(obufs[slot], r * SUBLANES, y3_ref, dst_prev_ref[0, 0, r] if start else 0, ssem_ref.at[0])
        cp.start() if start else cp.wait()

    def scatter(slot, start):
        @pl.when(n_prev == MOE_BM)
        def _():
            for r in range(MOE_BM):
                scatter_row(slot, start, r)

        @pl.when(n_prev < MOE_BM)
        def _():
            def body(r, c):
                scatter_row(slot, start, r)
                return c

            lax.fori_loop(0, n_prev, body, 0)

    @pl.when(i == 0)
    def _():
        gather(src_ref, 0, True)

    for slot in (0, 1):
        @pl.when(jnp.logical_and(i < n_used, (i & 1) == slot))
        def _():
            gather(src_ref, slot, False)
            gather(src_next_ref, 1 - slot, True)
            scatter(1 - slot, True)
            xb = _from_token_tiles(xbufs[slot], MOE_BM).astype(BF16)
            gate = _dot(xb, wg_ref[0].astype(BF16))
            upp = _dot(xb, wu_ref[0].astype(BF16))
            hid = (gate * _sigmoid(gate)) * upp
            _to_token_tiles(obufs[slot], _dot(hid.astype(BF16), wd_ref[0].astype(BF16)))
            scatter(1 - slot, False)

        @pl.when(jnp.logical_and(i == n_used, (i & 1) == slot))
        def _():
            gather(src_ref, slot, False)
            scatter(1 - slot, True)
            scatter(1 - slot, False)


def _experts_call(blk_expert, n_used, n_prev, src3, dst_prev3, h3, wg, wu, wd, n_rows_out):
    nb = src3.shape[0]
    d = wg.shape[1]
    last = nb - 1
    cur = lambda i, be, nu, npv: (jnp.minimum(i, last), 0, 0)
    nxt = lambda i, be, nu, npv: (jnp.minimum(i + 1, last), 0, 0)
    wmap = lambda i, be, nu, npv: (be[jnp.minimum(i, last)], 0, 0)
    idx_blk = (1, 1, MOE_BM)
    buf = pltpu.VMEM((MOE_BM * SUBLANES, LANES), F32)
    grid_spec = pltpu.PrefetchScalarGridSpec(
        num_scalar_prefetch=3,
        grid=(nb + 1,),
        in_specs=[pl.BlockSpec(idx_blk, cur, memory_space=pltpu.SMEM),
                  pl.BlockSpec(idx_blk, nxt, memory_space=pltpu.SMEM),
                  pl.BlockSpec(idx_blk, lambda i, be, nu, npv: (i, 0, 0), memory_space=pltpu.SMEM),
                  pl.BlockSpec(memory_space=pl.ANY),
                  pl.BlockSpec((1, d, D_EXPERT), wmap),
                  pl.BlockSpec((1, d, D_EXPERT), wmap),
                  pl.BlockSpec((1, D_EXPERT, d), wmap)],
        out_specs=pl.BlockSpec(memory_space=pl.ANY),
        scratch_shapes=[buf, buf, buf, buf, pltpu.SemaphoreType.DMA((2,)), pltpu.SemaphoreType.DMA((1,))],
    )
    return pl.pallas_call(
        _experts_body,
        grid_spec=grid_spec,
        out_shape=jax.ShapeDtypeStruct((n_rows_out * SUBLANES, LANES), F32),
        compiler_params=_cparams(),
        name="experts",
    )(blk_expert, n_used, n_prev, src3, src3, dst_prev3, h3, wg, wu, wd)


def _combine_body(x_ref, y0_ref, y1_ref, gate_ref, mod_ref, fg_ref, o_ref):
    tm, d = x_ref.shape
    y = gate_ref[:, 0:1] * _from_token_tiles(y0_ref, tm) + gate_ref[:, 1:2] * _from_token_tiles(y1_ref, tm)
    x = x_ref[...] + mod_ref[0:1, 5 * d:6 * d] * y
    ms = jnp.mean(x * x, axis=-1, keepdims=True)
    o_ref[...] = x * lax.rsqrt(ms + NORM_EPS) * fg_ref[...]


def _combine_call(x_mid, y3, gcol, mod, fg):
    s, d = x_mid.shape
    tm = TM_PROJ
    nt = s // tm
    const = lambda j: (0, 0)
    return pl.pallas_call(
        _combine_body,
        grid=(nt,),
        in_specs=[pl.BlockSpec((tm, d), lambda j: (j, 0)),
                  pl.BlockSpec((tm * SUBLANES, LANES), lambda j: (j, 0)),
                  pl.BlockSpec((tm * SUBLANES, LANES), lambda j: (j + nt, 0)),
                  pl.BlockSpec((tm, 2), lambda j: (j, 0)),
                  pl.BlockSpec(mod.shape, const),
                  pl.BlockSpec((1, d), const)],
        out_specs=pl.BlockSpec((tm, d), lambda j: (j, 0)),
        out_shape=jax.ShapeDtypeStruct((s, d), F32),
        compiler_params=_cparams(),
        name="combine",
    )(x_mid, y3, y3, gcol, mod, fg)


def _blockdiag2(wf, wb):
    z = jnp.zeros_like(wf)
    return jnp.concatenate([jnp.concatenate([wf, z], axis=1), jnp.concatenate([z, wb], axis=1)], axis=0)


def _routing_plan(eids, n_tok):
    m = 2 * n_tok
    experts = jnp.arange(N_EXPERTS, dtype=jnp.int32)
    flat_e = eids[:2].reshape(m)
    order = jnp.argsort(flat_e, stable=True).astype(jnp.int32)
    counts = jnp.sum((flat_e[:, None] == experts[None, :]).astype(jnp.int32), axis=0)
    starts = jnp.cumsum(counts) - counts
    pcounts = (counts + MOE_BM - 1) // MOE_BM * MOE_BM
    pends = jnp.cumsum(pcounts)
    pstarts = pends - pcounts
    n_blocks = (m + N_EXPERTS * (MOE_BM - 1) + MOE_BM - 1) // MOE_BM
    slot = jnp.arange(n_blocks * MOE_BM, dtype=jnp.int32)
    slot_e = jnp.minimum(jnp.sum((slot[:, None] >= pends[None, :]).astype(jnp.int32), axis=1), N_EXPERTS - 1)
    sel = (slot_e[:, None] == experts[None, :]).astype(jnp.int32)
    local = slot - jnp.sum(sel * pstarts[None, :], axis=1)
    valid = local < jnp.sum(sel * counts[None, :], axis=1)
    pair = order[jnp.clip(jnp.sum(sel * starts[None, :], axis=1) + local, 0, m - 1)]
    src_tok = jnp.where(valid, jnp.where(pair >= n_tok, pair - n_tok, pair), 0)
    dst_row = jnp.where(valid, pair, 0)
    blk_expert = slot_e[::MOE_BM]
    n_used = (pends[-1:] // MOE_BM).astype(jnp.int32)
    shape3 = (n_blocks, 1, MOE_BM)
    n_valid = jnp.sum(valid.reshape(n_blocks, MOE_BM).astype(jnp.int32), axis=1)
    n_prev = jnp.concatenate([jnp.zeros((1,), jnp.int32), n_valid])
    dst_prev = jnp.concatenate([jnp.zeros((MOE_BM,), jnp.int32), dst_row])
    return ((src_tok * SUBLANES).reshape(shape3), (dst_prev * SUBLANES).reshape(n_blocks + 1, 1, MOE_BM),
            blk_expert, n_used, n_prev)


def kernel(x, c, ctx, c_ctx, norm1_g, w_mod, b_mod, w_in, mu_shift, w0_f, w2_f, w0_b, w2_b, a0_f, a2_f, a0_b, a2_b, g2, k_k, k_a, r_k_f, r_k_b, gn_w, gn_b, w_pool, pool_scale, w_out, norm2_g, w_router_grp, b_router_grp, w_router_exp, b_router_exp, w_gate, w_up, w_down, final_g):
    b, s, d = x.shape
    n_ctx = ctx.shape[1]
    assert b == 1 and c.shape[0] == 1 and w_mod.shape[0] == 1
    assert s % TM_PROJ == 0 and n_ctx % CHUNK == 0 and n_ctx == TM_FEAT and d == 2 * D_RWKV
    x2 = x[0]
    rows = s // GRID_W

    quarter = d // 4
    freq = POS_THETA ** (-jnp.arange(quarter, dtype=F32) / quarter)
    rarg = jnp.arange(rows, dtype=F32)[:, None] * freq
    carg = jnp.arange(GRID_W, dtype=F32)[:, None] * freq
    rowtab = jnp.concatenate([jnp.sin(rarg), jnp.cos(rarg)], axis=-1)
    coltab = jnp.tile(jnp.concatenate([jnp.sin(carg), jnp.cos(carg)], axis=-1), (TM_PROJ // GRID_W, 1))

    cs = jnp.concatenate([c, c_ctx[None, :], jnp.zeros((6, d), F32)], axis=0)
    mod = _mod_call(cs, w_mod[0], b_mod)

    ctx_pad = jnp.concatenate([ctx[0], jnp.zeros((TM_PROJ - n_ctx, d), F32)], axis=0)
    par = jnp.concatenate([w0_f, w0_b, a0_f, a0_b, k_k, k_a, r_k_f, r_k_b], axis=0)
    outs = _front_call(x2, ctx_pad, rowtab, coltab, norm1_g, mod, w_in[0].astype(BF16), mu_shift, par,
                       _blockdiag2(w2_f[0], w2_b[0]).astype(BF16), _blockdiag2(a2_f[0], a2_b[0]).astype(BF16),
                       g2[0].astype(BF16))
    p, feats, gamf, gamb, bonus, g = outs[0], outs[1:10], outs[10], outs[11], outs[12], outs[13]
    yf, yb = _scan_call(feats, gamf.reshape(-1, 1, D_RWKV), gamb.reshape(-1, 1, D_RWKV),
                        s // CHUNK, n_ctx // CHUNK)

    gn = jnp.concatenate([gn_w, gn_b], axis=0)
    x_mid = _mix_call(x2, rowtab, coltab, yf, yb, bonus, g, p, gn, w_pool[0].astype(BF16), pool_scale,
                      w_out[0].astype(BF16), mod)

    wr = jnp.concatenate([w_router_grp[0].T, w_router_exp[0].T,
                          jnp.zeros((ROUTER_ROWS - N_GROUPS - N_EXPERTS, d), F32)], axis=0)
    br = jnp.concatenate([b_router_grp[0], b_router_exp[0],
                          jnp.zeros((ROUTER_ROWS - N_GROUPS - N_EXPERTS,), F32)])[:, None]
    h3, eids, gates = _router_call(x_mid, norm2_g, mod, wr, br)

    src3, dst_prev3, blk_expert, n_used, n_prev = _routing_plan(eids, s)
    y3 = _experts_call(blk_expert, n_used, n_prev, src3, dst_prev3, h3, w_gate[0], w_up[0], w_down[0], 2 * s)
    out = _combine_call(x_mid, y3, gates[:2].T, mod, final_g[None, :])
    return out[None]
```

```python
import functools

import jax
import jax.numpy as jnp
from jax import lax
from jax.experimental import pallas as pl
from jax.experimental.pallas import tpu as pltpu
from jax.experimental.pallas import tpu_sc as plsc

F32 = jnp.float32
BF16 = jnp.bfloat16

GRID_W = 64
SUBLANES, LANES = 8, 128
HEAD = 64
N_HEADS = 8
D_RWKV = HEAD * N_HEADS
D_POOL = 512
POOL_WINDOWS = (2, 4, 8, 16)
POOL_GW = 128
POOL_HALO = 8
D_SHIFT = 1920
N_GROUPS = 4
EXPERTS_PER_GROUP = 8
N_EXPERTS = 32
D_EXPERT = 512
NORM_EPS = 1e-6
GN_EPS = 64e-5
POS_THETA = 10000.0
DECAY_SCALE = 0.6065306597

CHUNK = 64
QUAD = 4 * HEAD
SCAN_CPS = 4
TM_PROJ = 512
TM_FEAT = 256
FEAT_SHIFT_BLOCKS = 3
MOE_BM = 256
ROUTER_ROWS = 48
V7X_VMEM_LIMIT = 56 * 1024 * 1024
V7X_SC_CORES, V7X_SC_SUBCORES = 2, 16
SC_GATHER_ROWS = 32


def _cparams(n_axes=1, vmem=V7X_VMEM_LIMIT):
    return pltpu.CompilerParams(dimension_semantics=("arbitrary",) * n_axes,
                                vmem_limit_bytes=vmem)


def _dot(a, b):
    return jnp.dot(a, b, preferred_element_type=F32)


def _dot_nt(a, b):
    return lax.dot_general(a, b, (((1,), (1,)), ((), ())), preferred_element_type=F32)


def _dot_tn(a, b):
    return lax.dot_general(a, b, (((0,), (0,)), ((), ())), preferred_element_type=F32)


def _split2(x):
    hi = x.astype(BF16)
    lo = (x - hi.astype(F32)).astype(BF16)
    return hi, lo


def _dot3(a, b):
    ah, al = _split2(a)
    bh, bl = _split2(b)
    return _dot(ah, bh) + (_dot(ah, bl) + _dot(al, bh))


def _sigmoid(x):
    return 0.5 * jnp.tanh(0.5 * x) + 0.5


def _head_ones():
    r = lax.broadcasted_iota(jnp.int32, (D_RWKV, D_RWKV), 0) >> 6
    c = lax.broadcasted_iota(jnp.int32, (D_RWKV, D_RWKV), 1) >> 6
    return (r == c).astype(BF16)


def _headsum(x, ones):
    hi, lo = _split2(x)
    return _dot(hi, ones) + _dot(lo, ones)


def _mod_body(cs_ref, w_ref, b_ref, o_ref):
    a = cs_ref[...]
    a = a * _sigmoid(a)
    o_ref[...] = _dot3(a, w_ref[...]) + b_ref[...]


def _mod_call(cs, w_mod, b_mod):
    d, n = w_mod.shape
    tn = 512
    return pl.pallas_call(
        _mod_body,
        grid=(n // tn,),
        in_specs=[pl.BlockSpec((8, d), lambda j: (0, 0)),
                  pl.BlockSpec((d, tn), lambda j: (0, j)),
                  pl.BlockSpec((1, tn), lambda j: (0, j))],
        out_specs=pl.BlockSpec((8, tn), lambda j: (0, j)),
        out_shape=jax.ShapeDtypeStruct((8, n), F32),
        compiler_params=_cparams(),
        name="mod",
    )(cs, w_mod, b_mod)


def _pos_tile(rowtab_ref, coltab_ref):
    rt = rowtab_ref[...]
    rowpart = jnp.concatenate(
        [jnp.broadcast_to(rt[r:r + 1, :], (GRID_W, rt.shape[1])) for r in range(TM_PROJ // GRID_W)], axis=0)
    return jnp.concatenate([rowpart, coltab_ref[...]], axis=1)


def _features(u, prow, nrow, mu, par_ref, w2_ref, a2_ref, g2_ref, rows, chunk0, outs, consts):
    (af_ref, rf_ref, btf_ref, ktf_ref, ab_ref, rb_ref, btb_ref, ktb_ref, v_ref,
     gamf_ref, gamb_ref, bonus_ref, g_ref) = outs
    tm = u.shape[0]
    rid = lax.broadcasted_iota(jnp.int32, (SUBLANES, 1), 0)

    def shifted(lo, hi):
        uc, m = u[:, lo:hi], mu[:, lo:hi]
        up = pltpu.roll(uc, 1, 0)
        up = jnp.concatenate([jnp.where(rid == 0, prow[:, lo:hi], up[:SUBLANES]), up[SUBLANES:]], axis=0)
        dn = pltpu.roll(uc, tm - 1, 0)
        dn = jnp.concatenate(
            [dn[:-SUBLANES], jnp.where(rid == SUBLANES - 1, nrow[:, lo:hi], dn[-SUBLANES:])], axis=0)
        return (1.0 - m) * uc + (0.5 * m) * (up + dn)

    w0f, w0b, a0f, a0b = par_ref[0:1, :], par_ref[1:2, :], par_ref[2:3, :], par_ref[3:4, :]
    k_k, k_a, rkf, rkb = par_ref[4:5, :], par_ref[5:6, :], par_ref[6:7, :], par_ref[7:8, :]

    r = shifted(0, 512)
    yield
    k = shifted(512, 1024)
    yield
    v = shifted(1024, 1536)
    v_ref[rows, :] = v.astype(BF16)
    yield
    lora = shifted(1536, D_SHIFT)
    zw = _dot(jnp.tanh(lora[:, 0:128]).astype(BF16), w2_ref[...])
    za = _dot(lora[:, 128:256].astype(BF16), a2_ref[...])
    g_ref[rows, :] = _dot(_sigmoid(lora[:, 256:384]).astype(BF16), g2_ref[...])
    yield
    lw_f = -DECAY_SCALE * _sigmoid(w0f + zw[:, :512])
    lw_b = -DECAY_SCALE * _sigmoid(w0b + zw[:, 512:])
    a_f = _sigmoid(a0f + za[:, :512])
    a_b = _sigmoid(a0b + za[:, 512:])
    yield

    ones, earlier, later = consts
    kkr = k * k_k
    kk = kkr * lax.rsqrt(jnp.maximum(_headsum(kkr * kkr, ones), 1e-24))
    yield
    k_f = k * (1.0 + (a_f - 1.0) * k_a)
    k_b = k * (1.0 + (a_b - 1.0) * k_a)
    bonus_ref[rows, :] = _headsum(r * (k_f * rkf + k_b * rkb), ones) * v
    yield

    n_ch = tm // CHUNK

    def visited_before(m, lwd):
        l1, l2 = _split2(lwd)
        return _dot(m, l1) + _dot(m, l2)

    def emit(excl, lwd, a_d, k_d, last_row, a_ref, r_ref, bt_ref, kt_ref, gam_ref):
        cum = excl + lwd
        gam_ref[0, chunk0:chunk0 + n_ch, :] = jnp.exp(jnp.concatenate(
            [cum[c * CHUNK + last_row:c * CHUNK + last_row + 1, :] for c in range(n_ch)], axis=0))
        a_ref[rows, :] = (-kk * jnp.exp(excl)).astype(BF16)
        r_ref[rows, :] = (r * jnp.exp(cum)).astype(BF16)
        yield
        e_neg = jnp.exp(-cum)
        bt_ref[rows, :] = (kk * a_d * e_neg).astype(BF16)
        kt_ref[rows, :] = (k_d * e_neg).astype(BF16)
        yield

    excl_f = visited_before(earlier, lw_f)
    yield
    yield from emit(excl_f, lw_f, a_f, k_f, CHUNK - 1, af_ref, rf_ref, btf_ref, ktf_ref, gamf_ref)
    excl_b = visited_before(later, lw_b)
    yield
    yield from emit(excl_b, lw_b, a_b, k_b, 0, ab_ref, rb_ref, btb_ref, ktb_ref, gamb_ref)


def _front_body(x_ref, ctx_ref, rowtab_ref, coltab_ref, g1_ref, mod_ref, w_ref, mu_ref, par_ref, w2_ref, a2_ref,
                g2_ref, p_ref, af_ref, rf_ref, btf_ref, ktf_ref, ab_ref, rb_ref, btb_ref, ktb_ref, v_ref,
                gamf_ref, gamb_ref, bonus_ref, g_ref, u_ref, unew_ref, carry_ref, *, n_x_tiles):
    i = pl.program_id(0)
    d = x_ref.shape[1]
    d_in = w_ref.shape[1]
    half = TM_FEAT
    last = carry_ref.shape[0] - 1

    @pl.when(i == 0)
    def _():
        u_ref[...] = jnp.zeros_like(u_ref)
        carry_ref[...] = jnp.zeros_like(carry_ref)

    is_ctx = i >= n_x_tiles
    xin = jnp.where(is_ctx, ctx_ref[...], x_ref[...] + _pos_tile(rowtab_ref, coltab_ref))
    ms = jnp.mean(xin * xin, axis=-1, keepdims=True)
    xn = xin * lax.rsqrt(ms + NORM_EPS) * g1_ref[...]
    sh = jnp.where(is_ctx, mod_ref[1:2, 0:d], mod_ref[0:1, 0:d])
    sc = jnp.where(is_ctx, mod_ref[1:2, d:2 * d], mod_ref[0:1, d:2 * d])
    hb = (xn * (1.0 + sc) + sh).astype(BF16)
    pending = [(slice(r0, r0 + half), lo, min(lo + QUAD, d_in))
               for lo in range(0, d_in, QUAD) for r0 in range(0, hb.shape[0], half)]

    def project_next():
        if not pending:
            return
        rs, lo, hi = pending.pop(0)
        uc = _dot(hb[rs, :], w_ref[:, lo:hi])
        if lo < D_SHIFT:
            unew_ref[rs, lo:min(hi, D_SHIFT)] = uc[:, :min(hi, D_SHIFT) - lo]
        if hi > D_SHIFT:
            p_ref[rs, max(lo, D_SHIFT) - D_SHIFT:hi - D_SHIFT] = uc[:, max(lo, D_SHIFT) - lo:]

    outs = (af_ref, rf_ref, btf_ref, ktf_ref, ab_ref, rb_ref, btb_ref, ktb_ref, v_ref,
            gamf_ref, gamb_ref, bonus_ref, g_ref)
    mu = mu_ref[...]
    ctx_block = 2 * n_x_tiles
    blocks = ((2 * i - 3, carry_ref[SUBLANES:, :], carry_ref[SUBLANES - 1:SUBLANES, :], u_ref[0:1, :]),
              (2 * i - 2, u_ref[0:half, :], carry_ref[last:last + 1, :], u_ref[half:half + 1, :]))
    rr = lax.broadcasted_iota(jnp.int32, (half, half), 0)
    cc = lax.broadcasted_iota(jnp.int32, (half, half), 1)
    same = (rr >> 6) == (cc >> 6)
    consts = (_head_ones(), jnp.logical_and(same, cc < rr).astype(BF16),
              jnp.logical_and(same, cc > rr).astype(BF16))
    project_next()
    for hi, (blk, ub, prow, nrow) in enumerate(blocks):
        prev_ok = jnp.logical_and(blk != 0, blk != ctx_block)
        next_ok = jnp.logical_and(blk != ctx_block - 1, blk != ctx_block)
        for _ in _features(ub, jnp.where(prev_ok, prow, 0.0), jnp.where(next_ok, nrow, 0.0), mu, par_ref, w2_ref,
                           a2_ref, g2_ref, slice(hi * half, (hi + 1) * half), hi * (half // CHUNK), outs, consts):
            project_next()
    while pending:
        project_next()
    carry_ref[...] = u_ref[half - SUBLANES:, :]
    u_ref[...] = unew_ref[...]


def _front_call(x2, ctx_pad, rowtab, coltab, g1, mod, w_in_bf, mu, par, w2, a2, g2):
    s, d = x2.shape
    tm = TM_PROJ
    nx = s // tm
    d_in = w_in_bf.shape[1]
    n_steps = nx + 2
    xmap = lambda i: (jnp.minimum(i, nx - 1), 0)
    const = lambda i: (0, 0)
    step = lambda i: (i, 0)
    seq_bf = jax.ShapeDtypeStruct((n_steps * tm, D_RWKV), BF16)
    seq_f32 = jax.ShapeDtypeStruct((n_steps * tm, D_RWKV), F32)
    gam = jax.ShapeDtypeStruct((n_steps, tm // CHUNK, D_RWKV), F32)
    seq_spec = pl.BlockSpec((tm, D_RWKV), step)
    gam_spec = pl.BlockSpec((1, tm // CHUNK, D_RWKV), lambda i: (i, 0, 0))
    return pl.pallas_call(
        functools.partial(_front_body, n_x_tiles=nx),
        grid=(n_steps,),
        in_specs=[pl.BlockSpec((tm, d), xmap),
                  pl.BlockSpec((tm, d), const),
                  pl.BlockSpec((tm // GRID_W, d // 2), xmap),
                  pl.BlockSpec((tm, d // 2), const),
                  pl.BlockSpec((1, d), const),
                  pl.BlockSpec(mod.shape, const),
                  pl.BlockSpec((d, d_in), const),
                  pl.BlockSpec((1, D_SHIFT), const),
                  pl.BlockSpec((8, D_RWKV), const),
                  pl.BlockSpec(w2.shape, const),
                  pl.BlockSpec(a2.shape, const),
                  pl.BlockSpec(g2.shape, const)],
        out_specs=([pl.BlockSpec((tm, D_POOL), lambda i: (jnp.minimum(i, nx), 0))] + [seq_spec] * 9
                   + [gam_spec, gam_spec, seq_spec, seq_spec]),
        out_shape=([jax.ShapeDtypeStruct(((nx + 1) * tm, D_POOL), F32)] + [seq_bf] * 9
                   + [gam, gam, seq_f32, seq_f32]),
        scratch_shapes=[pltpu.VMEM((tm, D_SHIFT), F32), pltpu.VMEM((tm, D_SHIFT), F32),
                        pltpu.VMEM((TM_FEAT + SUBLANES, D_SHIFT), F32)],
        compiler_params=_cparams(),
        name="front",
    )(x2, ctx_pad, rowtab, coltab, g1, mod, w_in_bf, mu, par, w2, a2, g2)


def _blockdiag(x, head_of_lane):
    xb = x.astype(BF16)
    zero = jnp.zeros_like(xb)
    return jnp.concatenate([jnp.where(head_of_lane == h, xb, zero) for h in range(QUAD // HEAD)], axis=0)


def _unblock(x, head_of_lane):
    out = jnp.where(head_of_lane == 0, x[0:HEAD, :], 0.0)
    for h in range(1, QUAD // HEAD):
        out = jnp.where(head_of_lane == h, x[h * HEAD:(h + 1) * HEAD, :], out)
    return out


def _scan_chunks(chains):
    lane = lax.broadcasted_iota(jnp.int32, (HEAD, QUAD), 1)
    head_of_lane = lane >> 6
    s_idx = lane & (HEAD - 1)
    t_idx = lax.broadcasted_iota(jnp.int32, (HEAD, QUAD), 0)
    eye = s_idx == t_idx
    lower = s_idx < t_idx
    upper = s_idx > t_idx
    bd = functools.partial(_blockdiag, head_of_lane=head_of_lane)

    def same_block(shift):
        return (s_idx >> shift) == (t_idx >> shift)

    def each(fn):
        return [fn(c) for c in chains]

    def stage_masks(c):
        ll = _dot_nt(jnp.concatenate([c["a"], c["r"]], axis=0),
                     jnp.concatenate([bd(c["bt"]), bd(c["kt"])], axis=0))
        strict = upper if c["reverse"] else lower
        incl = jnp.logical_or(strict, eye)
        c["lab"] = jnp.where(strict, ll[:HEAD, :QUAD], 0.0)
        c["lak"] = jnp.where(strict, ll[:HEAD, QUAD:], 0.0)
        c["mrb"] = jnp.where(incl, ll[HEAD:, :QUAD], 0.0)
        c["mrk"] = jnp.where(incl, ll[HEAD:, QUAD:], 0.0)

    each(stage_masks)

    def stage_square(c):
        ld = jnp.where(same_block(4), c["lab"], 0.0)
        c["tm"] = jnp.where(eye, 1.0, 0.0) + ld
        c["xk"] = _dot(ld.astype(BF16), bd(ld))

    def stage_double(c):
        y = _dot(jnp.concatenate([c["tm"], c["xk"]], axis=0).astype(BF16), bd(c["xk"]))
        c["tm"] = c["tm"] + y[:HEAD]
        c["xk"] = y[HEAD:]

    def stage_last_power(c):
        c["tm"] = c["tm"] + _dot(c["tm"].astype(BF16), bd(c["xk"]))

    each(stage_square)
    each(stage_double)
    each(stage_double)
    each(stage_last_power)
    each(lambda c: c.update(vv=_dot(jnp.concatenate([c["lak"], c["mrk"]], axis=0).astype(BF16), bd(c["v"]))))
    for shift in (4, 5):
        joined = jnp.logical_and(same_block(shift + 1), jnp.logical_not(same_block(shift)))
        each(lambda c: c.update(te=_dot(c["tm"].astype(BF16), bd(jnp.where(joined, c["lab"], 0.0)))))
        each(lambda c: c.update(tm=c["tm"] + _dot(c["te"].astype(BF16), bd(c["tm"]))))

    each(lambda c: c.update(x1=_dot(c["tm"].astype(BF16),
                                    jnp.concatenate([bd(c["a"]), bd(c["vv"][:HEAD])], axis=1))))
    each(lambda c: c.update(x2=_dot(c["mrb"].astype(BF16),
                                    jnp.concatenate([bd(c["x1"][:, :QUAD]), bd(c["x1"][:, QUAD:])], axis=1))))

    def stage_state_terms(c):
        rhs = jnp.concatenate(
            [c["x1"].astype(BF16), jnp.concatenate([jnp.zeros((HEAD, QUAD), BF16), c["v"]], axis=1)], axis=0)
        hat = (jnp.concatenate([c["bt"], c["kt"]], axis=0).astype(F32) * c["gam"]).astype(BF16)
        pz = _dot_tn(hat, rhs)
        c["pq"] = _unblock(pz[:, :QUAD], head_of_lane) + jnp.where(eye, c["gam"], 0.0)
        c["zq"] = _unblock(pz[:, QUAD:], head_of_lane)
        c["qeff"] = c["r"].astype(F32) + c["x2"][:, :QUAD]
        c["yin"] = c["x2"][:, QUAD:] + c["vv"][HEAD:]

    each(stage_state_terms)

    outs, state = [], {}
    for c in chains:
        hst = state.get(c["key"], c["hst"])
        qp = _dot(jnp.concatenate([c["qeff"], c["pq"]], axis=0).astype(BF16), bd(hst))
        state[c["key"]] = qp[HEAD:] + c["zq"]
        outs.append(qp[:HEAD] + c["yin"])
    return outs, state


def _scan_body(af, rf, btf, ktf, vf, gamf, ab, rb, btb, ktb, vb, gamb, yf_ref, yb_ref, hf_ref, hb_ref):
    @pl.when(pl.program_id(0) == 0)
    def _():
        hf_ref[...] = jnp.zeros_like(hf_ref)
        hb_ref[...] = jnp.zeros_like(hb_ref)

    names = ("a", "r", "bt", "kt", "v")
    chains, sinks, state_refs = [], [], {}
    for ci in range(SCAN_CPS):
        for refs, y_ref, h_ref, reverse in (((af, rf, btf, ktf, vf, gamf), yf_ref, hf_ref, False),
                                            ((ab, rb, btb, ktb, vb, gamb), yb_ref, hb_ref, True)):
            chunk = SCAN_CPS - 1 - ci if reverse else ci
            rows = slice(chunk * CHUNK, (chunk + 1) * CHUNK)
            for q in range(D_RWKV // QUAD):
                sl = slice(q * QUAD, (q + 1) * QUAD)
                chain = {n: ref[rows, sl] for n, ref in zip(names, refs[:5])}
                chain.update(gam=refs[5][chunk, :, sl], hst=h_ref[:, sl], reverse=reverse, key=(reverse, q))
                chains.append(chain)
                sinks.append((y_ref, rows, sl))
                state_refs[(reverse, q)] = (h_ref, sl)
    outs, state = _scan_chunks(chains)
    for y, (y_ref, rows, sl) in zip(outs, sinks):
        y_ref[rows, sl] = y
    for key, (h_ref, sl) in state_refs.items():
        h_ref[:, sl] = state[key]


def _scan_call(feats, gamf, gamb, n_x_chunks, n_c_chunks):
    af, rf, btf, ktf, ab, rb, btb, ktb, v = feats
    assert n_x_chunks % SCAN_CPS == 0 and n_c_chunks % SCAN_CPS == 0
    nxb, ncb = n_x_chunks // SCAN_CPS, n_c_chunks // SCAN_CPS
    n = nxb + ncb
    assert SCAN_CPS * CHUNK == TM_FEAT
    fwd_blk = lambda i: jnp.where(i < ncb, nxb + i, i - ncb)
    bwd_blk = lambda i: n - 1 - i
    fwd_y = lambda i: (fwd_blk(i), 0)
    bwd_y = lambda i: (bwd_blk(i), 0)
    fwd = lambda i: (fwd_blk(i) + FEAT_SHIFT_BLOCKS, 0)
    bwd = lambda i: (bwd_blk(i) + FEAT_SHIFT_BLOCKS, 0)
    fwd3 = lambda i: (fwd_blk(i) + FEAT_SHIFT_BLOCKS, 0, 0)
    bwd3 = lambda i: (bwd_blk(i) + FEAT_SHIFT_BLOCKS, 0, 0)
    blk = (SCAN_CPS * CHUNK, D_RWKV)
    in_specs = ([pl.BlockSpec(blk, fwd)] * 5 + [pl.BlockSpec((SCAN_CPS, 1, D_RWKV), fwd3)]
                + [pl.BlockSpec(blk, bwd)] * 5 + [pl.BlockSpec((SCAN_CPS, 1, D_RWKV), bwd3)])
    y = jax.ShapeDtypeStruct((n * SCAN_CPS * CHUNK, D_RWKV), F32)
    return pl.pallas_call(
        _scan_body,
        grid=(n,),
        in_specs=in_specs,
        out_specs=[pl.BlockSpec(blk, fwd_y), pl.BlockSpec(blk, bwd_y)],
        out_shape=[y, y],
        scratch_shapes=[pltpu.VMEM((HEAD, D_RWKV), F32), pltpu.VMEM((HEAD, D_RWKV), F32)],
        compiler_params=_cparams(),
        name="scan",
    )(af, rf, btf, ktf, v, gamf, ab, rb, btb, ktb, v, gamb)


def _mix_body(x_ref, rowtab_ref, coltab_ref, yf_ref, yb_ref, bonus0_ref, bonus1_ref, g0_ref, g1_ref,
              pm_ref, pp_ref, pn_ref,
              gn_ref, wpool_ref, pscale_ref, wout_ref, mod_ref, o_ref, slab_ref, *, n_x_tiles, seq_len):
    j = pl.program_id(0)
    tm, d = x_ref.shape
    hl = POOL_HALO
    slab_ref[0:hl, :] = jnp.where(j != 0, pp_ref[...], 0.0)
    slab_ref[hl:hl + tm, :] = pm_ref[...]
    slab_ref[hl + tm:2 * hl + tm, :] = jnp.where(j != n_x_tiles - 1, pn_ref[...], 0.0)
    tglob = j * tm + lax.broadcasted_iota(jnp.int32, (tm, 1), 0)
    pooled = []
    for gi, win in enumerate(POOL_WINDOWS):
        cols = slice(gi * POOL_GW, (gi + 1) * POOL_GW)
        acc = slab_ref[hl - win // 2:hl - win // 2 + tm, cols]
        for off in range(-win // 2 + 1, win // 2):
            acc = acc + slab_ref[hl + off:hl + off + tm, cols]
        lo = jnp.maximum(tglob - win // 2, 0)
        hi = jnp.minimum(tglob - win // 2 + win, seq_len)
        cnt = (hi - lo).astype(F32)
        pg = acc / cnt - slab_ref[hl:hl + tm, cols]
        pooled.append(_dot(pg.astype(BF16), wpool_ref[gi]))
    pool = jnp.concatenate(pooled, axis=1) * pscale_ref[...]

    ones = _head_ones()
    y = yf_ref[...] + yb_ref[...]
    yc = y - _headsum(y, ones) * (1.0 / HEAD)
    var = _headsum(yc * yc, ones) * (1.0 / HEAD)
    bonus = jnp.concatenate([bonus0_ref[...], bonus1_ref[...]], axis=0)
    gate = jnp.concatenate([g0_ref[...], g1_ref[...]], axis=0)
    rw = (yc * lax.rsqrt(var + GN_EPS) * gn_ref[0:1, :] + gn_ref[1:2, :] + bonus) * gate
    mix = _dot(jnp.concatenate([rw, pool], axis=1).astype(BF16), wout_ref[...])
    gt_a = mod_ref[0:1, 2 * d:3 * d]
    o_ref[...] = x_ref[...] + _pos_tile(rowtab_ref, coltab_ref) + gt_a * mix


def _mix_call(x2, rowtab, coltab, yf, yb, bonus, g, p, gn, w_pool_bf, pscale, w_out_bf, mod):
    s, d = x2.shape
    tm = TM_PROJ
    nx = s // tm
    hb = tm // POOL_HALO
    const = lambda j: (0, 0)
    tok = lambda j: (j, 0)
    seq_spec = pl.BlockSpec((tm, D_RWKV), tok)
    assert tm == 2 * TM_FEAT
    feat_lo = pl.BlockSpec((TM_FEAT, D_RWKV), lambda j: (2 * j + FEAT_SHIFT_BLOCKS, 0))
    feat_hi = pl.BlockSpec((TM_FEAT, D_RWKV), lambda j: (2 * j + FEAT_SHIFT_BLOCKS + 1, 0))
    return pl.pallas_call(
        functools.partial(_mix_body, n_x_tiles=nx, seq_len=s),
        grid=(nx,),
        in_specs=[pl.BlockSpec((tm, d), tok),
                  pl.BlockSpec((tm // GRID_W, d // 2), tok),
                  pl.BlockSpec((tm, d // 2), const),
                  seq_spec, seq_spec, feat_lo, feat_hi, feat_lo, feat_hi, seq_spec,
                  pl.BlockSpec((POOL_HALO, D_POOL), lambda j: (jnp.maximum(j * hb - 1, 0), 0)),
                  pl.BlockSpec((POOL_HALO, D_POOL), lambda j: ((j + 1) * hb, 0)),
                  pl.BlockSpec((2, D_RWKV), const),
                  pl.BlockSpec(w_pool_bf.shape, lambda j: (0, 0, 0)),
                  pl.BlockSpec((1, D_POOL), const),
                  pl.BlockSpec(w_out_bf.shape, const),
                  pl.BlockSpec(mod.shape, const)],
        out_specs=pl.BlockSpec((tm, d), tok),
        out_shape=jax.ShapeDtypeStruct((s, d), F32),
        scratch_shapes=[pltpu.VMEM((tm + 2 * POOL_HALO, D_POOL), F32)],
        compiler_params=_cparams(),
        name="mix",
    )(x2, rowtab, coltab, yf, yb, bonus, bonus, g, g, p, p, p, gn, w_pool_bf, pscale, w_out_bf, mod)


def _to_token_tiles(ref, x):
    tm = x.shape[0]
    for j in range(x.shape[1] // LANES):
        ref[pl.ds(j, tm, stride=SUBLANES), :] = x[:, j * LANES:(j + 1) * LANES]


def _from_token_tiles(ref, tm):
    return jnp.concatenate([ref[pl.ds(j, tm, stride=SUBLANES), :] for j in range(SUBLANES)], axis=1)


def _router_body(x_ref, g2_ref, mod_ref, wr_ref, br_ref, h3_ref, eid_ref, gate_ref):
    tm, d = x_ref.shape
    x = x_ref[...]
    ms = jnp.mean(x * x, axis=-1, keepdims=True)
    xn = x * lax.rsqrt(ms + NORM_EPS) * g2_ref[...]
    h = xn * (1.0 + mod_ref[0:1, 4 * d:5 * d]) + mod_ref[0:1, 3 * d:4 * d]
    _to_token_tiles(h3_ref, h)

    wh, wl = _split2(wr_ref[...])
    hh, hl = _split2(h)
    logits = _dot_nt(wh, hh) + (_dot_nt(wh, hl) + _dot_nt(wl, hh)) + br_ref[...]
    gl = logits[0:N_GROUPS, :]
    gmax = jnp.max(gl, axis=0, keepdims=True)
    gidx = lax.broadcasted_iota(jnp.int32, gl.shape, 0).astype(F32)
    grp = jnp.min(jnp.where(gl == gmax, gidx, float(N_GROUPS)), axis=0, keepdims=True)
    p_grp = 1.0 / jnp.sum(jnp.exp(gl - gmax), axis=0, keepdims=True)
    sel = jnp.zeros((EXPERTS_PER_GROUP, tm), F32)
    for gi in range(N_GROUPS):
        lo = N_GROUPS + gi * EXPERTS_PER_GROUP
        sel = jnp.where(grp == float(gi), logits[lo:lo + EXPERTS_PER_GROUP, :], sel)
    eidx = lax.broadcasted_iota(jnp.int32, sel.shape, 0).astype(F32)
    top1 = jnp.max(sel, axis=0, keepdims=True)
    i1 = jnp.min(jnp.where(sel == top1, eidx, float(EXPERTS_PER_GROUP)), axis=0, keepdims=True)
    sel2 = jnp.where(eidx == i1, -jnp.inf, sel)
    top2 = jnp.max(sel2, axis=0, keepdims=True)
    i2 = jnp.min(jnp.where(sel2 == top2, eidx, float(EXPERTS_PER_GROUP)), axis=0, keepdims=True)
    e2 = jnp.exp(top2 - top1)
    inv = 1.0 / (1.0 + e2)
    zf = jnp.zeros((6, tm), F32)
    gate_ref[...] = jnp.concatenate([p_grp * inv, p_grp * (e2 * inv), zf], axis=0)
    base = grp * float(EXPERTS_PER_GROUP)
    eid_ref[...] = jnp.concatenate([base + i1, base + i2, zf], axis=0).astype(jnp.int32)


def _router_call(x_mid, g2n, mod, wr, br):
    s, d = x_mid.shape
    tm = TM_PROJ
    const = lambda j: (0, 0)
    return pl.pallas_call(
        _router_body,
        grid=(s // tm,),
        in_specs=[pl.BlockSpec((tm, d), lambda j: (j, 0)),
                  pl.BlockSpec((1, d), const),
                  pl.BlockSpec(mod.shape, const),
                  pl.BlockSpec(wr.shape, const),
                  pl.BlockSpec(br.shape, const)],
        out_specs=[pl.BlockSpec((tm * SUBLANES, LANES), lambda j: (j, 0)),
                   pl.BlockSpec((8, tm), lambda j: (0, j)),
                   pl.BlockSpec((8, tm), lambda j: (0, j))],
        out_shape=[jax.ShapeDtypeStruct((s * SUBLANES, LANES), F32),
                   jax.ShapeDtypeStruct((8, s), jnp.int32),
                   jax.ShapeDtypeStruct((8, s), F32)],
        compiler_params=_cparams(),
        name="router",
    )(x_mid, g2n, mod, wr, br)


def _sc_gather_tiles(table3, idx):
    n_out = idx.shape[0]
    n_workers = V7X_SC_CORES * V7X_SC_SUBCORES
    per_worker = n_out // n_workers
    n_chunks = per_worker // SC_GATHER_ROWS
    assert per_worker * n_workers == n_out and n_chunks * SC_GATHER_ROWS == per_worker
    mesh = plsc.VectorSubcoreMesh(core_axis_name="c", subcore_axis_name="s",
                                  num_cores=V7X_SC_CORES, num_subcores=V7X_SC_SUBCORES)

    def body(table_hbm, idx_hbm, out_hbm, idx_v, rows_v, sem):
        worker = lax.axis_index("s") * V7X_SC_CORES + lax.axis_index("c")
        base = worker * per_worker
        pltpu.sync_copy(idx_hbm.at[pl.ds(base, per_worker)], idx_v)

        @pl.loop(0, n_chunks)
        def _(j):
            off = pl.multiple_of(j * SC_GATHER_ROWS, SC_GATHER_ROWS)
            pltpu.async_copy(table_hbm.at[idx_v.at[pl.ds(off, SC_GATHER_ROWS)]], rows_v, sem).wait()
            pltpu.sync_copy(rows_v, out_hbm.at[pl.ds(base + off, SC_GATHER_ROWS)])

    return pl.kernel(
        body,
        out_type=jax.ShapeDtypeStruct((n_out,) + table3.shape[1:], table3.dtype),
        mesh=mesh,
        scratch_types=[pltpu.VMEM((per_worker,), jnp.int32),
                       pltpu.VMEM((SC_GATHER_ROWS,) + table3.shape[1:], table3.dtype),
                       pltpu.SemaphoreType.DMA],
        name="sc_gather",
    )(table3, idx)


def _experts_body(blk_e_ref, n_used_ref, first_ref, next_e_ref, wslot_ref, xs_ref, wg_hbm, wu_hbm, wd_hbm, y_ref,
                  wg_buf, wu_buf, wd_buf, wsem_ref):
    i = pl.program_id(0)
    used = i < n_used_ref[0]
    slot = wslot_ref[i]

    def weight_copies(expert, to_slot):
        return [pltpu.make_async_copy(hbm.at[expert], buf.at[to_slot], wsem_ref.at[to_slot])
                for hbm, buf in ((wg_hbm, wg_buf), (wu_hbm, wu_buf), (wd_hbm, wd_buf))]

    @pl.when(i == 0)
    def _():
        for cp in weight_copies(blk_e_ref[0], 0):
            cp.start()

    @pl.when(jnp.logical_and(used, first_ref[i] == 1))
    def _():
        for cp in weight_copies(blk_e_ref[i], slot):
            cp.wait()

        @pl.when(next_e_ref[i] != blk_e_ref[i])
        def _():
            for cp in weight_copies(next_e_ref[i], 1 - slot):
                cp.start()

    @pl.when(used)
    def _():
        xb = _from_token_tiles(xs_ref, MOE_BM).astype(BF16)
        gate = _dot(xb, wg_buf[slot].astype(BF16))
        upp = _dot(xb, wu_buf[slot].astype(BF16))
        hid = (gate * _sigmoid(gate)) * upp
        _to_token_tiles(y_ref, _dot(hid.astype(BF16), wd_buf[slot].astype(BF16)))

    @pl.when(jnp.logical_not(used))
    def _():
        y_ref[...] = jnp.zeros_like(y_ref)


def _experts_call(blk_expert, n_used, first, next_e, wslot, xs3, wg, wu, wd):
    nb = blk_expert.shape[0]
    d = wg.shape[1]
    rows = MOE_BM * SUBLANES
    blk = lambda i, *_: (i, 0)
    grid_spec = pltpu.PrefetchScalarGridSpec(
        num_scalar_prefetch=5,
        grid=(nb,),
        in_specs=[pl.BlockSpec((rows, LANES), blk),
                  pl.BlockSpec(memory_space=pl.ANY),
                  pl.BlockSpec(memory_space=pl.ANY),
                  pl.BlockSpec(memory_space=pl.ANY)],
        out_specs=pl.BlockSpec((rows, LANES), blk),
        scratch_shapes=[pltpu.VMEM((2, d, D_EXPERT), F32), pltpu.VMEM((2, d, D_EXPERT), F32),
                        pltpu.VMEM((2, D_EXPERT, d), F32), pltpu.SemaphoreType.DMA((2,))],
    )
    return pl.pallas_call(
        _experts_body,
        grid_spec=grid_spec,
        out_shape=jax.ShapeDtypeStruct((nb * rows, LANES), F32),
        compiler_params=_cparams(),
        name="experts",
    )(blk_expert, n_used, first, next_e, wslot, xs3, wg, wu, wd)


def _combine_body(x_ref, y0_ref, y1_ref, gate_ref, mod_ref, fg_ref, o_ref):
    tm, d = x_ref.shape
    y = gate_ref[:, 0:1] * _from_token_tiles(y0_ref, tm) + gate_ref[:, 1:2] * _from_token_tiles(y1_ref, tm)
    x = x_ref[...] + mod_ref[0:1, 5 * d:6 * d] * y
    ms = jnp.mean(x * x, axis=-1, keepdims=True)
    o_ref[...] = x * lax.rsqrt(ms + NORM_EPS) * fg_ref[...]


def _combine_call(x_mid, y3, gcol, mod, fg):
    s, d = x_mid.shape
    tm = TM_PROJ
    nt = s // tm
    const = lambda j: (0, 0)
    return pl.pallas_call(
        _combine_body,
        grid=(nt,),
        in_specs=[pl.BlockSpec((tm, d), lambda j: (j, 0)),
                  pl.BlockSpec((tm * SUBLANES, LANES), lambda j: (j, 0)),
                  pl.BlockSpec((tm * SUBLANES, LANES), lambda j: (j + nt, 0)),
                  pl.BlockSpec((tm, 2), lambda j: (j, 0)),
                  pl.BlockSpec(mod.shape, const),
                  pl.BlockSpec((1, d), const)],
        out_specs=pl.BlockSpec((tm, d), lambda j: (j, 0)),
        out_shape=jax.ShapeDtypeStruct((s, d), F32),
        compiler_params=_cparams(),
        name="combine",
    )(x_mid, y3, y3, gcol, mod, fg)


def _blockdiag2(wf, wb):
    z = jnp.zeros_like(wf)
    return jnp.concatenate([jnp.concatenate([wf, z], axis=1), jnp.concatenate([z, wb], axis=1)], axis=0)


def _routing_plan(eids, n_tok):
    m = 2 * n_tok
    experts = jnp.arange(N_EXPERTS, dtype=jnp.int32)
    flat_e = eids[:2].reshape(m)
    order = jnp.argsort(flat_e, stable=True).astype(jnp.int32)
    onehot = (flat_e[:, None] == experts[None, :]).astype(jnp.int32)
    csum = jnp.cumsum(onehot, axis=0)
    counts = csum[-1]
    starts = jnp.cumsum(counts) - counts
    pcounts = (counts + MOE_BM - 1) // MOE_BM * MOE_BM
    pends = jnp.cumsum(pcounts)
    pstarts = pends - pcounts
    n_blocks = (m + N_EXPERTS * (MOE_BM - 1) + MOE_BM - 1) // MOE_BM
    slot = jnp.arange(n_blocks * MOE_BM, dtype=jnp.int32)
    slot_e = jnp.minimum(jnp.sum((slot[:, None] >= pends[None, :]).astype(jnp.int32), axis=1), N_EXPERTS - 1)
    sel = (slot_e[:, None] == experts[None, :]).astype(jnp.int32)
    local = slot - jnp.sum(sel * pstarts[None, :], axis=1)
    valid = local < jnp.sum(sel * counts[None, :], axis=1)
    pair = order[jnp.clip(jnp.sum(sel * starts[None, :], axis=1) + local, 0, m - 1)]
    src_tok = jnp.where(valid, jnp.where(pair >= n_tok, pair - n_tok, pair), slot % n_tok)
    dest = jnp.sum(onehot * (csum - 1 + pstarts[None, :]), axis=1).astype(jnp.int32)
    blk_expert = slot_e[::MOE_BM]
    n_used = (pends[-1:] // MOE_BM).astype(jnp.int32)
    blk = jnp.arange(n_blocks, dtype=jnp.int32)
    used = blk < n_used[0]
    first = jnp.logical_and(used, jnp.concatenate([jnp.ones((1,), bool), blk_expert[1:] != blk_expert[:-1]]))
    none = jnp.int32(N_EXPERTS)
    later_first = lax.cummin(jnp.where(first, blk_expert, none), axis=0, reverse=True)
    next_e = jnp.concatenate([later_first[1:], none[None]])
    next_e = jnp.where(next_e == none, blk_expert, next_e)
    wslot = (jnp.cumsum(first.astype(jnp.int32)) - 1) & 1
    return src_tok, dest, blk_expert, n_used, first.astype(jnp.int32), next_e, wslot


def kernel(x, c, ctx, c_ctx, norm1_g, w_mod, b_mod, w_in, mu_shift, w0_f, w2_f, w0_b, w2_b, a0_f, a2_f, a0_b, a2_b, g2, k_k, k_a, r_k_f, r_k_b, gn_w, gn_b, w_pool, pool_scale, w_out, norm2_g, w_router_grp, b_router_grp, w_router_exp, b_router_exp, w_gate, w_up, w_down, final_g):
    b, s, d = x.shape
    n_ctx = ctx.shape[1]
    assert b == 1 and c.shape[0] == 1 and w_mod.shape[0] == 1
    assert s % TM_PROJ == 0 and n_ctx % CHUNK == 0 and n_ctx == TM_FEAT and d == 2 * D_RWKV
    x2 = x[0]
    rows = s // GRID_W

    quarter = d // 4
    freq = POS_THETA ** (-jnp.arange(quarter, dtype=F32) / quarter)
    rarg = jnp.arange(rows, dtype=F32)[:, None] * freq
    carg = jnp.arange(GRID_W, dtype=F32)[:, None] * freq
    rowtab = jnp.concatenate([jnp.sin(rarg), jnp.cos(rarg)], axis=-1)
    coltab = jnp.tile(jnp.concatenate([jnp.sin(carg), jnp.cos(carg)], axis=-1), (TM_PROJ // GRID_W, 1))

    cs = jnp.concatenate([c, c_ctx[None, :], jnp.zeros((6, d), F32)], axis=0)
    mod = _mod_call(cs, w_mod[0], b_mod)

    ctx_pad = jnp.concatenate([ctx[0], jnp.zeros((TM_PROJ - n_ctx, d), F32)], axis=0)
    par = jnp.concatenate([w0_f, w0_b, a0_f, a0_b, k_k, k_a, r_k_f, r_k_b], axis=0)
    outs = _front_call(x2, ctx_pad, rowtab, coltab, norm1_g, mod, w_in[0].astype(BF16), mu_shift, par,
                       _blockdiag2(w2_f[0], w2_b[0]).astype(BF16), _blockdiag2(a2_f[0], a2_b[0]).astype(BF16),
                       g2[0].astype(BF16))
    p, feats, gamf, gamb, bonus, g = outs[0], outs[1:10], outs[10], outs[11], outs[12], outs[13]
    yf, yb = _scan_call(feats, gamf.reshape(-1, 1, D_RWKV), gamb.reshape(-1, 1, D_RWKV),
                        s // CHUNK, n_ctx // CHUNK)

    gn = jnp.concatenate([gn_w, gn_b], axis=0)
    x_mid = _mix_call(x2, rowtab, coltab, yf, yb, bonus, g, p, gn, w_pool[0].astype(BF16), pool_scale,
                      w_out[0].astype(BF16), mod)

    wr = jnp.concatenate([w_router_grp[0].T, w_router_exp[0].T,
                          jnp.zeros((ROUTER_ROWS - N_GROUPS - N_EXPERTS, d), F32)], axis=0)
    br = jnp.concatenate([b_router_grp[0], b_router_exp[0],
                          jnp.zeros((ROUTER_ROWS - N_GROUPS - N_EXPERTS,), F32)])[:, None]
    h3, eids, gates = _router_call(x_mid, norm2_g, mod, wr, br)

    src_tok, dest, blk_expert, n_used, first, next_e, wslot = _routing_plan(eids, s)
    tile = (SUBLANES, LANES)
    xs3 = _sc_gather_tiles(h3.reshape((s,) + tile), src_tok)
    yexp = _experts_call(blk_expert, n_used, first, next_e, wslot, xs3.reshape(-1, LANES),
                         w_gate[0], w_up[0], w_down[0])
    y3 = _sc_gather_tiles(yexp.reshape((-1,) + tile), dest).reshape(-1, LANES)
    out = _combine_call(x_mid, y3, gates[:2].T, mod, final_g[None, :])
    return out[None]
```

```python
import functools

import jax
import jax.numpy as jnp
from jax import lax
from jax.experimental import pallas as pl
from jax.experimental.pallas import tpu as pltpu
from jax.experimental.pallas import tpu_sc as plsc

F32 = jnp.float32
BF16 = jnp.bfloat16

GRID_W = 64
SUBLANES, LANES = 8, 128
HEAD = 64
N_HEADS = 8
D_RWKV = HEAD * N_HEADS
D_POOL = 512
POOL_WINDOWS = (2, 4, 8, 16)
POOL_GW = 128
POOL_HALO = 8
D_SHIFT = 1920
N_GROUPS = 4
EXPERTS_PER_GROUP = 8
N_EXPERTS = 32
D_EXPERT = 512
NORM_EPS = 1e-6
GN_EPS = 64e-5
POS_THETA = 10000.0
DECAY_SCALE = 0.6065306597

CHUNK = 64
QUAD = 4 * HEAD
SCAN_CPS = 4
TM_PROJ = 512
TM_FEAT = 256
FEAT_SHIFT_BLOCKS = 3
MOE_BM = 256
ROUTER_ROWS = 48
V7X_VMEM_LIMIT = 56 * 1024 * 1024
V7X_SC_CORES, V7X_SC_SUBCORES = 2, 16
SC_GATHER_ROWS = 32


def _cparams(n_axes=1, vmem=V7X_VMEM_LIMIT):
    return pltpu.CompilerParams(dimension_semantics=("arbitrary",) * n_axes,
                                vmem_limit_bytes=vmem)


def _dot(a, b):
    return jnp.dot(a, b, preferred_element_type=F32)


def _dot_nt(a, b):
    return lax.dot_general(a, b, (((1,), (1,)), ((), ())), preferred_element_type=F32)


def _dot_tn(a, b):
    return lax.dot_general(a, b, (((0,), (0,)), ((), ())), preferred_element_type=F32)


def _split2(x):
    hi = x.astype(BF16)
    lo = (x - hi.astype(F32)).astype(BF16)
    return hi, lo


def _dot3(a, b):
    ah, al = _split2(a)
    bh, bl = _split2(b)
    return _dot(ah, bh) + (_dot(ah, bl) + _dot(al, bh))


def _sigmoid(x):
    return 0.5 * jnp.tanh(0.5 * x) + 0.5


def _head_ones():
    r = lax.broadcasted_iota(jnp.int32, (D_RWKV, D_RWKV), 0) >> 6
    c = lax.broadcasted_iota(jnp.int32, (D_RWKV, D_RWKV), 1) >> 6
    return (r == c).astype(BF16)


def _headsum(x, ones):
    hi, lo = _split2(x)
    return _dot(hi, ones) + _dot(lo, ones)


def _mod_body(cs_ref, w_ref, b_ref, o_ref):
    a = cs_ref[...]
    a = a * _sigmoid(a)
    o_ref[...] = _dot3(a, w_ref[...]) + b_ref[...]


def _mod_call(cs, w_mod, b_mod):
    d, n = w_mod.shape
    tn = 512
    return pl.pallas_call(
        _mod_body,
        grid=(n // tn,),
        in_specs=[pl.BlockSpec((8, d), lambda j: (0, 0)),
                  pl.BlockSpec((d, tn), lambda j: (0, j)),
                  pl.BlockSpec((1, tn), lambda j: (0, j))],
        out_specs=pl.BlockSpec((8, tn), lambda j: (0, j)),
        out_shape=jax.ShapeDtypeStruct((8, n), F32),
        compiler_params=_cparams(),
        name="mod",
    )(cs, w_mod, b_mod)


def _pos_tile(rowtab_ref, coltab_ref):
    rt = rowtab_ref[...]
    rowpart = jnp.concatenate(
        [jnp.broadcast_to(rt[r:r + 1, :], (GRID_W, rt.shape[1])) for r in range(TM_PROJ // GRID_W)], axis=0)
    return jnp.concatenate([rowpart, coltab_ref[...]], axis=1)


def _features(u, prow, nrow, mu, par_ref, w2_ref, a2_ref, g2_ref, rows, chunk0, outs, consts):
    (af_ref, rf_ref, btf_ref, ktf_ref, ab_ref, rb_ref, btb_ref, ktb_ref, v_ref,
     gamf_ref, gamb_ref, bonus_ref, g_ref) = outs
    tm = u.shape[0]
    rid = lax.broadcasted_iota(jnp.int32, (SUBLANES, 1), 0)

    def shifted(lo, hi):
        uc, m = u[:, lo:hi], mu[:, lo:hi]
        up = pltpu.roll(uc, 1, 0)
        up = jnp.concatenate([jnp.where(rid == 0, prow[:, lo:hi], up[:SUBLANES]), up[SUBLANES:]], axis=0)
        dn = pltpu.roll(uc, tm - 1, 0)
        dn = jnp.concatenate(
            [dn[:-SUBLANES], jnp.where(rid == SUBLANES - 1, nrow[:, lo:hi], dn[-SUBLANES:])], axis=0)
        return (1.0 - m) * uc + (0.5 * m) * (up + dn)

    w0f, w0b, a0f, a0b = par_ref[0:1, :], par_ref[1:2, :], par_ref[2:3, :], par_ref[3:4, :]
    k_k, k_a, rkf, rkb = par_ref[4:5, :], par_ref[5:6, :], par_ref[6:7, :], par_ref[7:8, :]

    r = shifted(0, 512)
    yield
    k = shifted(512, 1024)
    yield
    v = shifted(1024, 1536)
    v_ref[rows, :] = v.astype(BF16)
    yield
    lora = shifted(1536, D_SHIFT)
    zw = _dot(jnp.tanh(lora[:, 0:128]).astype(BF16), w2_ref[...])
    za = _dot(lora[:, 128:256].astype(BF16), a2_ref[...])
    g_ref[rows, :] = _dot(_sigmoid(lora[:, 256:384]).astype(BF16), g2_ref[...])
    yield
    lw_f = -DECAY_SCALE * _sigmoid(w0f + zw[:, :512])
    lw_b = -DECAY_SCALE * _sigmoid(w0b + zw[:, 512:])
    a_f = _sigmoid(a0f + za[:, :512])
    a_b = _sigmoid(a0b + za[:, 512:])
    yield

    ones, earlier, later = consts
    kkr = k * k_k
    kk = kkr * lax.rsqrt(jnp.maximum(_headsum(kkr * kkr, ones), 1e-24))
    yield
    k_f = k * (1.0 + (a_f - 1.0) * k_a)
    k_b = k * (1.0 + (a_b - 1.0) * k_a)
    bonus_ref[rows, :] = _headsum(r * (k_f * rkf + k_b * rkb), ones) * v
    yield

    n_ch = tm // CHUNK

    def visited_before(m, lwd):
        l1, l2 = _split2(lwd)
        return _dot(m, l1) + _dot(m, l2)

    def emit(excl, lwd, a_d, k_d, last_row, a_ref, r_ref, bt_ref, kt_ref, gam_ref):
        cum = excl + lwd
        gam_ref[0, chunk0:chunk0 + n_ch, :] = jnp.exp(jnp.concatenate(
            [cum[c * CHUNK + last_row:c * CHUNK + last_row + 1, :] for c in range(n_ch)], axis=0))
        a_ref[rows, :] = (-kk * jnp.exp(excl)).astype(BF16)
        r_ref[rows, :] = (r * jnp.exp(cum)).astype(BF16)
        yield
        e_neg = jnp.exp(-cum)
        bt_ref[rows, :] = (kk * a_d * e_neg).astype(BF16)
        kt_ref[rows, :] = (k_d * e_neg).astype(BF16)
        yield

    excl_f = visited_before(earlier, lw_f)
    yield
    yield from emit(excl_f, lw_f, a_f, k_f, CHUNK - 1, af_ref, rf_ref, btf_ref, ktf_ref, gamf_ref)
    excl_b = visited_before(later, lw_b)
    yield
    yield from emit(excl_b, lw_b, a_b, k_b, 0, ab_ref, rb_ref, btb_ref, ktb_ref, gamb_ref)


def _front_body(x_ref, ctx_ref, rowtab_ref, coltab_ref, g1_ref, mod_ref, w_ref, mu_ref, par_ref, w2_ref, a2_ref,
                g2_ref, p_ref, af_ref, rf_ref, btf_ref, ktf_ref, ab_ref, rb_ref, btb_ref, ktb_ref, v_ref,
                gamf_ref, gamb_ref, bonus_ref, g_ref, u_ref, unew_ref, carry_ref, *, n_x_tiles):
    i = pl.program_id(0)
    d = x_ref.shape[1]
    d_in = w_ref.shape[1]
    half = TM_FEAT
    last = carry_ref.shape[0] - 1

    @pl.when(i == 0)
    def _():
        u_ref[...] = jnp.zeros_like(u_ref)
        carry_ref[...] = jnp.zeros_like(carry_ref)

    is_ctx = i >= n_x_tiles
    xin = jnp.where(is_ctx, ctx_ref[...], x_ref[...] + _pos_tile(rowtab_ref, coltab_ref))
    ms = jnp.mean(xin * xin, axis=-1, keepdims=True)
    xn = xin * lax.rsqrt(ms + NORM_EPS) * g1_ref[...]
    sh = jnp.where(is_ctx, mod_ref[1:2, 0:d], mod_ref[0:1, 0:d])
    sc = jnp.where(is_ctx, mod_ref[1:2, d:2 * d], mod_ref[0:1, d:2 * d])
    hb = (xn * (1.0 + sc) + sh).astype(BF16)
    pending = [(slice(r0, r0 + half), lo, min(lo + QUAD, d_in))
               for lo in range(0, d_in, QUAD) for r0 in range(0, hb.shape[0], half)]

    def project_next():
        if not pending:
            return
        rs, lo, hi = pending.pop(0)
        uc = _dot(hb[rs, :], w_ref[:, lo:hi])
        if lo < D_SHIFT:
            unew_ref[rs, lo:min(hi, D_SHIFT)] = uc[:, :min(hi, D_SHIFT) - lo]
        if hi > D_SHIFT:
            p_ref[rs, max(lo, D_SHIFT) - D_SHIFT:hi - D_SHIFT] = uc[:, max(lo, D_SHIFT) - lo:]

    outs = (af_ref, rf_ref, btf_ref, ktf_ref, ab_ref, rb_ref, btb_ref, ktb_ref, v_ref,
            gamf_ref, gamb_ref, bonus_ref, g_ref)
    mu = mu_ref[...]
    ctx_block = 2 * n_x_tiles
    blocks = ((2 * i - 3, carry_ref[SUBLANES:, :], carry_ref[SUBLANES - 1:SUBLANES, :], u_ref[0:1, :]),
              (2 * i - 2, u_ref[0:half, :], carry_ref[last:last + 1, :], u_ref[half:half + 1, :]))
    rr = lax.broadcasted_iota(jnp.int32, (half, half), 0)
    cc = lax.broadcasted_iota(jnp.int32, (half, half), 1)
    same = (rr >> 6) == (cc >> 6)
    consts = (_head_ones(), jnp.logical_and(same, cc < rr).astype(BF16),
              jnp.logical_and(same, cc > rr).astype(BF16))
    project_next()
    for hi, (blk, ub, prow, nrow) in enumerate(blocks):
        prev_ok = jnp.logical_and(blk != 0, blk != ctx_block)
        next_ok = jnp.logical_and(blk != ctx_block - 1, blk != ctx_block)
        for _ in _features(ub, jnp.where(prev_ok, prow, 0.0), jnp.where(next_ok, nrow, 0.0), mu, par_ref, w2_ref,
                           a2_ref, g2_ref, slice(hi * half, (hi + 1) * half), hi * (half // CHUNK), outs, consts):
            project_next()
    while pending:
        project_next()
    carry_ref[...] = u_ref[half - SUBLANES:, :]
    u_ref[...] = unew_ref[...]


def _front_call(x2, ctx_pad, rowtab, coltab, g1, mod, w_in_bf, mu, par, w2, a2, g2):
    s, d = x2.shape
    tm = TM_PROJ
    nx = s // tm
    d_in = w_in_bf.shape[1]
    n_steps = nx + 2
    xmap = lambda i: (jnp.minimum(i, nx - 1), 0)
    const = lambda i: (0, 0)
    step = lambda i: (i, 0)
    seq_bf = jax.ShapeDtypeStruct((n_steps * tm, D_RWKV), BF16)
    seq_f32 = jax.ShapeDtypeStruct((n_steps * tm, D_RWKV), F32)
    gam = jax.ShapeDtypeStruct((n_steps, tm // CHUNK, D_RWKV), F32)
    seq_spec = pl.BlockSpec((tm, D_RWKV), step)
    gam_spec = pl.BlockSpec((1, tm // CHUNK, D_RWKV), lambda i: (i, 0, 0))
    return pl.pallas_call(
        functools.partial(_front_body, n_x_tiles=nx),
        grid=(n_steps,),
        in_specs=[pl.BlockSpec((tm, d), xmap),
                  pl.BlockSpec((tm, d), const),
                  pl.BlockSpec((tm // GRID_W, d // 2), xmap),
                  pl.BlockSpec((tm, d // 2), const),
                  pl.BlockSpec((1, d), const),
                  pl.BlockSpec(mod.shape, const),
                  pl.BlockSpec((d, d_in), const),
                  pl.BlockSpec((1, D_SHIFT), const),
                  pl.BlockSpec((8, D_RWKV), const),
                  pl.BlockSpec(w2.shape, const),
                  pl.BlockSpec(a2.shape, const),
                  pl.BlockSpec(g2.shape, const)],
        out_specs=([pl.BlockSpec((tm, D_POOL), lambda i: (jnp.minimum(i, nx), 0))] + [seq_spec] * 9
                   + [gam_spec, gam_spec, seq_spec, seq_spec]),
        out_shape=([jax.ShapeDtypeStruct(((nx + 1) * tm, D_POOL), F32)] + [seq_bf] * 9
                   + [gam, gam, seq_f32, seq_f32]),
        scratch_shapes=[pltpu.VMEM((tm, D_SHIFT), F32), pltpu.VMEM((tm, D_SHIFT), F32),
                        pltpu.VMEM((TM_FEAT + SUBLANES, D_SHIFT), F32)],
        compiler_params=_cparams(),
        name="front",
    )(x2, ctx_pad, rowtab, coltab, g1, mod, w_in_bf, mu, par, w2, a2, g2)


def _blockdiag(x, head_of_lane):
    xb = x.astype(BF16)
    zero = jnp.zeros_like(xb)
    return jnp.concatenate([jnp.where(head_of_lane == h, xb, zero) for h in range(QUAD // HEAD)], axis=0)


def _unblock(x, head_of_lane):
    out = jnp.where(head_of_lane == 0, x[0:HEAD, :], 0.0)
    for h in range(1, QUAD // HEAD):
        out = jnp.where(head_of_lane == h, x[h * HEAD:(h + 1) * HEAD, :], out)
    return out


def _scan_chunks(chains):
    lane = lax.broadcasted_iota(jnp.int32, (HEAD, QUAD), 1)
    head_of_lane = lane >> 6
    s_idx = lane & (HEAD - 1)
    t_idx = lax.broadcasted_iota(jnp.int32, (HEAD, QUAD), 0)
    eye = s_idx == t_idx
    lower = s_idx < t_idx
    upper = s_idx > t_idx
    bd = functools.partial(_blockdiag, head_of_lane=head_of_lane)

    def same_block(shift):
        return (s_idx >> shift) == (t_idx >> shift)

    def each(fn):
        return [fn(c) for c in chains]

    def stage_masks(c):
        ll = _dot_nt(jnp.concatenate([c["a"], c["r"]], axis=0),
                     jnp.concatenate([bd(c["bt"]), bd(c["kt"])], axis=0))
        strict = upper if c["reverse"] else lower
        incl = jnp.logical_or(strict, eye)
        c["lab"] = jnp.where(strict, ll[:HEAD, :QUAD], 0.0)
        c["lak"] = jnp.where(strict, ll[:HEAD, QUAD:], 0.0)
        c["mrb"] = jnp.where(incl, ll[HEAD:, :QUAD], 0.0)
        c["mrk"] = jnp.where(incl, ll[HEAD:, QUAD:], 0.0)

    each(stage_masks)

    def stage_square(c):
        ld = jnp.where(same_block(4), c["lab"], 0.0)
        c["tm"] = jnp.where(eye, 1.0, 0.0) + ld
        c["xk"] = _dot(ld.astype(BF16), bd(ld))

    def stage_double(c):
        y = _dot(jnp.concatenate([c["tm"], c["xk"]], axis=0).astype(BF16), bd(c["xk"]))
        c["tm"] = c["tm"] + y[:HEAD]
        c["xk"] = y[HEAD:]

    def stage_last_power(c):
        c["tm"] = c["tm"] + _dot(c["tm"].astype(BF16), bd(c["xk"]))

    each(stage_square)
    each(stage_double)
    each(stage_double)
    each(stage_last_power)
    each(lambda c: c.update(vv=_dot(jnp.concatenate([c["lak"], c["mrk"]], axis=0).astype(BF16), bd(c["v"]))))
    for shift in (4, 5):
        joined = jnp.logical_and(same_block(shift + 1), jnp.logical_not(same_block(shift)))
        each(lambda c: c.update(te=_dot(c["tm"].astype(BF16), bd(jnp.where(joined, c["lab"], 0.0)))))
        each(lambda c: c.update(tm=c["tm"] + _dot(c["te"].astype(BF16), bd(c["tm"]))))

    each(lambda c: c.update(x1=_dot(c["tm"].astype(BF16),
                                    jnp.concatenate([bd(c["a"]), bd(c["vv"][:HEAD])], axis=1))))
    each(lambda c: c.update(x2=_dot(c["mrb"].astype(BF16),
                                    jnp.concatenate([bd(c["x1"][:, :QUAD]), bd(c["x1"][:, QUAD:])], axis=1))))

    def stage_state_terms(c):
        rhs = jnp.concatenate(
            [c["x1"].astype(BF16), jnp.concatenate([jnp.zeros((HEAD, QUAD), BF16), c["v"]], axis=1)], axis=0)
        hat = (jnp.concatenate([c["bt"], c["kt"]], axis=0).astype(F32) * c["gam"]).astype(BF16)
        pz = _dot_tn(hat, rhs)
        c["pq"] = _unblock(pz[:, :QUAD], head_of_lane) + jnp.where(eye, c["gam"], 0.0)
        c["zq"] = _unblock(pz[:, QUAD:], head_of_lane)
        c["qeff"] = c["r"].astype(F32) + c["x2"][:, :QUAD]
        c["yin"] = c["x2"][:, QUAD:] + c["vv"][HEAD:]

    each(stage_state_terms)

    outs, state = [], {}
    for c in chains:
        hst = state.get(c["key"], c["hst"])
        qp = _dot(jnp.concatenate([c["qeff"], c["pq"]], axis=0).astype(BF16), bd(hst))
        state[c["key"]] = qp[HEAD:] + c["zq"]
        outs.append(qp[:HEAD] + c["yin"])
    return outs, state


def _scan_body(af, rf, btf, ktf, vf, gamf, ab, rb, btb, ktb, vb, gamb, yf_ref, yb_ref, hf_ref, hb_ref):
    @pl.when(pl.program_id(0) == 0)
    def _():
        hf_ref[...] = jnp.zeros_like(hf_ref)
        hb_ref[...] = jnp.zeros_like(hb_ref)

    names = ("a", "r", "bt", "kt", "v")
    chains, sinks, state_refs = [], [], {}
    for ci in range(SCAN_CPS):
        for refs, y_ref, h_ref, reverse in (((af, rf, btf, ktf, vf, gamf), yf_ref, hf_ref, False),
                                            ((ab, rb, btb, ktb, vb, gamb), yb_ref, hb_ref, True)):
            chunk = SCAN_CPS - 1 - ci if reverse else ci
            rows = slice(chunk * CHUNK, (chunk + 1) * CHUNK)
            for q in range(D_RWKV // QUAD):
                sl = slice(q * QUAD, (q + 1) * QUAD)
                chain = {n: ref[rows, sl] for n, ref in zip(names, refs[:5])}
                chain.update(gam=refs[5][chunk, :, sl], hst=h_ref[:, sl], reverse=reverse, key=(reverse, q))
                chains.append(chain)
                sinks.append((y_ref, rows, sl))
                state_refs[(reverse, q)] = (h_ref, sl)
    outs, state = _scan_chunks(chains)
    for y, (y_ref, rows, sl) in zip(outs, sinks):
        y_ref[rows, sl] = y
    for key, (h_ref, sl) in state_refs.items():
        h_ref[:, sl] = state[key]


def _scan_call(feats, gamf, gamb, n_x_chunks, n_c_chunks):
    af, rf, btf, ktf, ab, rb, btb, ktb, v = feats
    assert n_x_chunks % SCAN_CPS == 0 and n_c_chunks % SCAN_CPS == 0
    nxb, ncb = n_x_chunks // SCAN_CPS, n_c_chunks // SCAN_CPS
    n = nxb + ncb
    assert SCAN_CPS * CHUNK == TM_FEAT
    fwd_blk = lambda i: jnp.where(i < ncb, nxb + i, i - ncb)
    bwd_blk = lambda i: n - 1 - i
    fwd_y = lambda i: (fwd_blk(i), 0)
    bwd_y = lambda i: (bwd_blk(i), 0)
    fwd = lambda i: (fwd_blk(i) + FEAT_SHIFT_BLOCKS, 0)
    bwd = lambda i: (bwd_blk(i) + FEAT_SHIFT_BLOCKS, 0)
    fwd3 = lambda i: (fwd_blk(i) + FEAT_SHIFT_BLOCKS, 0, 0)
    bwd3 = lambda i: (bwd_blk(i) + FEAT_SHIFT_BLOCKS, 0, 0)
    blk = (SCAN_CPS * CHUNK, D_RWKV)
    in_specs = ([pl.BlockSpec(blk, fwd)] * 5 + [pl.BlockSpec((SCAN_CPS, 1, D_RWKV), fwd3)]
                + [pl.BlockSpec(blk, bwd)] * 5 + [pl.BlockSpec((SCAN_CPS, 1, D_RWKV), bwd3)])
    y = jax.ShapeDtypeStruct((n * SCAN_CPS * CHUNK, D_RWKV), F32)
    return pl.pallas_call(
        _scan_body,
        grid=(n,),
        in_specs=in_specs,
        out_specs=[pl.BlockSpec(blk, fwd_y), pl.BlockSpec(blk, bwd_y)],
        out_shape=[y, y],
        scratch_shapes=[pltpu.VMEM((HEAD, D_RWKV), F32), pltpu.VMEM((HEAD, D_RWKV), F32)],
        compiler_params=_cparams(),
        name="scan",
    )(af, rf, btf, ktf, v, gamf, ab, rb, btb, ktb, v, gamb)


def _mix_body(x_ref, rowtab_ref, coltab_ref, yf_ref, yb_ref, bonus0_ref, bonus1_ref, g0_ref, g1_ref,
              pm_ref, pp_ref, pn_ref,
              gn_ref, wpool_ref, pscale_ref, wout_ref, mod_ref, o_ref, slab_ref, *, n_x_tiles, seq_len):
    j = pl.program_id(0)
    tm, d = x_ref.shape
    hl = POOL_HALO
    slab_ref[0:hl, :] = jnp.where(j != 0, pp_ref[...], 0.0)
    slab_ref[hl:hl + tm, :] = pm_ref[...]
    slab_ref[hl + tm:2 * hl + tm, :] = jnp.where(j != n_x_tiles - 1, pn_ref[...], 0.0)
    tglob = j * tm + lax.broadcasted_iota(jnp.int32, (tm, 1), 0)
    pooled = []
    for gi, win in enumerate(POOL_WINDOWS):
        cols = slice(gi * POOL_GW, (gi + 1) * POOL_GW)
        acc = slab_ref[hl - win // 2:hl - win // 2 + tm, cols]
        for off in range(-win // 2 + 1, win // 2):
            acc = acc + slab_ref[hl + off:hl + off + tm, cols]
        lo = jnp.maximum(tglob - win // 2, 0)
        hi = jnp.minimum(tglob - win // 2 + win, seq_len)
        cnt = (hi - lo).astype(F32)
        pg = acc / cnt - slab_ref[hl:hl + tm, cols]
        pooled.append(_dot(pg.astype(BF16), wpool_ref[gi]))
    pool = jnp.concatenate(pooled, axis=1) * pscale_ref[...]

    ones = _head_ones()
    y = yf_ref[...] + yb_ref[...]
    yc = y - _headsum(y, ones) * (1.0 / HEAD)
    var = _headsum(yc * yc, ones) * (1.0 / HEAD)
    bonus = jnp.concatenate([bonus0_ref[...], bonus1_ref[...]], axis=0)
    gate = jnp.concatenate([g0_ref[...], g1_ref[...]], axis=0)
    rw = (yc * lax.rsqrt(var + GN_EPS) * gn_ref[0:1, :] + gn_ref[1:2, :] + bonus) * gate
    mix = _dot(jnp.concatenate([rw, pool], axis=1).astype(BF16), wout_ref[...])
    gt_a = mod_ref[0:1, 2 * d:3 * d]
    o_ref[...] = x_ref[...] + _pos_tile(rowtab_ref, coltab_ref) + gt_a * mix


def _mix_call(x2, rowtab, coltab, yf, yb, bonus, g, p, gn, w_pool_bf, pscale, w_out_bf, mod):
    s, d = x2.shape
    tm = TM_PROJ
    nx = s // tm
    hb = tm // POOL_HALO
    const = lambda j: (0, 0)
    tok = lambda j: (j, 0)
    seq_spec = pl.BlockSpec((tm, D_RWKV), tok)
    assert tm == 2 * TM_FEAT
    feat_lo = pl.BlockSpec((TM_FEAT, D_RWKV), lambda j: (2 * j + FEAT_SHIFT_BLOCKS, 0))
    feat_hi = pl.BlockSpec((TM_FEAT, D_RWKV), lambda j: (2 * j + FEAT_SHIFT_BLOCKS + 1, 0))
    return pl.pallas_call(
        functools.partial(_mix_body, n_x_tiles=nx, seq_len=s),
        grid=(nx,),
        in_specs=[pl.BlockSpec((tm, d), tok),
                  pl.BlockSpec((tm // GRID_W, d // 2), tok),
                  pl.BlockSpec((tm, d // 2), const),
                  seq_spec, seq_spec, feat_lo, feat_hi, feat_lo, feat_hi, seq_spec,
                  pl.BlockSpec((POOL_HALO, D_POOL), lambda j: (jnp.maximum(j * hb - 1, 0), 0)),
                  pl.BlockSpec((POOL_HALO, D_POOL), lambda j: ((j + 1) * hb, 0)),
                  pl.BlockSpec((2, D_RWKV), const),
                  pl.BlockSpec(w_pool_bf.shape, lambda j: (0, 0, 0)),
                  pl.BlockSpec((1, D_POOL), const),
                  pl.BlockSpec(w_out_bf.shape, const),
                  pl.BlockSpec(mod.shape, const)],
        out_specs=pl.BlockSpec((tm, d), tok),
        out_shape=jax.ShapeDtypeStruct((s, d), F32),
        scratch_shapes=[pltpu.VMEM((tm + 2 * POOL_HALO, D_POOL), F32)],
        compiler_params=_cparams(),
        name="mix",
    )(x2, rowtab, coltab, yf, yb, bonus, bonus, g, g, p, p, p, gn, w_pool_bf, pscale, w_out_bf, mod)


def _to_token_tiles(ref, x):
    tm = x.shape[0]
    for j in range(x.shape[1] // LANES):
        ref[pl.ds(j, tm, stride=SUBLANES), :] = x[:, j * LANES:(j + 1) * LANES]


def _from_token_tiles(ref, tm):
    return jnp.concatenate([ref[pl.ds(j, tm, stride=SUBLANES), :] for j in range(SUBLANES)], axis=1)


def _router_body(x_ref, g2_ref, mod_ref, wr_ref, br_ref, h3_ref, eid_ref, gate_ref):
    tm, d = x_ref.shape
    x = x_ref[...]
    ms = jnp.mean(x * x, axis=-1, keepdims=True)
    xn = x * lax.rsqrt(ms + NORM_EPS) * g2_ref[...]
    h = xn * (1.0 + mod_ref[0:1, 4 * d:5 * d]) + mod_ref[0:1, 3 * d:4 * d]
    _to_token_tiles(h3_ref, h)

    wh, wl = _split2(wr_ref[...])
    hh, hl = _split2(h)
    logits = _dot_nt(wh, hh) + (_dot_nt(wh, hl) + _dot_nt(wl, hh)) + br_ref[...]
    gl = logits[0:N_GROUPS, :]
    gmax = jnp.max(gl, axis=0, keepdims=True)
    gidx = lax.broadcasted_iota(jnp.int32, gl.shape, 0).astype(F32)
    grp = jnp.min(jnp.where(gl == gmax, gidx, float(N_GROUPS)), axis=0, keepdims=True)
    p_grp = 1.0 / jnp.sum(jnp.exp(gl - gmax), axis=0, keepdims=True)
    sel = jnp.zeros((EXPERTS_PER_GROUP, tm), F32)
    for gi in range(N_GROUPS):
        lo = N_GROUPS + gi * EXPERTS_PER_GROUP
        sel = jnp.where(grp == float(gi), logits[lo:lo + EXPERTS_PER_GROUP, :], sel)
    eidx = lax.broadcasted_iota(jnp.int32, sel.shape, 0).astype(F32)
    top1 = jnp.max(sel, axis=0, keepdims=True)
    i1 = jnp.min(jnp.where(sel == top1, eidx, float(EXPERTS_PER_GROUP)), axis=0, keepdims=True)
    sel2 = jnp.where(eidx == i1, -jnp.inf, sel)
    top2 = jnp.max(sel2, axis=0, keepdims=True)
    i2 = jnp.min(jnp.where(sel2 == top2, eidx, float(EXPERTS_PER_GROUP)), axis=0, keepdims=True)
    e2 = jnp.exp(top2 - top1)
    inv = 1.0 / (1.0 + e2)
    zf = jnp.zeros((6, tm), F32)
    gate_ref[...] = jnp.concatenate([p_grp * inv, p_grp * (e2 * inv), zf], axis=0)
    base = grp * float(EXPERTS_PER_GROUP)
    eid_ref[...] = jnp.concatenate([base + i1, base + i2, zf], axis=0).astype(jnp.int32)


def _router_call(x_mid, g2n, mod, wr, br):
    s, d = x_mid.shape
    tm = TM_PROJ
    const = lambda j: (0, 0)
    return pl.pallas_call(
        _router_body,
        grid=(s // tm,),
        in_specs=[pl.BlockSpec((tm, d), lambda j: (j, 0)),
                  pl.BlockSpec((1, d), const),
                  pl.BlockSpec(mod.shape, const),
                  pl.BlockSpec(wr.shape, const),
                  pl.BlockSpec(br.shape, const)],
        out_specs=[pl.BlockSpec((tm * SUBLANES, LANES), lambda j: (j, 0)),
                   pl.BlockSpec((8, tm), lambda j: (0, j)),
                   pl.BlockSpec((8, tm), lambda j: (0, j))],
        out_shape=[jax.ShapeDtypeStruct((s * SUBLANES, LANES), F32),
                   jax.ShapeDtypeStruct((8, s), jnp.int32),
                   jax.ShapeDtypeStruct((8, s), F32)],
        compiler_params=_cparams(),
        name="router",
    )(x_mid, g2n, mod, wr, br)


def _sc_gather_tiles(table3, idx):
    n_out = idx.shape[0]
    n_workers = V7X_SC_CORES * V7X_SC_SUBCORES
    per_worker = n_out // n_workers
    n_chunks = per_worker // SC_GATHER_ROWS
    assert per_worker * n_workers == n_out and n_chunks * SC_GATHER_ROWS == per_worker and n_chunks % 2 == 0
    mesh = plsc.VectorSubcoreMesh(core_axis_name="c", subcore_axis_name="s",
                                  num_cores=V7X_SC_CORES, num_subcores=V7X_SC_SUBCORES)

    def body(table_hbm, idx_hbm, out_hbm, idx_v, rows0, rows1, gsem0, gsem1, wsem0, wsem1):
        worker = lax.axis_index("s") * V7X_SC_CORES + lax.axis_index("c")
        base = worker * per_worker
        pltpu.sync_copy(idx_hbm.at[pl.ds(base, per_worker)], idx_v)

        def gather(j, buf, sem):
            off = pl.multiple_of(j * SC_GATHER_ROWS, SC_GATHER_ROWS)
            return pltpu.make_async_copy(table_hbm.at[idx_v.at[pl.ds(off, SC_GATHER_ROWS)]], buf, sem)

        def write(j, buf, sem):
            off = pl.multiple_of(j * SC_GATHER_ROWS, SC_GATHER_ROWS)
            return pltpu.make_async_copy(buf, out_hbm.at[pl.ds(base + off, SC_GATHER_ROWS)], sem)

        gather(0, rows0, gsem0).start()

        @pl.loop(0, n_chunks, step=2)
        def _(j):
            gather(j, rows0, gsem0).wait()

            @pl.when(j >= 2)
            def _():
                write(j - 1, rows1, wsem1).wait()

            gather(j + 1, rows1, gsem1).start()
            write(j, rows0, wsem0).start()
            gather(j + 1, rows1, gsem1).wait()
            write(j, rows0, wsem0).wait()

            @pl.when(j + 2 < n_chunks)
            def _():
                gather(j + 2, rows0, gsem0).start()

            write(j + 1, rows1, wsem1).start()

        write(n_chunks - 1, rows1, wsem1).wait()

    rows = pltpu.VMEM((SC_GATHER_ROWS,) + table3.shape[1:], table3.dtype)
    return pl.kernel(
        body,
        out_type=jax.ShapeDtypeStruct((n_out,) + table3.shape[1:], table3.dtype),
        mesh=mesh,
        scratch_types=[pltpu.VMEM((per_worker,), jnp.int32), rows, rows,
                       pltpu.SemaphoreType.DMA, pltpu.SemaphoreType.DMA,
                       pltpu.SemaphoreType.DMA, pltpu.SemaphoreType.DMA],
        name="sc_gather",
    )(table3, idx)


def _experts_body(blk_e_ref, n_used_ref, first_ref, next_e_ref, wslot_ref, xs_ref, wg_hbm, wu_hbm, wd_hbm, y_ref,
                  wg_buf, wu_buf, wd_buf, wsem_ref):
    i = pl.program_id(0)
    used = i < n_used_ref[0]
    slot = wslot_ref[i]

    def weight_copies(expert, to_slot):
        return [pltpu.make_async_copy(hbm.at[expert], buf.at[to_slot], wsem_ref.at[to_slot])
                for hbm, buf in ((wg_hbm, wg_buf), (wu_hbm, wu_buf), (wd_hbm, wd_buf))]

    @pl.when(i == 0)
    def _():
        for cp in weight_copies(blk_e_ref[0], 0):
            cp.start()

    @pl.when(jnp.logical_and(used, first_ref[i] == 1))
    def _():
        for cp in weight_copies(blk_e_ref[i], slot):
            cp.wait()

        @pl.when(next_e_ref[i] != blk_e_ref[i])
        def _():
            for cp in weight_copies(next_e_ref[i], 1 - slot):
                cp.start()

    @pl.when(used)
    def _():
        xb = _from_token_tiles(xs_ref, MOE_BM).astype(BF16)
        gate = _dot(xb, wg_buf[slot].astype(BF16))
        upp = _dot(xb, wu_buf[slot].astype(BF16))
        hid = (gate * _sigmoid(gate)) * upp
        _to_token_tiles(y_ref, _dot(hid.astype(BF16), wd_buf[slot].astype(BF16)))

    @pl.when(jnp.logical_not(used))
    def _():
        y_ref[...] = jnp.zeros_like(y_ref)


def _experts_call(blk_expert, n_used, first, next_e, wslot, xs3, wg, wu, wd):
    nb = blk_expert.shape[0]
    d = wg.shape[1]
    rows = MOE_BM * SUBLANES
    blk = lambda i, *_: (i, 0)
    grid_spec = pltpu.PrefetchScalarGridSpec(
        num_scalar_prefetch=5,
        grid=(nb,),
        in_specs=[pl.BlockSpec((rows, LANES), blk),
                  pl.BlockSpec(memory_space=pl.ANY),
                  pl.BlockSpec(memory_space=pl.ANY),
                  pl.BlockSpec(memory_space=pl.ANY)],
        out_specs=pl.BlockSpec((rows, LANES), blk),
        scratch_shapes=[pltpu.VMEM((2, d, D_EXPERT), F32), pltpu.VMEM((2, d, D_EXPERT), F32),
                        pltpu.VMEM((2, D_EXPERT, d), F32), pltpu.SemaphoreType.DMA((2,))],
    )
    return pl.pallas_call(
        _experts_body,
        grid_spec=grid_spec,
        out_shape=jax.ShapeDtypeStruct((nb * rows, LANES), F32),
        compiler_params=_cparams(),
        name="experts",
    )(blk_expert, n_used, first, next_e, wslot, xs3, wg, wu, wd)


def _combine_body(x_ref, y0_ref, y1_ref, gate_ref, mod_ref, fg_ref, o_ref):
    tm, d = x_ref.shape
    y = gate_ref[:, 0:1] * _from_token_tiles(y0_ref, tm) + gate_ref[:, 1:2] * _from_token_tiles(y1_ref, tm)
    x = x_ref[...] + mod_ref[0:1, 5 * d:6 * d] * y
    ms = jnp.mean(x * x, axis=-1, keepdims=True)
    o_ref[...] = x * lax.rsqrt(ms + NORM_EPS) * fg_ref[...]


def _combine_call(x_mid, y3, gcol, mod, fg):
    s, d = x_mid.shape
    tm = TM_PROJ
    nt = s // tm
    const = lambda j: (0, 0)
    return pl.pallas_call(
        _combine_body,
        grid=(nt,),
        in_specs=[pl.BlockSpec((tm, d), lambda j: (j, 0)),
                  pl.BlockSpec((tm * SUBLANES, LANES), lambda j: (j, 0)),
                  pl.BlockSpec((tm * SUBLANES, LANES), lambda j: (j + nt, 0)),
                  pl.BlockSpec((tm, 2), lambda j: (j, 0)),
                  pl.BlockSpec(mod.shape, const),
                  pl.BlockSpec((1, d), const)],
        out_specs=pl.BlockSpec((tm, d), lambda j: (j, 0)),
        out_shape=jax.ShapeDtypeStruct((s, d), F32),
        compiler_params=_cparams(),
        name="combine",
    )(x_mid, y3, y3, gcol, mod, fg)


def _blockdiag2(wf, wb):
    z = jnp.zeros_like(wf)
    return jnp.concatenate([jnp.concatenate([wf, z], axis=1), jnp.concatenate([z, wb], axis=1)], axis=0)


def _routing_plan(eids, n_tok):
    m = 2 * n_tok
    experts = jnp.arange(N_EXPERTS, dtype=jnp.int32)
    flat_e = eids[:2].reshape(m)
    order = jnp.argsort(flat_e, stable=True).astype(jnp.int32)
    onehot = (flat_e[:, None] == experts[None, :]).astype(jnp.int32)
    csum = jnp.cumsum(onehot, axis=0)
    counts = csum[-1]
    starts = jnp.cumsum(counts) - counts
    pcounts = (counts + MOE_BM - 1) // MOE_BM * MOE_BM
    pends = jnp.cumsum(pcounts)
    pstarts = pends - pcounts
    n_blocks = (m + N_EXPERTS * (MOE_BM - 1) + MOE_BM - 1) // MOE_BM
    slot = jnp.arange(n_blocks * MOE_BM, dtype=jnp.int32)
    slot_e = jnp.minimum(jnp.sum((slot[:, None] >= pends[None, :]).astype(jnp.int32), axis=1), N_EXPERTS - 1)
    sel = (slot_e[:, None] == experts[None, :]).astype(jnp.int32)
    local = slot - jnp.sum(sel * pstarts[None, :], axis=1)
    valid = local < jnp.sum(sel * counts[None, :], axis=1)
    pair = order[jnp.clip(jnp.sum(sel * starts[None, :], axis=1) + local, 0, m - 1)]
    src_tok = jnp.where(valid, jnp.where(pair >= n_tok, pair - n_tok, pair), slot % n_tok)
    dest = jnp.sum(onehot * (csum - 1 + pstarts[None, :]), axis=1).astype(jnp.int32)
    blk_expert = slot_e[::MOE_BM]
    n_used = (pends[-1:] // MOE_BM).astype(jnp.int32)
    blk = jnp.arange(n_blocks, dtype=jnp.int32)
    used = blk < n_used[0]
    first = jnp.logical_and(used, jnp.concatenate([jnp.ones((1,), bool), blk_expert[1:] != blk_expert[:-1]]))
    none = jnp.int32(N_EXPERTS)
    later_first = lax.cummin(jnp.where(first, blk_expert, none), axis=0, reverse=True)
    next_e = jnp.concatenate([later_first[1:], none[None]])
    next_e = jnp.where(next_e == none, blk_expert, next_e)
    wslot = (jnp.cumsum(first.astype(jnp.int32)) - 1) & 1
    return src_tok, dest, blk_expert, n_used, first.astype(jnp.int32), next_e, wslot


def kernel(x, c, ctx, c_ctx, norm1_g, w_mod, b_mod, w_in, mu_shift, w0_f, w2_f, w0_b, w2_b, a0_f, a2_f, a0_b, a2_b, g2, k_k, k_a, r_k_f, r_k_b, gn_w, gn_b, w_pool, pool_scale, w_out, norm2_g, w_router_grp, b_router_grp, w_router_exp, b_router_exp, w_gate, w_up, w_down, final_g):
    b, s, d = x.shape
    n_ctx = ctx.shape[1]
    assert b == 1 and c.shape[0] == 1 and w_mod.shape[0] == 1
    assert s % TM_PROJ == 0 and n_ctx % CHUNK == 0 and n_ctx == TM_FEAT and d == 2 * D_RWKV
    x2 = x[0]
    rows = s // GRID_W

    quarter = d // 4
    freq = POS_THETA ** (-jnp.arange(quarter, dtype=F32) / quarter)
    rarg = jnp.arange(rows, dtype=F32)[:, None] * freq
    carg = jnp.arange(GRID_W, dtype=F32)[:, None] * freq
    rowtab = jnp.concatenate([jnp.sin(rarg), jnp.cos(rarg)], axis=-1)
    coltab = jnp.tile(jnp.concatenate([jnp.sin(carg), jnp.cos(carg)], axis=-1), (TM_PROJ // GRID_W, 1))

    cs = jnp.concatenate([c, c_ctx[None, :], jnp.zeros((6, d), F32)], axis=0)
    mod = _mod_call(cs, w_mod[0], b_mod)

    ctx_pad = jnp.concatenate([ctx[0], jnp.zeros((TM_PROJ - n_ctx, d), F32)], axis=0)
    par = jnp.concatenate([w0_f, w0_b, a0_f, a0_b, k_k, k_a, r_k_f, r_k_b], axis=0)
    outs = _front_call(x2, ctx_pad, rowtab, coltab, norm1_g, mod, w_in[0].astype(BF16), mu_shift, par,
                       _blockdiag2(w2_f[0], w2_b[0]).astype(BF16), _blockdiag2(a2_f[0], a2_b[0]).astype(BF16),
                       g2[0].astype(BF16))
    p, feats, gamf, gamb, bonus, g = outs[0], outs[1:10], outs[10], outs[11], outs[12], outs[13]
    yf, yb = _scan_call(feats, gamf.reshape(-1, 1, D_RWKV), gamb.reshape(-1, 1, D_RWKV),
                        s // CHUNK, n_ctx // CHUNK)

    gn = jnp.concatenate([gn_w, gn_b], axis=0)
    x_mid = _mix_call(x2, rowtab, coltab, yf, yb, bonus, g, p, gn, w_pool[0].astype(BF16), pool_scale,
                      w_out[0].astype(BF16), mod)

    wr = jnp.concatenate([w_router_grp[0].T, w_router_exp[0].T,
                          jnp.zeros((ROUTER_ROWS - N_GROUPS - N_EXPERTS, d), F32)], axis=0)
    br = jnp.concatenate([b_router_grp[0], b_router_exp[0],
                          jnp.zeros((ROUTER_ROWS - N_GROUPS - N_EXPERTS,), F32)])[:, None]
    h3, eids, gates = _router_call(x_mid, norm2_g, mod, wr, br)

    src_tok, dest, blk_expert, n_used, first, next_e, wslot = _routing_plan(eids, s)
    tile = (SUBLANES, LANES)
    xs3 = _sc_gather_tiles(h3.reshape((s,) + tile), src_tok)
    yexp = _experts_call(blk_expert, n_used, first, next_e, wslot, xs3.reshape(-1, LANES),
                         w_gate[0], w_up[0], w_down[0])
    y3 = _sc_gather_tiles(yexp.reshape((-1,) + tile), dest).reshape(-1, LANES)
    out = _combine_call(x_mid, y3, gates[:2].T, mod, final_g[None, :])
    return out[None]
```

```python
import functools

import jax
import jax.numpy as jnp
from jax import lax
from jax.experimental import pallas as pl
from jax.experimental.pallas import tpu as pltpu
from jax.experimental.pallas import tpu_sc as plsc

F32 = jnp.float32
BF16 = jnp.bfloat16

GRID_W = 64
SUBLANES, LANES = 8, 128
HEAD = 64
N_HEADS = 8
D_RWKV = HEAD * N_HEADS
D_POOL = 512
POOL_WINDOWS = (2, 4, 8, 16)
POOL_GW = 128
POOL_HALO = 8
D_SHIFT = 1920
N_GROUPS = 4
EXPERTS_PER_GROUP = 8
N_EXPERTS = 32
D_EXPERT = 512
NORM_EPS = 1e-6
GN_EPS = 64e-5
POS_THETA = 10000.0
DECAY_SCALE = 0.6065306597

CHUNK = 64
QUAD = 4 * HEAD
SCAN_CPS = 4
TM_PROJ = 512
TM_FEAT = 256
FEAT_SHIFT_BLOCKS = 3
MOE_BM = 256
MOE_PARTS = 2
ROUTER_ROWS = 48
V7X_VMEM_LIMIT = 56 * 1024 * 1024
V7X_SC_CORES, V7X_SC_SUBCORES = 2, 16
SC_GATHER_ROWS = 32


def _cparams(n_axes=1, vmem=V7X_VMEM_LIMIT):
    return pltpu.CompilerParams(dimension_semantics=("arbitrary",) * n_axes,
                                vmem_limit_bytes=vmem)


def _dot(a, b):
    return jnp.dot(a, b, preferred_element_type=F32)


def _dot_nt(a, b):
    return lax.dot_general(a, b, (((1,), (1,)), ((), ())), preferred_element_type=F32)


def _dot_tn(a, b):
    return lax.dot_general(a, b, (((0,), (0,)), ((), ())), preferred_element_type=F32)


def _split2(x):
    hi = x.astype(BF16)
    lo = (x - hi.astype(F32)).astype(BF16)
    return hi, lo


def _dot3(a, b):
    ah, al = _split2(a)
    bh, bl = _split2(b)
    return _dot(ah, bh) + (_dot(ah, bl) + _dot(al, bh))


def _sigmoid(x):
    return 0.5 * jnp.tanh(0.5 * x) + 0.5


def _head_ones():
    r = lax.broadcasted_iota(jnp.int32, (D_RWKV, D_RWKV), 0) >> 6
    c = lax.broadcasted_iota(jnp.int32, (D_RWKV, D_RWKV), 1) >> 6
    return (r == c).astype(BF16)


def _headsum(x, ones):
    hi, lo = _split2(x)
    return _dot(hi, ones) + _dot(lo, ones)


def _mod_body(cs_ref, w_ref, b_ref, o_ref):
    a = cs_ref[...]
    a = a * _sigmoid(a)
    o_ref[...] = _dot3(a, w_ref[...]) + b_ref[...]


def _mod_call(cs, w_mod, b_mod):
    d, n = w_mod.shape
    tn = 512
    return pl.pallas_call(
        _mod_body,
        grid=(n // tn,),
        in_specs=[pl.BlockSpec((8, d), lambda j: (0, 0)),
                  pl.BlockSpec((d, tn), lambda j: (0, j)),
                  pl.BlockSpec((1, tn), lambda j: (0, j))],
        out_specs=pl.BlockSpec((8, tn), lambda j: (0, j)),
        out_shape=jax.ShapeDtypeStruct((8, n), F32),
        compiler_params=_cparams(),
        name="mod",
    )(cs, w_mod, b_mod)


def _pos_tile(rowtab_ref, coltab_ref):
    rt = rowtab_ref[...]
    rowpart = jnp.concatenate(
        [jnp.broadcast_to(rt[r:r + 1, :], (GRID_W, rt.shape[1])) for r in range(TM_PROJ // GRID_W)], axis=0)
    return jnp.concatenate([rowpart, coltab_ref[...]], axis=1)


def _features(u, prow, nrow, mu, par_ref, w2_ref, a2_ref, g2_ref, rows, chunk0, outs, consts):
    (af_ref, rf_ref, btf_ref, ktf_ref, ab_ref, rb_ref, btb_ref, ktb_ref, v_ref,
     gamf_ref, gamb_ref, bonus_ref, g_ref) = outs
    tm = u.shape[0]
    rid = lax.broadcasted_iota(jnp.int32, (SUBLANES, 1), 0)

    def shifted(lo, hi):
        uc, m = u[:, lo:hi], mu[:, lo:hi]
        up = pltpu.roll(uc, 1, 0)
        up = jnp.concatenate([jnp.where(rid == 0, prow[:, lo:hi], up[:SUBLANES]), up[SUBLANES:]], axis=0)
        dn = pltpu.roll(uc, tm - 1, 0)
        dn = jnp.concatenate(
            [dn[:-SUBLANES], jnp.where(rid == SUBLANES - 1, nrow[:, lo:hi], dn[-SUBLANES:])], axis=0)
        return (1.0 - m) * uc + (0.5 * m) * (up + dn)

    w0f, w0b, a0f, a0b = par_ref[0:1, :], par_ref[1:2, :], par_ref[2:3, :], par_ref[3:4, :]
    k_k, k_a, rkf, rkb = par_ref[4:5, :], par_ref[5:6, :], par_ref[6:7, :], par_ref[7:8, :]

    r = shifted(0, 512)
    yield
    k = shifted(512, 1024)
    yield
    v = shifted(1024, 1536)
    v_ref[rows, :] = v.astype(BF16)
    yield
    lora = shifted(1536, D_SHIFT)
    zw = _dot(jnp.tanh(lora[:, 0:128]).astype(BF16), w2_ref[...])
    za = _dot(lora[:, 128:256].astype(BF16), a2_ref[...])
    g_ref[rows, :] = _dot(_sigmoid(lora[:, 256:384]).astype(BF16), g2_ref[...])
    yield
    lw_f = -DECAY_SCALE * _sigmoid(w0f + zw[:, :512])
    lw_b = -DECAY_SCALE * _sigmoid(w0b + zw[:, 512:])
    a_f = _sigmoid(a0f + za[:, :512])
    a_b = _sigmoid(a0b + za[:, 512:])
    yield

    ones, earlier, later = consts
    kkr = k * k_k
    kk = kkr * lax.rsqrt(jnp.maximum(_headsum(kkr * kkr, ones), 1e-24))
    yield
    k_f = k * (1.0 + (a_f - 1.0) * k_a)
    k_b = k * (1.0 + (a_b - 1.0) * k_a)
    bonus_ref[rows, :] = _headsum(r * (k_f * rkf + k_b * rkb), ones) * v
    yield

    n_ch = tm // CHUNK

    def visited_before(m, lwd):
        l1, l2 = _split2(lwd)
        return _dot(m, l1) + _dot(m, l2)

    def emit(excl, lwd, a_d, k_d, last_row, a_ref, r_ref, bt_ref, kt_ref, gam_ref):
        cum = excl + lwd
        gam_ref[0, chunk0:chunk0 + n_ch, :] = jnp.exp(jnp.concatenate(
            [cum[c * CHUNK + last_row:c * CHUNK + last_row + 1, :] for c in range(n_ch)], axis=0))
        a_ref[rows, :] = (-kk * jnp.exp(excl)).astype(BF16)
        r_ref[rows, :] = (r * jnp.exp(cum)).astype(BF16)
        yield
        e_neg = jnp.exp(-cum)
        bt_ref[rows, :] = (kk * a_d * e_neg).astype(BF16)
        kt_ref[rows, :] = (k_d * e_neg).astype(BF16)
        yield

    excl_f = visited_before(earlier, lw_f)
    yield
    yield from emit(excl_f, lw_f, a_f, k_f, CHUNK - 1, af_ref, rf_ref, btf_ref, ktf_ref, gamf_ref)
    excl_b = visited_before(later, lw_b)
    yield
    yield from emit(excl_b, lw_b, a_b, k_b, 0, ab_ref, rb_ref, btb_ref, ktb_ref, gamb_ref)


def _front_body(x_ref, ctx_ref, rowtab_ref, coltab_ref, g1_ref, mod_ref, w_ref, mu_ref, par_ref, w2_ref, a2_ref,
                g2_ref, p_ref, af_ref, rf_ref, btf_ref, ktf_ref, ab_ref, rb_ref, btb_ref, ktb_ref, v_ref,
                gamf_ref, gamb_ref, bonus_ref, g_ref, u_ref, unew_ref, carry_ref, *, n_x_tiles):
    i = pl.program_id(0)
    d = x_ref.shape[1]
    d_in = w_ref.shape[1]
    half = TM_FEAT
    last = carry_ref.shape[0] - 1

    @pl.when(i == 0)
    def _():
        u_ref[...] = jnp.zeros_like(u_ref)
        carry_ref[...] = jnp.zeros_like(carry_ref)

    is_ctx = i >= n_x_tiles
    xin = jnp.where(is_ctx, ctx_ref[...], x_ref[...] + _pos_tile(rowtab_ref, coltab_ref))
    ms = jnp.mean(xin * xin, axis=-1, keepdims=True)
    xn = xin * lax.rsqrt(ms + NORM_EPS) * g1_ref[...]
    sh = jnp.where(is_ctx, mod_ref[1:2, 0:d], mod_ref[0:1, 0:d])
    sc = jnp.where(is_ctx, mod_ref[1:2, d:2 * d], mod_ref[0:1, d:2 * d])
    hb = (xn * (1.0 + sc) + sh).astype(BF16)
    pending = [(slice(r0, r0 + half), lo, min(lo + QUAD, d_in))
               for lo in range(0, d_in, QUAD) for r0 in range(0, hb.shape[0], half)]

    def project_next():
        if not pending:
            return
        rs, lo, hi = pending.pop(0)
        uc = _dot(hb[rs, :], w_ref[:, lo:hi])
        if lo < D_SHIFT:
            unew_ref[rs, lo:min(hi, D_SHIFT)] = uc[:, :min(hi, D_SHIFT) - lo]
        if hi > D_SHIFT:
            p_ref[rs, max(lo, D_SHIFT) - D_SHIFT:hi - D_SHIFT] = uc[:, max(lo, D_SHIFT) - lo:]

    outs = (af_ref, rf_ref, btf_ref, ktf_ref, ab_ref, rb_ref, btb_ref, ktb_ref, v_ref,
            gamf_ref, gamb_ref, bonus_ref, g_ref)
    mu = mu_ref[...]
    ctx_block = 2 * n_x_tiles
    blocks = ((2 * i - 3, carry_ref[SUBLANES:, :], carry_ref[SUBLANES - 1:SUBLANES, :], u_ref[0:1, :]),
              (2 * i - 2, u_ref[0:half, :], carry_ref[last:last + 1, :], u_ref[half:half + 1, :]))
    rr = lax.broadcasted_iota(jnp.int32, (half, half), 0)
    cc = lax.broadcasted_iota(jnp.int32, (half, half), 1)
    same = (rr >> 6) == (cc >> 6)
    consts = (_head_ones(), jnp.logical_and(same, cc < rr).astype(BF16),
              jnp.logical_and(same, cc > rr).astype(BF16))
    project_next()
    for hi, (blk, ub, prow, nrow) in enumerate(blocks):
        prev_ok = jnp.logical_and(blk != 0, blk != ctx_block)
        next_ok = jnp.logical_and(blk != ctx_block - 1, blk != ctx_block)
        for _ in _features(ub, jnp.where(prev_ok, prow, 0.0), jnp.where(next_ok, nrow, 0.0), mu, par_ref, w2_ref,
                           a2_ref, g2_ref, slice(hi * half, (hi + 1) * half), hi * (half // CHUNK), outs, consts):
            project_next()
    while pending:
        project_next()
    carry_ref[...] = u_ref[half - SUBLANES:, :]
    u_ref[...] = unew_ref[...]


def _front_call(x2, ctx_pad, rowtab, coltab, g1, mod, w_in_bf, mu, par, w2, a2, g2):
    s, d = x2.shape
    tm = TM_PROJ
    nx = s // tm
    d_in = w_in_bf.shape[1]
    n_steps = nx + 2
    xmap = lambda i: (jnp.minimum(i, nx - 1), 0)
    const = lambda i: (0, 0)
    step = lambda i: (i, 0)
    seq_bf = jax.ShapeDtypeStruct((n_steps * tm, D_RWKV), BF16)
    seq_f32 = jax.ShapeDtypeStruct((n_steps * tm, D_RWKV), F32)
    gam = jax.ShapeDtypeStruct((n_steps, tm // CHUNK, D_RWKV), F32)
    seq_spec = pl.BlockSpec((tm, D_RWKV), step)
    gam_spec = pl.BlockSpec((1, tm // CHUNK, D_RWKV), lambda i: (i, 0, 0))
    return pl.pallas_call(
        functools.partial(_front_body, n_x_tiles=nx),
        grid=(n_steps,),
        in_specs=[pl.BlockSpec((tm, d), xmap),
                  pl.BlockSpec((tm, d), const),
                  pl.BlockSpec((tm // GRID_W, d // 2), xmap),
                  pl.BlockSpec((tm, d // 2), const),
                  pl.BlockSpec((1, d), const),
                  pl.BlockSpec(mod.shape, const),
                  pl.BlockSpec((d, d_in), const),
                  pl.BlockSpec((1, D_SHIFT), const),
                  pl.BlockSpec((8, D_RWKV), const),
                  pl.BlockSpec(w2.shape, const),
                  pl.BlockSpec(a2.shape, const),
                  pl.BlockSpec(g2.shape, const)],
        out_specs=([pl.BlockSpec((tm, D_POOL), lambda i: (jnp.minimum(i, nx), 0))] + [seq_spec] * 9
                   + [gam_spec, gam_spec, seq_spec, seq_spec]),
        out_shape=([jax.ShapeDtypeStruct(((nx + 1) * tm, D_POOL), F32)] + [seq_bf] * 9
                   + [gam, gam, seq_f32, seq_f32]),
        scratch_shapes=[pltpu.VMEM((tm, D_SHIFT), F32), pltpu.VMEM((tm, D_SHIFT), F32),
                        pltpu.VMEM((TM_FEAT + SUBLANES, D_SHIFT), F32)],
        compiler_params=_cparams(),
        name="front",
    )(x2, ctx_pad, rowtab, coltab, g1, mod, w_in_bf, mu, par, w2, a2, g2)


def _blockdiag(x, head_of_lane):
    xb = x.astype(BF16)
    zero = jnp.zeros_like(xb)
    return jnp.concatenate([jnp.where(head_of_lane == h, xb, zero) for h in range(QUAD // HEAD)], axis=0)


def _unblock(x, head_of_lane):
    out = jnp.where(head_of_lane == 0, x[0:HEAD, :], 0.0)
    for h in range(1, QUAD // HEAD):
        out = jnp.where(head_of_lane == h, x[h * HEAD:(h + 1) * HEAD, :], out)
    return out


def _scan_chunks(chains):
    lane = lax.broadcasted_iota(jnp.int32, (HEAD, QUAD), 1)
    head_of_lane = lane >> 6
    s_idx = lane & (HEAD - 1)
    t_idx = lax.broadcasted_iota(jnp.int32, (HEAD, QUAD), 0)
    eye = s_idx == t_idx
    lower = s_idx < t_idx
    upper = s_idx > t_idx
    bd = functools.partial(_blockdiag, head_of_lane=head_of_lane)

    def same_block(shift):
        return (s_idx >> shift) == (t_idx >> shift)

    def each(fn):
        return [fn(c) for c in chains]

    def stage_masks(c):
        ll = _dot_nt(jnp.concatenate([c["a"], c["r"]], axis=0),
                     jnp.concatenate([bd(c["bt"]), bd(c["kt"])], axis=0))
        strict = upper if c["reverse"] else lower
        incl = jnp.logical_or(strict, eye)
        c["lab"] = jnp.where(strict, ll[:HEAD, :QUAD], 0.0)
        c["lak"] = jnp.where(strict, ll[:HEAD, QUAD:], 0.0)
        c["mrb"] = jnp.where(incl, ll[HEAD:, :QUAD], 0.0)
        c["mrk"] = jnp.where(incl, ll[HEAD:, QUAD:], 0.0)

    each(stage_masks)

    def stage_square(c):
        ld = jnp.where(same_block(4), c["lab"], 0.0)
        c["tm"] = jnp.where(eye, 1.0, 0.0) + ld
        c["xk"] = _dot(ld.astype(BF16), bd(ld))

    def stage_double(c):
        y = _dot(jnp.concatenate([c["tm"], c["xk"]], axis=0).astype(BF16), bd(c["xk"]))
        c["tm"] = c["tm"] + y[:HEAD]
        c["xk"] = y[HEAD:]

    def stage_last_power(c):
        c["tm"] = c["tm"] + _dot(c["tm"].astype(BF16), bd(c["xk"]))

    each(stage_square)
    each(stage_double)
    each(stage_double)
    each(stage_last_power)
    each(lambda c: c.update(vv=_dot(jnp.concatenate([c["lak"], c["mrk"]], axis=0).astype(BF16), bd(c["v"]))))
    for shift in (4, 5):
        joined = jnp.logical_and(same_block(shift + 1), jnp.logical_not(same_block(shift)))
        each(lambda c: c.update(te=_dot(c["tm"].astype(BF16), bd(jnp.where(joined, c["lab"], 0.0)))))
        each(lambda c: c.update(tm=c["tm"] + _dot(c["te"].astype(BF16), bd(c["tm"]))))

    each(lambda c: c.update(x1=_dot(c["tm"].astype(BF16),
                                    jnp.concatenate([bd(c["a"]), bd(c["vv"][:HEAD])], axis=1))))
    each(lambda c: c.update(x2=_dot(c["mrb"].astype(BF16),
                                    jnp.concatenate([bd(c["x1"][:, :QUAD]), bd(c["x1"][:, QUAD:])], axis=1))))

    def stage_state_terms(c):
        rhs = jnp.concatenate(
            [c["x1"].astype(BF16), jnp.concatenate([jnp.zeros((HEAD, QUAD), BF16), c["v"]], axis=1)], axis=0)
        hat = (jnp.concatenate([c["bt"], c["kt"]], axis=0).astype(F32) * c["gam"]).astype(BF16)
        pz = _dot_tn(hat, rhs)
        c["pq"] = _unblock(pz[:, :QUAD], head_of_lane) + jnp.where(eye, c["gam"], 0.0)
        c["zq"] = _unblock(pz[:, QUAD:], head_of_lane)
        c["qeff"] = c["r"].astype(F32) + c["x2"][:, :QUAD]
        c["yin"] = c["x2"][:, QUAD:] + c["vv"][HEAD:]

    each(stage_state_terms)

    outs, state = [], {}
    for c in chains:
        hst = state.get(c["key"], c["hst"])
        qp = _dot(jnp.concatenate([c["qeff"], c["pq"]], axis=0).astype(BF16), bd(hst))
        state[c["key"]] = qp[HEAD:] + c["zq"]
        outs.append(qp[:HEAD] + c["yin"])
    return outs, state


def _scan_body(af, rf, btf, ktf, vf, gamf, ab, rb, btb, ktb, vb, gamb, yf_ref, yb_ref, hf_ref, hb_ref):
    @pl.when(pl.program_id(0) == 0)
    def _():
        hf_ref[...] = jnp.zeros_like(hf_ref)
        hb_ref[...] = jnp.zeros_like(hb_ref)

    names = ("a", "r", "bt", "kt", "v")
    chains, sinks, state_refs = [], [], {}
    for ci in range(SCAN_CPS):
        for refs, y_ref, h_ref, reverse in (((af, rf, btf, ktf, vf, gamf), yf_ref, hf_ref, False),
                                            ((ab, rb, btb, ktb, vb, gamb), yb_ref, hb_ref, True)):
            chunk = SCAN_CPS - 1 - ci if reverse else ci
            rows = slice(chunk * CHUNK, (chunk + 1) * CHUNK)
            for q in range(D_RWKV // QUAD):
                sl = slice(q * QUAD, (q + 1) * QUAD)
                chain = {n: ref[rows, sl] for n, ref in zip(names, refs[:5])}
                chain.update(gam=refs[5][chunk, :, sl], hst=h_ref[:, sl], reverse=reverse, key=(reverse, q))
                chains.append(chain)
                sinks.append((y_ref, rows, sl))
                state_refs[(reverse, q)] = (h_ref, sl)
    outs, state = _scan_chunks(chains)
    for y, (y_ref, rows, sl) in zip(outs, sinks):
        y_ref[rows, sl] = y
    for key, (h_ref, sl) in state_refs.items():
        h_ref[:, sl] = state[key]


def _scan_call(feats, gamf, gamb, n_x_chunks, n_c_chunks):
    af, rf, btf, ktf, ab, rb, btb, ktb, v = feats
    assert n_x_chunks % SCAN_CPS == 0 and n_c_chunks % SCAN_CPS == 0
    nxb, ncb = n_x_chunks // SCAN_CPS, n_c_chunks // SCAN_CPS
    n = nxb + ncb
    assert SCAN_CPS * CHUNK == TM_FEAT
    fwd_blk = lambda i: jnp.where(i < ncb, nxb + i, i - ncb)
    bwd_blk = lambda i: n - 1 - i
    fwd_y = lambda i: (fwd_blk(i), 0)
    bwd_y = lambda i: (bwd_blk(i), 0)
    fwd = lambda i: (fwd_blk(i) + FEAT_SHIFT_BLOCKS, 0)
    bwd = lambda i: (bwd_blk(i) + FEAT_SHIFT_BLOCKS, 0)
    fwd3 = lambda i: (fwd_blk(i) + FEAT_SHIFT_BLOCKS, 0, 0)
    bwd3 = lambda i: (bwd_blk(i) + FEAT_SHIFT_BLOCKS, 0, 0)
    blk = (SCAN_CPS * CHUNK, D_RWKV)
    in_specs = ([pl.BlockSpec(blk, fwd)] * 5 + [pl.BlockSpec((SCAN_CPS, 1, D_RWKV), fwd3)]
                + [pl.BlockSpec(blk, bwd)] * 5 + [pl.BlockSpec((SCAN_CPS, 1, D_RWKV), bwd3)])
    y = jax.ShapeDtypeStruct((n * SCAN_CPS * CHUNK, D_RWKV), F32)
    return pl.pallas_call(
        _scan_body,
        grid=(n,),
        in_specs=in_specs,
        out_specs=[pl.BlockSpec(blk, fwd_y), pl.BlockSpec(blk, bwd_y)],
        out_shape=[y, y],
        scratch_shapes=[pltpu.VMEM((HEAD, D_RWKV), F32), pltpu.VMEM((HEAD, D_RWKV), F32)],
        compiler_params=_cparams(),
        name="scan",
    )(af, rf, btf, ktf, v, gamf, ab, rb, btb, ktb, v, gamb)


def _mix_body(x_ref, rowtab_ref, coltab_ref, yf_ref, yb_ref, bonus0_ref, bonus1_ref, g0_ref, g1_ref,
              pm_ref, pp_ref, pn_ref,
              gn_ref, wpool_ref, pscale_ref, wout_ref, mod_ref, o_ref, slab_ref, *, n_x_tiles, seq_len):
    j = pl.program_id(0)
    tm, d = x_ref.shape
    hl = POOL_HALO
    slab_ref[0:hl, :] = jnp.where(j != 0, pp_ref[...], 0.0)
    slab_ref[hl:hl + tm, :] = pm_ref[...]
    slab_ref[hl + tm:2 * hl + tm, :] = jnp.where(j != n_x_tiles - 1, pn_ref[...], 0.0)
    tglob = j * tm + lax.broadcasted_iota(jnp.int32, (tm, 1), 0)
    pooled = []
    for gi, win in enumerate(POOL_WINDOWS):
        cols = slice(gi * POOL_GW, (gi + 1) * POOL_GW)
        acc = slab_ref[hl - win // 2:hl - win // 2 + tm, cols]
        for off in range(-win // 2 + 1, win // 2):
            acc = acc + slab_ref[hl + off:hl + off + tm, cols]
        lo = jnp.maximum(tglob - win // 2, 0)
        hi = jnp.minimum(tglob - win // 2 + win, seq_len)
        cnt = (hi - lo).astype(F32)
        pg = acc / cnt - slab_ref[hl:hl + tm, cols]
        pooled.append(_dot(pg.astype(BF16), wpool_ref[gi]))
    pool = jnp.concatenate(pooled, axis=1) * pscale_ref[...]

    ones = _head_ones()
    y = yf_ref[...] + yb_ref[...]
    yc = y - _headsum(y, ones) * (1.0 / HEAD)
    var = _headsum(yc * yc, ones) * (1.0 / HEAD)
    bonus = jnp.concatenate([bonus0_ref[...], bonus1_ref[...]], axis=0)
    gate = jnp.concatenate([g0_ref[...], g1_ref[...]], axis=0)
    rw = (yc * lax.rsqrt(var + GN_EPS) * gn_ref[0:1, :] + gn_ref[1:2, :] + bonus) * gate
    mix = _dot(jnp.concatenate([rw, pool], axis=1).astype(BF16), wout_ref[...])
    gt_a = mod_ref[0:1, 2 * d:3 * d]
    o_ref[...] = x_ref[...] + _pos_tile(rowtab_ref, coltab_ref) + gt_a * mix


def _mix_call(x2, rowtab, coltab, yf, yb, bonus, g, p, gn, w_pool_bf, pscale, w_out_bf, mod):
    s, d = x2.shape
    tm = TM_PROJ
    nx = s // tm
    hb = tm // POOL_HALO
    const = lambda j: (0, 0)
    tok = lambda j: (j, 0)
    seq_spec = pl.BlockSpec((tm, D_RWKV), tok)
    assert tm == 2 * TM_FEAT
    feat_lo = pl.BlockSpec((TM_FEAT, D_RWKV), lambda j: (2 * j + FEAT_SHIFT_BLOCKS, 0))
    feat_hi = pl.BlockSpec((TM_FEAT, D_RWKV), lambda j: (2 * j + FEAT_SHIFT_BLOCKS + 1, 0))
    return pl.pallas_call(
        functools.partial(_mix_body, n_x_tiles=nx, seq_len=s),
        grid=(nx,),
        in_specs=[pl.BlockSpec((tm, d), tok),
                  pl.BlockSpec((tm // GRID_W, d // 2), tok),
                  pl.BlockSpec((tm, d // 2), const),
                  seq_spec, seq_spec, feat_lo, feat_hi, feat_lo, feat_hi, seq_spec,
                  pl.BlockSpec((POOL_HALO, D_POOL), lambda j: (jnp.maximum(j * hb - 1, 0), 0)),
                  pl.BlockSpec((POOL_HALO, D_POOL), lambda j: ((j + 1) * hb, 0)),
                  pl.BlockSpec((2, D_RWKV), const),
                  pl.BlockSpec(w_pool_bf.shape, lambda j: (0, 0, 0)),
                  pl.BlockSpec((1, D_POOL), const),
                  pl.BlockSpec(w_out_bf.shape, const),
                  pl.BlockSpec(mod.shape, const)],
        out_specs=pl.BlockSpec((tm, d), tok),
        out_shape=jax.ShapeDtypeStruct((s, d), F32),
        scratch_shapes=[pltpu.VMEM((tm + 2 * POOL_HALO, D_POOL), F32)],
        compiler_params=_cparams(),
        name="mix",
    )(x2, rowtab, coltab, yf, yb, bonus, bonus, g, g, p, p, p, gn, w_pool_bf, pscale, w_out_bf, mod)


def _to_token_tiles(ref, x):
    tm = x.shape[0]
    for j in range(x.shape[1] // LANES):
        ref[pl.ds(j, tm, stride=SUBLANES), :] = x[:, j * LANES:(j + 1) * LANES]


def _from_token_tiles(ref, tm):
    return jnp.concatenate([ref[pl.ds(j, tm, stride=SUBLANES), :] for j in range(SUBLANES)], axis=1)


def _router_body(x_ref, g2_ref, mod_ref, wr_ref, br_ref, h3_ref, eid_ref, gate_ref):
    tm, d = x_ref.shape
    x = x_ref[...]
    ms = jnp.mean(x * x, axis=-1, keepdims=True)
    xn = x * lax.rsqrt(ms + NORM_EPS) * g2_ref[...]
    h = xn * (1.0 + mod_ref[0:1, 4 * d:5 * d]) + mod_ref[0:1, 3 * d:4 * d]
    _to_token_tiles(h3_ref, h)

    wh, wl = _split2(wr_ref[...])
    hh, hl = _split2(h)
    logits = _dot_nt(wh, hh) + (_dot_nt(wh, hl) + _dot_nt(wl, hh)) + br_ref[...]
    gl = logits[0:N_GROUPS, :]
    gmax = jnp.max(gl, axis=0, keepdims=True)
    gidx = lax.broadcasted_iota(jnp.int32, gl.shape, 0).astype(F32)
    grp = jnp.min(jnp.where(gl == gmax, gidx, float(N_GROUPS)), axis=0, keepdims=True)
    p_grp = 1.0 / jnp.sum(jnp.exp(gl - gmax), axis=0, keepdims=True)
    sel = jnp.zeros((EXPERTS_PER_GROUP, tm), F32)
    for gi in range(N_GROUPS):
        lo = N_GROUPS + gi * EXPERTS_PER_GROUP
        sel = jnp.where(grp == float(gi), logits[lo:lo + EXPERTS_PER_GROUP, :], sel)
    eidx = lax.broadcasted_iota(jnp.int32, sel.shape, 0).astype(F32)
    top1 = jnp.max(sel, axis=0, keepdims=True)
    i1 = jnp.min(jnp.where(sel == top1, eidx, float(EXPERTS_PER_GROUP)), axis=0, keepdims=True)
    sel2 = jnp.where(eidx == i1, -jnp.inf, sel)
    top2 = jnp.max(sel2, axis=0, keepdims=True)
    i2 = jnp.min(jnp.where(sel2 == top2, eidx, float(EXPERTS_PER_GROUP)), axis=0, keepdims=True)
    e2 = jnp.exp(top2 - top1)
    inv = 1.0 / (1.0 + e2)
    zf = jnp.zeros((6, tm), F32)
    gate_ref[...] = jnp.concatenate([p_grp * inv, p_grp * (e2 * inv), zf], axis=0)
    base = grp * float(EXPERTS_PER_GROUP)
    eid_ref[...] = jnp.concatenate([base + i1, base + i2, zf], axis=0).astype(jnp.int32)


def _router_call(x_mid, g2n, mod, wr, br):
    s, d = x_mid.shape
    tm = TM_PROJ
    const = lambda j: (0, 0)
    return pl.pallas_call(
        _router_body,
        grid=(s // tm,),
        in_specs=[pl.BlockSpec((tm, d), lambda j: (j, 0)),
                  pl.BlockSpec((1, d), const),
                  pl.BlockSpec(mod.shape, const),
                  pl.BlockSpec(wr.shape, const),
                  pl.BlockSpec(br.shape, const)],
        out_specs=[pl.BlockSpec((tm * SUBLANES, LANES), lambda j: (j, 0)),
                   pl.BlockSpec((8, tm), lambda j: (0, j)),
                   pl.BlockSpec((8, tm), lambda j: (0, j))],
        out_shape=[jax.ShapeDtypeStruct((s * SUBLANES, LANES), F32),
                   jax.ShapeDtypeStruct((8, s), jnp.int32),
                   jax.ShapeDtypeStruct((8, s), F32)],
        compiler_params=_cparams(),
        name="router",
    )(x_mid, g2n, mod, wr, br)


def _sc_gather_tiles(table3, idx):
    n_out = idx.shape[0]
    n_workers = V7X_SC_CORES * V7X_SC_SUBCORES
    per_worker = n_out // n_workers
    n_chunks = per_worker // SC_GATHER_ROWS
    assert per_worker * n_workers == n_out and n_chunks * SC_GATHER_ROWS == per_worker and n_chunks % 2 == 0
    mesh = plsc.VectorSubcoreMesh(core_axis_name="c", subcore_axis_name="s",
                                  num_cores=V7X_SC_CORES, num_subcores=V7X_SC_SUBCORES)

    def body(table_hbm, idx_hbm, out_hbm, idx_v, rows0, rows1, gsem0, gsem1, wsem0, wsem1):
        worker = lax.axis_index("s") * V7X_SC_CORES + lax.axis_index("c")
        base = worker * per_worker
        pltpu.sync_copy(idx_hbm.at[pl.ds(base, per_worker)], idx_v)

        def gather(j, buf, sem):
            off = pl.multiple_of(j * SC_GATHER_ROWS, SC_GATHER_ROWS)
            return pltpu.make_async_copy(table_hbm.at[idx_v.at[pl.ds(off, SC_GATHER_ROWS)]], buf, sem)

        def write(j, buf, sem):
            off = pl.multiple_of(j * SC_GATHER_ROWS, SC_GATHER_ROWS)
            return pltpu.make_async_copy(buf, out_hbm.at[pl.ds(base + off, SC_GATHER_ROWS)], sem)

        gather(0, rows0, gsem0).start()

        @pl.loop(0, n_chunks, step=2)
        def _(j):
            gather(j, rows0, gsem0).wait()

            @pl.when(j >= 2)
            def _():
                write(j - 1, rows1, wsem1).wait()

            gather(j + 1, rows1, gsem1).start()
            write(j, rows0, wsem0).start()
            gather(j + 1, rows1, gsem1).wait()
            write(j, rows0, wsem0).wait()

            @pl.when(j + 2 < n_chunks)
            def _():
                gather(j + 2, rows0, gsem0).start()

            write(j + 1, rows1, wsem1).start()

        write(n_chunks - 1, rows1, wsem1).wait()

    rows = pltpu.VMEM((SC_GATHER_ROWS,) + table3.shape[1:], table3.dtype)
    return pl.kernel(
        body,
        out_type=jax.ShapeDtypeStruct((n_out,) + table3.shape[1:], table3.dtype),
        mesh=mesh,
        scratch_types=[pltpu.VMEM((per_worker,), jnp.int32), rows, rows,
                       pltpu.SemaphoreType.DMA, pltpu.SemaphoreType.DMA,
                       pltpu.SemaphoreType.DMA, pltpu.SemaphoreType.DMA],
        name="sc_gather",
    )(table3, idx)


def _experts_body(blk_e_ref, n_used_ref, first_ref, next_e_ref, wslot_ref, xs_ref, wg_hbm, wu_hbm, wd_hbm, *rest):
    y_ref, wg_buf, wu_buf, wd_buf, wsem_ref = rest[-5:]
    i = pl.program_id(0)
    used = i < n_used_ref[0]
    slot = wslot_ref[i]

    def weight_copies(expert, to_slot):
        return [pltpu.make_async_copy(hbm.at[expert], buf.at[to_slot], wsem_ref.at[to_slot])
                for hbm, buf in ((wg_hbm, wg_buf), (wu_hbm, wu_buf), (wd_hbm, wd_buf))]

    @pl.when(jnp.logical_and(i == 0, used))
    def _():
        for cp in weight_copies(blk_e_ref[0], 0):
            cp.start()

    @pl.when(jnp.logical_and(used, first_ref[i] == 1))
    def _():
        for cp in weight_copies(blk_e_ref[i], slot):
            cp.wait()

        @pl.when(next_e_ref[i] != blk_e_ref[i])
        def _():
            for cp in weight_copies(next_e_ref[i], 1 - slot):
                cp.start()

    @pl.when(used)
    def _():
        xb = _from_token_tiles(xs_ref, MOE_BM).astype(BF16)
        gate = _dot(xb, wg_buf[slot].astype(BF16))
        upp = _dot(xb, wu_buf[slot].astype(BF16))
        hid = (gate * _sigmoid(gate)) * upp
        _to_token_tiles(y_ref, _dot(hid.astype(BF16), wd_buf[slot].astype(BF16)))

    @pl.when(jnp.logical_not(used))
    def _():
        y_ref[...] = jnp.zeros_like(y_ref)


def _experts_call(blk_expert, n_used, xs3, wg, wu, wd, blk0, n_blocks_total, y_prev=None):
    nb = blk_expert.shape[0]
    d = wg.shape[1]
    rows = MOE_BM * SUBLANES
    first, next_e, wslot = _weight_schedule(blk_expert, n_used)
    in_specs = [pl.BlockSpec((rows, LANES), lambda i, *_: (i, 0)),
                pl.BlockSpec(memory_space=pl.ANY),
                pl.BlockSpec(memory_space=pl.ANY),
                pl.BlockSpec(memory_space=pl.ANY)]
    args = [blk_expert, n_used, first, next_e, wslot, xs3, wg, wu, wd]
    aliases = {}
    if y_prev is not None:
        in_specs.append(pl.BlockSpec(memory_space=pl.ANY))
        args.append(y_prev)
        aliases = {len(args) - 1: 0}
    grid_spec = pltpu.PrefetchScalarGridSpec(
        num_scalar_prefetch=5,
        grid=(nb,),
        in_specs=in_specs,
        out_specs=pl.BlockSpec((rows, LANES), lambda i, *_: (i + blk0, 0)),
        scratch_shapes=[pltpu.VMEM((2, d, D_EXPERT), F32), pltpu.VMEM((2, d, D_EXPERT), F32),
                        pltpu.VMEM((2, D_EXPERT, d), F32), pltpu.SemaphoreType.DMA((2,))],
    )
    return pl.pallas_call(
        _experts_body,
        grid_spec=grid_spec,
        out_shape=jax.ShapeDtypeStruct((n_blocks_total * rows, LANES), F32),
        input_output_aliases=aliases,
        compiler_params=_cparams(),
        name="experts",
    )(*args)


def _combine_body(x_ref, y0_ref, y1_ref, gate_ref, mod_ref, fg_ref, o_ref):
    tm, d = x_ref.shape
    y = gate_ref[:, 0:1] * _from_token_tiles(y0_ref, tm) + gate_ref[:, 1:2] * _from_token_tiles(y1_ref, tm)
    x = x_ref[...] + mod_ref[0:1, 5 * d:6 * d] * y
    ms = jnp.mean(x * x, axis=-1, keepdims=True)
    o_ref[...] = x * lax.rsqrt(ms + NORM_EPS) * fg_ref[...]


def _combine_call(x_mid, y3, gcol, mod, fg):
    s, d = x_mid.shape
    tm = TM_PROJ
    nt = s // tm
    const = lambda j: (0, 0)
    return pl.pallas_call(
        _combine_body,
        grid=(nt,),
        in_specs=[pl.BlockSpec((tm, d), lambda j: (j, 0)),
                  pl.BlockSpec((tm * SUBLANES, LANES), lambda j: (j, 0)),
                  pl.BlockSpec((tm * SUBLANES, LANES), lambda j: (j + nt, 0)),
                  pl.BlockSpec((tm, 2), lambda j: (j, 0)),
                  pl.BlockSpec(mod.shape, const),
                  pl.BlockSpec((1, d), const)],
        out_specs=pl.BlockSpec((tm, d), lambda j: (j, 0)),
        out_shape=jax.ShapeDtypeStruct((s, d), F32),
        compiler_params=_cparams(),
        name="combine",
    )(x_mid, y3, y3, gcol, mod, fg)


def _blockdiag2(wf, wb):
    z = jnp.zeros_like(wf)
    return jnp.concatenate([jnp.concatenate([wf, z], axis=1), jnp.concatenate([z, wb], axis=1)], axis=0)


def _routing_plan(eids, n_tok):
    m = 2 * n_tok
    experts = jnp.arange(N_EXPERTS, dtype=jnp.int32)
    flat_e = eids[:2].reshape(m)
    order = jnp.argsort(flat_e, stable=True).astype(jnp.int32)
    onehot = (flat_e[:, None] == experts[None, :]).astype(jnp.int32)
    csum = jnp.cumsum(onehot, axis=0)
    counts = csum[-1]
    starts = jnp.cumsum(counts) - counts
    pcounts = (counts + MOE_BM - 1) // MOE_BM * MOE_BM
    pends = jnp.cumsum(pcounts)
    pstarts = pends - pcounts
    n_blocks = (m + N_EXPERTS * (MOE_BM - 1) + MOE_BM - 1) // MOE_BM
    slot = jnp.arange(n_blocks * MOE_BM, dtype=jnp.int32)
    slot_e = jnp.minimum(jnp.sum((slot[:, None] >= pends[None, :]).astype(jnp.int32), axis=1), N_EXPERTS - 1)
    sel = (slot_e[:, None] == experts[None, :]).astype(jnp.int32)
    local = slot - jnp.sum(sel * pstarts[None, :], axis=1)
    valid = local < jnp.sum(sel * counts[None, :], axis=1)
    pair = order[jnp.clip(jnp.sum(sel * starts[None, :], axis=1) + local, 0, m - 1)]
    src_tok = jnp.where(valid, jnp.where(pair >= n_tok, pair - n_tok, pair), slot % n_tok)
    dest = jnp.sum(onehot * (csum - 1 + pstarts[None, :]), axis=1).astype(jnp.int32)
    blk_expert = slot_e[::MOE_BM]
    n_used = (pends[-1:] // MOE_BM).astype(jnp.int32)
    return src_tok, dest, blk_expert, n_used


def _weight_schedule(blk_expert, n_used):
    used = jnp.arange(blk_expert.shape[0], dtype=jnp.int32) < n_used[0]
    first = jnp.logical_and(used, jnp.concatenate([jnp.ones((1,), bool), blk_expert[1:] != blk_expert[:-1]]))
    none = jnp.int32(N_EXPERTS)
    later_first = lax.cummin(jnp.where(first, blk_expert, none), axis=0, reverse=True)
    next_e = jnp.concatenate([later_first[1:], none[None]])
    next_e = jnp.where(next_e == none, blk_expert, next_e)
    wslot = jnp.maximum(jnp.cumsum(first.astype(jnp.int32)) - 1, 0) & 1
    return first.astype(jnp.int32), next_e, wslot


def kernel(x, c, ctx, c_ctx, norm1_g, w_mod, b_mod, w_in, mu_shift, w0_f, w2_f, w0_b, w2_b, a0_f, a2_f, a0_b, a2_b, g2, k_k, k_a, r_k_f, r_k_b, gn_w, gn_b, w_pool, pool_scale, w_out, norm2_g, w_router_grp, b_router_grp, w_router_exp, b_router_exp, w_gate, w_up, w_down, final_g):
    b, s, d = x.shape
    n_ctx = ctx.shape[1]
    assert b == 1 and c.shape[0] == 1 and w_mod.shape[0] == 1
    assert s % TM_PROJ == 0 and n_ctx % CHUNK == 0 and n_ctx == TM_FEAT and d == 2 * D_RWKV
    x2 = x[0]
    rows = s // GRID_W

    quarter = d // 4
    freq = POS_THETA ** (-jnp.arange(quarter, dtype=F32) / quarter)
    rarg = jnp.arange(rows, dtype=F32)[:, None] * freq
    carg = jnp.arange(GRID_W, dtype=F32)[:, None] * freq
    rowtab = jnp.concatenate([jnp.sin(rarg), jnp.cos(rarg)], axis=-1)
    coltab = jnp.tile(jnp.concatenate([jnp.sin(carg), jnp.cos(carg)], axis=-1), (TM_PROJ // GRID_W, 1))

    cs = jnp.concatenate([c, c_ctx[None, :], jnp.zeros((6, d), F32)], axis=0)
    mod = _mod_call(cs, w_mod[0], b_mod)

    ctx_pad = jnp.concatenate([ctx[0], jnp.zeros((TM_PROJ - n_ctx, d), F32)], axis=0)
    par = jnp.concatenate([w0_f, w0_b, a0_f, a0_b, k_k, k_a, r_k_f, r_k_b], axis=0)
    outs = _front_call(x2, ctx_pad, rowtab, coltab, norm1_g, mod, w_in[0].astype(BF16), mu_shift, par,
                       _blockdiag2(w2_f[0], w2_b[0]).astype(BF16), _blockdiag2(a2_f[0], a2_b[0]).astype(BF16),
                       g2[0].astype(BF16))
    p, feats, gamf, gamb, bonus, g = outs[0], outs[1:10], outs[10], outs[11], outs[12], outs[13]
    yf, yb = _scan_call(feats, gamf.reshape(-1, 1, D_RWKV), gamb.reshape(-1, 1, D_RWKV),
                        s // CHUNK, n_ctx // CHUNK)

    gn = jnp.concatenate([gn_w, gn_b], axis=0)
    x_mid = _mix_call(x2, rowtab, coltab, yf, yb, bonus, g, p, gn, w_pool[0].astype(BF16), pool_scale,
                      w_out[0].astype(BF16), mod)

    wr = jnp.concatenate([w_router_grp[0].T, w_router_exp[0].T,
                          jnp.zeros((ROUTER_ROWS - N_GROUPS - N_EXPERTS, d), F32)], axis=0)
    br = jnp.concatenate([b_router_grp[0], b_router_exp[0],
                          jnp.zeros((ROUTER_ROWS - N_GROUPS - N_EXPERTS,), F32)])[:, None]
    h3, eids, gates = _router_call(x_mid, norm2_g, mod, wr, br)

    src_tok, dest, blk_expert, n_used = _routing_plan(eids, s)
    tile = (SUBLANES, LANES)
    h3t = h3.reshape((s,) + tile)
    nb = blk_expert.shape[0]
    yexp = None
    for part in range(MOE_PARTS):
        b0, b1 = part * nb // MOE_PARTS, (part + 1) * nb // MOE_PARTS
        xs3 = _sc_gather_tiles(h3t, src_tok[b0 * MOE_BM:b1 * MOE_BM])
        yexp = _experts_call(blk_expert[b0:b1], jnp.clip(n_used - b0, 0, b1 - b0), xs3.reshape(-1, LANES),
                             w_gate[0], w_up[0], w_down[0], b0, nb, yexp)
    y3 = _sc_gather_tiles(yexp.reshape((-1,) + tile), dest).reshape(-1, LANES)
    out = _combine_call(x_mid, y3, gates[:2].T, mod, final_g[None, :])
    return out[None]
```

```python
import functools

import jax
import jax.numpy as jnp
from jax import lax
from jax.experimental import pallas as pl
from jax.experimental.pallas import tpu as pltpu
from jax.experimental.pallas import tpu_sc as plsc

F32 = jnp.float32
BF16 = jnp.bfloat16

GRID_W = 64
SUBLANES, LANES = 8, 128
HEAD = 64
N_HEADS = 8
D_RWKV = HEAD * N_HEADS
D_POOL = 512
POOL_WINDOWS = (2, 4, 8, 16)
POOL_GW = 128
POOL_HALO = 8
D_SHIFT = 1920
N_GROUPS = 4
EXPERTS_PER_GROUP = 8
N_EXPERTS = 32
D_EXPERT = 512
NORM_EPS = 1e-6
GN_EPS = 64e-5
POS_THETA = 10000.0
DECAY_SCALE = 0.6065306597

CHUNK = 64
QUAD = 4 * HEAD
SCAN_CPS = 4
TM_PROJ = 512
TM_FEAT = 256
FEAT_SHIFT_BLOCKS = 3
MOE_BM = 256
ROUTER_ROWS = 48
V7X_VMEM_LIMIT = 56 * 1024 * 1024
V7X_SC_CORES, V7X_SC_SUBCORES = 2, 16
SC_GATHER_ROWS = 32


def _cparams(n_axes=1, vmem=V7X_VMEM_LIMIT):
    return pltpu.CompilerParams(dimension_semantics=("arbitrary",) * n_axes,
                                vmem_limit_bytes=vmem)


def _dot(a, b):
    return jnp.dot(a, b, preferred_element_type=F32)


def _dot_nt(a, b):
    return lax.dot_general(a, b, (((1,), (1,)), ((), ())), preferred_element_type=F32)


def _dot_tn(a, b):
    return lax.dot_general(a, b, (((0,), (0,)), ((), ())), preferred_element_type=F32)


def _split2(x):
    hi = x.astype(BF16)
    lo = (x - hi.astype(F32)).astype(BF16)
    return hi, lo


def _dot3(a, b):
    ah, al = _split2(a)
    bh, bl = _split2(b)
    return _dot(ah, bh) + (_dot(ah, bl) + _dot(al, bh))


def _sigmoid(x):
    return 0.5 * jnp.tanh(0.5 * x) + 0.5


def _head_ones():
    r = lax.broadcasted_iota(jnp.int32, (D_RWKV, D_RWKV), 0) >> 6
    c = lax.broadcasted_iota(jnp.int32, (D_RWKV, D_RWKV), 1) >> 6
    return (r == c).astype(BF16)


def _headsum(x, ones):
    hi, lo = _split2(x)
    return _dot(hi, ones) + _dot(lo, ones)


def _mod_body(cs_ref, w_ref, b_ref, o_ref):
    a = cs_ref[...]
    a = a * _sigmoid(a)
    o_ref[...] = _dot3(a, w_ref[...]) + b_ref[...]


def _mod_call(cs, w_mod, b_mod):
    d, n = w_mod.shape
    tn = 512
    return pl.pallas_call(
        _mod_body,
        grid=(n // tn,),
        in_specs=[pl.BlockSpec((8, d), lambda j: (0, 0)),
                  pl.BlockSpec((d, tn), lambda j: (0, j)),
                  pl.BlockSpec((1, tn), lambda j: (0, j))],
        out_specs=pl.BlockSpec((8, tn), lambda j: (0, j)),
        out_shape=jax.ShapeDtypeStruct((8, n), F32),
        compiler_params=_cparams(),
        name="mod",
    )(cs, w_mod, b_mod)


def _pos_tile(rowtab_ref, coltab_ref):
    rt = rowtab_ref[...]
    rowpart = jnp.concatenate(
        [jnp.broadcast_to(rt[r:r + 1, :], (GRID_W, rt.shape[1])) for r in range(TM_PROJ // GRID_W)], axis=0)
    return jnp.concatenate([rowpart, coltab_ref[...]], axis=1)


def _features(u, prow, nrow, mu, par_ref, w2_ref, a2_ref, g2_ref, rows, chunk0, outs, consts):
    (af_ref, rf_ref, btf_ref, ktf_ref, ab_ref, rb_ref, btb_ref, ktb_ref, v_ref,
     gamf_ref, gamb_ref, bonus_ref, g_ref) = outs
    tm = u.shape[0]
    rid = lax.broadcasted_iota(jnp.int32, (SUBLANES, 1), 0)

    def shifted(lo, hi):
        uc, m = u[:, lo:hi], mu[:, lo:hi]
        up = pltpu.roll(uc, 1, 0)
        up = jnp.concatenate([jnp.where(rid == 0, prow[:, lo:hi], up[:SUBLANES]), up[SUBLANES:]], axis=0)
        dn = pltpu.roll(uc, tm - 1, 0)
        dn = jnp.concatenate(
            [dn[:-SUBLANES], jnp.where(rid == SUBLANES - 1, nrow[:, lo:hi], dn[-SUBLANES:])], axis=0)
        return (1.0 - m) * uc + (0.5 * m) * (up + dn)

    w0f, w0b, a0f, a0b = par_ref[0:1, :], par_ref[1:2, :], par_ref[2:3, :], par_ref[3:4, :]
    k_k, k_a, rkf, rkb = par_ref[4:5, :], par_ref[5:6, :], par_ref[6:7, :], par_ref[7:8, :]

    r = shifted(0, 512)
    yield
    k = shifted(512, 1024)
    yield
    v = shifted(1024, 1536)
    v_ref[rows, :] = v.astype(BF16)
    yield
    lora = shifted(1536, D_SHIFT)
    zw = _dot(jnp.tanh(lora[:, 0:128]).astype(BF16), w2_ref[...])
    za = _dot(lora[:, 128:256].astype(BF16), a2_ref[...])
    g_ref[rows, :] = _dot(_sigmoid(lora[:, 256:384]).astype(BF16), g2_ref[...])
    yield
    lw_f = -DECAY_SCALE * _sigmoid(w0f + zw[:, :512])
    lw_b = -DECAY_SCALE * _sigmoid(w0b + zw[:, 512:])
    a_f = _sigmoid(a0f + za[:, :512])
    a_b = _sigmoid(a0b + za[:, 512:])
    yield

    ones, earlier, later = consts
    kkr = k * k_k
    kk = kkr * lax.rsqrt(jnp.maximum(_headsum(kkr * kkr, ones), 1e-24))
    yield
    k_f = k * (1.0 + (a_f - 1.0) * k_a)
    k_b = k * (1.0 + (a_b - 1.0) * k_a)
    bonus_ref[rows, :] = _headsum(r * (k_f * rkf + k_b * rkb), ones) * v
    yield

    n_ch = tm // CHUNK

    def visited_before(m, lwd):
        l1, l2 = _split2(lwd)
        return _dot(m, l1) + _dot(m, l2)

    def emit(excl, lwd, a_d, k_d, last_row, a_ref, r_ref, bt_ref, kt_ref, gam_ref):
        cum = excl + lwd
        gam_ref[0, chunk0:chunk0 + n_ch, :] = jnp.exp(jnp.concatenate(
            [cum[c * CHUNK + last_row:c * CHUNK + last_row + 1, :] for c in range(n_ch)], axis=0))
        a_ref[rows, :] = (-kk * jnp.exp(excl)).astype(BF16)
        r_ref[rows, :] = (r * jnp.exp(cum)).astype(BF16)
        yield
        e_neg = jnp.exp(-cum)
        bt_ref[rows, :] = (kk * a_d * e_neg).astype(BF16)
        kt_ref[rows, :] = (k_d * e_neg).astype(BF16)
        yield

    excl_f = visited_before(earlier, lw_f)
    yield
    yield from emit(excl_f, lw_f, a_f, k_f, CHUNK - 1, af_ref, rf_ref, btf_ref, ktf_ref, gamf_ref)
    excl_b = visited_before(later, lw_b)
    yield
    yield from emit(excl_b, lw_b, a_b, k_b, 0, ab_ref, rb_ref, btb_ref, ktb_ref, gamb_ref)


def _front_body(x_ref, ctx_ref, rowtab_ref, coltab_ref, g1_ref, mod_ref, w_ref, mu_ref, par_ref, w2_ref, a2_ref,
                g2_ref, p_ref, af_ref, rf_ref, btf_ref, ktf_ref, ab_ref, rb_ref, btb_ref, ktb_ref, v_ref,
                gamf_ref, gamb_ref, bonus_ref, g_ref, u_ref, unew_ref, carry_ref, *, n_x_tiles):
    i = pl.program_id(0)
    d = x_ref.shape[1]
    d_in = w_ref.shape[1]
    half = TM_FEAT
    last = carry_ref.shape[0] - 1

    @pl.when(i == 0)
    def _():
        u_ref[...] = jnp.zeros_like(u_ref)
        carry_ref[...] = jnp.zeros_like(carry_ref)

    is_ctx = i >= n_x_tiles
    xin = jnp.where(is_ctx, ctx_ref[...], x_ref[...] + _pos_tile(rowtab_ref, coltab_ref))
    ms = jnp.mean(xin * xin, axis=-1, keepdims=True)
    xn = xin * lax.rsqrt(ms + NORM_EPS) * g1_ref[...]
    sh = jnp.where(is_ctx, mod_ref[1:2, 0:d], mod_ref[0:1, 0:d])
    sc = jnp.where(is_ctx, mod_ref[1:2, d:2 * d], mod_ref[0:1, d:2 * d])
    hb = (xn * (1.0 + sc) + sh).astype(BF16)
    pending = [(slice(r0, r0 + half), lo, min(lo + QUAD, d_in))
               for lo in range(0, d_in, QUAD) for r0 in range(0, hb.shape[0], half)]

    def project_next():
        if not pending:
            return
        rs, lo, hi = pending.pop(0)
        uc = _dot(hb[rs, :], w_ref[:, lo:hi])
        if lo < D_SHIFT:
            unew_ref[rs, lo:min(hi, D_SHIFT)] = uc[:, :min(hi, D_SHIFT) - lo]
        if hi > D_SHIFT:
            p_ref[rs, max(lo, D_SHIFT) - D_SHIFT:hi - D_SHIFT] = uc[:, max(lo, D_SHIFT) - lo:]

    outs = (af_ref, rf_ref, btf_ref, ktf_ref, ab_ref, rb_ref, btb_ref, ktb_ref, v_ref,
            gamf_ref, gamb_ref, bonus_ref, g_ref)
    mu = mu_ref[...]
    ctx_block = 2 * n_x_tiles
    blocks = ((2 * i - 3, carry_ref[SUBLANES:, :], carry_ref[SUBLANES - 1:SUBLANES, :], u_ref[0:1, :]),
              (2 * i - 2, u_ref[0:half, :], carry_ref[last:last + 1, :], u_ref[half:half + 1, :]))
    rr = lax.broadcasted_iota(jnp.int32, (half, half), 0)
    cc = lax.broadcasted_iota(jnp.int32, (half, half), 1)
    same = (rr >> 6) == (cc >> 6)
    consts = (_head_ones(), jnp.logical_and(same, cc < rr).astype(BF16),
              jnp.logical_and(same, cc > rr).astype(BF16))
    project_next()
    for hi, (blk, ub, prow, nrow) in enumerate(blocks):
        prev_ok = jnp.logical_and(blk != 0, blk != ctx_block)
        next_ok = jnp.logical_and(blk != ctx_block - 1, blk != ctx_block)
        for _ in _features(ub, jnp.where(prev_ok, prow, 0.0), jnp.where(next_ok, nrow, 0.0), mu, par_ref, w2_ref,
                           a2_ref, g2_ref, slice(hi * half, (hi + 1) * half), hi * (half // CHUNK), outs, consts):
            project_next()
    while pending:
        project_next()
    carry_ref[...] = u_ref[half - SUBLANES:, :]
    u_ref[...] = unew_ref[...]


def _front_call(x2, ctx_pad, rowtab, coltab, g1, mod, w_in_bf, mu, par, w2, a2, g2):
    s, d = x2.shape
    tm = TM_PROJ
    nx = s // tm
    d_in = w_in_bf.shape[1]
    n_steps = nx + 2
    xmap = lambda i: (jnp.minimum(i, nx - 1), 0)
    const = lambda i: (0, 0)
    step = lambda i: (i, 0)
    seq_bf = jax.ShapeDtypeStruct((n_steps * tm, D_RWKV), BF16)
    seq_f32 = jax.ShapeDtypeStruct((n_steps * tm, D_RWKV), F32)
    gam = jax.ShapeDtypeStruct((n_steps, tm // CHUNK, D_RWKV), F32)
    seq_spec = pl.BlockSpec((tm, D_RWKV), step)
    gam_spec = pl.BlockSpec((1, tm // CHUNK, D_RWKV), lambda i: (i, 0, 0))
    return pl.pallas_call(
        functools.partial(_front_body, n_x_tiles=nx),
        grid=(n_steps,),
        in_specs=[pl.BlockSpec((tm, d), xmap),
                  pl.BlockSpec((tm, d), const),
                  pl.BlockSpec((tm // GRID_W, d // 2), xmap),
                  pl.BlockSpec((tm, d // 2), const),
                  pl.BlockSpec((1, d), const),
                  pl.BlockSpec(mod.shape, const),
                  pl.BlockSpec((d, d_in), const),
                  pl.BlockSpec((1, D_SHIFT), const),
                  pl.BlockSpec((8, D_RWKV), const),
                  pl.BlockSpec(w2.shape, const),
                  pl.BlockSpec(a2.shape, const),
                  pl.BlockSpec(g2.shape, const)],
        out_specs=([pl.BlockSpec((tm, D_POOL), lambda i: (jnp.minimum(i, nx), 0))] + [seq_spec] * 9
                   + [gam_spec, gam_spec, seq_spec, seq_spec]),
        out_shape=([jax.ShapeDtypeStruct(((nx + 1) * tm, D_POOL), F32)] + [seq_bf] * 9
                   + [gam, gam, seq_f32, seq_f32]),
        scratch_shapes=[pltpu.VMEM((tm, D_SHIFT), F32), pltpu.VMEM((tm, D_SHIFT), F32),
                        pltpu.VMEM((TM_FEAT + SUBLANES, D_SHIFT), F32)],
        compiler_params=_cparams(),
        name="front",
    )(x2, ctx_pad, rowtab, coltab, g1, mod, w_in_bf, mu, par, w2, a2, g2)


def _blockdiag(x, head_of_lane):
    xb = x.astype(BF16)
    zero = jnp.zeros_like(xb)
    return jnp.concatenate([jnp.where(head_of_lane == h, xb, zero) for h in range(QUAD // HEAD)], axis=0)


def _unblock(x, head_of_lane):
    out = jnp.where(head_of_lane == 0, x[0:HEAD, :], 0.0)
    for h in range(1, QUAD // HEAD):
        out = jnp.where(head_of_lane == h, x[h * HEAD:(h + 1) * HEAD, :], out)
    return out


def _scan_chunks(chains):
    lane = lax.broadcasted_iota(jnp.int32, (HEAD, QUAD), 1)
    head_of_lane = lane >> 6
    s_idx = lane & (HEAD - 1)
    t_idx = lax.broadcasted_iota(jnp.int32, (HEAD, QUAD), 0)
    eye = s_idx == t_idx
    lower = s_idx < t_idx
    upper = s_idx > t_idx
    bd = functools.partial(_blockdiag, head_of_lane=head_of_lane)

    def same_block(shift):
        return (s_idx >> shift) == (t_idx >> shift)

    def each(fn):
        return [fn(c) for c in chains]

    def stage_masks(c):
        ll = _dot_nt(jnp.concatenate([c["a"], c["r"]], axis=0),
                     jnp.concatenate([bd(c["bt"]), bd(c["kt"])], axis=0))
        strict = upper if c["reverse"] else lower
        incl = jnp.logical_or(strict, eye)
        c["lab"] = jnp.where(strict, ll[:HEAD, :QUAD], 0.0)
        c["lak"] = jnp.where(strict, ll[:HEAD, QUAD:], 0.0)
        c["mrb"] = jnp.where(incl, ll[HEAD:, :QUAD], 0.0)
        c["mrk"] = jnp.where(incl, ll[HEAD:, QUAD:], 0.0)

    each(stage_masks)

    def stage_square(c):
        ld = jnp.where(same_block(4), c["lab"], 0.0)
        c["tm"] = jnp.where(eye, 1.0, 0.0) + ld
        c["xk"] = _dot(ld.astype(BF16), bd(ld))

    def stage_double(c):
        y = _dot(jnp.concatenate([c["tm"], c["xk"]], axis=0).astype(BF16), bd(c["xk"]))
        c["tm"] = c["tm"] + y[:HEAD]
        c["xk"] = y[HEAD:]

    def stage_last_power(c):
        c["tm"] = c["tm"] + _dot(c["tm"].astype(BF16), bd(c["xk"]))

    each(stage_square)
    each(stage_double)
    each(stage_double)
    each(stage_last_power)
    each(lambda c: c.update(vv=_dot(jnp.concatenate([c["lak"], c["mrk"]], axis=0).astype(BF16), bd(c["v"]))))
    for shift in (4, 5):
        joined = jnp.logical_and(same_block(shift + 1), jnp.logical_not(same_block(shift)))
        each(lambda c: c.update(te=_dot(c["tm"].astype(BF16), bd(jnp.where(joined, c["lab"], 0.0)))))
        each(lambda c: c.update(tm=c["tm"] + _dot(c["te"].astype(BF16), bd(c["tm"]))))

    each(lambda c: c.update(x1=_dot(c["tm"].astype(BF16),
                                    jnp.concatenate([bd(c["a"]), bd(c["vv"][:HEAD])], axis=1))))
    each(lambda c: c.update(x2=_dot(c["mrb"].astype(BF16),
                                    jnp.concatenate([bd(c["x1"][:, :QUAD]), bd(c["x1"][:, QUAD:])], axis=1))))

    def stage_state_terms(c):
        rhs = jnp.concatenate(
            [c["x1"].astype(BF16), jnp.concatenate([jnp.zeros((HEAD, QUAD), BF16), c["v"]], axis=1)], axis=0)
        hat = (jnp.concatenate([c["bt"], c["kt"]], axis=0).astype(F32) * c["gam"]).astype(BF16)
        pz = _dot_tn(hat, rhs)
        c["pq"] = _unblock(pz[:, :QUAD], head_of_lane) + jnp.where(eye, c["gam"], 0.0)
        c["zq"] = _unblock(pz[:, QUAD:], head_of_lane)
        c["qeff"] = c["r"].astype(F32) + c["x2"][:, :QUAD]
        c["yin"] = c["x2"][:, QUAD:] + c["vv"][HEAD:]

    each(stage_state_terms)

    outs, state = [], {}
    for c in chains:
        hst = state.get(c["key"], c["hst"])
        qp = _dot(jnp.concatenate([c["qeff"], c["pq"]], axis=0).astype(BF16), bd(hst))
        state[c["key"]] = qp[HEAD:] + c["zq"]
        outs.append(qp[:HEAD] + c["yin"])
    return outs, state


def _scan_body(af, rf, btf, ktf, vf, gamf, ab, rb, btb, ktb, vb, gamb, yf_ref, yb_ref, hf_ref, hb_ref):
    @pl.when(pl.program_id(0) == 0)
    def _():
        hf_ref[...] = jnp.zeros_like(hf_ref)
        hb_ref[...] = jnp.zeros_like(hb_ref)

    names = ("a", "r", "bt", "kt", "v")
    chains, sinks, state_refs = [], [], {}
    for ci in range(SCAN_CPS):
        for refs, y_ref, h_ref, reverse in (((af, rf, btf, ktf, vf, gamf), yf_ref, hf_ref, False),
                                            ((ab, rb, btb, ktb, vb, gamb), yb_ref, hb_ref, True)):
            chunk = SCAN_CPS - 1 - ci if reverse else ci
            rows = slice(chunk * CHUNK, (chunk + 1) * CHUNK)
            for q in range(D_RWKV // QUAD):
                sl = slice(q * QUAD, (q + 1) * QUAD)
                chain = {n: ref[rows, sl] for n, ref in zip(names, refs[:5])}
                chain.update(gam=refs[5][chunk, :, sl], hst=h_ref[:, sl], reverse=reverse, key=(reverse, q))
                chains.append(chain)
                sinks.append((y_ref, rows, sl))
                state_refs[(reverse, q)] = (h_ref, sl)
    outs, state = _scan_chunks(chains)
    for y, (y_ref, rows, sl) in zip(outs, sinks):
        y_ref[rows, sl] = y
    for key, (h_ref, sl) in state_refs.items():
        h_ref[:, sl] = state[key]


def _scan_call(feats, gamf, gamb, n_x_chunks, n_c_chunks):
    af, rf, btf, ktf, ab, rb, btb, ktb, v = feats
    assert n_x_chunks % SCAN_CPS == 0 and n_c_chunks % SCAN_CPS == 0
    nxb, ncb = n_x_chunks // SCAN_CPS, n_c_chunks // SCAN_CPS
    n = nxb + ncb
    assert SCAN_CPS * CHUNK == TM_FEAT
    fwd_blk = lambda i: jnp.where(i < ncb, nxb + i, i - ncb)
    bwd_blk = lambda i: n - 1 - i
    fwd_y = lambda i: (fwd_blk(i), 0)
    bwd_y = lambda i: (bwd_blk(i), 0)
    fwd = lambda i: (fwd_blk(i) + FEAT_SHIFT_BLOCKS, 0)
    bwd = lambda i: (bwd_blk(i) + FEAT_SHIFT_BLOCKS, 0)
    fwd3 = lambda i: (fwd_blk(i) + FEAT_SHIFT_BLOCKS, 0, 0)
    bwd3 = lambda i: (bwd_blk(i) + FEAT_SHIFT_BLOCKS, 0, 0)
    blk = (SCAN_CPS * CHUNK, D_RWKV)
    in_specs = ([pl.BlockSpec(blk, fwd)] * 5 + [pl.BlockSpec((SCAN_CPS, 1, D_RWKV), fwd3)]
                + [pl.BlockSpec(blk, bwd)] * 5 + [pl.BlockSpec((SCAN_CPS, 1, D_RWKV), bwd3)])
    y = jax.ShapeDtypeStruct((n * SCAN_CPS * CHUNK, D_RWKV), F32)
    return pl.pallas_call(
        _scan_body,
        grid=(n,),
        in_specs=in_specs,
        out_specs=[pl.BlockSpec(blk, fwd_y), pl.BlockSpec(blk, bwd_y)],
        out_shape=[y, y],
        scratch_shapes=[pltpu.VMEM((HEAD, D_RWKV), F32), pltpu.VMEM((HEAD, D_RWKV), F32)],
        compiler_params=_cparams(),
        name="scan",
    )(af, rf, btf, ktf, v, gamf, ab, rb, btb, ktb, v, gamb)


def _mix_body(x_ref, rowtab_ref, coltab_ref, yf_ref, yb_ref, bonus0_ref, bonus1_ref, g0_ref, g1_ref,
              pm_ref, pp_ref, pn_ref, gn_ref, wpool_ref, pscale_ref, wout_ref, mod_ref, g2n_ref, wr_ref, br_ref,
              o_ref, h3_ref, eid_ref, gate_ref, slab_ref, *, n_x_tiles, seq_len):
    j = pl.program_id(0)
    tm, d = x_ref.shape
    hl = POOL_HALO
    slab_ref[0:hl, :] = jnp.where(j != 0, pp_ref[...], 0.0)
    slab_ref[hl:hl + tm, :] = pm_ref[...]
    slab_ref[hl + tm:2 * hl + tm, :] = jnp.where(j != n_x_tiles - 1, pn_ref[...], 0.0)
    tglob = j * tm + lax.broadcasted_iota(jnp.int32, (tm, 1), 0)
    pooled = []
    for gi, win in enumerate(POOL_WINDOWS):
        cols = slice(gi * POOL_GW, (gi + 1) * POOL_GW)
        acc = slab_ref[hl - win // 2:hl - win // 2 + tm, cols]
        for off in range(-win // 2 + 1, win // 2):
            acc = acc + slab_ref[hl + off:hl + off + tm, cols]
        lo = jnp.maximum(tglob - win // 2, 0)
        hi = jnp.minimum(tglob - win // 2 + win, seq_len)
        cnt = (hi - lo).astype(F32)
        pg = acc / cnt - slab_ref[hl:hl + tm, cols]
        pooled.append(_dot(pg.astype(BF16), wpool_ref[gi]))
    pool = jnp.concatenate(pooled, axis=1) * pscale_ref[...]

    ones = _head_ones()
    y = yf_ref[...] + yb_ref[...]
    yc = y - _headsum(y, ones) * (1.0 / HEAD)
    var = _headsum(yc * yc, ones) * (1.0 / HEAD)
    bonus = jnp.concatenate([bonus0_ref[...], bonus1_ref[...]], axis=0)
    gate = jnp.concatenate([g0_ref[...], g1_ref[...]], axis=0)
    rw = (yc * lax.rsqrt(var + GN_EPS) * gn_ref[0:1, :] + gn_ref[1:2, :] + bonus) * gate
    mix = _dot(jnp.concatenate([rw, pool], axis=1).astype(BF16), wout_ref[...])
    gt_a = mod_ref[0:1, 2 * d:3 * d]
    x_mid = x_ref[...] + _pos_tile(rowtab_ref, coltab_ref) + gt_a * mix
    o_ref[...] = x_mid
    _route(x_mid, g2n_ref, mod_ref, wr_ref, br_ref, h3_ref, eid_ref, gate_ref)


def _mix_call(x2, rowtab, coltab, yf, yb, bonus, g, p, gn, w_pool_bf, pscale, w_out_bf, mod, g2n, wr, br):
    s, d = x2.shape
    tm = TM_PROJ
    nx = s // tm
    hb = tm // POOL_HALO
    const = lambda j: (0, 0)
    tok = lambda j: (j, 0)
    seq_spec = pl.BlockSpec((tm, D_RWKV), tok)
    assert tm == 2 * TM_FEAT
    feat_lo = pl.BlockSpec((TM_FEAT, D_RWKV), lambda j: (2 * j + FEAT_SHIFT_BLOCKS, 0))
    feat_hi = pl.BlockSpec((TM_FEAT, D_RWKV), lambda j: (2 * j + FEAT_SHIFT_BLOCKS + 1, 0))
    return pl.pallas_call(
        functools.partial(_mix_body, n_x_tiles=nx, seq_len=s),
        grid=(nx,),
        in_specs=[pl.BlockSpec((tm, d), tok),
                  pl.BlockSpec((tm // GRID_W, d // 2), tok),
                  pl.BlockSpec((tm, d // 2), const),
                  seq_spec, seq_spec, feat_lo, feat_hi, feat_lo, feat_hi, seq_spec,
                  pl.BlockSpec((POOL_HALO, D_POOL), lambda j: (jnp.maximum(j * hb - 1, 0), 0)),
                  pl.BlockSpec((POOL_HALO, D_POOL), lambda j: ((j + 1) * hb, 0)),
                  pl.BlockSpec((2, D_RWKV), const),
                  pl.BlockSpec(w_pool_bf.shape, lambda j: (0, 0, 0)),
                  pl.BlockSpec((1, D_POOL), const),
                  pl.BlockSpec(w_out_bf.shape, const),
                  pl.BlockSpec(mod.shape, const),
                  pl.BlockSpec((1, d), const),
                  pl.BlockSpec(wr.shape, const),
                  pl.BlockSpec(br.shape, const)],
        out_specs=[pl.BlockSpec((tm, d), tok),
                   pl.BlockSpec((tm * SUBLANES, LANES), tok),
                   pl.BlockSpec((8, tm), lambda j: (0, j)),
                   pl.BlockSpec((8, tm), lambda j: (0, j))],
        out_shape=[jax.ShapeDtypeStruct((s, d), F32),
                   jax.ShapeDtypeStruct((s * SUBLANES, LANES), F32),
                   jax.ShapeDtypeStruct((8, s), jnp.int32),
                   jax.ShapeDtypeStruct((8, s), F32)],
        scratch_shapes=[pltpu.VMEM((tm + 2 * POOL_HALO, D_POOL), F32)],
        compiler_params=_cparams(),
        name="mix",
    )(x2, rowtab, coltab, yf, yb, bonus, bonus, g, g, p, p, p, gn, w_pool_bf, pscale, w_out_bf, mod, g2n, wr, br)


def _to_token_tiles(ref, x):
    tm = x.shape[0]
    for j in range(x.shape[1] // LANES):
        ref[pl.ds(j, tm, stride=SUBLANES), :] = x[:, j * LANES:(j + 1) * LANES]


def _from_token_tiles(ref, tm):
    return jnp.concatenate([ref[pl.ds(j, tm, stride=SUBLANES), :] for j in range(SUBLANES)], axis=1)


def _route(x, g2_ref, mod_ref, wr_ref, br_ref, h3_ref, eid_ref, gate_ref):
    tm, d = x.shape
    ms = jnp.mean(x * x, axis=-1, keepdims=True)
    xn = x * lax.rsqrt(ms + NORM_EPS) * g2_ref[...]
    h = xn * (1.0 + mod_ref[0:1, 4 * d:5 * d]) + mod_ref[0:1, 3 * d:4 * d]
    _to_token_tiles(h3_ref, h)

    wh, wl = _split2(wr_ref[...])
    hh, hl = _split2(h)
    logits = _dot_nt(wh, hh) + (_dot_nt(wh, hl) + _dot_nt(wl, hh)) + br_ref[...]
    gl = logits[0:N_GROUPS, :]
    gmax = jnp.max(gl, axis=0, keepdims=True)
    gidx = lax.broadcasted_iota(jnp.int32, gl.shape, 0).astype(F32)
    grp = jnp.min(jnp.where(gl == gmax, gidx, float(N_GROUPS)), axis=0, keepdims=True)
    p_grp = 1.0 / jnp.sum(jnp.exp(gl - gmax), axis=0, keepdims=True)
    sel = jnp.zeros((EXPERTS_PER_GROUP, tm), F32)
    for gi in range(N_GROUPS):
        lo = N_GROUPS + gi * EXPERTS_PER_GROUP
        sel = jnp.where(grp == float(gi), logits[lo:lo + EXPERTS_PER_GROUP, :], sel)
    eidx = lax.broadcasted_iota(jnp.int32, sel.shape, 0).astype(F32)
    top1 = jnp.max(sel, axis=0, keepdims=True)
    i1 = jnp.min(jnp.where(sel == top1, eidx, float(EXPERTS_PER_GROUP)), axis=0, keepdims=True)
    sel2 = jnp.where(eidx == i1, -jnp.inf, sel)
    top2 = jnp.max(sel2, axis=0, keepdims=True)
    i2 = jnp.min(jnp.where(sel2 == top2, eidx, float(EXPERTS_PER_GROUP)), axis=0, keepdims=True)
    e2 = jnp.exp(top2 - top1)
    inv = 1.0 / (1.0 + e2)
    zf = jnp.zeros((6, tm), F32)
    gate_ref[...] = jnp.concatenate([p_grp * inv, p_grp * (e2 * inv), zf], axis=0)
    base = grp * float(EXPERTS_PER_GROUP)
    eid_ref[...] = jnp.concatenate([base + i1, base + i2, zf], axis=0).astype(jnp.int32)


def _sc_gather_tiles(table3, idx):
    n_out = idx.shape[0]
    n_workers = V7X_SC_CORES * V7X_SC_SUBCORES
    per_worker = n_out // n_workers
    n_chunks = per_worker // SC_GATHER_ROWS
    assert per_worker * n_workers == n_out and n_chunks * SC_GATHER_ROWS == per_worker and n_chunks % 2 == 0
    mesh = plsc.VectorSubcoreMesh(core_axis_name="c", subcore_axis_name="s",
                                  num_cores=V7X_SC_CORES, num_subcores=V7X_SC_SUBCORES)

    def body(table_hbm, idx_hbm, out_hbm, idx_v, rows0, rows1, gsem0, gsem1, wsem0, wsem1):
        worker = lax.axis_index("s") * V7X_SC_CORES + lax.axis_index("c")
        base = worker * per_worker
        pltpu.sync_copy(idx_hbm.at[pl.ds(base, per_worker)], idx_v)

        def gather(j, buf, sem):
            off = pl.multiple_of(j * SC_GATHER_ROWS, SC_GATHER_ROWS)
            return pltpu.make_async_copy(table_hbm.at[idx_v.at[pl.ds(off, SC_GATHER_ROWS)]], buf, sem)

        def write(j, buf, sem):
            off = pl.multiple_of(j * SC_GATHER_ROWS, SC_GATHER_ROWS)
            return pltpu.make_async_copy(buf, out_hbm.at[pl.ds(base + off, SC_GATHER_ROWS)], sem)

        gather(0, rows0, gsem0).start()

        @pl.loop(0, n_chunks, step=2)
        def _(j):
            gather(j, rows0, gsem0).wait()

            @pl.when(j >= 2)
            def _():
                write(j - 1, rows1, wsem1).wait()

            gather(j + 1, rows1, gsem1).start()
            write(j, rows0, wsem0).start()
            gather(j + 1, rows1, gsem1).wait()
            write(j, rows0, wsem0).wait()

            @pl.when(j + 2 < n_chunks)
            def _():
                gather(j + 2, rows0, gsem0).start()

            write(j + 1, rows1, wsem1).start()

        write(n_chunks - 1, rows1, wsem1).wait()

    rows = pltpu.VMEM((SC_GATHER_ROWS,) + table3.shape[1:], table3.dtype)
    return pl.kernel(
        body,
        out_type=jax.ShapeDtypeStruct((n_out,) + table3.shape[1:], table3.dtype),
        mesh=mesh,
        scratch_types=[pltpu.VMEM((per_worker,), jnp.int32), rows, rows,
                       pltpu.SemaphoreType.DMA, pltpu.SemaphoreType.DMA,
                       pltpu.SemaphoreType.DMA, pltpu.SemaphoreType.DMA],
        name="sc_gather",
    )(table3, idx)


def _experts_body(blk_e_ref, n_used_ref, first_ref, next_e_ref, wslot_ref, xs_ref, wg_hbm, wu_hbm, wd_hbm, y_ref,
                  wg_buf, wu_buf, wd_buf, wsem_ref):
    i = pl.program_id(0)
    used = i < n_used_ref[0]
    slot = wslot_ref[i]

    def weight_copies(expert, to_slot):
        return [pltpu.make_async_copy(hbm.at[expert], buf.at[to_slot], wsem_ref.at[to_slot])
                for hbm, buf in ((wg_hbm, wg_buf), (wu_hbm, wu_buf), (wd_hbm, wd_buf))]

    @pl.when(jnp.logical_and(i == 0, used))
    def _():
        for cp in weight_copies(blk_e_ref[0], 0):
            cp.start()

    @pl.when(jnp.logical_and(used, first_ref[i] == 1))
    def _():
        for cp in weight_copies(blk_e_ref[i], slot):
            cp.wait()

        @pl.when(next_e_ref[i] != blk_e_ref[i])
        def _():
            for cp in weight_copies(next_e_ref[i], 1 - slot):
                cp.start()

    @pl.when(used)
    def _():
        xb = _from_token_tiles(xs_ref, MOE_BM).astype(BF16)
        gate = _dot(xb, wg_buf[slot].astype(BF16))
        upp = _dot(xb, wu_buf[slot].astype(BF16))
        hid = (gate * _sigmoid(gate)) * upp
        _to_token_tiles(y_ref, _dot(hid.astype(BF16), wd_buf[slot].astype(BF16)))

    @pl.when(jnp.logical_not(used))
    def _():
        y_ref[...] = jnp.zeros_like(y_ref)


def _experts_call(blk_expert, n_used, xs3, wg, wu, wd):
    nb = blk_expert.shape[0]
    d = wg.shape[1]
    rows = MOE_BM * SUBLANES
    first, next_e, wslot = _weight_schedule(blk_expert, n_used)
    blk = lambda i, *_: (i, 0)
    grid_spec = pltpu.PrefetchScalarGridSpec(
        num_scalar_prefetch=5,
        grid=(nb,),
        in_specs=[pl.BlockSpec((rows, LANES), blk),
                  pl.BlockSpec(memory_space=pl.ANY),
                  pl.BlockSpec(memory_space=pl.ANY),
                  pl.BlockSpec(memory_space=pl.ANY)],
        out_specs=pl.BlockSpec((rows, LANES), blk),
        scratch_shapes=[pltpu.VMEM((2, d, D_EXPERT), F32), pltpu.VMEM((2, d, D_EXPERT), F32),
                        pltpu.VMEM((2, D_EXPERT, d), F32), pltpu.SemaphoreType.DMA((2,))],
    )
    return pl.pallas_call(
        _experts_body,
        grid_spec=grid_spec,
        out_shape=jax.ShapeDtypeStruct((nb * rows, LANES), F32),
        compiler_params=_cparams(),
        name="experts",
    )(blk_expert, n_used, first, next_e, wslot, xs3, wg, wu, wd)


def _combine_body(x_ref, y0_ref, y1_ref, gate_ref, mod_ref, fg_ref, o_ref):
    tm, d = x_ref.shape
    y = gate_ref[:, 0:1] * _from_token_tiles(y0_ref, tm) + gate_ref[:, 1:2] * _from_token_tiles(y1_ref, tm)
    x = x_ref[...] + mod_ref[0:1, 5 * d:6 * d] * y
    ms = jnp.mean(x * x, axis=-1, keepdims=True)
    o_ref[...] = x * lax.rsqrt(ms + NORM_EPS) * fg_ref[...]


def _combine_call(x_mid, y3, gcol, mod, fg):
    s, d = x_mid.shape
    tm = TM_PROJ
    nt = s // tm
    const = lambda j: (0, 0)
    return pl.pallas_call(
        _combine_body,
        grid=(nt,),
        in_specs=[pl.BlockSpec((tm, d), lambda j: (j, 0)),
                  pl.BlockSpec((tm * SUBLANES, LANES), lambda j: (j, 0)),
                  pl.BlockSpec((tm * SUBLANES, LANES), lambda j: (j + nt, 0)),
                  pl.BlockSpec((tm, 2), lambda j: (j, 0)),
                  pl.BlockSpec(mod.shape, const),
                  pl.BlockSpec((1, d), const)],
        out_specs=pl.BlockSpec((tm, d), lambda j: (j, 0)),
        out_shape=jax.ShapeDtypeStruct((s, d), F32),
        compiler_params=_cparams(),
        name="combine",
    )(x_mid, y3, y3, gcol, mod, fg)


def _blockdiag2(wf, wb):
    z = jnp.zeros_like(wf)
    return jnp.concatenate([jnp.concatenate([wf, z], axis=1), jnp.concatenate([z, wb], axis=1)], axis=0)


def _routing_plan(eids, n_tok):
    m = 2 * n_tok
    experts = jnp.arange(N_EXPERTS, dtype=jnp.int32)
    flat_e = eids[:2].reshape(m)
    order = jnp.argsort(flat_e, stable=True).astype(jnp.int32)
    onehot = (flat_e[:, None] == experts[None, :]).astype(jnp.int32)
    csum = jnp.cumsum(onehot, axis=0)
    counts = csum[-1]
    starts = jnp.cumsum(counts) - counts
    pcounts = (counts + MOE_BM - 1) // MOE_BM * MOE_BM
    pends = jnp.cumsum(pcounts)
    pstarts = pends - pcounts
    n_blocks = (m + N_EXPERTS * (MOE_BM - 1) + MOE_BM - 1) // MOE_BM
    slot = jnp.arange(n_blocks * MOE_BM, dtype=jnp.int32)
    slot_e = jnp.minimum(jnp.sum((slot[:, None] >= pends[None, :]).astype(jnp.int32), axis=1), N_EXPERTS - 1)
    sel = (slot_e[:, None] == experts[None, :]).astype(jnp.int32)
    local = slot - jnp.sum(sel * pstarts[None, :], axis=1)
    valid = local < jnp.sum(sel * counts[None, :], axis=1)
    pair = order[jnp.clip(jnp.sum(sel * starts[None, :], axis=1) + local, 0, m - 1)]
    src_tok = jnp.where(valid, jnp.where(pair >= n_tok, pair - n_tok, pair), slot % n_tok)
    dest = jnp.sum(onehot * (csum - 1 + pstarts[None, :]), axis=1).astype(jnp.int32)
    blk_expert = slot_e[::MOE_BM]
    n_used = (pends[-1:] // MOE_BM).astype(jnp.int32)
    return src_tok, dest, blk_expert, n_used


def _weight_schedule(blk_expert, n_used):
    used = jnp.arange(blk_expert.shape[0], dtype=jnp.int32) < n_used[0]
    first = jnp.logical_and(used, jnp.concatenate([jnp.ones((1,), bool), blk_expert[1:] != blk_expert[:-1]]))
    none = jnp.int32(N_EXPERTS)
    later_first = lax.cummin(jnp.where(first, blk_expert, none), axis=0, reverse=True)
    next_e = jnp.concatenate([later_first[1:], none[None]])
    next_e = jnp.where(next_e == none, blk_expert, next_e)
    wslot = jnp.maximum(jnp.cumsum(first.astype(jnp.int32)) - 1, 0) & 1
    return first.astype(jnp.int32), next_e, wslot


def kernel(x, c, ctx, c_ctx, norm1_g, w_mod, b_mod, w_in, mu_shift, w0_f, w2_f, w0_b, w2_b, a0_f, a2_f, a0_b, a2_b, g2, k_k, k_a, r_k_f, r_k_b, gn_w, gn_b, w_pool, pool_scale, w_out, norm2_g, w_router_grp, b_router_grp, w_router_exp, b_router_exp, w_gate, w_up, w_down, final_g):
    b, s, d = x.shape
    n_ctx = ctx.shape[1]
    assert b == 1 and c.shape[0] == 1 and w_mod.shape[0] == 1
    assert s % TM_PROJ == 0 and n_ctx % CHUNK == 0 and n_ctx == TM_FEAT and d == 2 * D_RWKV
    x2 = x[0]
    rows = s // GRID_W

    quarter = d // 4
    freq = POS_THETA ** (-jnp.arange(quarter, dtype=F32) / quarter)
    rarg = jnp.arange(rows, dtype=F32)[:, None] * freq
    carg = jnp.arange(GRID_W, dtype=F32)[:, None] * freq
    rowtab = jnp.concatenate([jnp.sin(rarg), jnp.cos(rarg)], axis=-1)
    coltab = jnp.tile(jnp.concatenate([jnp.sin(carg), jnp.cos(carg)], axis=-1), (TM_PROJ // GRID_W, 1))

    cs = jnp.concatenate([c, c_ctx[None, :], jnp.zeros((6, d), F32)], axis=0)
    mod = _mod_call(cs, w_mod[0], b_mod)

    ctx_pad = jnp.concatenate([ctx[0], jnp.zeros((TM_PROJ - n_ctx, d), F32)], axis=0)
    par = jnp.concatenate([w0_f, w0_b, a0_f, a0_b, k_k, k_a, r_k_f, r_k_b], axis=0)
    outs = _front_call(x2, ctx_pad, rowtab, coltab, norm1_g, mod, w_in[0].astype(BF16), mu_shift, par,
                       _blockdiag2(w2_f[0], w2_b[0]).astype(BF16), _blockdiag2(a2_f[0], a2_b[0]).astype(BF16),
                       g2[0].astype(BF16))
    p, feats, gamf, gamb, bonus, g = outs[0], outs[1:10], outs[10], outs[11], outs[12], outs[13]
    yf, yb = _scan_call(feats, gamf.reshape(-1, 1, D_RWKV), gamb.reshape(-1, 1, D_RWKV),
                        s // CHUNK, n_ctx // CHUNK)

    gn = jnp.concatenate([gn_w, gn_b], axis=0)
    wr = jnp.concatenate([w_router_grp[0].T, w_router_exp[0].T,
                          jnp.zeros((ROUTER_ROWS - N_GROUPS - N_EXPERTS, d), F32)], axis=0)
    br = jnp.concatenate([b_router_grp[0], b_router_exp[0],
                          jnp.zeros((ROUTER_ROWS - N_GROUPS - N_EXPERTS,), F32)])[:, None]
    x_mid, h3, eids, gates = _mix_call(x2, rowtab, coltab, yf, yb, bonus, g, p, gn, w_pool[0].astype(BF16),
                                       pool_scale, w_out[0].astype(BF16), mod, norm2_g, wr, br)

    src_tok, dest, blk_expert, n_used = _routing_plan(eids, s)
    tile = (SUBLANES, LANES)
    xs3 = _sc_gather_tiles(h3.reshape((s,) + tile), src_tok)
    yexp = _experts_call(blk_expert, n_used, xs3.reshape(-1, LANES), w_gate[0], w_up[0], w_down[0])
    y3 = _sc_gather_tiles(yexp.reshape((-1,) + tile), dest).reshape(-1, LANES)
    out = _combine_call(x_mid, y3, gates[:2].T, mod, final_g[None, :])
    return out[None]
```

```python
import functools

import jax
import jax.numpy as jnp
from jax import lax
from jax.experimental import pallas as pl
from jax.experimental.pallas import tpu as pltpu
from jax.experimental.pallas import tpu_sc as plsc

F32 = jnp.float32
BF16 = jnp.bfloat16

GRID_W = 64
SUBLANES, LANES = 8, 128
HEAD = 64
N_HEADS = 8
D_RWKV = HEAD * N_HEADS
D_POOL = 512
POOL_WINDOWS = (2, 4, 8, 16)
POOL_GW = 128
POOL_HALO = 8
D_SHIFT = 1920
N_GROUPS = 4
EXPERTS_PER_GROUP = 8
N_EXPERTS = 32
D_EXPERT = 512
NORM_EPS = 1e-6
GN_EPS = 64e-5
POS_THETA = 10000.0
DECAY_SCALE = 0.6065306597

CHUNK = 64
QUAD = 4 * HEAD
SCAN_CPS = 4
TM_PROJ = 512
TM_FEAT = 256
FEAT_SHIFT_BLOCKS = 3
MOE_BM = 256
ROUTER_ROWS = 48
V7X_VMEM_LIMIT = 56 * 1024 * 1024
V7X_SC_CORES, V7X_SC_SUBCORES = 2, 16
SC_GATHER_ROWS = 32


def _cparams(n_axes=1, vmem=V7X_VMEM_LIMIT):
    return pltpu.CompilerParams(dimension_semantics=("arbitrary",) * n_axes,
                                vmem_limit_bytes=vmem)


def _dot(a, b):
    return jnp.dot(a, b, preferred_element_type=F32)


def _dot_nt(a, b):
    return lax.dot_general(a, b, (((1,), (1,)), ((), ())), preferred_element_type=F32)


def _dot_tn(a, b):
    return lax.dot_general(a, b, (((0,), (0,)), ((), ())), preferred_element_type=F32)


def _split2(x):
    hi = x.astype(BF16)
    lo = (x - hi.astype(F32)).astype(BF16)
    return hi, lo


def _dot3(a, b):
    ah, al = _split2(a)
    bh, bl = _split2(b)
    return _dot(ah, bh) + (_dot(ah, bl) + _dot(al, bh))


def _sigmoid(x):
    return 0.5 * jnp.tanh(0.5 * x) + 0.5


def _head_ones():
    r = lax.broadcasted_iota(jnp.int32, (D_RWKV, D_RWKV), 0) >> 6
    c = lax.broadcasted_iota(jnp.int32, (D_RWKV, D_RWKV), 1) >> 6
    return (r == c).astype(BF16)


def _headsum(x, ones):
    hi, lo = _split2(x)
    return _dot(hi, ones) + _dot(lo, ones)


def _mod_body(cs_ref, w_ref, b_ref, o_ref):
    a = cs_ref[...]
    a = a * _sigmoid(a)
    o_ref[...] = _dot3(a, w_ref[...]) + b_ref[...]


def _mod_call(cs, w_mod, b_mod):
    d, n = w_mod.shape
    tn = 512
    return pl.pallas_call(
        _mod_body,
        grid=(n // tn,),
        in_specs=[pl.BlockSpec((8, d), lambda j: (0, 0)),
                  pl.BlockSpec((d, tn), lambda j: (0, j)),
                  pl.BlockSpec((1, tn), lambda j: (0, j))],
        out_specs=pl.BlockSpec((8, tn), lambda j: (0, j)),
        out_shape=jax.ShapeDtypeStruct((8, n), F32),
        compiler_params=_cparams(),
        name="mod",
    )(cs, w_mod, b_mod)


def _pos_tile(rowtab_ref, coltab_ref):
    rt = rowtab_ref[...]
    rowpart = jnp.concatenate(
        [jnp.broadcast_to(rt[r:r + 1, :], (GRID_W, rt.shape[1])) for r in range(TM_PROJ // GRID_W)], axis=0)
    return jnp.concatenate([rowpart, coltab_ref[...]], axis=1)


def _features(u, prow, nrow, mu, par_ref, w2_ref, a2_ref, g2_ref, rows, chunk0, outs, consts):
    (af_ref, rf_ref, btf_ref, ktf_ref, ab_ref, rb_ref, btb_ref, ktb_ref, v_ref,
     gamf_ref, gamb_ref, bonus_ref, g_ref) = outs
    tm = u.shape[0]
    rid = lax.broadcasted_iota(jnp.int32, (SUBLANES, 1), 0)

    def shifted(lo, hi):
        uc, m = u[:, lo:hi], mu[:, lo:hi]
        up = pltpu.roll(uc, 1, 0)
        up = jnp.concatenate([jnp.where(rid == 0, prow[:, lo:hi], up[:SUBLANES]), up[SUBLANES:]], axis=0)
        dn = pltpu.roll(uc, tm - 1, 0)
        dn = jnp.concatenate(
            [dn[:-SUBLANES], jnp.where(rid == SUBLANES - 1, nrow[:, lo:hi], dn[-SUBLANES:])], axis=0)
        return (1.0 - m) * uc + (0.5 * m) * (up + dn)

    w0f, w0b, a0f, a0b = par_ref[0:1, :], par_ref[1:2, :], par_ref[2:3, :], par_ref[3:4, :]
    k_k, k_a, rkf, rkb = par_ref[4:5, :], par_ref[5:6, :], par_ref[6:7, :], par_ref[7:8, :]

    r = shifted(0, 512)
    yield
    k = shifted(512, 1024)
    yield
    v = shifted(1024, 1536)
    v_ref[rows, :] = v.astype(BF16)
    yield
    lora = shifted(1536, D_SHIFT)
    zw = _dot(jnp.tanh(lora[:, 0:128]).astype(BF16), w2_ref[...])
    za = _dot(lora[:, 128:256].astype(BF16), a2_ref[...])
    g_ref[rows, :] = _dot(_sigmoid(lora[:, 256:384]).astype(BF16), g2_ref[...])
    yield
    lw_f = -DECAY_SCALE * _sigmoid(w0f + zw[:, :512])
    lw_b = -DECAY_SCALE * _sigmoid(w0b + zw[:, 512:])
    a_f = _sigmoid(a0f + za[:, :512])
    a_b = _sigmoid(a0b + za[:, 512:])
    yield

    ones, earlier, later = consts
    kkr = k * k_k
    kk = kkr * lax.rsqrt(jnp.maximum(_headsum(kkr * kkr, ones), 1e-24))
    yield
    k_f = k * (1.0 + (a_f - 1.0) * k_a)
    k_b = k * (1.0 + (a_b - 1.0) * k_a)
    bonus_ref[rows, :] = _headsum(r * (k_f * rkf + k_b * rkb), ones) * v
    yield

    n_ch = tm // CHUNK

    def visited_before(m, lwd):
        l1, l2 = _split2(lwd)
        return _dot(m, l1) + _dot(m, l2)

    def emit(excl, lwd, a_d, k_d, last_row, a_ref, r_ref, bt_ref, kt_ref, gam_ref):
        cum = excl + lwd
        gam_ref[0, chunk0:chunk0 + n_ch, :] = jnp.exp(jnp.concatenate(
            [cum[c * CHUNK + last_row:c * CHUNK + last_row + 1, :] for c in range(n_ch)], axis=0))
        a_ref[rows, :] = (-kk * jnp.exp(excl)).astype(BF16)
        r_ref[rows, :] = (r * jnp.exp(cum)).astype(BF16)
        yield
        e_neg = jnp.exp(-cum)
        bt_ref[rows, :] = (kk * a_d * e_neg).astype(BF16)
        kt_ref[rows, :] = (k_d * e_neg).astype(BF16)
        yield

    excl_f = visited_before(earlier, lw_f)
    yield
    yield from emit(excl_f, lw_f, a_f, k_f, CHUNK - 1, af_ref, rf_ref, btf_ref, ktf_ref, gamf_ref)
    excl_b = visited_before(later, lw_b)
    yield
    yield from emit(excl_b, lw_b, a_b, k_b, 0, ab_ref, rb_ref, btb_ref, ktb_ref, gamb_ref)


def _front_body(x_ref, ctx_ref, rowtab_ref, coltab_ref, g1_ref, mod_ref, w_ref, mu_ref, par_ref, w2_ref, a2_ref,
                g2_ref, p_ref, af_ref, rf_ref, btf_ref, ktf_ref, ab_ref, rb_ref, btb_ref, ktb_ref, v_ref,
                gamf_ref, gamb_ref, bonus_ref, g_ref, u_ref, unew_ref, carry_ref, *, n_x_tiles):
    i = pl.program_id(0)
    d = x_ref.shape[1]
    d_in = w_ref.shape[1]
    half = TM_FEAT
    last = carry_ref.shape[0] - 1

    @pl.when(i == 0)
    def _():
        u_ref[...] = jnp.zeros_like(u_ref)
        carry_ref[...] = jnp.zeros_like(carry_ref)

    is_ctx = i >= n_x_tiles
    xin = jnp.where(is_ctx, ctx_ref[...], x_ref[...] + _pos_tile(rowtab_ref, coltab_ref))
    ms = jnp.mean(xin * xin, axis=-1, keepdims=True)
    xn = xin * lax.rsqrt(ms + NORM_EPS) * g1_ref[...]
    sh = jnp.where(is_ctx, mod_ref[1:2, 0:d], mod_ref[0:1, 0:d])
    sc = jnp.where(is_ctx, mod_ref[1:2, d:2 * d], mod_ref[0:1, d:2 * d])
    hb = (xn * (1.0 + sc) + sh).astype(BF16)
    pending = [(slice(r0, r0 + half), lo, min(lo + QUAD, d_in))
               for lo in range(0, d_in, QUAD) for r0 in range(0, hb.shape[0], half)]

    def project_next():
        if not pending:
            return
        rs, lo, hi = pending.pop(0)
        uc = _dot(hb[rs, :], w_ref[:, lo:hi])
        if lo < D_SHIFT:
            unew_ref[rs, lo:min(hi, D_SHIFT)] = uc[:, :min(hi, D_SHIFT) - lo]
        if hi > D_SHIFT:
            p_ref[rs, max(lo, D_SHIFT) - D_SHIFT:hi - D_SHIFT] = uc[:, max(lo, D_SHIFT) - lo:]

    outs = (af_ref, rf_ref, btf_ref, ktf_ref, ab_ref, rb_ref, btb_ref, ktb_ref, v_ref,
            gamf_ref, gamb_ref, bonus_ref, g_ref)
    mu = mu_ref[...]
    ctx_block = 2 * n_x_tiles
    blocks = ((2 * i - 3, carry_ref[SUBLANES:, :], carry_ref[SUBLANES - 1:SUBLANES, :], u_ref[0:1, :]),
              (2 * i - 2, u_ref[0:half, :], carry_ref[last:last + 1, :], u_ref[half:half + 1, :]))
    rr = lax.broadcasted_iota(jnp.int32, (half, half), 0)
    cc = lax.broadcasted_iota(jnp.int32, (half, half), 1)
    same = (rr >> 6) == (cc >> 6)
    consts = (_head_ones(), jnp.logical_and(same, cc < rr).astype(BF16),
              jnp.logical_and(same, cc > rr).astype(BF16))
    project_next()
    for hi, (blk, ub, prow, nrow) in enumerate(blocks):
        prev_ok = jnp.logical_and(blk != 0, blk != ctx_block)
        next_ok = jnp.logical_and(blk != ctx_block - 1, blk != ctx_block)
        for _ in _features(ub, jnp.where(prev_ok, prow, 0.0), jnp.where(next_ok, nrow, 0.0), mu, par_ref, w2_ref,
                           a2_ref, g2_ref, slice(hi * half, (hi + 1) * half), hi * (half // CHUNK), outs, consts):
            project_next()
    while pending:
        project_next()
    carry_ref[...] = u_ref[half - SUBLANES:, :]
    u_ref[...] = unew_ref[...]


def _front_call(x2, ctx_pad, rowtab, coltab, g1, mod, w_in_bf, mu, par, w2, a2, g2):
    s, d = x2.shape
    tm = TM_PROJ
    nx = s // tm
    d_in = w_in_bf.shape[1]
    n_steps = nx + 2
    xmap = lambda i: (jnp.minimum(i, nx - 1), 0)
    const = lambda i: (0, 0)
    step = lambda i: (i, 0)
    seq_bf = jax.ShapeDtypeStruct((n_steps * tm, D_RWKV), BF16)
    seq_f32 = jax.ShapeDtypeStruct((n_steps * tm, D_RWKV), F32)
    gam = jax.ShapeDtypeStruct((n_steps, tm // CHUNK, D_RWKV), F32)
    seq_spec = pl.BlockSpec((tm, D_RWKV), step)
    gam_spec = pl.BlockSpec((1, tm // CHUNK, D_RWKV), lambda i: (i, 0, 0))
    return pl.pallas_call(
        functools.partial(_front_body, n_x_tiles=nx),
        grid=(n_steps,),
        in_specs=[pl.BlockSpec((tm, d), xmap),
                  pl.BlockSpec((tm, d), const),
                  pl.BlockSpec((tm // GRID_W, d // 2), xmap),
                  pl.BlockSpec((tm, d // 2), const),
                  pl.BlockSpec((1, d), const),
                  pl.BlockSpec(mod.shape, const),
                  pl.BlockSpec((d, d_in), const),
                  pl.BlockSpec((1, D_SHIFT), const),
                  pl.BlockSpec((8, D_RWKV), const),
                  pl.BlockSpec(w2.shape, const),
                  pl.BlockSpec(a2.shape, const),
                  pl.BlockSpec(g2.shape, const)],
        out_specs=([pl.BlockSpec((tm, D_POOL), lambda i: (jnp.minimum(i, nx), 0))] + [seq_spec] * 9
                   + [gam_spec, gam_spec, seq_spec, seq_spec]),
        out_shape=([jax.ShapeDtypeStruct(((nx + 1) * tm, D_POOL), F32)] + [seq_bf] * 9
                   + [gam, gam, seq_f32, seq_f32]),
        scratch_shapes=[pltpu.VMEM((tm, D_SHIFT), F32), pltpu.VMEM((tm, D_SHIFT), F32),
                        pltpu.VMEM((TM_FEAT + SUBLANES, D_SHIFT), F32)],
        compiler_params=_cparams(),
        name="front",
    )(x2, ctx_pad, rowtab, coltab, g1, mod, w_in_bf, mu, par, w2, a2, g2)


def _blockdiag(x, head_of_lane):
    xb = x.astype(BF16)
    zero = jnp.zeros_like(xb)
    return jnp.concatenate([jnp.where(head_of_lane == h, xb, zero) for h in range(QUAD // HEAD)], axis=0)


def _unblock(x, head_of_lane):
    out = jnp.where(head_of_lane == 0, x[0:HEAD, :], 0.0)
    for h in range(1, QUAD // HEAD):
        out = jnp.where(head_of_lane == h, x[h * HEAD:(h + 1) * HEAD, :], out)
    return out


def _scan_chunks(chains):
    lane = lax.broadcasted_iota(jnp.int32, (HEAD, QUAD), 1)
    head_of_lane = lane >> 6
    s_idx = lane & (HEAD - 1)
    t_idx = lax.broadcasted_iota(jnp.int32, (HEAD, QUAD), 0)
    eye = s_idx == t_idx
    lower = s_idx < t_idx
    upper = s_idx > t_idx
    bd = functools.partial(_blockdiag, head_of_lane=head_of_lane)

    def same_block(shift):
        return (s_idx >> shift) == (t_idx >> shift)

    def each(fn):
        return [fn(c) for c in chains]

    def stage_masks(c):
        ll = _dot_nt(jnp.concatenate([c["a"], c["r"]], axis=0),
                     jnp.concatenate([bd(c["bt"]), bd(c["kt"])], axis=0))
        strict = upper if c["reverse"] else lower
        incl = jnp.logical_or(strict, eye)
        c["lab"] = jnp.where(strict, ll[:HEAD, :QUAD], 0.0)
        c["lak"] = jnp.where(strict, ll[:HEAD, QUAD:], 0.0)
        c["mrb"] = jnp.where(incl, ll[HEAD:, :QUAD], 0.0)
        c["mrk"] = jnp.where(incl, ll[HEAD:, QUAD:], 0.0)

    each(stage_masks)

    def stage_square(c):
        ld = jnp.where(same_block(4), c["lab"], 0.0)
        c["tm"] = jnp.where(eye, 1.0, 0.0) + ld
        c["xk"] = _dot(ld.astype(BF16), bd(ld))

    def stage_double(c):
        y = _dot(jnp.concatenate([c["tm"], c["xk"]], axis=0).astype(BF16), bd(c["xk"]))
        c["tm"] = c["tm"] + y[:HEAD]
        c["xk"] = y[HEAD:]

    def stage_last_power(c):
        c["tm"] = c["tm"] + _dot(c["tm"].astype(BF16), bd(c["xk"]))

    each(stage_square)
    each(stage_double)
    each(stage_double)
    each(stage_last_power)
    each(lambda c: c.update(vv=_dot(jnp.concatenate([c["lak"], c["mrk"]], axis=0).astype(BF16), bd(c["v"]))))
    for shift in (4, 5):
        joined = jnp.logical_and(same_block(shift + 1), jnp.logical_not(same_block(shift)))
        each(lambda c: c.update(te=_dot(c["tm"].astype(BF16), bd(jnp.where(joined, c["lab"], 0.0)))))
        each(lambda c: c.update(tm=c["tm"] + _dot(c["te"].astype(BF16), bd(c["tm"]))))

    each(lambda c: c.update(x1=_dot(c["tm"].astype(BF16),
                                    jnp.concatenate([bd(c["a"]), bd(c["vv"][:HEAD])], axis=1))))
    each(lambda c: c.update(x2=_dot(c["mrb"].astype(BF16),
                                    jnp.concatenate([bd(c["x1"][:, :QUAD]), bd(c["x1"][:, QUAD:])], axis=1))))

    def stage_state_terms(c):
        rhs = jnp.concatenate(
            [c["x1"].astype(BF16), jnp.concatenate([jnp.zeros((HEAD, QUAD), BF16), c["v"]], axis=1)], axis=0)
        hat = (jnp.concatenate([c["bt"], c["kt"]], axis=0).astype(F32) * c["gam"]).astype(BF16)
        pz = _dot_tn(hat, rhs)
        c["pq"] = _unblock(pz[:, :QUAD], head_of_lane) + jnp.where(eye, c["gam"], 0.0)
        c["zq"] = _unblock(pz[:, QUAD:], head_of_lane)
        c["qeff"] = c["r"].astype(F32) + c["x2"][:, :QUAD]
        c["yin"] = c["x2"][:, QUAD:] + c["vv"][HEAD:]

    each(stage_state_terms)

    outs, state = [], {}
    for c in chains:
        hst = state.get(c["key"], c["hst"])
        qp = _dot(jnp.concatenate([c["qeff"], c["pq"]], axis=0).astype(BF16), bd(hst))
        state[c["key"]] = qp[HEAD:] + c["zq"]
        outs.append(qp[:HEAD] + c["yin"])
    return outs, state


def _scan_body(af, rf, btf, ktf, vf, gamf, ab, rb, btb, ktb, vb, gamb, yf_ref, yb_ref, hf_ref, hb_ref):
    @pl.when(pl.program_id(0) == 0)
    def _():
        hf_ref[...] = jnp.zeros_like(hf_ref)
        hb_ref[...] = jnp.zeros_like(hb_ref)

    names = ("a", "r", "bt", "kt", "v")
    chains, sinks, state_refs = [], [], {}
    for ci in range(SCAN_CPS):
        for refs, y_ref, h_ref, reverse in (((af, rf, btf, ktf, vf, gamf), yf_ref, hf_ref, False),
                                            ((ab, rb, btb, ktb, vb, gamb), yb_ref, hb_ref, True)):
            chunk = SCAN_CPS - 1 - ci if reverse else ci
            rows = slice(chunk * CHUNK, (chunk + 1) * CHUNK)
            for q in range(D_RWKV // QUAD):
                sl = slice(q * QUAD, (q + 1) * QUAD)
                chain = {n: ref[rows, sl] for n, ref in zip(names, refs[:5])}
                chain.update(gam=refs[5][chunk, :, sl], hst=h_ref[:, sl], reverse=reverse, key=(reverse, q))
                chains.append(chain)
                sinks.append((y_ref, rows, sl))
                state_refs[(reverse, q)] = (h_ref, sl)
    outs, state = _scan_chunks(chains)
    for y, (y_ref, rows, sl) in zip(outs, sinks):
        y_ref[rows, sl] = y
    for key, (h_ref, sl) in state_refs.items():
        h_ref[:, sl] = state[key]


def _scan_call(feats, gamf, gamb, n_x_chunks, n_c_chunks):
    af, rf, btf, ktf, ab, rb, btb, ktb, v = feats
    assert n_x_chunks % SCAN_CPS == 0 and n_c_chunks % SCAN_CPS == 0
    nxb, ncb = n_x_chunks // SCAN_CPS, n_c_chunks // SCAN_CPS
    n = nxb + ncb
    assert SCAN_CPS * CHUNK == TM_FEAT
    fwd_blk = lambda i: jnp.where(i < ncb, nxb + i, i - ncb)
    bwd_blk = lambda i: n - 1 - i
    fwd_y = lambda i: (fwd_blk(i), 0)
    bwd_y = lambda i: (bwd_blk(i), 0)
    fwd = lambda i: (fwd_blk(i) + FEAT_SHIFT_BLOCKS, 0)
    bwd = lambda i: (bwd_blk(i) + FEAT_SHIFT_BLOCKS, 0)
    fwd3 = lambda i: (fwd_blk(i) + FEAT_SHIFT_BLOCKS, 0, 0)
    bwd3 = lambda i: (bwd_blk(i) + FEAT_SHIFT_BLOCKS, 0, 0)
    blk = (SCAN_CPS * CHUNK, D_RWKV)
    in_specs = ([pl.BlockSpec(blk, fwd)] * 5 + [pl.BlockSpec((SCAN_CPS, 1, D_RWKV), fwd3)]
                + [pl.BlockSpec(blk, bwd)] * 5 + [pl.BlockSpec((SCAN_CPS, 1, D_RWKV), bwd3)])
    y = jax.ShapeDtypeStruct((n * SCAN_CPS * CHUNK, D_RWKV), F32)
    return pl.pallas_call(
        _scan_body,
        grid=(n,),
        in_specs=in_specs,
        out_specs=[pl.BlockSpec(blk, fwd_y), pl.BlockSpec(blk, bwd_y)],
        out_shape=[y, y],
        scratch_shapes=[pltpu.VMEM((HEAD, D_RWKV), F32), pltpu.VMEM((HEAD, D_RWKV), F32)],
        compiler_params=_cparams(),
        name="scan",
    )(af, rf, btf, ktf, v, gamf, ab, rb, btb, ktb, v, gamb)


def _mix_body(x_ref, rowtab_ref, coltab_ref, yf_ref, yb_ref, bonus0_ref, bonus1_ref, g0_ref, g1_ref,
              pm_ref, pp_ref, pn_ref, gn_ref, wpool_ref, pscale_ref, wout_ref, mod_ref, g2n_ref, wr_ref, br_ref,
              o_ref, h3_ref, eid_ref, gate_ref, slab_ref, *, n_x_tiles, seq_len):
    j = pl.program_id(0)
    tm, d = x_ref.shape
    hl = POOL_HALO
    slab_ref[0:hl, :] = jnp.where(j != 0, pp_ref[...], 0.0)
    slab_ref[hl:hl + tm, :] = pm_ref[...]
    slab_ref[hl + tm:2 * hl + tm, :] = jnp.where(j != n_x_tiles - 1, pn_ref[...], 0.0)
    tglob = j * tm + lax.broadcasted_iota(jnp.int32, (tm, 1), 0)
    pooled = []
    for gi, win in enumerate(POOL_WINDOWS):
        cols = slice(gi * POOL_GW, (gi + 1) * POOL_GW)
        acc = slab_ref[hl - win // 2:hl - win // 2 + tm, cols]
        for off in range(-win // 2 + 1, win // 2):
            acc = acc + slab_ref[hl + off:hl + off + tm, cols]
        lo = jnp.maximum(tglob - win // 2, 0)
        hi = jnp.minimum(tglob - win // 2 + win, seq_len)
        cnt = (hi - lo).astype(F32)
        pg = acc / cnt - slab_ref[hl:hl + tm, cols]
        pooled.append(_dot(pg.astype(BF16), wpool_ref[gi]))
    pool = jnp.concatenate(pooled, axis=1) * pscale_ref[...]

    ones = _head_ones()
    y = yf_ref[...] + yb_ref[...]
    yc = y - _headsum(y, ones) * (1.0 / HEAD)
    var = _headsum(yc * yc, ones) * (1.0 / HEAD)
    bonus = jnp.concatenate([bonus0_ref[...], bonus1_ref[...]], axis=0)
    gate = jnp.concatenate([g0_ref[...], g1_ref[...]], axis=0)
    rw = (yc * lax.rsqrt(var + GN_EPS) * gn_ref[0:1, :] + gn_ref[1:2, :] + bonus) * gate
    mix = _dot(jnp.concatenate([rw, pool], axis=1).astype(BF16), wout_ref[...])
    gt_a = mod_ref[0:1, 2 * d:3 * d]
    x_mid = x_ref[...] + _pos_tile(rowtab_ref, coltab_ref) + gt_a * mix
    o_ref[...] = x_mid
    _route(x_mid, g2n_ref, mod_ref, wr_ref, br_ref, h3_ref, eid_ref, gate_ref)


def _mix_call(x2, rowtab, coltab, yf, yb, bonus, g, p, gn, w_pool_bf, pscale, w_out_bf, mod, g2n, wr, br):
    s, d = x2.shape
    tm = TM_PROJ
    nx = s // tm
    hb = tm // POOL_HALO
    const = lambda j: (0, 0)
    tok = lambda j: (j, 0)
    seq_spec = pl.BlockSpec((tm, D_RWKV), tok)
    assert tm == 2 * TM_FEAT
    feat_lo = pl.BlockSpec((TM_FEAT, D_RWKV), lambda j: (2 * j + FEAT_SHIFT_BLOCKS, 0))
    feat_hi = pl.BlockSpec((TM_FEAT, D_RWKV), lambda j: (2 * j + FEAT_SHIFT_BLOCKS + 1, 0))
    return pl.pallas_call(
        functools.partial(_mix_body, n_x_tiles=nx, seq_len=s),
        grid=(nx,),
        in_specs=[pl.BlockSpec((tm, d), tok),
                  pl.BlockSpec((tm // GRID_W, d // 2), tok),
                  pl.BlockSpec((tm, d // 2), const),
                  seq_spec, seq_spec, feat_lo, feat_hi, feat_lo, feat_hi, seq_spec,
                  pl.BlockSpec((POOL_HALO, D_POOL), lambda j: (jnp.maximum(j * hb - 1, 0), 0)),
                  pl.BlockSpec((POOL_HALO, D_POOL), lambda j: ((j + 1) * hb, 0)),
                  pl.BlockSpec((2, D_RWKV), const),
                  pl.BlockSpec(w_pool_bf.shape, lambda j: (0, 0, 0)),
                  pl.BlockSpec((1, D_POOL), const),
                  pl.BlockSpec(w_out_bf.shape, const),
                  pl.BlockSpec(mod.shape, const),
                  pl.BlockSpec((1, d), const),
                  pl.BlockSpec(wr.shape, const),
                  pl.BlockSpec(br.shape, const)],
        out_specs=[pl.BlockSpec((tm, d), tok),
                   pl.BlockSpec((tm * SUBLANES, LANES), tok),
                   pl.BlockSpec((8, tm), lambda j: (0, j)),
                   pl.BlockSpec((8, tm), lambda j: (0, j))],
        out_shape=[jax.ShapeDtypeStruct((s, d), F32),
                   jax.ShapeDtypeStruct((s * SUBLANES, LANES), F32),
                   jax.ShapeDtypeStruct((8, s), jnp.int32),
                   jax.ShapeDtypeStruct((8, s), F32)],
        scratch_shapes=[pltpu.VMEM((tm + 2 * POOL_HALO, D_POOL), F32)],
        compiler_params=_cparams(),
        name="mix",
    )(x2, rowtab, coltab, yf, yb, bonus, bonus, g, g, p, p, p, gn, w_pool_bf, pscale, w_out_bf, mod, g2n, wr, br)


def _to_token_tiles(ref, x):
    tm = x.shape[0]
    for j in range(x.shape[1] // LANES):
        ref[pl.ds(j, tm, stride=SUBLANES), :] = x[:, j * LANES:(j + 1) * LANES]


def _from_token_tiles(ref, tm):
    return jnp.concatenate([ref[pl.ds(j, tm, stride=SUBLANES), :] for j in range(SUBLANES)], axis=1)


def _route(x, g2_ref, mod_ref, wr_ref, br_ref, h3_ref, eid_ref, gate_ref):
    tm, d = x.shape
    ms = jnp.mean(x * x, axis=-1, keepdims=True)
    xn = x * lax.rsqrt(ms + NORM_EPS) * g2_ref[...]
    h = xn * (1.0 + mod_ref[0:1, 4 * d:5 * d]) + mod_ref[0:1, 3 * d:4 * d]
    _to_token_tiles(h3_ref, h)

    wh, wl = _split2(wr_ref[...])
    hh, hl = _split2(h)
    logits = _dot_nt(wh, hh) + (_dot_nt(wh, hl) + _dot_nt(wl, hh)) + br_ref[...]
    gl = logits[0:N_GROUPS, :]
    gmax = jnp.max(gl, axis=0, keepdims=True)
    gidx = lax.broadcasted_iota(jnp.int32, gl.shape, 0).astype(F32)
    grp = jnp.min(jnp.where(gl == gmax, gidx, float(N_GROUPS)), axis=0, keepdims=True)
    p_grp = 1.0 / jnp.sum(jnp.exp(gl - gmax), axis=0, keepdims=True)
    sel = jnp.zeros((EXPERTS_PER_GROUP, tm), F32)
    for gi in range(N_GROUPS):
        lo = N_GROUPS + gi * EXPERTS_PER_GROUP
        sel = jnp.where(grp == float(gi), logits[lo:lo + EXPERTS_PER_GROUP, :], sel)
    eidx = lax.broadcasted_iota(jnp.int32, sel.shape, 0).astype(F32)
    top1 = jnp.max(sel, axis=0, keepdims=True)
    i1 = jnp.min(jnp.where(sel == top1, eidx, float(EXPERTS_PER_GROUP)), axis=0, keepdims=True)
    sel2 = jnp.where(eidx == i1, -jnp.inf, sel)
    top2 = jnp.max(sel2, axis=0, keepdims=True)
    i2 = jnp.min(jnp.where(sel2 == top2, eidx, float(EXPERTS_PER_GROUP)), axis=0, keepdims=True)
    e2 = jnp.exp(top2 - top1)
    inv = 1.0 / (1.0 + e2)
    zf = jnp.zeros((6, tm), F32)
    gate_ref[...] = jnp.concatenate([p_grp * inv, p_grp * (e2 * inv), zf], axis=0)
    base = grp * float(EXPERTS_PER_GROUP)
    eid_ref[...] = jnp.concatenate([base + i1, base + i2, zf], axis=0).astype(jnp.int32)


def _sc_gather_tiles(table3, idx):
    n_out = idx.shape[0]
    n_workers = V7X_SC_CORES * V7X_SC_SUBCORES
    per_worker = n_out // n_workers
    n_chunks = per_worker // SC_GATHER_ROWS
    assert per_worker * n_workers == n_out and n_chunks * SC_GATHER_ROWS == per_worker and n_chunks % 2 == 0
    mesh = plsc.VectorSubcoreMesh(core_axis_name="c", subcore_axis_name="s",
                                  num_cores=V7X_SC_CORES, num_subcores=V7X_SC_SUBCORES)

    def body(table_hbm, idx_hbm, out_hbm, idx_v, rows0, rows1, gsem0, gsem1, wsem0, wsem1):
        worker = lax.axis_index("s") * V7X_SC_CORES + lax.axis_index("c")
        base = worker * per_worker
        pltpu.sync_copy(idx_hbm.at[pl.ds(base, per_worker)], idx_v)

        def gather(j, buf, sem):
            off = pl.multiple_of(j * SC_GATHER_ROWS, SC_GATHER_ROWS)
            return pltpu.make_async_copy(table_hbm.at[idx_v.at[pl.ds(off, SC_GATHER_ROWS)]], buf, sem)

        def write(j, buf, sem):
            off = pl.multiple_of(j * SC_GATHER_ROWS, SC_GATHER_ROWS)
            return pltpu.make_async_copy(buf, out_hbm.at[pl.ds(base + off, SC_GATHER_ROWS)], sem)

        gather(0, rows0, gsem0).start()

        @pl.loop(0, n_chunks, step=2)
        def _(j):
            gather(j, rows0, gsem0).wait()

            @pl.when(j >= 2)
            def _():
                write(j - 1, rows1, wsem1).wait()

            gather(j + 1, rows1, gsem1).start()
            write(j, rows0, wsem0).start()
            gather(j + 1, rows1, gsem1).wait()
            write(j, rows0, wsem0).wait()

            @pl.when(j + 2 < n_chunks)
            def _():
                gather(j + 2, rows0, gsem0).start()

            write(j + 1, rows1, wsem1).start()

        write(n_chunks - 1, rows1, wsem1).wait()

    rows = pltpu.VMEM((SC_GATHER_ROWS,) + table3.shape[1:], table3.dtype)
    return pl.kernel(
        body,
        out_type=jax.ShapeDtypeStruct((n_out,) + table3.shape[1:], table3.dtype),
        mesh=mesh,
        scratch_types=[pltpu.VMEM((per_worker,), jnp.int32), rows, rows,
                       pltpu.SemaphoreType.DMA, pltpu.SemaphoreType.DMA,
                       pltpu.SemaphoreType.DMA, pltpu.SemaphoreType.DMA],
        name="sc_gather",
    )(table3, idx)


def _experts_body(blk_e_ref, n_used_ref, first_ref, next_e_ref, wslot_ref, xs_ref, wg_hbm, wu_hbm, wd_hbm, y_ref,
                  wg_buf, wu_buf, wd_buf, wsem_ref):
    i = pl.program_id(0)
    used = i < n_used_ref[0]
    slot = wslot_ref[i]

    def weight_copies(expert, to_slot):
        return [pltpu.make_async_copy(hbm.at[expert], buf.at[to_slot], wsem_ref.at[to_slot])
                for hbm, buf in ((wg_hbm, wg_buf), (wu_hbm, wu_buf), (wd_hbm, wd_buf))]

    @pl.when(jnp.logical_and(i == 0, used))
    def _():
        for cp in weight_copies(blk_e_ref[0], 0):
            cp.start()

    @pl.when(jnp.logical_and(used, first_ref[i] == 1))
    def _():
        for cp in weight_copies(blk_e_ref[i], slot):
            cp.wait()

        @pl.when(next_e_ref[i] != blk_e_ref[i])
        def _():
            for cp in weight_copies(next_e_ref[i], 1 - slot):
                cp.start()

    @pl.when(used)
    def _():
        xb = _from_token_tiles(xs_ref, MOE_BM).astype(BF16)
        gate = _dot(xb, wg_buf[slot].astype(BF16))
        upp = _dot(xb, wu_buf[slot].astype(BF16))
        hid = (gate * _sigmoid(gate)) * upp
        _to_token_tiles(y_ref, _dot(hid.astype(BF16), wd_buf[slot].astype(BF16)))

    @pl.when(jnp.logical_not(used))
    def _():
        y_ref[...] = jnp.zeros_like(y_ref)


def _experts_call(blk_expert, n_used, xs3, wg, wu, wd):
    nb = blk_expert.shape[0]
    d = wg.shape[1]
    rows = MOE_BM * SUBLANES
    first, next_e, wslot = _weight_schedule(blk_expert, n_used)
    blk = lambda i, *_: (i, 0)
    grid_spec = pltpu.PrefetchScalarGridSpec(
        num_scalar_prefetch=5,
        grid=(nb,),
        in_specs=[pl.BlockSpec((rows, LANES), blk),
                  pl.BlockSpec(memory_space=pl.ANY),
                  pl.BlockSpec(memory_space=pl.ANY),
                  pl.BlockSpec(memory_space=pl.ANY)],
        out_specs=pl.BlockSpec((rows, LANES), blk),
        scratch_shapes=[pltpu.VMEM((2, d, D_EXPERT), F32), pltpu.VMEM((2, d, D_EXPERT), F32),
                        pltpu.VMEM((2, D_EXPERT, d), F32), pltpu.SemaphoreType.DMA((2,))],
    )
    return pl.pallas_call(
        _experts_body,
        grid_spec=grid_spec,
        out_shape=jax.ShapeDtypeStruct((nb * rows, LANES), F32),
        compiler_params=_cparams(),
        name="experts",
    )(blk_expert, n_used, first, next_e, wslot, xs3, wg, wu, wd)


def _combine_body(x_ref, y0_ref, y1_ref, gate_ref, mod_ref, fg_ref, o_ref):
    tm, d = x_ref.shape
    y = gate_ref[:, 0:1] * _from_token_tiles(y0_ref, tm) + gate_ref[:, 1:2] * _from_token_tiles(y1_ref, tm)
    x = x_ref[...] + mod_ref[0:1, 5 * d:6 * d] * y
    ms = jnp.mean(x * x, axis=-1, keepdims=True)
    o_ref[...] = x * lax.rsqrt(ms + NORM_EPS) * fg_ref[...]


def _combine_call(x_mid, y3, gcol, mod, fg):
    s, d = x_mid.shape
    tm = TM_PROJ
    nt = s // tm
    const = lambda j: (0, 0)
    return pl.pallas_call(
        _combine_body,
        grid=(nt,),
        in_specs=[pl.BlockSpec((tm, d), lambda j: (j, 0)),
                  pl.BlockSpec((tm * SUBLANES, LANES), lambda j: (j, 0)),
                  pl.BlockSpec((tm * SUBLANES, LANES), lambda j: (j + nt, 0)),
                  pl.BlockSpec((tm, 2), lambda j: (j, 0)),
                  pl.BlockSpec(mod.shape, const),
                  pl.BlockSpec((1, d), const)],
        out_specs=pl.BlockSpec((tm, d), lambda j: (j, 0)),
        out_shape=jax.ShapeDtypeStruct((s, d), F32),
        compiler_params=_cparams(),
        name="combine",
    )(x_mid, y3, y3, gcol, mod, fg)


def _blockdiag2(wf, wb):
    z = jnp.zeros_like(wf)
    return jnp.concatenate([jnp.concatenate([wf, z], axis=1), jnp.concatenate([z, wb], axis=1)], axis=0)


def _routing_plan(eids, n_tok):
    m = 2 * n_tok
    experts = jnp.arange(N_EXPERTS, dtype=jnp.int32)
    flat_e = eids[:2].reshape(m)
    counts = jnp.sum((flat_e[:, None] == experts[None, :]).astype(jnp.int32), axis=0)
    pcounts = (counts + MOE_BM - 1) // MOE_BM * MOE_BM
    n_blocks = (m + N_EXPERTS * (MOE_BM - 1) + MOE_BM - 1) // MOE_BM
    n_slots = n_blocks * MOE_BM
    pad_ends = jnp.cumsum(pcounts - counts)
    pad = jnp.arange(n_slots - m, dtype=jnp.int32)
    pad_e = jnp.sum((pad[:, None] >= pad_ends[None, :]).astype(jnp.int32), axis=1)
    ids = jnp.arange(n_slots, dtype=jnp.int32)
    slot_e, slot_id = lax.sort((jnp.concatenate([flat_e, pad_e]), ids), num_keys=1, is_stable=True)
    src_tok = slot_id % n_tok
    _, slot_of_id = lax.sort((slot_id, ids), num_keys=1)
    dest = slot_of_id[:m]
    blk_expert = jnp.minimum(slot_e[::MOE_BM], N_EXPERTS - 1)
    n_used = (jnp.sum(pcounts)[None] // MOE_BM).astype(jnp.int32)
    return src_tok, dest, blk_expert, n_used


def _weight_schedule(blk_expert, n_used):
    used = jnp.arange(blk_expert.shape[0], dtype=jnp.int32) < n_used[0]
    first = jnp.logical_and(used, jnp.concatenate([jnp.ones((1,), bool), blk_expert[1:] != blk_expert[:-1]]))
    none = jnp.int32(N_EXPERTS)
    later_first = lax.cummin(jnp.where(first, blk_expert, none), axis=0, reverse=True)
    next_e = jnp.concatenate([later_first[1:], none[None]])
    next_e = jnp.where(next_e == none, blk_expert, next_e)
    wslot = jnp.maximum(jnp.cumsum(first.astype(jnp.int32)) - 1, 0) & 1
    return first.astype(jnp.int32), next_e, wslot


def kernel(x, c, ctx, c_ctx, norm1_g, w_mod, b_mod, w_in, mu_shift, w0_f, w2_f, w0_b, w2_b, a0_f, a2_f, a0_b, a2_b, g2, k_k, k_a, r_k_f, r_k_b, gn_w, gn_b, w_pool, pool_scale, w_out, norm2_g, w_router_grp, b_router_grp, w_router_exp, b_router_exp, w_gate, w_up, w_down, final_g):
    b, s, d = x.shape
    n_ctx = ctx.shape[1]
    assert b == 1 and c.shape[0] == 1 and w_mod.shape[0] == 1
    assert s % TM_PROJ == 0 and n_ctx % CHUNK == 0 and n_ctx == TM_FEAT and d == 2 * D_RWKV
    x2 = x[0]
    rows = s // GRID_W

    quarter = d // 4
    freq = POS_THETA ** (-jnp.arange(quarter, dtype=F32) / quarter)
    rarg = jnp.arange(rows, dtype=F32)[:, None] * freq
    carg = jnp.arange(GRID_W, dtype=F32)[:, None] * freq
    rowtab = jnp.concatenate([jnp.sin(rarg), jnp.cos(rarg)], axis=-1)
    coltab = jnp.tile(jnp.concatenate([jnp.sin(carg), jnp.cos(carg)], axis=-1), (TM_PROJ // GRID_W, 1))

    cs = jnp.concatenate([c, c_ctx[None, :], jnp.zeros((6, d), F32)], axis=0)
    mod = _mod_call(cs, w_mod[0], b_mod)

    ctx_pad = jnp.concatenate([ctx[0], jnp.zeros((TM_PROJ - n_ctx, d), F32)], axis=0)
    par = jnp.concatenate([w0_f, w0_b, a0_f, a0_b, k_k, k_a, r_k_f, r_k_b], axis=0)
    outs = _front_call(x2, ctx_pad, rowtab, coltab, norm1_g, mod, w_in[0].astype(BF16), mu_shift, par,
                       _blockdiag2(w2_f[0], w2_b[0]).astype(BF16), _blockdiag2(a2_f[0], a2_b[0]).astype(BF16),
                       g2[0].astype(BF16))
    p, feats, gamf, gamb, bonus, g = outs[0], outs[1:10], outs[10], outs[11], outs[12], outs[13]
    yf, yb = _scan_call(feats, gamf.reshape(-1, 1, D_RWKV), gamb.reshape(-1, 1, D_RWKV),
                        s // CHUNK, n_ctx // CHUNK)

    gn = jnp.concatenate([gn_w, gn_b], axis=0)
    wr = jnp.concatenate([w_router_grp[0].T, w_router_exp[0].T,
                          jnp.zeros((ROUTER_ROWS - N_GROUPS - N_EXPERTS, d), F32)], axis=0)
    br = jnp.concatenate([b_router_grp[0], b_router_exp[0],
                          jnp.zeros((ROUTER_ROWS - N_GROUPS - N_EXPERTS,), F32)])[:, None]
    x_mid, h3, eids, gates = _mix_call(x2, rowtab, coltab, yf, yb, bonus, g, p, gn, w_pool[0].astype(BF16),
                                       pool_scale, w_out[0].astype(BF16), mod, norm2_g, wr, br)

    src_tok, dest, blk_expert, n_used = _routing_plan(eids, s)
    tile = (SUBLANES, LANES)
    xs3 = _sc_gather_tiles(h3.reshape((s,) + tile), src_tok)
    yexp = _experts_call(blk_expert, n_used, xs3.reshape(-1, LANES), w_gate[0], w_up[0], w_down[0])
    y3 = _sc_gather_tiles(yexp.reshape((-1,) + tile), dest).reshape(-1, LANES)
    out = _combine_call(x_mid, y3, gates[:2].T, mod, final_g[None, :])
    return out[None]
```

```python
import functools

import jax
import jax.numpy as jnp
from jax import lax
from jax.experimental import pallas as pl
from jax.experimental.pallas import tpu as pltpu
from jax.experimental.pallas import tpu_sc as plsc

F32 = jnp.float32
BF16 = jnp.bfloat16

GRID_W = 64
SUBLANES, LANES = 8, 128
HEAD = 64
N_HEADS = 8
D_RWKV = HEAD * N_HEADS
D_POOL = 512
POOL_WINDOWS = (2, 4, 8, 16)
POOL_GW = 128
POOL_HALO = 8
D_SHIFT = 1920
N_GROUPS = 4
EXPERTS_PER_GROUP = 8
N_EXPERTS = 32
D_EXPERT = 512
NORM_EPS = 1e-6
GN_EPS = 64e-5
POS_THETA = 10000.0
DECAY_SCALE = 0.6065306597

CHUNK = 64
QUAD = 4 * HEAD
SCAN_CPS = 4
TM_PROJ = 512
TM_FEAT = 256
FEAT_SHIFT_BLOCKS = 3
MOE_BM = 256
ROUTER_ROWS = 48
V7X_VMEM_LIMIT = 56 * 1024 * 1024
V7X_SC_CORES, V7X_SC_SUBCORES = 2, 16
SC_GATHER_ROWS = 32


def _cparams(n_axes=1, vmem=V7X_VMEM_LIMIT):
    return pltpu.CompilerParams(dimension_semantics=("arbitrary",) * n_axes,
                                vmem_limit_bytes=vmem)


def _dot(a, b):
    return jnp.dot(a, b, preferred_element_type=F32)


def _dot_nt(a, b):
    return lax.dot_general(a, b, (((1,), (1,)), ((), ())), preferred_element_type=F32)


def _dot_tn(a, b):
    return lax.dot_general(a, b, (((0,), (0,)), ((), ())), preferred_element_type=F32)


def _split2(x):
    hi = x.astype(BF16)
    lo = (x - hi.astype(F32)).astype(BF16)
    return hi, lo


def _dot3(a, b):
    ah, al = _split2(a)
    bh, bl = _split2(b)
    return _dot(ah, bh) + (_dot(ah, bl) + _dot(al, bh))


def _sigmoid(x):
    return 0.5 * jnp.tanh(0.5 * x) + 0.5


def _head_ones():
    r = lax.broadcasted_iota(jnp.int32, (D_RWKV, D_RWKV), 0) >> 6
    c = lax.broadcasted_iota(jnp.int32, (D_RWKV, D_RWKV), 1) >> 6
    return (r == c).astype(BF16)


def _headsum(x, ones):
    hi, lo = _split2(x)
    return _dot(hi, ones) + _dot(lo, ones)


def _mod_body(cs_ref, w_ref, b_ref, o_ref):
    a = cs_ref[...]
    a = a * _sigmoid(a)
    o_ref[...] = _dot3(a, w_ref[...]) + b_ref[...]


def _mod_call(cs, w_mod, b_mod):
    d, n = w_mod.shape
    tn = 512
    return pl.pallas_call(
        _mod_body,
        grid=(n // tn,),
        in_specs=[pl.BlockSpec((8, d), lambda j: (0, 0)),
                  pl.BlockSpec((d, tn), lambda j: (0, j)),
                  pl.BlockSpec((1, tn), lambda j: (0, j))],
        out_specs=pl.BlockSpec((8, tn), lambda j: (0, j)),
        out_shape=jax.ShapeDtypeStruct((8, n), F32),
        compiler_params=_cparams(),
        name="mod",
    )(cs, w_mod, b_mod)


def _pos_tile(rowtab_ref, coltab_ref):
    rt = rowtab_ref[...]
    rowpart = jnp.concatenate(
        [jnp.broadcast_to(rt[r:r + 1, :], (GRID_W, rt.shape[1])) for r in range(TM_PROJ // GRID_W)], axis=0)
    return jnp.concatenate([rowpart, coltab_ref[...]], axis=1)


def _features(u, prow, nrow, mu, par_ref, w2_ref, a2_ref, g2_ref, rows, chunk0, outs, consts):
    (af_ref, rf_ref, btf_ref, ktf_ref, ab_ref, rb_ref, btb_ref, ktb_ref, v_ref,
     gamf_ref, gamb_ref, bonus_ref, g_ref) = outs
    tm = u.shape[0]
    rid = lax.broadcasted_iota(jnp.int32, (SUBLANES, 1), 0)

    def shifted(lo, hi):
        uc, m = u[:, lo:hi], mu[:, lo:hi]
        up = pltpu.roll(uc, 1, 0)
        up = jnp.concatenate([jnp.where(rid == 0, prow[:, lo:hi], up[:SUBLANES]), up[SUBLANES:]], axis=0)
        dn = pltpu.roll(uc, tm - 1, 0)
        dn = jnp.concatenate(
            [dn[:-SUBLANES], jnp.where(rid == SUBLANES - 1, nrow[:, lo:hi], dn[-SUBLANES:])], axis=0)
        return (1.0 - m) * uc + (0.5 * m) * (up + dn)

    w0f, w0b, a0f, a0b = par_ref[0:1, :], par_ref[1:2, :], par_ref[2:3, :], par_ref[3:4, :]
    k_k, k_a, rkf, rkb = par_ref[4:5, :], par_ref[5:6, :], par_ref[6:7, :], par_ref[7:8, :]

    r = shifted(0, 512)
    yield
    k = shifted(512, 1024)
    yield
    v = shifted(1024, 1536)
    v_ref[rows, :] = v.astype(BF16)
    yield
    lora = shifted(1536, D_SHIFT)
    zw = _dot(jnp.tanh(lora[:, 0:128]).astype(BF16), w2_ref[...])
    za = _dot(lora[:, 128:256].astype(BF16), a2_ref[...])
    g_ref[rows, :] = _dot(_sigmoid(lora[:, 256:384]).astype(BF16), g2_ref[...])
    yield
    lw_f = -DECAY_SCALE * _sigmoid(w0f + zw[:, :512])
    lw_b = -DECAY_SCALE * _sigmoid(w0b + zw[:, 512:])
    a_f = _sigmoid(a0f + za[:, :512])
    a_b = _sigmoid(a0b + za[:, 512:])
    yield

    ones, earlier, later = consts
    kkr = k * k_k
    kk = kkr * lax.rsqrt(jnp.maximum(_headsum(kkr * kkr, ones), 1e-24))
    yield
    k_f = k * (1.0 + (a_f - 1.0) * k_a)
    k_b = k * (1.0 + (a_b - 1.0) * k_a)
    bonus_ref[rows, :] = _headsum(r * (k_f * rkf + k_b * rkb), ones) * v
    yield

    n_ch = tm // CHUNK

    def visited_before(m, lwd):
        l1, l2 = _split2(lwd)
        return _dot(m, l1) + _dot(m, l2)

    def emit(excl, lwd, a_d, k_d, last_row, a_ref, r_ref, bt_ref, kt_ref, gam_ref):
        cum = excl + lwd
        gam_ref[0, chunk0:chunk0 + n_ch, :] = jnp.exp(jnp.concatenate(
            [cum[c * CHUNK + last_row:c * CHUNK + last_row + 1, :] for c in range(n_ch)], axis=0))
        a_ref[rows, :] = (-kk * jnp.exp(excl)).astype(BF16)
        r_ref[rows, :] = (r * jnp.exp(cum)).astype(BF16)
        yield
        e_neg = jnp.exp(-cum)
        bt_ref[rows, :] = (kk * a_d * e_neg).astype(BF16)
        kt_ref[rows, :] = (k_d * e_neg).astype(BF16)
        yield

    excl_f = visited_before(earlier, lw_f)
    yield
    yield from emit(excl_f, lw_f, a_f, k_f, CHUNK - 1, af_ref, rf_ref, btf_ref, ktf_ref, gamf_ref)
    excl_b = visited_before(later, lw_b)
    yield
    yield from emit(excl_b, lw_b, a_b, k_b, 0, ab_ref, rb_ref, btb_ref, ktb_ref, gamb_ref)


def _front_body(x_ref, ctx_ref, rowtab_ref, coltab_ref, g1_ref, mod_ref, w_ref, mu_ref, par_ref, w2_ref, a2_ref,
                g2_ref, p_ref, af_ref, rf_ref, btf_ref, ktf_ref, ab_ref, rb_ref, btb_ref, ktb_ref, v_ref,
                gamf_ref, gamb_ref, bonus_ref, g_ref, u_ref, unew_ref, carry_ref, *, n_x_tiles):
    i = pl.program_id(0)
    d = x_ref.shape[1]
    d_in = w_ref.shape[1]
    half = TM_FEAT
    last = carry_ref.shape[0] - 1

    @pl.when(i == 0)
    def _():
        u_ref[...] = jnp.zeros_like(u_ref)
        carry_ref[...] = jnp.zeros_like(carry_ref)

    is_ctx = i >= n_x_tiles
    xin = jnp.where(is_ctx, ctx_ref[...], x_ref[...] + _pos_tile(rowtab_ref, coltab_ref))
    ms = jnp.mean(xin * xin, axis=-1, keepdims=True)
    xn = xin * lax.rsqrt(ms + NORM_EPS) * g1_ref[...]
    sh = jnp.where(is_ctx, mod_ref[1:2, 0:d], mod_ref[0:1, 0:d])
    sc = jnp.where(is_ctx, mod_ref[1:2, d:2 * d], mod_ref[0:1, d:2 * d])
    hb = (xn * (1.0 + sc) + sh).astype(BF16)
    pending = [(slice(r0, r0 + half), lo, min(lo + QUAD, d_in))
               for lo in range(0, d_in, QUAD) for r0 in range(0, hb.shape[0], half)]

    def project_next():
        if not pending:
            return
        rs, lo, hi = pending.pop(0)
        uc = _dot(hb[rs, :], w_ref[:, lo:hi])
        if lo < D_SHIFT:
            unew_ref[rs, lo:min(hi, D_SHIFT)] = uc[:, :min(hi, D_SHIFT) - lo]
        if hi > D_SHIFT:
            p_ref[rs, max(lo, D_SHIFT) - D_SHIFT:hi - D_SHIFT] = uc[:, max(lo, D_SHIFT) - lo:]

    outs = (af_ref, rf_ref, btf_ref, ktf_ref, ab_ref, rb_ref, btb_ref, ktb_ref, v_ref,
            gamf_ref, gamb_ref, bonus_ref, g_ref)
    mu = mu_ref[...]
    ctx_block = 2 * n_x_tiles
    blocks = ((2 * i - 3, carry_ref[SUBLANES:, :], carry_ref[SUBLANES - 1:SUBLANES, :], u_ref[0:1, :]),
              (2 * i - 2, u_ref[0:half, :], carry_ref[last:last + 1, :], u_ref[half:half + 1, :]))
    rr = lax.broadcasted_iota(jnp.int32, (half, half), 0)
    cc = lax.broadcasted_iota(jnp.int32, (half, half), 1)
    same = (rr >> 6) == (cc >> 6)
    consts = (_head_ones(), jnp.logical_and(same, cc < rr).astype(BF16),
              jnp.logical_and(same, cc > rr).astype(BF16))
    project_next()
    for hi, (blk, ub, prow, nrow) in enumerate(blocks):
        prev_ok = jnp.logical_and(blk != 0, blk != ctx_block)
        next_ok = jnp.logical_and(blk != ctx_block - 1, blk != ctx_block)
        for _ in _features(ub, jnp.where(prev_ok, prow, 0.0), jnp.where(next_ok, nrow, 0.0), mu, par_ref, w2_ref,
                           a2_ref, g2_ref, slice(hi * half, (hi + 1) * half), hi * (half // CHUNK), outs, consts):
            project_next()
    while pending:
        project_next()
    carry_ref[...] = u_ref[half - SUBLANES:, :]
    u_ref[...] = unew_ref[...]


def _front_call(x2, ctx_pad, rowtab, coltab, g1, mod, w_in_bf, mu, par, w2, a2, g2):
    s, d = x2.shape
    tm = TM_PROJ
    nx = s // tm
    d_in = w_in_bf.shape[1]
    n_steps = nx + 2
    xmap = lambda i: (jnp.minimum(i, nx - 1), 0)
    const = lambda i: (0, 0)
    step = lambda i: (i, 0)
    seq_bf = jax.ShapeDtypeStruct((n_steps * tm, D_RWKV), BF16)
    seq_f32 = jax.ShapeDtypeStruct((n_steps * tm, D_RWKV), F32)
    gam = jax.ShapeDtypeStruct((n_steps, tm // CHUNK, D_RWKV), F32)
    seq_spec = pl.BlockSpec((tm, D_RWKV), step)
    gam_spec = pl.BlockSpec((1, tm // CHUNK, D_RWKV), lambda i: (i, 0, 0))
    return pl.pallas_call(
        functools.partial(_front_body, n_x_tiles=nx),
        grid=(n_steps,),
        in_specs=[pl.BlockSpec((tm, d), xmap),
                  pl.BlockSpec((tm, d), const),
                  pl.BlockSpec((tm // GRID_W, d // 2), xmap),
                  pl.BlockSpec((tm, d // 2), const),
                  pl.BlockSpec((1, d), const),
                  pl.BlockSpec(mod.shape, const),
                  pl.BlockSpec((d, d_in), const),
                  pl.BlockSpec((1, D_SHIFT), const),
                  pl.BlockSpec((8, D_RWKV), const),
                  pl.BlockSpec(w2.shape, const),
                  pl.BlockSpec(a2.shape, const),
                  pl.BlockSpec(g2.shape, const)],
        out_specs=([pl.BlockSpec((tm, D_POOL), lambda i: (jnp.minimum(i, nx), 0))] + [seq_spec] * 9
                   + [gam_spec, gam_spec, seq_spec, seq_spec]),
        out_shape=([jax.ShapeDtypeStruct(((nx + 1) * tm, D_POOL), F32)] + [seq_bf] * 9
                   + [gam, gam, seq_f32, seq_f32]),
        scratch_shapes=[pltpu.VMEM((tm, D_SHIFT), F32), pltpu.VMEM((tm, D_SHIFT), F32),
                        pltpu.VMEM((TM_FEAT + SUBLANES, D_SHIFT), F32)],
        compiler_params=_cparams(),
        name="front",
    )(x2, ctx_pad, rowtab, coltab, g1, mod, w_in_bf, mu, par, w2, a2, g2)


def _blockdiag(x, head_of_lane):
    xb = x.astype(BF16)
    zero = jnp.zeros_like(xb)
    return jnp.concatenate([jnp.where(head_of_lane == h, xb, zero) for h in range(QUAD // HEAD)], axis=0)


def _unblock(x, head_of_lane):
    out = jnp.where(head_of_lane == 0, x[0:HEAD, :], 0.0)
    for h in range(1, QUAD // HEAD):
        out = jnp.where(head_of_lane == h, x[h * HEAD:(h + 1) * HEAD, :], out)
    return out


def _scan_chunks(chains):
    lane = lax.broadcasted_iota(jnp.int32, (HEAD, QUAD), 1)
    head_of_lane = lane >> 6
    s_idx = lane & (HEAD - 1)
    t_idx = lax.broadcasted_iota(jnp.int32, (HEAD, QUAD), 0)
    eye = s_idx == t_idx
    lower = s_idx < t_idx
    upper = s_idx > t_idx
    bd = functools.partial(_blockdiag, head_of_lane=head_of_lane)

    def same_block(shift):
        return (s_idx >> shift) == (t_idx >> shift)

    def each(fn):
        return [fn(c) for c in chains]

    def stage_masks(c):
        ll = _dot_nt(jnp.concatenate([c["a"], c["r"]], axis=0),
                     jnp.concatenate([bd(c["bt"]), bd(c["kt"])], axis=0))
        strict = upper if c["reverse"] else lower
        incl = jnp.logical_or(strict, eye)
        c["lab"] = jnp.where(strict, ll[:HEAD, :QUAD], 0.0)
        c["lak"] = jnp.where(strict, ll[:HEAD, QUAD:], 0.0)
        c["mrb"] = jnp.where(incl, ll[HEAD:, :QUAD], 0.0)
        c["mrk"] = jnp.where(incl, ll[HEAD:, QUAD:], 0.0)

    each(stage_masks)

    def stage_square(c):
        ld = jnp.where(same_block(4), c["lab"], 0.0)
        c["tm"] = jnp.where(eye, 1.0, 0.0) + ld
        c["xk"] = _dot(ld.astype(BF16), bd(ld))

    def stage_double(c):
        y = _dot(jnp.concatenate([c["tm"], c["xk"]], axis=0).astype(BF16), bd(c["xk"]))
        c["tm"] = c["tm"] + y[:HEAD]
        c["xk"] = y[HEAD:]

    def stage_last_power(c):
        c["tm"] = c["tm"] + _dot(c["tm"].astype(BF16), bd(c["xk"]))

    each(stage_square)
    each(stage_double)
    each(stage_double)
    each(stage_last_power)
    each(lambda c: c.update(vv=_dot(jnp.concatenate([c["lak"], c["mrk"]], axis=0).astype(BF16), bd(c["v"]))))
    for shift in (4, 5):
        joined = jnp.logical_and(same_block(shift + 1), jnp.logical_not(same_block(shift)))
        each(lambda c: c.update(te=_dot(c["tm"].astype(BF16), bd(jnp.where(joined, c["lab"], 0.0)))))
        each(lambda c: c.update(tm=c["tm"] + _dot(c["te"].astype(BF16), bd(c["tm"]))))

    each(lambda c: c.update(x1=_dot(c["tm"].astype(BF16),
                                    jnp.concatenate([bd(c["a"]), bd(c["vv"][:HEAD])], axis=1))))
    each(lambda c: c.update(x2=_dot(c["mrb"].astype(BF16),
                                    jnp.concatenate([bd(c["x1"][:, :QUAD]), bd(c["x1"][:, QUAD:])], axis=1))))

    def stage_state_terms(c):
        rhs = jnp.concatenate(
            [c["x1"].astype(BF16), jnp.concatenate([jnp.zeros((HEAD, QUAD), BF16), c["v"]], axis=1)], axis=0)
        hat = (jnp.concatenate([c["bt"], c["kt"]], axis=0).astype(F32) * c["gam"]).astype(BF16)
        pz = _dot_tn(hat, rhs)
        c["pq"] = _unblock(pz[:, :QUAD], head_of_lane) + jnp.where(eye, c["gam"], 0.0)
        c["zq"] = _unblock(pz[:, QUAD:], head_of_lane)
        c["qeff"] = c["r"].astype(F32) + c["x2"][:, :QUAD]
        c["yin"] = c["x2"][:, QUAD:] + c["vv"][HEAD:]

    each(stage_state_terms)

    outs, state = [], {}
    for c in chains:
        hst = state.get(c["key"], c["hst"])
        qp = _dot(jnp.concatenate([c["qeff"], c["pq"]], axis=0).astype(BF16), bd(hst))
        state[c["key"]] = qp[HEAD:] + c["zq"]
        outs.append(qp[:HEAD] + c["yin"])
    return outs, state


def _scan_body(af, rf, btf, ktf, vf, gamf, ab, rb, btb, ktb, vb, gamb, yf_ref, yb_ref, hf_ref, hb_ref):
    @pl.when(pl.program_id(0) == 0)
    def _():
        hf_ref[...] = jnp.zeros_like(hf_ref)
        hb_ref[...] = jnp.zeros_like(hb_ref)

    names = ("a", "r", "bt", "kt", "v")
    chains, sinks, state_refs = [], [], {}
    for ci in range(SCAN_CPS):
        for refs, y_ref, h_ref, reverse in (((af, rf, btf, ktf, vf, gamf), yf_ref, hf_ref, False),
                                            ((ab, rb, btb, ktb, vb, gamb), yb_ref, hb_ref, True)):
            chunk = SCAN_CPS - 1 - ci if reverse else ci
            rows = slice(chunk * CHUNK, (chunk + 1) * CHUNK)
            for q in range(D_RWKV // QUAD):
                sl = slice(q * QUAD, (q + 1) * QUAD)
                chain = {n: ref[rows, sl] for n, ref in zip(names, refs[:5])}
                chain.update(gam=refs[5][chunk, :, sl], hst=h_ref[:, sl], reverse=reverse, key=(reverse, q))
                chains.append(chain)
                sinks.append((y_ref, rows, sl))
                state_refs[(reverse, q)] = (h_ref, sl)
    outs, state = _scan_chunks(chains)
    for y, (y_ref, rows, sl) in zip(outs, sinks):
        y_ref[rows, sl] = y
    for key, (h_ref, sl) in state_refs.items():
        h_ref[:, sl] = state[key]


def _scan_call(feats, gamf, gamb, n_x_chunks, n_c_chunks):
    af, rf, btf, ktf, ab, rb, btb, ktb, v = feats
    assert n_x_chunks % SCAN_CPS == 0 and n_c_chunks % SCAN_CPS == 0
    nxb, ncb = n_x_chunks // SCAN_CPS, n_c_chunks // SCAN_CPS
    n = nxb + ncb
    assert SCAN_CPS * CHUNK == TM_FEAT
    fwd_blk = lambda i: jnp.where(i < ncb, nxb + i, i - ncb)
    bwd_blk = lambda i: n - 1 - i
    fwd_y = lambda i: (fwd_blk(i), 0)
    bwd_y = lambda i: (bwd_blk(i), 0)
    fwd = lambda i: (fwd_blk(i) + FEAT_SHIFT_BLOCKS, 0)
    bwd = lambda i: (bwd_blk(i) + FEAT_SHIFT_BLOCKS, 0)
    fwd3 = lambda i: (fwd_blk(i) + FEAT_SHIFT_BLOCKS, 0, 0)
    bwd3 = lambda i: (bwd_blk(i) + FEAT_SHIFT_BLOCKS, 0, 0)
    blk = (SCAN_CPS * CHUNK, D_RWKV)
    in_specs = ([pl.BlockSpec(blk, fwd)] * 5 + [pl.BlockSpec((SCAN_CPS, 1, D_RWKV), fwd3)]
                + [pl.BlockSpec(blk, bwd)] * 5 + [pl.BlockSpec((SCAN_CPS, 1, D_RWKV), bwd3)])
    y = jax.ShapeDtypeStruct((n * SCAN_CPS * CHUNK, D_RWKV), F32)
    return pl.pallas_call(
        _scan_body,
        grid=(n,),
        in_specs=in_specs,
        out_specs=[pl.BlockSpec(blk, fwd_y), pl.BlockSpec(blk, bwd_y)],
        out_shape=[y, y],
        scratch_shapes=[pltpu.VMEM((HEAD, D_RWKV), F32), pltpu.VMEM((HEAD, D_RWKV), F32)],
        compiler_params=_cparams(),
        name="scan",
    )(af, rf, btf, ktf, v, gamf, ab, rb, btb, ktb, v, gamb)


def _mix_body(x_ref, rowtab_ref, coltab_ref, yf_ref, yb_ref, bonus0_ref, bonus1_ref, g0_ref, g1_ref,
              pm_ref, pp_ref, pn_ref, gn_ref, wpool_ref, pscale_ref, wout_ref, mod_ref, g2n_ref, wr_ref, br_ref,
              o_ref, h3_ref, eid_ref, gate_ref, slab_ref, *, n_x_tiles, seq_len):
    j = pl.program_id(0)
    tm, d = x_ref.shape
    hl = POOL_HALO
    slab_ref[0:hl, :] = jnp.where(j != 0, pp_ref[...], 0.0)
    slab_ref[hl:hl + tm, :] = pm_ref[...]
    slab_ref[hl + tm:2 * hl + tm, :] = jnp.where(j != n_x_tiles - 1, pn_ref[...], 0.0)
    tglob = j * tm + lax.broadcasted_iota(jnp.int32, (tm, 1), 0)
    pooled = []
    for gi, win in enumerate(POOL_WINDOWS):
        cols = slice(gi * POOL_GW, (gi + 1) * POOL_GW)
        acc = slab_ref[hl - win // 2:hl - win // 2 + tm, cols]
        for off in range(-win // 2 + 1, win // 2):
            acc = acc + slab_ref[hl + off:hl + off + tm, cols]
        lo = jnp.maximum(tglob - win // 2, 0)
        hi = jnp.minimum(tglob - win // 2 + win, seq_len)
        cnt = (hi - lo).astype(F32)
        pg = acc / cnt - slab_ref[hl:hl + tm, cols]
        pooled.append(_dot(pg.astype(BF16), wpool_ref[gi]))
    pool = jnp.concatenate(pooled, axis=1) * pscale_ref[...]

    ones = _head_ones()
    y = yf_ref[...] + yb_ref[...]
    yc = y - _headsum(y, ones) * (1.0 / HEAD)
    var = _headsum(yc * yc, ones) * (1.0 / HEAD)
    bonus = jnp.concatenate([bonus0_ref[...], bonus1_ref[...]], axis=0)
    gate = jnp.concatenate([g0_ref[...], g1_ref[...]], axis=0)
    rw = (yc * lax.rsqrt(var + GN_EPS) * gn_ref[0:1, :] + gn_ref[1:2, :] + bonus) * gate
    mix = _dot(jnp.concatenate([rw, pool], axis=1).astype(BF16), wout_ref[...])
    gt_a = mod_ref[0:1, 2 * d:3 * d]
    x_mid = x_ref[...] + _pos_tile(rowtab_ref, coltab_ref) + gt_a * mix
    o_ref[...] = x_mid
    _route(x_mid, g2n_ref, mod_ref, wr_ref, br_ref, h3_ref, eid_ref, gate_ref)


def _mix_call(x2, rowtab, coltab, yf, yb, bonus, g, p, gn, w_pool_bf, pscale, w_out_bf, mod, g2n, wr, br):
    s, d = x2.shape
    tm = TM_PROJ
    nx = s // tm
    hb = tm // POOL_HALO
    const = lambda j: (0, 0)
    tok = lambda j: (j, 0)
    seq_spec = pl.BlockSpec((tm, D_RWKV), tok)
    assert tm == 2 * TM_FEAT
    feat_lo = pl.BlockSpec((TM_FEAT, D_RWKV), lambda j: (2 * j + FEAT_SHIFT_BLOCKS, 0))
    feat_hi = pl.BlockSpec((TM_FEAT, D_RWKV), lambda j: (2 * j + FEAT_SHIFT_BLOCKS + 1, 0))
    return pl.pallas_call(
        functools.partial(_mix_body, n_x_tiles=nx, seq_len=s),
        grid=(nx,),
        in_specs=[pl.BlockSpec((tm, d), tok),
                  pl.BlockSpec((tm // GRID_W, d // 2), tok),
                  pl.BlockSpec((tm, d // 2), const),
                  seq_spec, seq_spec, feat_lo, feat_hi, feat_lo, feat_hi, seq_spec,
                  pl.BlockSpec((POOL_HALO, D_POOL), lambda j: (jnp.maximum(j * hb - 1, 0), 0)),
                  pl.BlockSpec((POOL_HALO, D_POOL), lambda j: ((j + 1) * hb, 0)),
                  pl.BlockSpec((2, D_RWKV), const),
                  pl.BlockSpec(w_pool_bf.shape, lambda j: (0, 0, 0)),
                  pl.BlockSpec((1, D_POOL), const),
                  pl.BlockSpec(w_out_bf.shape, const),
                  pl.BlockSpec(mod.shape, const),
                  pl.BlockSpec((1, d), const),
                  pl.BlockSpec(wr.shape, const),
                  pl.BlockSpec(br.shape, const)],
        out_specs=[pl.BlockSpec((tm, d), tok),
                   pl.BlockSpec((tm * SUBLANES, LANES), tok),
                   pl.BlockSpec((8, tm), lambda j: (0, j)),
                   pl.BlockSpec((8, tm), lambda j: (0, j))],
        out_shape=[jax.ShapeDtypeStruct((s, d), F32),
                   jax.ShapeDtypeStruct((s * SUBLANES, LANES), F32),
                   jax.ShapeDtypeStruct((8, s), jnp.int32),
                   jax.ShapeDtypeStruct((8, s), F32)],
        scratch_shapes=[pltpu.VMEM((tm + 2 * POOL_HALO, D_POOL), F32)],
        compiler_params=_cparams(),
        name="mix",
    )(x2, rowtab, coltab, yf, yb, bonus, bonus, g, g, p, p, p, gn, w_pool_bf, pscale, w_out_bf, mod, g2n, wr, br)


def _to_token_tiles(ref, x):
    tm = x.shape[0]
    for j in range(x.shape[1] // LANES):
        ref[pl.ds(j, tm, stride=SUBLANES), :] = x[:, j * LANES:(j + 1) * LANES]


def _from_token_tiles(ref, tm):
    return jnp.concatenate([ref[pl.ds(j, tm, stride=SUBLANES), :] for j in range(SUBLANES)], axis=1)


def _route(x, g2_ref, mod_ref, wr_ref, br_ref, h3_ref, eid_ref, gate_ref):
    tm, d = x.shape
    ms = jnp.mean(x * x, axis=-1, keepdims=True)
    xn = x * lax.rsqrt(ms + NORM_EPS) * g2_ref[...]
    h = xn * (1.0 + mod_ref[0:1, 4 * d:5 * d]) + mod_ref[0:1, 3 * d:4 * d]
    _to_token_tiles(h3_ref, h)

    wh, wl = _split2(wr_ref[...])
    hh, hl = _split2(h)
    logits = _dot_nt(wh, hh) + (_dot_nt(wh, hl) + _dot_nt(wl, hh)) + br_ref[...]
    gl = logits[0:N_GROUPS, :]
    gmax = jnp.max(gl, axis=0, keepdims=True)
    gidx = lax.broadcasted_iota(jnp.int32, gl.shape, 0).astype(F32)
    grp = jnp.min(jnp.where(gl == gmax, gidx, float(N_GROUPS)), axis=0, keepdims=True)
    p_grp = 1.0 / jnp.sum(jnp.exp(gl - gmax), axis=0, keepdims=True)
    sel = jnp.zeros((EXPERTS_PER_GROUP, tm), F32)
    for gi in range(N_GROUPS):
        lo = N_GROUPS + gi * EXPERTS_PER_GROUP
        sel = jnp.where(grp == float(gi), logits[lo:lo + EXPERTS_PER_GROUP, :], sel)
    eidx = lax.broadcasted_iota(jnp.int32, sel.shape, 0).astype(F32)
    top1 = jnp.max(sel, axis=0, keepdims=True)
    i1 = jnp.min(jnp.where(sel == top1, eidx, float(EXPERTS_PER_GROUP)), axis=0, keepdims=True)
    sel2 = jnp.where(eidx == i1, -jnp.inf, sel)
    top2 = jnp.max(sel2, axis=0, keepdims=True)
    i2 = jnp.min(jnp.where(sel2 == top2, eidx, float(EXPERTS_PER_GROUP)), axis=0, keepdims=True)
    e2 = jnp.exp(top2 - top1)
    inv = 1.0 / (1.0 + e2)
    zf = jnp.zeros((6, tm), F32)
    gate_ref[...] = jnp.concatenate([p_grp * inv, p_grp * (e2 * inv), zf], axis=0)
    base = grp * float(EXPERTS_PER_GROUP)
    eid_ref[...] = jnp.concatenate([base + i1, base + i2, zf], axis=0).astype(jnp.int32)


def _sc_move_tiles(src3, idx, scatter):
    n_out = idx.shape[0]
    n_workers = V7X_SC_CORES * V7X_SC_SUBCORES
    per_worker = n_out // n_workers
    n_chunks = per_worker // SC_GATHER_ROWS
    assert per_worker * n_workers == n_out and n_chunks * SC_GATHER_ROWS == per_worker and n_chunks % 2 == 0
    mesh = plsc.VectorSubcoreMesh(core_axis_name="c", subcore_axis_name="s",
                                  num_cores=V7X_SC_CORES, num_subcores=V7X_SC_SUBCORES)

    def body(table_hbm, idx_hbm, out_hbm, idx_v, rows0, rows1, gsem0, gsem1, wsem0, wsem1):
        worker = lax.axis_index("s") * V7X_SC_CORES + lax.axis_index("c")
        base = worker * per_worker
        pltpu.sync_copy(idx_hbm.at[worker], idx_v)

        def linear(ref, j):
            off = pl.multiple_of(j * SC_GATHER_ROWS, SC_GATHER_ROWS)
            return ref.at[pl.ds(base + off, SC_GATHER_ROWS)]

        def gather(j, buf, sem):
            src = linear(table_hbm, j) if scatter else table_hbm.at[idx_v.at[j]]
            return pltpu.make_async_copy(src, buf, sem)

        def write(j, buf, sem):
            dst = out_hbm.at[idx_v.at[j]] if scatter else linear(out_hbm, j)
            return pltpu.make_async_copy(buf, dst, sem)

        gather(0, rows0, gsem0).start()

        @pl.loop(0, n_chunks, step=2)
        def _(j):
            gather(j, rows0, gsem0).wait()

            @pl.when(j >= 2)
            def _():
                write(j - 1, rows1, wsem1).wait()

            gather(j + 1, rows1, gsem1).start()
            write(j, rows0, wsem0).start()
            gather(j + 1, rows1, gsem1).wait()
            write(j, rows0, wsem0).wait()

            @pl.when(j + 2 < n_chunks)
            def _():
                gather(j + 2, rows0, gsem0).start()

            write(j + 1, rows1, wsem1).start()

        write(n_chunks - 1, rows1, wsem1).wait()

    rows = pltpu.VMEM((SC_GATHER_ROWS,) + src3.shape[1:], src3.dtype)
    return pl.kernel(
        body,
        out_type=jax.ShapeDtypeStruct((n_out,) + src3.shape[1:], src3.dtype),
        mesh=mesh,
        scratch_types=[pltpu.VMEM((n_chunks, SC_GATHER_ROWS), jnp.int32), rows, rows,
                       pltpu.SemaphoreType.DMA, pltpu.SemaphoreType.DMA,
                       pltpu.SemaphoreType.DMA, pltpu.SemaphoreType.DMA],
        name="sc_scatter" if scatter else "sc_gather",
    )(src3, idx.reshape(n_workers, n_chunks, SC_GATHER_ROWS))


def _experts_body(blk_e_ref, n_used_ref, first_ref, next_e_ref, wslot_ref, xs_ref, wg_hbm, wu_hbm, wd_hbm, y_ref,
                  wg_buf, wu_buf, wd_buf, wsem_ref):
    i = pl.program_id(0)
    used = i < n_used_ref[0]
    slot = wslot_ref[i]

    def weight_copies(expert, to_slot):
        return [pltpu.make_async_copy(hbm.at[expert], buf.at[to_slot], wsem_ref.at[to_slot])
                for hbm, buf in ((wg_hbm, wg_buf), (wu_hbm, wu_buf), (wd_hbm, wd_buf))]

    @pl.when(jnp.logical_and(i == 0, used))
    def _():
        for cp in weight_copies(blk_e_ref[0], 0):
            cp.start()

    @pl.when(jnp.logical_and(used, first_ref[i] == 1))
    def _():
        for cp in weight_copies(blk_e_ref[i], slot):
            cp.wait()

        @pl.when(next_e_ref[i] != blk_e_ref[i])
        def _():
            for cp in weight_copies(next_e_ref[i], 1 - slot):
                cp.start()

    @pl.when(used)
    def _():
        xb = _from_token_tiles(xs_ref, MOE_BM).astype(BF16)
        gate = _dot(xb, wg_buf[slot].astype(BF16))
        upp = _dot(xb, wu_buf[slot].astype(BF16))
        hid = (gate * _sigmoid(gate)) * upp
        _to_token_tiles(y_ref, _dot(hid.astype(BF16), wd_buf[slot].astype(BF16)))

    @pl.when(jnp.logical_not(used))
    def _():
        y_ref[...] = jnp.zeros_like(y_ref)


def _experts_call(blk_expert, n_used, xs3, wg, wu, wd):
    nb = blk_expert.shape[0]
    d = wg.shape[1]
    rows = MOE_BM * SUBLANES
    first, next_e, wslot = _weight_schedule(blk_expert, n_used)
    blk = lambda i, *_: (i, 0)
    grid_spec = pltpu.PrefetchScalarGridSpec(
        num_scalar_prefetch=5,
        grid=(nb,),
        in_specs=[pl.BlockSpec((rows, LANES), blk),
                  pl.BlockSpec(memory_space=pl.ANY),
                  pl.BlockSpec(memory_space=pl.ANY),
                  pl.BlockSpec(memory_space=pl.ANY)],
        out_specs=pl.BlockSpec((rows, LANES), blk),
        scratch_shapes=[pltpu.VMEM((2, d, D_EXPERT), F32), pltpu.VMEM((2, d, D_EXPERT), F32),
                        pltpu.VMEM((2, D_EXPERT, d), F32), pltpu.SemaphoreType.DMA((2,))],
    )
    return pl.pallas_call(
        _experts_body,
        grid_spec=grid_spec,
        out_shape=jax.ShapeDtypeStruct((nb * rows, LANES), F32),
        compiler_params=_cparams(),
        name="experts",
    )(blk_expert, n_used, first, next_e, wslot, xs3, wg, wu, wd)


def _combine_body(x_ref, y0_ref, y1_ref, gate_ref, mod_ref, fg_ref, o_ref):
    tm, d = x_ref.shape
    y = gate_ref[:, 0:1] * _from_token_tiles(y0_ref, tm) + gate_ref[:, 1:2] * _from_token_tiles(y1_ref, tm)
    x = x_ref[...] + mod_ref[0:1, 5 * d:6 * d] * y
    ms = jnp.mean(x * x, axis=-1, keepdims=True)
    o_ref[...] = x * lax.rsqrt(ms + NORM_EPS) * fg_ref[...]


def _combine_call(x_mid, y3, gcol, mod, fg):
    s, d = x_mid.shape
    tm = TM_PROJ
    nt = s // tm
    const = lambda j: (0, 0)
    return pl.pallas_call(
        _combine_body,
        grid=(nt,),
        in_specs=[pl.BlockSpec((tm, d), lambda j: (j, 0)),
                  pl.BlockSpec((tm * SUBLANES, LANES), lambda j: (j, 0)),
                  pl.BlockSpec((tm * SUBLANES, LANES), lambda j: (j + nt, 0)),
                  pl.BlockSpec((tm, 2), lambda j: (j, 0)),
                  pl.BlockSpec(mod.shape, const),
                  pl.BlockSpec((1, d), const)],
        out_specs=pl.BlockSpec((tm, d), lambda j: (j, 0)),
        out_shape=jax.ShapeDtypeStruct((s, d), F32),
        compiler_params=_cparams(),
        name="combine",
    )(x_mid, y3, y3, gcol, mod, fg)


def _blockdiag2(wf, wb):
    z = jnp.zeros_like(wf)
    return jnp.concatenate([jnp.concatenate([wf, z], axis=1), jnp.concatenate([z, wb], axis=1)], axis=0)


def _routing_plan(eids, n_tok):
    m = 2 * n_tok
    experts = jnp.arange(N_EXPERTS, dtype=jnp.int32)
    flat_e = eids[:2].reshape(m)
    counts = jnp.sum((flat_e[:, None] == experts[None, :]).astype(jnp.int32), axis=0)
    pcounts = (counts + MOE_BM - 1) // MOE_BM * MOE_BM
    n_blocks = (m + N_EXPERTS * (MOE_BM - 1) + MOE_BM - 1) // MOE_BM
    n_slots = n_blocks * MOE_BM
    pad_ends = jnp.cumsum(pcounts - counts)
    pad = jnp.arange(n_slots - m, dtype=jnp.int32)
    pad_e = jnp.sum((pad[:, None] >= pad_ends[None, :]).astype(jnp.int32), axis=1)
    ids = jnp.arange(n_slots, dtype=jnp.int32)
    slot_e, slot_id = lax.sort((jnp.concatenate([flat_e, pad_e]), ids), num_keys=1, is_stable=True)
    blk_expert = jnp.minimum(slot_e[::MOE_BM], N_EXPERTS - 1)
    n_used = (jnp.sum(pcounts)[None] // MOE_BM).astype(jnp.int32)
    return slot_id, blk_expert, n_used


def _weight_schedule(blk_expert, n_used):
    used = jnp.arange(blk_expert.shape[0], dtype=jnp.int32) < n_used[0]
    first = jnp.logical_and(used, jnp.concatenate([jnp.ones((1,), bool), blk_expert[1:] != blk_expert[:-1]]))
    none = jnp.int32(N_EXPERTS)
    later_first = lax.cummin(jnp.where(first, blk_expert, none), axis=0, reverse=True)
    next_e = jnp.concatenate([later_first[1:], none[None]])
    next_e = jnp.where(next_e == none, blk_expert, next_e)
    wslot = jnp.maximum(jnp.cumsum(first.astype(jnp.int32)) - 1, 0) & 1
    return first.astype(jnp.int32), next_e, wslot


def kernel(x, c, ctx, c_ctx, norm1_g, w_mod, b_mod, w_in, mu_shift, w0_f, w2_f, w0_b, w2_b, a0_f, a2_f, a0_b, a2_b, g2, k_k, k_a, r_k_f, r_k_b, gn_w, gn_b, w_pool, pool_scale, w_out, norm2_g, w_router_grp, b_router_grp, w_router_exp, b_router_exp, w_gate, w_up, w_down, final_g):
    b, s, d = x.shape
    n_ctx = ctx.shape[1]
    assert b == 1 and c.shape[0] == 1 and w_mod.shape[0] == 1
    assert s % TM_PROJ == 0 and n_ctx % CHUNK == 0 and n_ctx == TM_FEAT and d == 2 * D_RWKV
    x2 = x[0]
    rows = s // GRID_W

    quarter = d // 4
    freq = POS_THETA ** (-jnp.arange(quarter, dtype=F32) / quarter)
    rarg = jnp.arange(rows, dtype=F32)[:, None] * freq
    carg = jnp.arange(GRID_W, dtype=F32)[:, None] * freq
    rowtab = jnp.concatenate([jnp.sin(rarg), jnp.cos(rarg)], axis=-1)
    coltab = jnp.tile(jnp.concatenate([jnp.sin(carg), jnp.cos(carg)], axis=-1), (TM_PROJ // GRID_W, 1))

    cs = jnp.concatenate([c, c_ctx[None, :], jnp.zeros((6, d), F32)], axis=0)
    mod = _mod_call(cs, w_mod[0], b_mod)

    ctx_pad = jnp.concatenate([ctx[0], jnp.zeros((TM_PROJ - n_ctx, d), F32)], axis=0)
    par = jnp.concatenate([w0_f, w0_b, a0_f, a0_b, k_k, k_a, r_k_f, r_k_b], axis=0)
    outs = _front_call(x2, ctx_pad, rowtab, coltab, norm1_g, mod, w_in[0].astype(BF16), mu_shift, par,
                       _blockdiag2(w2_f[0], w2_b[0]).astype(BF16), _blockdiag2(a2_f[0], a2_b[0]).astype(BF16),
                       g2[0].astype(BF16))
    p, feats, gamf, gamb, bonus, g = outs[0], outs[1:10], outs[10], outs[11], outs[12], outs[13]
    yf, yb = _scan_call(feats, gamf.reshape(-1, 1, D_RWKV), gamb.reshape(-1, 1, D_RWKV),
                        s // CHUNK, n_ctx // CHUNK)

    gn = jnp.concatenate([gn_w, gn_b], axis=0)
    wr = jnp.concatenate([w_router_grp[0].T, w_router_exp[0].T,
                          jnp.zeros((ROUTER_ROWS - N_GROUPS - N_EXPERTS, d), F32)], axis=0)
    br = jnp.concatenate([b_router_grp[0], b_router_exp[0],
                          jnp.zeros((ROUTER_ROWS - N_GROUPS - N_EXPERTS,), F32)])[:, None]
    x_mid, h3, eids, gates = _mix_call(x2, rowtab, coltab, yf, yb, bonus, g, p, gn, w_pool[0].astype(BF16),
                                       pool_scale, w_out[0].astype(BF16), mod, norm2_g, wr, br)

    slot_id, blk_expert, n_used = _routing_plan(eids, s)
    tile = (SUBLANES, LANES)
    xs3 = _sc_move_tiles(h3.reshape((s,) + tile), slot_id % s, scatter=False)
    yexp = _experts_call(blk_expert, n_used, xs3.reshape(-1, LANES), w_gate[0], w_up[0], w_down[0])
    y3 = _sc_move_tiles(yexp.reshape((-1,) + tile), slot_id, scatter=True).reshape(-1, LANES)
    out = _combine_call(x_mid, y3, gates[:2].T, mod, final_g[None, :])
    return out[None]
```

```python
import functools

import jax
import jax.numpy as jnp
from jax import lax
from jax.experimental import pallas as pl
from jax.experimental.pallas import tpu as pltpu
from jax.experimental.pallas import tpu_sc as plsc

F32 = jnp.float32
BF16 = jnp.bfloat16

GRID_W = 64
SUBLANES, LANES = 8, 128
HEAD = 64
N_HEADS = 8
D_RWKV = HEAD * N_HEADS
D_POOL = 512
POOL_WINDOWS = (2, 4, 8, 16)
POOL_GW = 128
POOL_HALO = 8
D_SHIFT = 1920
N_GROUPS = 4
EXPERTS_PER_GROUP = 8
N_EXPERTS = 32
D_EXPERT = 512
NORM_EPS = 1e-6
GN_EPS = 64e-5
POS_THETA = 10000.0
DECAY_SCALE = 0.6065306597

CHUNK = 64
QUAD = 4 * HEAD
SCAN_CPS = 4
TM_PROJ = 512
TM_FEAT = 256
FEAT_SHIFT_BLOCKS = 3
MOE_BM = 256
ROUTER_ROWS = 48
V7X_VMEM_LIMIT = 56 * 1024 * 1024
V7X_SC_CORES, V7X_SC_SUBCORES = 2, 16
SC_GATHER_ROWS = 32


def _cparams(n_axes=1, vmem=V7X_VMEM_LIMIT):
    return pltpu.CompilerParams(dimension_semantics=("arbitrary",) * n_axes,
                                vmem_limit_bytes=vmem)


def _dot(a, b):
    return jnp.dot(a, b, preferred_element_type=F32)


def _dot_nt(a, b):
    return lax.dot_general(a, b, (((1,), (1,)), ((), ())), preferred_element_type=F32)


def _dot_tn(a, b):
    return lax.dot_general(a, b, (((0,), (0,)), ((), ())), preferred_element_type=F32)


def _split2(x):
    hi = x.astype(BF16)
    lo = (x - hi.astype(F32)).astype(BF16)
    return hi, lo


def _dot3(a, b):
    ah, al = _split2(a)
    bh, bl = _split2(b)
    return _dot(ah, bh) + (_dot(ah, bl) + _dot(al, bh))


def _sigmoid(x):
    return 0.5 * jnp.tanh(0.5 * x) + 0.5


def _head_ones():
    r = lax.broadcasted_iota(jnp.int32, (D_RWKV, D_RWKV), 0) >> 6
    c = lax.broadcasted_iota(jnp.int32, (D_RWKV, D_RWKV), 1) >> 6
    return (r == c).astype(BF16)


def _headsum(x, ones):
    hi, lo = _split2(x)
    both = _dot(jnp.concatenate([hi, lo], axis=0), ones)
    return both[:x.shape[0]] + both[x.shape[0]:]


def _mod_body(cs_ref, w_ref, b_ref, o_ref):
    a = cs_ref[...]
    a = a * _sigmoid(a)
    o_ref[...] = _dot3(a, w_ref[...]) + b_ref[...]


def _mod_call(cs, w_mod, b_mod):
    d, n = w_mod.shape
    tn = 512
    return pl.pallas_call(
        _mod_body,
        grid=(n // tn,),
        in_specs=[pl.BlockSpec((8, d), lambda j: (0, 0)),
                  pl.BlockSpec((d, tn), lambda j: (0, j)),
                  pl.BlockSpec((1, tn), lambda j: (0, j))],
        out_specs=pl.BlockSpec((8, tn), lambda j: (0, j)),
        out_shape=jax.ShapeDtypeStruct((8, n), F32),
        compiler_params=_cparams(),
        name="mod",
    )(cs, w_mod, b_mod)


def _pos_tile(rowtab_ref, coltab_ref):
    rt = rowtab_ref[...]
    rowpart = jnp.concatenate(
        [jnp.broadcast_to(rt[r:r + 1, :], (GRID_W, rt.shape[1])) for r in range(TM_PROJ // GRID_W)], axis=0)
    return jnp.concatenate([rowpart, coltab_ref[...]], axis=1)


def _features(u, prow, nrow, mu, par_ref, w2_ref, a2_ref, g2_ref, rows, chunk0, outs, consts):
    (af_ref, rf_ref, btf_ref, ktf_ref, ab_ref, rb_ref, btb_ref, ktb_ref, v_ref,
     gamf_ref, gamb_ref, bonus_ref, g_ref) = outs
    tm = u.shape[0]
    rid = lax.broadcasted_iota(jnp.int32, (SUBLANES, 1), 0)

    def shifted(lo, hi):
        uc, m = u[:, lo:hi], mu[:, lo:hi]
        up = pltpu.roll(uc, 1, 0)
        up = jnp.concatenate([jnp.where(rid == 0, prow[:, lo:hi], up[:SUBLANES]), up[SUBLANES:]], axis=0)
        dn = pltpu.roll(uc, tm - 1, 0)
        dn = jnp.concatenate(
            [dn[:-SUBLANES], jnp.where(rid == SUBLANES - 1, nrow[:, lo:hi], dn[-SUBLANES:])], axis=0)
        return (1.0 - m) * uc + (0.5 * m) * (up + dn)

    w0f, w0b, a0f, a0b = par_ref[0:1, :], par_ref[1:2, :], par_ref[2:3, :], par_ref[3:4, :]
    k_k, k_a, rkf, rkb = par_ref[4:5, :], par_ref[5:6, :], par_ref[6:7, :], par_ref[7:8, :]

    r = shifted(0, 512)
    yield
    k = shifted(512, 1024)
    yield
    v = shifted(1024, 1536)
    v_ref[rows, :] = v.astype(BF16)
    yield
    lora = shifted(1536, D_SHIFT)
    zw = _dot(jnp.tanh(lora[:, 0:128]).astype(BF16), w2_ref[...])
    za = _dot(lora[:, 128:256].astype(BF16), a2_ref[...])
    g_ref[rows, :] = _dot(_sigmoid(lora[:, 256:384]).astype(BF16), g2_ref[...])
    yield
    lw_f = -DECAY_SCALE * _sigmoid(w0f + zw[:, :512])
    lw_b = -DECAY_SCALE * _sigmoid(w0b + zw[:, 512:])
    a_f = _sigmoid(a0f + za[:, :512])
    a_b = _sigmoid(a0b + za[:, 512:])
    yield

    ones, earlier, later = consts
    kkr = k * k_k
    kk = kkr * lax.rsqrt(jnp.maximum(_headsum(kkr * kkr, ones), 1e-24))
    yield
    k_f = k * (1.0 + (a_f - 1.0) * k_a)
    k_b = k * (1.0 + (a_b - 1.0) * k_a)
    bonus_ref[rows, :] = _headsum(r * (k_f * rkf + k_b * rkb), ones) * v
    yield

    n_ch = tm // CHUNK

    def visited_before(m, lwd):
        l1, l2 = _split2(lwd)
        return _dot(m, l1) + _dot(m, l2)

    def emit(excl, lwd, a_d, k_d, last_row, a_ref, r_ref, bt_ref, kt_ref, gam_ref):
        cum = excl + lwd
        gam_ref[0, chunk0:chunk0 + n_ch, :] = jnp.exp(jnp.concatenate(
            [cum[c * CHUNK + last_row:c * CHUNK + last_row + 1, :] for c in range(n_ch)], axis=0))
        a_ref[rows, :] = (-kk * jnp.exp(excl)).astype(BF16)
        r_ref[rows, :] = (r * jnp.exp(cum)).astype(BF16)
        yield
        e_neg = jnp.exp(-cum)
        bt_ref[rows, :] = (kk * a_d * e_neg).astype(BF16)
        kt_ref[rows, :] = (k_d * e_neg).astype(BF16)
        yield

    excl_f = visited_before(earlier, lw_f)
    yield
    yield from emit(excl_f, lw_f, a_f, k_f, CHUNK - 1, af_ref, rf_ref, btf_ref, ktf_ref, gamf_ref)
    excl_b = visited_before(later, lw_b)
    yield
    yield from emit(excl_b, lw_b, a_b, k_b, 0, ab_ref, rb_ref, btb_ref, ktb_ref, gamb_ref)


def _front_body(x_ref, ctx_ref, rowtab_ref, coltab_ref, g1_ref, mod_ref, w_ref, mu_ref, par_ref, w2_ref, a2_ref,
                g2_ref, p_ref, af_ref, rf_ref, btf_ref, ktf_ref, ab_ref, rb_ref, btb_ref, ktb_ref, v_ref,
                gamf_ref, gamb_ref, bonus_ref, g_ref, u_ref, unew_ref, carry_ref, *, n_x_tiles):
    i = pl.program_id(0)
    d = x_ref.shape[1]
    d_in = w_ref.shape[1]
    half = TM_FEAT
    last = carry_ref.shape[0] - 1

    @pl.when(i == 0)
    def _():
        u_ref[...] = jnp.zeros_like(u_ref)
        carry_ref[...] = jnp.zeros_like(carry_ref)

    is_ctx = i >= n_x_tiles
    xin = jnp.where(is_ctx, ctx_ref[...], x_ref[...] + _pos_tile(rowtab_ref, coltab_ref))
    ms = jnp.mean(xin * xin, axis=-1, keepdims=True)
    xn = xin * lax.rsqrt(ms + NORM_EPS) * g1_ref[...]
    sh = jnp.where(is_ctx, mod_ref[1:2, 0:d], mod_ref[0:1, 0:d])
    sc = jnp.where(is_ctx, mod_ref[1:2, d:2 * d], mod_ref[0:1, d:2 * d])
    hb = (xn * (1.0 + sc) + sh).astype(BF16)
    pending = [(slice(r0, r0 + half), lo, min(lo + QUAD, d_in))
               for lo in range(0, d_in, QUAD) for r0 in range(0, hb.shape[0], half)]

    def project_next():
        if not pending:
            return
        rs, lo, hi = pending.pop(0)
        uc = _dot(hb[rs, :], w_ref[:, lo:hi])
        if lo < D_SHIFT:
            unew_ref[rs, lo:min(hi, D_SHIFT)] = uc[:, :min(hi, D_SHIFT) - lo]
        if hi > D_SHIFT:
            p_ref[rs, max(lo, D_SHIFT) - D_SHIFT:hi - D_SHIFT] = uc[:, max(lo, D_SHIFT) - lo:]

    outs = (af_ref, rf_ref, btf_ref, ktf_ref, ab_ref, rb_ref, btb_ref, ktb_ref, v_ref,
            gamf_ref, gamb_ref, bonus_ref, g_ref)
    mu = mu_ref[...]
    ctx_block = 2 * n_x_tiles
    blocks = ((2 * i - 3, carry_ref[SUBLANES:, :], carry_ref[SUBLANES - 1:SUBLANES, :], u_ref[0:1, :]),
              (2 * i - 2, u_ref[0:half, :], carry_ref[last:last + 1, :], u_ref[half:half + 1, :]))
    rr = lax.broadcasted_iota(jnp.int32, (half, half), 0)
    cc = lax.broadcasted_iota(jnp.int32, (half, half), 1)
    same = (rr >> 6) == (cc >> 6)
    consts = (_head_ones(), jnp.logical_and(same, cc < rr).astype(BF16),
              jnp.logical_and(same, cc > rr).astype(BF16))
    project_next()
    for hi, (blk, ub, prow, nrow) in enumerate(blocks):
        prev_ok = jnp.logical_and(blk != 0, blk != ctx_block)
        next_ok = jnp.logical_and(blk != ctx_block - 1, blk != ctx_block)
        for _ in _features(ub, jnp.where(prev_ok, prow, 0.0), jnp.where(next_ok, nrow, 0.0), mu, par_ref, w2_ref,
                           a2_ref, g2_ref, slice(hi * half, (hi + 1) * half), hi * (half // CHUNK), outs, consts):
            project_next()
    while pending:
        project_next()
    carry_ref[...] = u_ref[half - SUBLANES:, :]
    u_ref[...] = unew_ref[...]


def _front_call(x2, ctx_pad, rowtab, coltab, g1, mod, w_in_bf, mu, par, w2, a2, g2):
    s, d = x2.shape
    tm = TM_PROJ
    nx = s // tm
    d_in = w_in_bf.shape[1]
    n_steps = nx + 2
    xmap = lambda i: (jnp.minimum(i, nx - 1), 0)
    const = lambda i: (0, 0)
    step = lambda i: (i, 0)
    seq_bf = jax.ShapeDtypeStruct((n_steps * tm, D_RWKV), BF16)
    seq_f32 = jax.ShapeDtypeStruct((n_steps * tm, D_RWKV), F32)
    gam = jax.ShapeDtypeStruct((n_steps, tm // CHUNK, D_RWKV), F32)
    seq_spec = pl.BlockSpec((tm, D_RWKV), step)
    gam_spec = pl.BlockSpec((1, tm // CHUNK, D_RWKV), lambda i: (i, 0, 0))
    return pl.pallas_call(
        functools.partial(_front_body, n_x_tiles=nx),
        grid=(n_steps,),
        in_specs=[pl.BlockSpec((tm, d), xmap),
                  pl.BlockSpec((tm, d), const),
                  pl.BlockSpec((tm // GRID_W, d // 2), xmap),
                  pl.BlockSpec((tm, d // 2), const),
                  pl.BlockSpec((1, d), const),
                  pl.BlockSpec(mod.shape, const),
                  pl.BlockSpec((d, d_in), const),
                  pl.BlockSpec((1, D_SHIFT), const),
                  pl.BlockSpec((8, D_RWKV), const),
                  pl.BlockSpec(w2.shape, const),
                  pl.BlockSpec(a2.shape, const),
                  pl.BlockSpec(g2.shape, const)],
        out_specs=([pl.BlockSpec((tm, D_POOL), lambda i: (jnp.minimum(i, nx), 0))] + [seq_spec] * 9
                   + [gam_spec, gam_spec, seq_spec, seq_spec]),
        out_shape=([jax.ShapeDtypeStruct(((nx + 1) * tm, D_POOL), F32)] + [seq_bf] * 9
                   + [gam, gam, seq_f32, seq_f32]),
        scratch_shapes=[pltpu.VMEM((tm, D_SHIFT), F32), pltpu.VMEM((tm, D_SHIFT), F32),
                        pltpu.VMEM((TM_FEAT + SUBLANES, D_SHIFT), F32)],
        compiler_params=_cparams(),
        name="front",
    )(x2, ctx_pad, rowtab, coltab, g1, mod, w_in_bf, mu, par, w2, a2, g2)


def _blockdiag(x, head_of_lane):
    xb = x.astype(BF16)
    zero = jnp.zeros_like(xb)
    return jnp.concatenate([jnp.where(head_of_lane == h, xb, zero) for h in range(QUAD // HEAD)], axis=0)


def _unblock(x, head_of_lane):
    out = jnp.where(head_of_lane == 0, x[0:HEAD, :], 0.0)
    for h in range(1, QUAD // HEAD):
        out = jnp.where(head_of_lane == h, x[h * HEAD:(h + 1) * HEAD, :], out)
    return out


def _scan_chunks(chains):
    lane = lax.broadcasted_iota(jnp.int32, (HEAD, QUAD), 1)
    head_of_lane = lane >> 6
    s_idx = lane & (HEAD - 1)
    t_idx = lax.broadcasted_iota(jnp.int32, (HEAD, QUAD), 0)
    eye = s_idx == t_idx
    lower = s_idx < t_idx
    upper = s_idx > t_idx
    bd = functools.partial(_blockdiag, head_of_lane=head_of_lane)

    def same_block(shift):
        return (s_idx >> shift) == (t_idx >> shift)

    def each(fn):
        return [fn(c) for c in chains]

    def stage_masks(c):
        ll = _dot_nt(jnp.concatenate([c["a"], c["r"]], axis=0),
                     jnp.concatenate([bd(c["bt"]), bd(c["kt"])], axis=0))
        strict = upper if c["reverse"] else lower
        incl = jnp.logical_or(strict, eye)
        c["lab"] = jnp.where(strict, ll[:HEAD, :QUAD], 0.0)
        c["lak"] = jnp.where(strict, ll[:HEAD, QUAD:], 0.0)
        c["mrb"] = jnp.where(incl, ll[HEAD:, :QUAD], 0.0)
        c["mrk"] = jnp.where(incl, ll[HEAD:, QUAD:], 0.0)

    each(stage_masks)

    def stage_square(c):
        ld = jnp.where(same_block(4), c["lab"], 0.0)
        c["tm"] = jnp.where(eye, 1.0, 0.0) + ld
        c["xk"] = _dot(ld.astype(BF16), bd(ld))

    def stage_double(c):
        y = _dot(jnp.concatenate([c["tm"], c["xk"]], axis=0).astype(BF16), bd(c["xk"]))
        c["tm"] = c["tm"] + y[:HEAD]
        c["xk"] = y[HEAD:]

    def stage_last_power(c):
        c["tm"] = c["tm"] + _dot(c["tm"].astype(BF16), bd(c["xk"]))

    each(stage_square)
    each(stage_double)
    each(stage_double)
    each(stage_last_power)
    each(lambda c: c.update(vv=_dot(jnp.concatenate([c["lak"], c["mrk"]], axis=0).astype(BF16), bd(c["v"]))))
    for shift in (4, 5):
        joined = jnp.logical_and(same_block(shift + 1), jnp.logical_not(same_block(shift)))
        each(lambda c: c.update(te=_dot(c["tm"].astype(BF16), bd(jnp.where(joined, c["lab"], 0.0)))))
        each(lambda c: c.update(tm=c["tm"] + _dot(c["te"].astype(BF16), bd(c["tm"]))))

    each(lambda c: c.update(x1=_dot(c["tm"].astype(BF16),
                                    jnp.concatenate([bd(c["a"]), bd(c["vv"][:HEAD])], axis=1))))
    each(lambda c: c.update(x2=_dot(c["mrb"].astype(BF16),
                                    jnp.concatenate([bd(c["x1"][:, :QUAD]), bd(c["x1"][:, QUAD:])], axis=1))))

    def stage_state_terms(c):
        rhs = jnp.concatenate(
            [c["x1"].astype(BF16), jnp.concatenate([jnp.zeros((HEAD, QUAD), BF16), c["v"]], axis=1)], axis=0)
        hat = (jnp.concatenate([c["bt"], c["kt"]], axis=0).astype(F32) * c["gam"]).astype(BF16)
        pz = _dot_tn(hat, rhs)
        c["pq"] = _unblock(pz[:, :QUAD], head_of_lane) + jnp.where(eye, c["gam"], 0.0)
        c["zq"] = _unblock(pz[:, QUAD:], head_of_lane)
        c["qeff"] = c["r"].astype(F32) + c["x2"][:, :QUAD]
        c["yin"] = c["x2"][:, QUAD:] + c["vv"][HEAD:]

    each(stage_state_terms)

    outs, state = [], {}
    for c in chains:
        hst = state.get(c["key"], c["hst"])
        qp = _dot(jnp.concatenate([c["qeff"], c["pq"]], axis=0).astype(BF16), bd(hst))
        state[c["key"]] = qp[HEAD:] + c["zq"]
        outs.append(qp[:HEAD] + c["yin"])
    return outs, state


def _scan_body(af, rf, btf, ktf, vf, gamf, ab, rb, btb, ktb, vb, gamb, yf_ref, yb_ref, hf_ref, hb_ref):
    @pl.when(pl.program_id(0) == 0)
    def _():
        hf_ref[...] = jnp.zeros_like(hf_ref)
        hb_ref[...] = jnp.zeros_like(hb_ref)

    names = ("a", "r", "bt", "kt", "v")
    chains, sinks, state_refs = [], [], {}
    for ci in range(SCAN_CPS):
        for refs, y_ref, h_ref, reverse in (((af, rf, btf, ktf, vf, gamf), yf_ref, hf_ref, False),
                                            ((ab, rb, btb, ktb, vb, gamb), yb_ref, hb_ref, True)):
            chunk = SCAN_CPS - 1 - ci if reverse else ci
            rows = slice(chunk * CHUNK, (chunk + 1) * CHUNK)
            for q in range(D_RWKV // QUAD):
                sl = slice(q * QUAD, (q + 1) * QUAD)
                chain = {n: ref[rows, sl] for n, ref in zip(names, refs[:5])}
                chain.update(gam=refs[5][chunk, :, sl], hst=h_ref[:, sl], reverse=reverse, key=(reverse, q))
                chains.append(chain)
                sinks.append((y_ref, rows, sl))
                state_refs[(reverse, q)] = (h_ref, sl)
    outs, state = _scan_chunks(chains)
    for y, (y_ref, rows, sl) in zip(outs, sinks):
        y_ref[rows, sl] = y
    for key, (h_ref, sl) in state_refs.items():
        h_ref[:, sl] = state[key]


def _scan_call(feats, gamf, gamb, n_x_chunks, n_c_chunks):
    af, rf, btf, ktf, ab, rb, btb, ktb, v = feats
    assert n_x_chunks % SCAN_CPS == 0 and n_c_chunks % SCAN_CPS == 0
    nxb, ncb = n_x_chunks // SCAN_CPS, n_c_chunks // SCAN_CPS
    n = nxb + ncb
    assert SCAN_CPS * CHUNK == TM_FEAT
    fwd_blk = lambda i: jnp.where(i < ncb, nxb + i, i - ncb)
    bwd_blk = lambda i: n - 1 - i
    fwd_y = lambda i: (fwd_blk(i), 0)
    bwd_y = lambda i: (bwd_blk(i), 0)
    fwd = lambda i: (fwd_blk(i) + FEAT_SHIFT_BLOCKS, 0)
    bwd = lambda i: (bwd_blk(i) + FEAT_SHIFT_BLOCKS, 0)
    fwd3 = lambda i: (fwd_blk(i) + FEAT_SHIFT_BLOCKS, 0, 0)
    bwd3 = lambda i: (bwd_blk(i) + FEAT_SHIFT_BLOCKS, 0, 0)
    blk = (SCAN_CPS * CHUNK, D_RWKV)
    in_specs = ([pl.BlockSpec(blk, fwd)] * 5 + [pl.BlockSpec((SCAN_CPS, 1, D_RWKV), fwd3)]
                + [pl.BlockSpec(blk, bwd)] * 5 + [pl.BlockSpec((SCAN_CPS, 1, D_RWKV), bwd3)])
    y = jax.ShapeDtypeStruct((n * SCAN_CPS * CHUNK, D_RWKV), F32)
    return pl.pallas_call(
        _scan_body,
        grid=(n,),
        in_specs=in_specs,
        out_specs=[pl.BlockSpec(blk, fwd_y), pl.BlockSpec(blk, bwd_y)],
        out_shape=[y, y],
        scratch_shapes=[pltpu.VMEM((HEAD, D_RWKV), F32), pltpu.VMEM((HEAD, D_RWKV), F32)],
        compiler_params=_cparams(),
        name="scan",
    )(af, rf, btf, ktf, v, gamf, ab, rb, btb, ktb, v, gamb)


def _mix_body(x_ref, rowtab_ref, coltab_ref, yf_ref, yb_ref, bonus0_ref, bonus1_ref, g0_ref, g1_ref,
              pm_ref, pp_ref, pn_ref, gn_ref, wpool_ref, pscale_ref, wout_ref, mod_ref, g2n_ref, wr_ref, br_ref,
              o_ref, hp_ref, eid_ref, gate_ref, slab_ref, *, n_x_tiles, seq_len):
    j = pl.program_id(0)
    tm, d = x_ref.shape
    hl = POOL_HALO
    slab_ref[0:hl, :] = jnp.where(j != 0, pp_ref[...], 0.0)
    slab_ref[hl:hl + tm, :] = pm_ref[...]
    slab_ref[hl + tm:2 * hl + tm, :] = jnp.where(j != n_x_tiles - 1, pn_ref[...], 0.0)
    tglob = j * tm + lax.broadcasted_iota(jnp.int32, (tm, 1), 0)
    pooled = []
    for gi, win in enumerate(POOL_WINDOWS):
        cols = slice(gi * POOL_GW, (gi + 1) * POOL_GW)
        acc = slab_ref[hl - win // 2:hl - win // 2 + tm, cols]
        for off in range(-win // 2 + 1, win // 2):
            acc = acc + slab_ref[hl + off:hl + off + tm, cols]
        lo = jnp.maximum(tglob - win // 2, 0)
        hi = jnp.minimum(tglob - win // 2 + win, seq_len)
        cnt = (hi - lo).astype(F32)
        pg = acc / cnt - slab_ref[hl:hl + tm, cols]
        pooled.append(_dot(pg.astype(BF16), wpool_ref[gi]))
    pool = jnp.concatenate(pooled, axis=1) * pscale_ref[...]

    ones = _head_ones()
    y = yf_ref[...] + yb_ref[...]
    yc = y - _headsum(y, ones) * (1.0 / HEAD)
    var = _headsum(yc * yc, ones) * (1.0 / HEAD)
    bonus = jnp.concatenate([bonus0_ref[...], bonus1_ref[...]], axis=0)
    gate = jnp.concatenate([g0_ref[...], g1_ref[...]], axis=0)
    rw = (yc * lax.rsqrt(var + GN_EPS) * gn_ref[0:1, :] + gn_ref[1:2, :] + bonus) * gate
    mix = _dot(jnp.concatenate([rw, pool], axis=1).astype(BF16), wout_ref[...])
    gt_a = mod_ref[0:1, 2 * d:3 * d]
    x_mid = x_ref[...] + _pos_tile(rowtab_ref, coltab_ref) + gt_a * mix
    o_ref[...] = x_mid
    _route(x_mid, g2n_ref, mod_ref, wr_ref, br_ref, hp_ref, eid_ref, gate_ref)


def _mix_call(x2, rowtab, coltab, yf, yb, bonus, g, p, gn, w_pool_bf, pscale, w_out_bf, mod, g2n, wr, br):
    s, d = x2.shape
    tm = TM_PROJ
    nx = s // tm
    hb = tm // POOL_HALO
    const = lambda j: (0, 0)
    tok = lambda j: (j, 0)
    seq_spec = pl.BlockSpec((tm, D_RWKV), tok)
    assert tm == 2 * TM_FEAT
    feat_lo = pl.BlockSpec((TM_FEAT, D_RWKV), lambda j: (2 * j + FEAT_SHIFT_BLOCKS, 0))
    feat_hi = pl.BlockSpec((TM_FEAT, D_RWKV), lambda j: (2 * j + FEAT_SHIFT_BLOCKS + 1, 0))
    return pl.pallas_call(
        functools.partial(_mix_body, n_x_tiles=nx, seq_len=s),
        grid=(nx,),
        in_specs=[pl.BlockSpec((tm, d), tok),
                  pl.BlockSpec((tm // GRID_W, d // 2), tok),
                  pl.BlockSpec((tm, d // 2), const),
                  seq_spec, seq_spec, feat_lo, feat_hi, feat_lo, feat_hi, seq_spec,
                  pl.BlockSpec((POOL_HALO, D_POOL), lambda j: (jnp.maximum(j * hb - 1, 0), 0)),
                  pl.BlockSpec((POOL_HALO, D_POOL), lambda j: ((j + 1) * hb, 0)),
                  pl.BlockSpec((2, D_RWKV), const),
                  pl.BlockSpec(w_pool_bf.shape, lambda j: (0, 0, 0)),
                  pl.BlockSpec((1, D_POOL), const),
                  pl.BlockSpec(w_out_bf.shape, const),
                  pl.BlockSpec(mod.shape, const),
                  pl.BlockSpec((1, d), const),
                  pl.BlockSpec(wr.shape, const),
                  pl.BlockSpec(br.shape, const)],
        out_specs=[pl.BlockSpec((tm, d), tok),
                   pl.BlockSpec((tm, d // 2), tok),
                   pl.BlockSpec((8, tm), lambda j: (0, j)),
                   pl.BlockSpec((8, tm), lambda j: (0, j))],
        out_shape=[jax.ShapeDtypeStruct((s, d), F32),
                   jax.ShapeDtypeStruct((s, d // 2), jnp.uint32),
                   jax.ShapeDtypeStruct((8, s), jnp.int32),
                   jax.ShapeDtypeStruct((8, s), F32)],
        scratch_shapes=[pltpu.VMEM((tm + 2 * POOL_HALO, D_POOL), F32)],
        compiler_params=_cparams(),
        name="mix",
    )(x2, rowtab, coltab, yf, yb, bonus, bonus, g, g, p, p, p, gn, w_pool_bf, pscale, w_out_bf, mod, g2n, wr, br)


def _to_token_tiles(ref, x):
    tm = x.shape[0]
    for j in range(x.shape[1] // LANES):
        ref[pl.ds(j, tm, stride=SUBLANES), :] = x[:, j * LANES:(j + 1) * LANES]


def _from_token_tiles(ref, tm):
    return jnp.concatenate([ref[pl.ds(j, tm, stride=SUBLANES), :] for j in range(SUBLANES)], axis=1)


def _pack_bf16_halves(x):
    half = x.shape[1] // 2
    bits = lax.bitcast_convert_type(x.astype(F32), jnp.uint32)
    return (bits[:, :half] >> 16) | bits[:, half:]


def _unpack_bf16_halves(words):
    lo = lax.bitcast_convert_type(words << 16, F32)
    hi = lax.bitcast_convert_type(words & jnp.uint32(0xFFFF0000), F32)
    return jnp.concatenate([lo, hi], axis=1).astype(BF16)


def _route(x, g2_ref, mod_ref, wr_ref, br_ref, hp_ref, eid_ref, gate_ref):
    tm, d = x.shape
    ms = jnp.mean(x * x, axis=-1, keepdims=True)
    xn = x * lax.rsqrt(ms + NORM_EPS) * g2_ref[...]
    h = xn * (1.0 + mod_ref[0:1, 4 * d:5 * d]) + mod_ref[0:1, 3 * d:4 * d]

    wh, wl = _split2(wr_ref[...])
    hh, hl = _split2(h)
    hp_ref[...] = _pack_bf16_halves(hh)
    logits = _dot_nt(wh, hh) + (_dot_nt(wh, hl) + _dot_nt(wl, hh)) + br_ref[...]
    gl = logits[0:N_GROUPS, :]
    gmax = jnp.max(gl, axis=0, keepdims=True)
    gidx = lax.broadcasted_iota(jnp.int32, gl.shape, 0).astype(F32)
    grp = jnp.min(jnp.where(gl == gmax, gidx, float(N_GROUPS)), axis=0, keepdims=True)
    p_grp = 1.0 / jnp.sum(jnp.exp(gl - gmax), axis=0, keepdims=True)
    sel = jnp.zeros((EXPERTS_PER_GROUP, tm), F32)
    for gi in range(N_GROUPS):
        lo = N_GROUPS + gi * EXPERTS_PER_GROUP
        sel = jnp.where(grp == float(gi), logits[lo:lo + EXPERTS_PER_GROUP, :], sel)
    eidx = lax.broadcasted_iota(jnp.int32, sel.shape, 0).astype(F32)
    top1 = jnp.max(sel, axis=0, keepdims=True)
    i1 = jnp.min(jnp.where(sel == top1, eidx, float(EXPERTS_PER_GROUP)), axis=0, keepdims=True)
    sel2 = jnp.where(eidx == i1, -jnp.inf, sel)
    top2 = jnp.max(sel2, axis=0, keepdims=True)
    i2 = jnp.min(jnp.where(sel2 == top2, eidx, float(EXPERTS_PER_GROUP)), axis=0, keepdims=True)
    e2 = jnp.exp(top2 - top1)
    inv = 1.0 / (1.0 + e2)
    zf = jnp.zeros((6, tm), F32)
    gate_ref[...] = jnp.concatenate([p_grp * inv, p_grp * (e2 * inv), zf], axis=0)
    base = grp * float(EXPERTS_PER_GROUP)
    eid_ref[...] = jnp.concatenate([base + i1, base + i2, zf], axis=0).astype(jnp.int32)


def _sc_gather_tiles(table3, idx):
    n_out = idx.shape[0]
    n_workers = V7X_SC_CORES * V7X_SC_SUBCORES
    per_worker = n_out // n_workers
    n_chunks = per_worker // SC_GATHER_ROWS
    assert per_worker * n_workers == n_out and n_chunks * SC_GATHER_ROWS == per_worker and n_chunks % 2 == 0
    mesh = plsc.VectorSubcoreMesh(core_axis_name="c", subcore_axis_name="s",
                                  num_cores=V7X_SC_CORES, num_subcores=V7X_SC_SUBCORES)

    def body(table_hbm, idx_hbm, out_hbm, idx_v, rows0, rows1, gsem0, gsem1, wsem0, wsem1):
        worker = lax.axis_index("s") * V7X_SC_CORES + lax.axis_index("c")
        base = worker * per_worker
        pltpu.sync_copy(idx_hbm.at[pl.ds(base, per_worker)], idx_v)

        def gather(j, buf, sem):
            off = pl.multiple_of(j * SC_GATHER_ROWS, SC_GATHER_ROWS)
            return pltpu.make_async_copy(table_hbm.at[idx_v.at[pl.ds(off, SC_GATHER_ROWS)]], buf, sem)

        def write(j, buf, sem):
            off = pl.multiple_of(j * SC_GATHER_ROWS, SC_GATHER_ROWS)
            return pltpu.make_async_copy(buf, out_hbm.at[pl.ds(base + off, SC_GATHER_ROWS)], sem)

        gather(0, rows0, gsem0).start()

        @pl.loop(0, n_chunks, step=2)
        def _(j):
            gather(j, rows0, gsem0).wait()

            @pl.when(j >= 2)
            def _():
                write(j - 1, rows1, wsem1).wait()

            gather(j + 1, rows1, gsem1).start()
            write(j, rows0, wsem0).start()
            gather(j + 1, rows1, gsem1).wait()
            write(j, rows0, wsem0).wait()

            @pl.when(j + 2 < n_chunks)
            def _():
                gather(j + 2, rows0, gsem0).start()

            write(j + 1, rows1, wsem1).start()

        write(n_chunks - 1, rows1, wsem1).wait()

    rows = pltpu.VMEM((SC_GATHER_ROWS,) + table3.shape[1:], table3.dtype)
    return pl.kernel(
        body,
        out_type=jax.ShapeDtypeStruct((n_out,) + table3.shape[1:], table3.dtype),
        mesh=mesh,
        scratch_types=[pltpu.VMEM((per_worker,), jnp.int32), rows, rows,
                       pltpu.SemaphoreType.DMA, pltpu.SemaphoreType.DMA,
                       pltpu.SemaphoreType.DMA, pltpu.SemaphoreType.DMA],
        name="sc_gather",
    )(table3, idx)


def _experts_body(blk_e_ref, n_used_ref, first_ref, next_e_ref, wslot_ref, xs_ref, wg_hbm, wu_hbm, wd_hbm, y_ref,
                  wg_buf, wu_buf, wd_buf, wsem_ref):
    i = pl.program_id(0)
    used = i < n_used_ref[0]
    slot = wslot_ref[i]

    def weight_copies(expert, to_slot):
        return [pltpu.make_async_copy(hbm.at[expert], buf.at[to_slot], wsem_ref.at[to_slot])
                for hbm, buf in ((wg_hbm, wg_buf), (wu_hbm, wu_buf), (wd_hbm, wd_buf))]

    @pl.when(jnp.logical_and(i == 0, used))
    def _():
        for cp in weight_copies(blk_e_ref[0], 0):
            cp.start()

    @pl.when(jnp.logical_and(used, first_ref[i] == 1))
    def _():
        for cp in weight_copies(blk_e_ref[i], slot):
            cp.wait()

        @pl.when(next_e_ref[i] != blk_e_ref[i])
        def _():
            for cp in weight_copies(next_e_ref[i], 1 - slot):
                cp.start()

    @pl.when(used)
    def _():
        xb = _unpack_bf16_halves(xs_ref[...])
        gate = _dot(xb, wg_buf[slot].astype(BF16))
        upp = _dot(xb, wu_buf[slot].astype(BF16))
        hid = (gate * _sigmoid(gate)) * upp
        _to_token_tiles(y_ref, _dot(hid.astype(BF16), wd_buf[slot].astype(BF16)))

    @pl.when(jnp.logical_not(used))
    def _():
        y_ref[...] = jnp.zeros_like(y_ref)


def _experts_call(blk_expert, n_used, xs3, wg, wu, wd):
    nb = blk_expert.shape[0]
    d = wg.shape[1]
    rows = MOE_BM * SUBLANES
    first, next_e, wslot = _weight_schedule(blk_expert, n_used)
    blk = lambda i, *_: (i, 0)
    grid_spec = pltpu.PrefetchScalarGridSpec(
        num_scalar_prefetch=5,
        grid=(nb,),
        in_specs=[pl.BlockSpec((MOE_BM, d // 2), blk),
                  pl.BlockSpec(memory_space=pl.ANY),
                  pl.BlockSpec(memory_space=pl.ANY),
                  pl.BlockSpec(memory_space=pl.ANY)],
        out_specs=pl.BlockSpec((rows, LANES), blk),
        scratch_shapes=[pltpu.VMEM((2, d, D_EXPERT), F32), pltpu.VMEM((2, d, D_EXPERT), F32),
                        pltpu.VMEM((2, D_EXPERT, d), F32), pltpu.SemaphoreType.DMA((2,))],
    )
    return pl.pallas_call(
        _experts_body,
        grid_spec=grid_spec,
        out_shape=jax.ShapeDtypeStruct((nb * rows, LANES), F32),
        compiler_params=_cparams(),
        name="experts",
    )(blk_expert, n_used, first, next_e, wslot, xs3, wg, wu, wd)


def _combine_body(x_ref, y0_ref, y1_ref, gate_ref, mod_ref, fg_ref, o_ref):
    tm, d = x_ref.shape
    y = gate_ref[:, 0:1] * _from_token_tiles(y0_ref, tm) + gate_ref[:, 1:2] * _from_token_tiles(y1_ref, tm)
    x = x_ref[...] + mod_ref[0:1, 5 * d:6 * d] * y
    ms = jnp.mean(x * x, axis=-1, keepdims=True)
    o_ref[...] = x * lax.rsqrt(ms + NORM_EPS) * fg_ref[...]


def _combine_call(x_mid, y3, gcol, mod, fg):
    s, d = x_mid.shape
    tm = TM_PROJ
    nt = s // tm
    const = lambda j: (0, 0)
    return pl.pallas_call(
        _combine_body,
        grid=(nt,),
        in_specs=[pl.BlockSpec((tm, d), lambda j: (j, 0)),
                  pl.BlockSpec((tm * SUBLANES, LANES), lambda j: (j, 0)),
                  pl.BlockSpec((tm * SUBLANES, LANES), lambda j: (j + nt, 0)),
                  pl.BlockSpec((tm, 2), lambda j: (j, 0)),
                  pl.BlockSpec(mod.shape, const),
                  pl.BlockSpec((1, d), const)],
        out_specs=pl.BlockSpec((tm, d), lambda j: (j, 0)),
        out_shape=jax.ShapeDtypeStruct((s, d), F32),
        compiler_params=_cparams(),
        name="combine",
    )(x_mid, y3, y3, gcol, mod, fg)


def _blockdiag2(wf, wb):
    z = jnp.zeros_like(wf)
    return jnp.concatenate([jnp.concatenate([wf, z], axis=1), jnp.concatenate([z, wb], axis=1)], axis=0)


def _routing_plan(eids, n_tok):
    m = 2 * n_tok
    experts = jnp.arange(N_EXPERTS, dtype=jnp.int32)
    flat_e = eids[:2].reshape(m)
    counts = jnp.sum((flat_e[:, None] == experts[None, :]).astype(jnp.int32), axis=0)
    pcounts = (counts + MOE_BM - 1) // MOE_BM * MOE_BM
    n_blocks = (m + N_EXPERTS * (MOE_BM - 1) + MOE_BM - 1) // MOE_BM
    n_slots = n_blocks * MOE_BM
    pad_ends = jnp.cumsum(pcounts - counts)
    pad = jnp.arange(n_slots - m, dtype=jnp.int32)
    pad_e = jnp.sum((pad[:, None] >= pad_ends[None, :]).astype(jnp.int32), axis=1)
    ids = jnp.arange(n_slots, dtype=jnp.int32)
    slot_e, slot_id = lax.sort((jnp.concatenate([flat_e, pad_e]), ids), num_keys=1, is_stable=True)
    src_tok = slot_id % n_tok
    _, slot_of_id = lax.sort((slot_id, ids), num_keys=1)
    dest = slot_of_id[:m]
    blk_expert = jnp.minimum(slot_e[::MOE_BM], N_EXPERTS - 1)
    n_used = (jnp.sum(pcounts)[None] // MOE_BM).astype(jnp.int32)
    return src_tok, dest, blk_expert, n_used


def _weight_schedule(blk_expert, n_used):
    used = jnp.arange(blk_expert.shape[0], dtype=jnp.int32) < n_used[0]
    first = jnp.logical_and(used, jnp.concatenate([jnp.ones((1,), bool), blk_expert[1:] != blk_expert[:-1]]))
    none = jnp.int32(N_EXPERTS)
    later_first = lax.cummin(jnp.where(first, blk_expert, none), axis=0, reverse=True)
    next_e = jnp.concatenate([later_first[1:], none[None]])
    next_e = jnp.where(next_e == none, blk_expert, next_e)
    wslot = jnp.maximum(jnp.cumsum(first.astype(jnp.int32)) - 1, 0) & 1
    return first.astype(jnp.int32), next_e, wslot


def kernel(x, c, ctx, c_ctx, norm1_g, w_mod, b_mod, w_in, mu_shift, w0_f, w2_f, w0_b, w2_b, a0_f, a2_f, a0_b, a2_b, g2, k_k, k_a, r_k_f, r_k_b, gn_w, gn_b, w_pool, pool_scale, w_out, norm2_g, w_router_grp, b_router_grp, w_router_exp, b_router_exp, w_gate, w_up, w_down, final_g):
    b, s, d = x.shape
    n_ctx = ctx.shape[1]
    assert b == 1 and c.shape[0] == 1 and w_mod.shape[0] == 1
    assert s % TM_PROJ == 0 and n_ctx % CHUNK == 0 and n_ctx == TM_FEAT and d == 2 * D_RWKV
    x2 = x[0]
    rows = s // GRID_W

    quarter = d // 4
    freq = POS_THETA ** (-jnp.arange(quarter, dtype=F32) / quarter)
    rarg = jnp.arange(rows, dtype=F32)[:, None] * freq
    carg = jnp.arange(GRID_W, dtype=F32)[:, None] * freq
    rowtab = jnp.concatenate([jnp.sin(rarg), jnp.cos(rarg)], axis=-1)
    coltab = jnp.tile(jnp.concatenate([jnp.sin(carg), jnp.cos(carg)], axis=-1), (TM_PROJ // GRID_W, 1))

    cs = jnp.concatenate([c, c_ctx[None, :], jnp.zeros((6, d), F32)], axis=0)
    mod = _mod_call(cs, w_mod[0], b_mod)

    ctx_pad = jnp.concatenate([ctx[0], jnp.zeros((TM_PROJ - n_ctx, d), F32)], axis=0)
    par = jnp.concatenate([w0_f, w0_b, a0_f, a0_b, k_k, k_a, r_k_f, r_k_b], axis=0)
    outs = _front_call(x2, ctx_pad, rowtab, coltab, norm1_g, mod, w_in[0].astype(BF16), mu_shift, par,
                       _blockdiag2(w2_f[0], w2_b[0]).astype(BF16), _blockdiag2(a2_f[0], a2_b[0]).astype(BF16),
                       g2[0].astype(BF16))
    p, feats, gamf, gamb, bonus, g = outs[0], outs[1:10], outs[10], outs[11], outs[12], outs[13]
    yf, yb = _scan_call(feats, gamf.reshape(-1, 1, D_RWKV), gamb.reshape(-1, 1, D_RWKV),
                        s // CHUNK, n_ctx // CHUNK)

    gn = jnp.concatenate([gn_w, gn_b], axis=0)
    wr = jnp.concatenate([w_router_grp[0].T, w_router_exp[0].T,
                          jnp.zeros((ROUTER_ROWS - N_GROUPS - N_EXPERTS, d), F32)], axis=0)
    br = jnp.concatenate([b_router_grp[0], b_router_exp[0],
                          jnp.zeros((ROUTER_ROWS - N_GROUPS - N_EXPERTS,), F32)])[:, None]
    x_mid, hp, eids, gates = _mix_call(x2, rowtab, coltab, yf, yb, bonus, g, p, gn, w_pool[0].astype(BF16),
                                       pool_scale, w_out[0].astype(BF16), mod, norm2_g, wr, br)

    src_tok, dest, blk_expert, n_used = _routing_plan(eids, s)
    tile = (SUBLANES, LANES)
    xs = _sc_gather_tiles(hp, src_tok)
    yexp = _experts_call(blk_expert, n_used, xs, w_gate[0], w_up[0], w_down[0])
    y3 = _sc_gather_tiles(yexp.reshape((-1,) + tile), dest).reshape(-1, LANES)
    out = _combine_call(x_mid, y3, gates[:2].T, mod, final_g[None, :])
    return out[None]
```

```python
import functools
import math

import jax
import jax.numpy as jnp
from jax import lax
from jax.experimental import pallas as pl
from jax.experimental.pallas import tpu as pltpu
from jax.experimental.pallas import tpu_sc as plsc

F32 = jnp.float32
BF16 = jnp.bfloat16

GRID_W = 64
SUBLANES, LANES = 8, 128
HEAD = 64
N_HEADS = 8
D_RWKV = HEAD * N_HEADS
D_POOL = 512
POOL_WINDOWS = (2, 4, 8, 16)
POOL_GW = 128
POOL_HALO = 8
D_SHIFT = 1920
N_GROUPS = 4
EXPERTS_PER_GROUP = 8
N_EXPERTS = 32
D_EXPERT = 512
NORM_EPS = 1e-6
GN_EPS = 64e-5
POS_THETA = 10000.0
DECAY_SCALE = 0.6065306597

CHUNK = 64
QUAD = 4 * HEAD
SCAN_CPS = 4
TM_PROJ = 512
TM_FEAT = 256
FEAT_SHIFT_BLOCKS = 3
MOE_BM = 512
ROUTER_ROWS = 48
V7X_VMEM_LIMIT = 56 * 1024 * 1024
V7X_SC_CORES, V7X_SC_SUBCORES = 2, 16
SC_GATHER_BYTES = 128 * 1024


def _cparams(n_axes=1, vmem=V7X_VMEM_LIMIT):
    return pltpu.CompilerParams(dimension_semantics=("arbitrary",) * n_axes,
                                vmem_limit_bytes=vmem)


def _dot(a, b):
    return jnp.dot(a, b, preferred_element_type=F32)


def _dot_nt(a, b):
    return lax.dot_general(a, b, (((1,), (1,)), ((), ())), preferred_element_type=F32)


def _dot_tn(a, b):
    return lax.dot_general(a, b, (((0,), (0,)), ((), ())), preferred_element_type=F32)


def _split2(x):
    hi = x.astype(BF16)
    lo = (x - hi.astype(F32)).astype(BF16)
    return hi, lo


def _dot3(a, b):
    ah, al = _split2(a)
    bh, bl = _split2(b)
    return _dot(ah, bh) + (_dot(ah, bl) + _dot(al, bh))


def _sigmoid(x):
    return 0.5 * jnp.tanh(0.5 * x) + 0.5


def _head_ones():
    r = lax.broadcasted_iota(jnp.int32, (D_RWKV, D_RWKV), 0) >> 6
    c = lax.broadcasted_iota(jnp.int32, (D_RWKV, D_RWKV), 1) >> 6
    return (r == c).astype(BF16)


def _headsum(x, ones):
    hi, lo = _split2(x)
    both = _dot(jnp.concatenate([hi, lo], axis=0), ones)
    return both[:x.shape[0]] + both[x.shape[0]:]


def _mod_body(cs_ref, w_ref, b_ref, o_ref):
    a = cs_ref[...]
    a = a * _sigmoid(a)
    o_ref[...] = _dot3(a, w_ref[...]) + b_ref[...]


def _mod_call(cs, w_mod, b_mod):
    d, n = w_mod.shape
    tn = 512
    return pl.pallas_call(
        _mod_body,
        grid=(n // tn,),
        in_specs=[pl.BlockSpec((8, d), lambda j: (0, 0)),
                  pl.BlockSpec((d, tn), lambda j: (0, j)),
                  pl.BlockSpec((1, tn), lambda j: (0, j))],
        out_specs=pl.BlockSpec((8, tn), lambda j: (0, j)),
        out_shape=jax.ShapeDtypeStruct((8, n), F32),
        compiler_params=_cparams(),
        name="mod",
    )(cs, w_mod, b_mod)


def _pos_tile(rowtab_ref, coltab_ref):
    rt = rowtab_ref[...]
    rowpart = jnp.concatenate(
        [jnp.broadcast_to(rt[r:r + 1, :], (GRID_W, rt.shape[1])) for r in range(TM_PROJ // GRID_W)], axis=0)
    return jnp.concatenate([rowpart, coltab_ref[...]], axis=1)


def _features(u, prow, nrow, mu, par_ref, w2_ref, a2_ref, g2_ref, rows, chunk0, outs, consts):
    (af_ref, rf_ref, btf_ref, ktf_ref, ab_ref, rb_ref, btb_ref, ktb_ref, v_ref,
     gamf_ref, gamb_ref, bonus_ref, g_ref) = outs
    tm = u.shape[0]
    rid = lax.broadcasted_iota(jnp.int32, (SUBLANES, 1), 0)

    def shifted(lo, hi):
        uc, m = u[:, lo:hi], mu[:, lo:hi]
        up = pltpu.roll(uc, 1, 0)
        up = jnp.concatenate([jnp.where(rid == 0, prow[:, lo:hi], up[:SUBLANES]), up[SUBLANES:]], axis=0)
        dn = pltpu.roll(uc, tm - 1, 0)
        dn = jnp.concatenate(
            [dn[:-SUBLANES], jnp.where(rid == SUBLANES - 1, nrow[:, lo:hi], dn[-SUBLANES:])], axis=0)
        return (1.0 - m) * uc + (0.5 * m) * (up + dn)

    w0f, w0b, a0f, a0b = par_ref[0:1, :], par_ref[1:2, :], par_ref[2:3, :], par_ref[3:4, :]
    k_k, k_a, rkf, rkb = par_ref[4:5, :], par_ref[5:6, :], par_ref[6:7, :], par_ref[7:8, :]

    r = shifted(0, 512)
    yield
    k = shifted(512, 1024)
    yield
    v = shifted(1024, 1536)
    v_ref[rows, :] = v.astype(BF16)
    yield
    lora = shifted(1536, D_SHIFT)
    zw = _dot(jnp.tanh(lora[:, 0:128]).astype(BF16), w2_ref[...])
    za = _dot(lora[:, 128:256].astype(BF16), a2_ref[...])
    g_ref[rows, :] = _dot(_sigmoid(lora[:, 256:384]).astype(BF16), g2_ref[...])
    yield
    lw_f = -DECAY_SCALE * _sigmoid(w0f + zw[:, :512])
    lw_b = -DECAY_SCALE * _sigmoid(w0b + zw[:, 512:])
    a_f = _sigmoid(a0f + za[:, :512])
    a_b = _sigmoid(a0b + za[:, 512:])
    yield

    ones, earlier, later = consts
    kkr = k * k_k
    kk = kkr * lax.rsqrt(jnp.maximum(_headsum(kkr * kkr, ones), 1e-24))
    yield
    k_f = k * (1.0 + (a_f - 1.0) * k_a)
    k_b = k * (1.0 + (a_b - 1.0) * k_a)
    bonus_ref[rows, :] = _headsum(r * (k_f * rkf + k_b * rkb), ones) * v
    yield

    n_ch = tm // CHUNK

    def visited_before(m, lwd):
        l1, l2 = _split2(lwd)
        return _dot(m, l1) + _dot(m, l2)

    def emit(excl, lwd, a_d, k_d, last_row, a_ref, r_ref, bt_ref, kt_ref, gam_ref):
        cum = excl + lwd
        gam_ref[0, chunk0:chunk0 + n_ch, :] = jnp.exp(jnp.concatenate(
            [cum[c * CHUNK + last_row:c * CHUNK + last_row + 1, :] for c in range(n_ch)], axis=0))
        a_ref[rows, :] = (-kk * jnp.exp(excl)).astype(BF16)
        r_ref[rows, :] = (r * jnp.exp(cum)).astype(BF16)
        yield
        e_neg = jnp.exp(-cum)
        bt_ref[rows, :] = (kk * a_d * e_neg).astype(BF16)
        kt_ref[rows, :] = (k_d * e_neg).astype(BF16)
        yield

    excl_f = visited_before(earlier, lw_f)
    yield
    yield from emit(excl_f, lw_f, a_f, k_f, CHUNK - 1, af_ref, rf_ref, btf_ref, ktf_ref, gamf_ref)
    excl_b = visited_before(later, lw_b)
    yield
    yield from emit(excl_b, lw_b, a_b, k_b, 0, ab_ref, rb_ref, btb_ref, ktb_ref, gamb_ref)


def _front_body(x_ref, ctx_ref, rowtab_ref, coltab_ref, g1_ref, mod_ref, w_ref, mu_ref, par_ref, w2_ref, a2_ref,
                g2_ref, p_ref, af_ref, rf_ref, btf_ref, ktf_ref, ab_ref, rb_ref, btb_ref, ktb_ref, v_ref,
                gamf_ref, gamb_ref, bonus_ref, g_ref, u_ref, unew_ref, carry_ref, *, n_x_tiles):
    i = pl.program_id(0)
    d = x_ref.shape[1]
    d_in = w_ref.shape[1]
    half = TM_FEAT
    last = carry_ref.shape[0] - 1

    @pl.when(i == 0)
    def _():
        u_ref[...] = jnp.zeros_like(u_ref)
        carry_ref[...] = jnp.zeros_like(carry_ref)

    is_ctx = i >= n_x_tiles
    xin = jnp.where(is_ctx, ctx_ref[...], x_ref[...] + _pos_tile(rowtab_ref, coltab_ref))
    ms = jnp.mean(xin * xin, axis=-1, keepdims=True)
    xn = xin * lax.rsqrt(ms + NORM_EPS) * g1_ref[...]
    sh = jnp.where(is_ctx, mod_ref[1:2, 0:d], mod_ref[0:1, 0:d])
    sc = jnp.where(is_ctx, mod_ref[1:2, d:2 * d], mod_ref[0:1, d:2 * d])
    hb = (xn * (1.0 + sc) + sh).astype(BF16)
    pending = [(slice(r0, r0 + half), lo, min(lo + QUAD, d_in))
               for lo in range(0, d_in, QUAD) for r0 in range(0, hb.shape[0], half)]

    def project_next():
        if not pending:
            return
        rs, lo, hi = pending.pop(0)
        uc = _dot(hb[rs, :], w_ref[:, lo:hi])
        if lo < D_SHIFT:
            unew_ref[rs, lo:min(hi, D_SHIFT)] = uc[:, :min(hi, D_SHIFT) - lo]
        if hi > D_SHIFT:
            p_ref[rs, max(lo, D_SHIFT) - D_SHIFT:hi - D_SHIFT] = uc[:, max(lo, D_SHIFT) - lo:]

    outs = (af_ref, rf_ref, btf_ref, ktf_ref, ab_ref, rb_ref, btb_ref, ktb_ref, v_ref,
            gamf_ref, gamb_ref, bonus_ref, g_ref)
    mu = mu_ref[...]
    ctx_block = 2 * n_x_tiles
    blocks = ((2 * i - 3, carry_ref[SUBLANES:, :], carry_ref[SUBLANES - 1:SUBLANES, :], u_ref[0:1, :]),
              (2 * i - 2, u_ref[0:half, :], carry_ref[last:last + 1, :], u_ref[half:half + 1, :]))
    rr = lax.broadcasted_iota(jnp.int32, (half, half), 0)
    cc = lax.broadcasted_iota(jnp.int32, (half, half), 1)
    same = (rr >> 6) == (cc >> 6)
    consts = (_head_ones(), jnp.logical_and(same, cc < rr).astype(BF16),
              jnp.logical_and(same, cc > rr).astype(BF16))
    project_next()
    for hi, (blk, ub, prow, nrow) in enumerate(blocks):
        prev_ok = jnp.logical_and(blk != 0, blk != ctx_block)
        next_ok = jnp.logical_and(blk != ctx_block - 1, blk != ctx_block)
        for _ in _features(ub, jnp.where(prev_ok, prow, 0.0), jnp.where(next_ok, nrow, 0.0), mu, par_ref, w2_ref,
                           a2_ref, g2_ref, slice(hi * half, (hi + 1) * half), hi * (half // CHUNK), outs, consts):
            project_next()
    while pending:
        project_next()
    carry_ref[...] = u_ref[half - SUBLANES:, :]
    u_ref[...] = unew_ref[...]


def _front_call(x2, ctx_pad, rowtab, coltab, g1, mod, w_in_bf, mu, par, w2, a2, g2):
    s, d = x2.shape
    tm = TM_PROJ
    nx = s // tm
    d_in = w_in_bf.shape[1]
    n_steps = nx + 2
    xmap = lambda i: (jnp.minimum(i, nx - 1), 0)
    const = lambda i: (0, 0)
    step = lambda i: (i, 0)
    seq_bf = jax.ShapeDtypeStruct((n_steps * tm, D_RWKV), BF16)
    seq_f32 = jax.ShapeDtypeStruct((n_steps * tm, D_RWKV), F32)
    gam = jax.ShapeDtypeStruct((n_steps, tm // CHUNK, D_RWKV), F32)
    seq_spec = pl.BlockSpec((tm, D_RWKV), step)
    gam_spec = pl.BlockSpec((1, tm // CHUNK, D_RWKV), lambda i: (i, 0, 0))
    return pl.pallas_call(
        functools.partial(_front_body, n_x_tiles=nx),
        grid=(n_steps,),
        in_specs=[pl.BlockSpec((tm, d), xmap),
                  pl.BlockSpec((tm, d), const),
                  pl.BlockSpec((tm // GRID_W, d // 2), xmap),
                  pl.BlockSpec((tm, d // 2), const),
                  pl.BlockSpec((1, d), const),
                  pl.BlockSpec(mod.shape, const),
                  pl.BlockSpec((d, d_in), const),
                  pl.BlockSpec((1, D_SHIFT), const),
                  pl.BlockSpec((8, D_RWKV), const),
                  pl.BlockSpec(w2.shape, const),
                  pl.BlockSpec(a2.shape, const),
                  pl.BlockSpec(g2.shape, const)],
        out_specs=([pl.BlockSpec((tm, D_POOL), lambda i: (jnp.minimum(i, nx), 0))] + [seq_spec] * 9
                   + [gam_spec, gam_spec, seq_spec, seq_spec]),
        out_shape=([jax.ShapeDtypeStruct(((nx + 1) * tm, D_POOL), F32)] + [seq_bf] * 9
                   + [gam, gam, seq_f32, seq_f32]),
        scratch_shapes=[pltpu.VMEM((tm, D_SHIFT), F32), pltpu.VMEM((tm, D_SHIFT), F32),
                        pltpu.VMEM((TM_FEAT + SUBLANES, D_SHIFT), F32)],
        compiler_params=_cparams(),
        name="front",
    )(x2, ctx_pad, rowtab, coltab, g1, mod, w_in_bf, mu, par, w2, a2, g2)


def _blockdiag(x, head_of_lane):
    xb = x.astype(BF16)
    zero = jnp.zeros_like(xb)
    return jnp.concatenate([jnp.where(head_of_lane == h, xb, zero) for h in range(QUAD // HEAD)], axis=0)


def _unblock(x, head_of_lane):
    out = jnp.where(head_of_lane == 0, x[0:HEAD, :], 0.0)
    for h in range(1, QUAD // HEAD):
        out = jnp.where(head_of_lane == h, x[h * HEAD:(h + 1) * HEAD, :], out)
    return out


def _scan_chunks(chains):
    lane = lax.broadcasted_iota(jnp.int32, (HEAD, QUAD), 1)
    head_of_lane = lane >> 6
    s_idx = lane & (HEAD - 1)
    t_idx = lax.broadcasted_iota(jnp.int32, (HEAD, QUAD), 0)
    eye = s_idx == t_idx
    lower = s_idx < t_idx
    upper = s_idx > t_idx
    bd = functools.partial(_blockdiag, head_of_lane=head_of_lane)

    def same_block(shift):
        return (s_idx >> shift) == (t_idx >> shift)

    def each(fn):
        return [fn(c) for c in chains]

    def stage_masks(c):
        ll = _dot_nt(jnp.concatenate([c["a"], c["r"]], axis=0),
                     jnp.concatenate([bd(c["bt"]), bd(c["kt"])], axis=0))
        strict = upper if c["reverse"] else lower
        incl = jnp.logical_or(strict, eye)
        c["lab"] = jnp.where(strict, ll[:HEAD, :QUAD], 0.0)
        c["lak"] = jnp.where(strict, ll[:HEAD, QUAD:], 0.0)
        c["mrb"] = jnp.where(incl, ll[HEAD:, :QUAD], 0.0)
        c["mrk"] = jnp.where(incl, ll[HEAD:, QUAD:], 0.0)

    each(stage_masks)

    def stage_square(c):
        ld = jnp.where(same_block(4), c["lab"], 0.0)
        c["tm"] = jnp.where(eye, 1.0, 0.0) + ld
        c["xk"] = _dot(ld.astype(BF16), bd(ld))

    def stage_double(c):
        y = _dot(jnp.concatenate([c["tm"], c["xk"]], axis=0).astype(BF16), bd(c["xk"]))
        c["tm"] = c["tm"] + y[:HEAD]
        c["xk"] = y[HEAD:]

    def stage_last_power(c):
        c["tm"] = c["tm"] + _dot(c["tm"].astype(BF16), bd(c["xk"]))

    each(stage_square)
    each(stage_double)
    each(stage_double)
    each(stage_last_power)
    each(lambda c: c.update(vv=_dot(jnp.concatenate([c["lak"], c["mrk"]], axis=0).astype(BF16), bd(c["v"]))))
    for shift in (4, 5):
        joined = jnp.logical_and(same_block(shift + 1), jnp.logical_not(same_block(shift)))
        each(lambda c: c.update(te=_dot(c["tm"].astype(BF16), bd(jnp.where(joined, c["lab"], 0.0)))))
        each(lambda c: c.update(tm=c["tm"] + _dot(c["te"].astype(BF16), bd(c["tm"]))))

    each(lambda c: c.update(x1=_dot(c["tm"].astype(BF16),
                                    jnp.concatenate([bd(c["a"]), bd(c["vv"][:HEAD])], axis=1))))
    each(lambda c: c.update(x2=_dot(c["mrb"].astype(BF16),
                                    jnp.concatenate([bd(c["x1"][:, :QUAD]), bd(c["x1"][:, QUAD:])], axis=1))))

    def stage_state_terms(c):
        rhs = jnp.concatenate(
            [c["x1"].astype(BF16), jnp.concatenate([jnp.zeros((HEAD, QUAD), BF16), c["v"]], axis=1)], axis=0)
        hat = (jnp.concatenate([c["bt"], c["kt"]], axis=0).astype(F32) * c["gam"]).astype(BF16)
        pz = _dot_tn(hat, rhs)
        c["pq"] = _unblock(pz[:, :QUAD], head_of_lane) + jnp.where(eye, c["gam"], 0.0)
        c["zq"] = _unblock(pz[:, QUAD:], head_of_lane)
        c["qeff"] = c["r"].astype(F32) + c["x2"][:, :QUAD]
        c["yin"] = c["x2"][:, QUAD:] + c["vv"][HEAD:]

    each(stage_state_terms)

    outs, state = [], {}
    for c in chains:
        hst = state.get(c["key"], c["hst"])
        qp = _dot(jnp.concatenate([c["qeff"], c["pq"]], axis=0).astype(BF16), bd(hst))
        state[c["key"]] = qp[HEAD:] + c["zq"]
        outs.append(qp[:HEAD] + c["yin"])
    return outs, state


def _scan_body(af, rf, btf, ktf, vf, gamf, ab, rb, btb, ktb, vb, gamb, yf_ref, yb_ref, hf_ref, hb_ref):
    @pl.when(pl.program_id(0) == 0)
    def _():
        hf_ref[...] = jnp.zeros_like(hf_ref)
        hb_ref[...] = jnp.zeros_like(hb_ref)

    names = ("a", "r", "bt", "kt", "v")
    chains, sinks, state_refs = [], [], {}
    for ci in range(SCAN_CPS):
        for refs, y_ref, h_ref, reverse in (((af, rf, btf, ktf, vf, gamf), yf_ref, hf_ref, False),
                                            ((ab, rb, btb, ktb, vb, gamb), yb_ref, hb_ref, True)):
            chunk = SCAN_CPS - 1 - ci if reverse else ci
            rows = slice(chunk * CHUNK, (chunk + 1) * CHUNK)
            for q in range(D_RWKV // QUAD):
                sl = slice(q * QUAD, (q + 1) * QUAD)
                chain = {n: ref[rows, sl] for n, ref in zip(names, refs[:5])}
                chain.update(gam=refs[5][chunk, :, sl], hst=h_ref[:, sl], reverse=reverse, key=(reverse, q))
                chains.append(chain)
                sinks.append((y_ref, rows, sl))
                state_refs[(reverse, q)] = (h_ref, sl)
    outs, state = _scan_chunks(chains)
    for y, (y_ref, rows, sl) in zip(outs, sinks):
        y_ref[rows, sl] = y
    for key, (h_ref, sl) in state_refs.items():
        h_ref[:, sl] = state[key]


def _scan_call(feats, gamf, gamb, n_x_chunks, n_c_chunks):
    af, rf, btf, ktf, ab, rb, btb, ktb, v = feats
    assert n_x_chunks % SCAN_CPS == 0 and n_c_chunks % SCAN_CPS == 0
    nxb, ncb = n_x_chunks // SCAN_CPS, n_c_chunks // SCAN_CPS
    n = nxb + ncb
    assert SCAN_CPS * CHUNK == TM_FEAT
    fwd_blk = lambda i: jnp.where(i < ncb, nxb + i, i - ncb)
    bwd_blk = lambda i: n - 1 - i
    fwd_y = lambda i: (fwd_blk(i), 0)
    bwd_y = lambda i: (bwd_blk(i), 0)
    fwd = lambda i: (fwd_blk(i) + FEAT_SHIFT_BLOCKS, 0)
    bwd = lambda i: (bwd_blk(i) + FEAT_SHIFT_BLOCKS, 0)
    fwd3 = lambda i: (fwd_blk(i) + FEAT_SHIFT_BLOCKS, 0, 0)
    bwd3 = lambda i: (bwd_blk(i) + FEAT_SHIFT_BLOCKS, 0, 0)
    blk = (SCAN_CPS * CHUNK, D_RWKV)
    in_specs = ([pl.BlockSpec(blk, fwd)] * 5 + [pl.BlockSpec((SCAN_CPS, 1, D_RWKV), fwd3)]
                + [pl.BlockSpec(blk, bwd)] * 5 + [pl.BlockSpec((SCAN_CPS, 1, D_RWKV), bwd3)])
    y = jax.ShapeDtypeStruct((n * SCAN_CPS * CHUNK, D_RWKV), F32)
    return pl.pallas_call(
        _scan_body,
        grid=(n,),
        in_specs=in_specs,
        out_specs=[pl.BlockSpec(blk, fwd_y), pl.BlockSpec(blk, bwd_y)],
        out_shape=[y, y],
        scratch_shapes=[pltpu.VMEM((HEAD, D_RWKV), F32), pltpu.VMEM((HEAD, D_RWKV), F32)],
        compiler_params=_cparams(),
        name="scan",
    )(af, rf, btf, ktf, v, gamf, ab, rb, btb, ktb, v, gamb)


def _mix_body(x_ref, rowtab_ref, coltab_ref, yf_ref, yb_ref, bonus0_ref, bonus1_ref, g0_ref, g1_ref,
              pm_ref, pp_ref, pn_ref, gn_ref, wpool_ref, pscale_ref, wout_ref, mod_ref, g2n_ref, wr_ref, br_ref,
              o_ref, hp_ref, eid_ref, gate_ref, slab_ref, *, n_x_tiles, seq_len):
    j = pl.program_id(0)
    tm, d = x_ref.shape
    hl = POOL_HALO
    slab_ref[0:hl, :] = jnp.where(j != 0, pp_ref[...], 0.0)
    slab_ref[hl:hl + tm, :] = pm_ref[...]
    slab_ref[hl + tm:2 * hl + tm, :] = jnp.where(j != n_x_tiles - 1, pn_ref[...], 0.0)
    tglob = j * tm + lax.broadcasted_iota(jnp.int32, (tm, 1), 0)
    pooled = []
    for gi, win in enumerate(POOL_WINDOWS):
        cols = slice(gi * POOL_GW, (gi + 1) * POOL_GW)
        acc = slab_ref[hl - win // 2:hl - win // 2 + tm, cols]
        for off in range(-win // 2 + 1, win // 2):
            acc = acc + slab_ref[hl + off:hl + off + tm, cols]
        lo = jnp.maximum(tglob - win // 2, 0)
        hi = jnp.minimum(tglob - win // 2 + win, seq_len)
        cnt = (hi - lo).astype(F32)
        pg = acc / cnt - slab_ref[hl:hl + tm, cols]
        pooled.append(_dot(pg.astype(BF16), wpool_ref[gi]))
    pool = jnp.concatenate(pooled, axis=1) * pscale_ref[...]

    ones = _head_ones()
    y = yf_ref[...] + yb_ref[...]
    yc = y - _headsum(y, ones) * (1.0 / HEAD)
    var = _headsum(yc * yc, ones) * (1.0 / HEAD)
    bonus = jnp.concatenate([bonus0_ref[...], bonus1_ref[...]], axis=0)
    gate = jnp.concatenate([g0_ref[...], g1_ref[...]], axis=0)
    rw = (yc * lax.rsqrt(var + GN_EPS) * gn_ref[0:1, :] + gn_ref[1:2, :] + bonus) * gate
    mix = _dot(jnp.concatenate([rw, pool], axis=1).astype(BF16), wout_ref[...])
    gt_a = mod_ref[0:1, 2 * d:3 * d]
    x_mid = x_ref[...] + _pos_tile(rowtab_ref, coltab_ref) + gt_a * mix
    o_ref[...] = x_mid
    _route(x_mid, g2n_ref, mod_ref, wr_ref, br_ref, hp_ref, eid_ref, gate_ref)


def _mix_call(x2, rowtab, coltab, yf, yb, bonus, g, p, gn, w_pool_bf, pscale, w_out_bf, mod, g2n, wr, br):
    s, d = x2.shape
    tm = TM_PROJ
    nx = s // tm
    hb = tm // POOL_HALO
    const = lambda j: (0, 0)
    tok = lambda j: (j, 0)
    seq_spec = pl.BlockSpec((tm, D_RWKV), tok)
    assert tm == 2 * TM_FEAT
    feat_lo = pl.BlockSpec((TM_FEAT, D_RWKV), lambda j: (2 * j + FEAT_SHIFT_BLOCKS, 0))
    feat_hi = pl.BlockSpec((TM_FEAT, D_RWKV), lambda j: (2 * j + FEAT_SHIFT_BLOCKS + 1, 0))
    return pl.pallas_call(
        functools.partial(_mix_body, n_x_tiles=nx, seq_len=s),
        grid=(nx,),
        in_specs=[pl.BlockSpec((tm, d), tok),
                  pl.BlockSpec((tm // GRID_W, d // 2), tok),
                  pl.BlockSpec((tm, d // 2), const),
                  seq_spec, seq_spec, feat_lo, feat_hi, feat_lo, feat_hi, seq_spec,
                  pl.BlockSpec((POOL_HALO, D_POOL), lambda j: (jnp.maximum(j * hb - 1, 0), 0)),
                  pl.BlockSpec((POOL_HALO, D_POOL), lambda j: ((j + 1) * hb, 0)),
                  pl.BlockSpec((2, D_RWKV), const),
                  pl.BlockSpec(w_pool_bf.shape, lambda j: (0, 0, 0)),
                  pl.BlockSpec((1, D_POOL), const),
                  pl.BlockSpec(w_out_bf.shape, const),
                  pl.BlockSpec(mod.shape, const),
                  pl.BlockSpec((1, d), const),
                  pl.BlockSpec(wr.shape, const),
                  pl.BlockSpec(br.shape, const)],
        out_specs=[pl.BlockSpec((tm, d), tok),
                   pl.BlockSpec((tm, d // 2), tok),
                   pl.BlockSpec((8, tm), lambda j: (0, j)),
                   pl.BlockSpec((8, tm), lambda j: (0, j))],
        out_shape=[jax.ShapeDtypeStruct((s, d), F32),
                   jax.ShapeDtypeStruct((s, d // 2), jnp.uint32),
                   jax.ShapeDtypeStruct((8, s), jnp.int32),
                   jax.ShapeDtypeStruct((8, s), F32)],
        scratch_shapes=[pltpu.VMEM((tm + 2 * POOL_HALO, D_POOL), F32)],
        compiler_params=_cparams(),
        name="mix",
    )(x2, rowtab, coltab, yf, yb, bonus, bonus, g, g, p, p, p, gn, w_pool_bf, pscale, w_out_bf, mod, g2n, wr, br)


def _pack_bf16_halves(x):
    half = x.shape[1] // 2
    bits = lax.bitcast_convert_type(x.astype(F32), jnp.uint32)
    return (bits[:, :half] >> 16) | bits[:, half:]


def _unpack_bf16_halves(words):
    lo = lax.bitcast_convert_type(words << 16, F32)
    hi = lax.bitcast_convert_type(words & jnp.uint32(0xFFFF0000), F32)
    return jnp.concatenate([lo, hi], axis=1).astype(BF16)


def _route(x, g2_ref, mod_ref, wr_ref, br_ref, hp_ref, eid_ref, gate_ref):
    tm, d = x.shape
    ms = jnp.mean(x * x, axis=-1, keepdims=True)
    xn = x * lax.rsqrt(ms + NORM_EPS) * g2_ref[...]
    h = xn * (1.0 + mod_ref[0:1, 4 * d:5 * d]) + mod_ref[0:1, 3 * d:4 * d]

    wh, wl = _split2(wr_ref[...])
    hh, hl = _split2(h)
    hp_ref[...] = _pack_bf16_halves(hh)
    logits = _dot_nt(wh, hh) + (_dot_nt(wh, hl) + _dot_nt(wl, hh)) + br_ref[...]
    gl = logits[0:N_GROUPS, :]
    gmax = jnp.max(gl, axis=0, keepdims=True)
    gidx = lax.broadcasted_iota(jnp.int32, gl.shape, 0).astype(F32)
    grp = jnp.min(jnp.where(gl == gmax, gidx, float(N_GROUPS)), axis=0, keepdims=True)
    p_grp = 1.0 / jnp.sum(jnp.exp(gl - gmax), axis=0, keepdims=True)
    sel = jnp.zeros((EXPERTS_PER_GROUP, tm), F32)
    for gi in range(N_GROUPS):
        lo = N_GROUPS + gi * EXPERTS_PER_GROUP
        sel = jnp.where(grp == float(gi), logits[lo:lo + EXPERTS_PER_GROUP, :], sel)
    eidx = lax.broadcasted_iota(jnp.int32, sel.shape, 0).astype(F32)
    top1 = jnp.max(sel, axis=0, keepdims=True)
    i1 = jnp.min(jnp.where(sel == top1, eidx, float(EXPERTS_PER_GROUP)), axis=0, keepdims=True)
    sel2 = jnp.where(eidx == i1, -jnp.inf, sel)
    top2 = jnp.max(sel2, axis=0, keepdims=True)
    i2 = jnp.min(jnp.where(sel2 == top2, eidx, float(EXPERTS_PER_GROUP)), axis=0, keepdims=True)
    e2 = jnp.exp(top2 - top1)
    inv = 1.0 / (1.0 + e2)
    zf = jnp.zeros((6, tm), F32)
    gate_ref[...] = jnp.concatenate([p_grp * inv, p_grp * (e2 * inv), zf], axis=0)
    base = grp * float(EXPERTS_PER_GROUP)
    eid_ref[...] = jnp.concatenate([base + i1, base + i2, zf], axis=0).astype(jnp.int32)


def _sc_gather_rows(table, idx):
    n_out = idx.shape[0]
    n_workers = V7X_SC_CORES * V7X_SC_SUBCORES
    per_worker = n_out // n_workers
    chunk_rows = SC_GATHER_BYTES // (4 * math.prod(table.shape[1:]))
    n_chunks = per_worker // chunk_rows
    assert per_worker * n_workers == n_out and n_chunks * chunk_rows == per_worker and n_chunks % 2 == 0
    mesh = plsc.VectorSubcoreMesh(core_axis_name="c", subcore_axis_name="s",
                                  num_cores=V7X_SC_CORES, num_subcores=V7X_SC_SUBCORES)

    def body(table_hbm, idx_hbm, out_hbm, idx_v, rows0, rows1, gsem0, gsem1, wsem0, wsem1):
        worker = lax.axis_index("s") * V7X_SC_CORES + lax.axis_index("c")
        base = worker * per_worker
        pltpu.sync_copy(idx_hbm.at[pl.ds(base, per_worker)], idx_v)

        def gather(j, buf, sem):
            off = pl.multiple_of(j * chunk_rows, chunk_rows)
            return pltpu.make_async_copy(table_hbm.at[idx_v.at[pl.ds(off, chunk_rows)]], buf, sem)

        def write(j, buf, sem):
            off = pl.multiple_of(j * chunk_rows, chunk_rows)
            return pltpu.make_async_copy(buf, out_hbm.at[pl.ds(base + off, chunk_rows)], sem)

        gather(0, rows0, gsem0).start()

        @pl.loop(0, n_chunks, step=2)
        def _(j):
            gather(j, rows0, gsem0).wait()

            @pl.when(j >= 2)
            def _():
                write(j - 1, rows1, wsem1).wait()

            gather(j + 1, rows1, gsem1).start()
            write(j, rows0, wsem0).start()
            gather(j + 1, rows1, gsem1).wait()
            write(j, rows0, wsem0).wait()

            @pl.when(j + 2 < n_chunks)
            def _():
                gather(j + 2, rows0, gsem0).start()

            write(j + 1, rows1, wsem1).start()

        write(n_chunks - 1, rows1, wsem1).wait()

    rows = pltpu.VMEM((chunk_rows,) + table.shape[1:], table.dtype)
    return pl.kernel(
        body,
        out_type=jax.ShapeDtypeStruct((n_out,) + table.shape[1:], table.dtype),
        mesh=mesh,
        scratch_types=[pltpu.VMEM((per_worker,), jnp.int32), rows, rows,
                       pltpu.SemaphoreType.DMA, pltpu.SemaphoreType.DMA,
                       pltpu.SemaphoreType.DMA, pltpu.SemaphoreType.DMA],
        name="sc_gather",
    )(table, idx)


def _experts_body(blk_e_ref, n_used_ref, first_ref, next_e_ref, wslot_ref, xs_ref, wg_hbm, wu_hbm, wd_hbm, y_ref,
                  wg_buf, wu_buf, wd_buf, wsem_ref):
    i = pl.program_id(0)
    used = i < n_used_ref[0]
    slot = wslot_ref[i]

    def weight_copies(expert, to_slot):
        return [pltpu.make_async_copy(hbm.at[expert], buf.at[to_slot], wsem_ref.at[to_slot])
                for hbm, buf in ((wg_hbm, wg_buf), (wu_hbm, wu_buf), (wd_hbm, wd_buf))]

    @pl.when(jnp.logical_and(i == 0, used))
    def _():
        for cp in weight_copies(blk_e_ref[0], 0):
            cp.start()

    @pl.when(jnp.logical_and(used, first_ref[i] == 1))
    def _():
        for cp in weight_copies(blk_e_ref[i], slot):
            cp.wait()

        @pl.when(next_e_ref[i] != blk_e_ref[i])
        def _():
            for cp in weight_copies(next_e_ref[i], 1 - slot):
                cp.start()

    @pl.when(used)
    def _():
        xb = _unpack_bf16_halves(xs_ref[...])
        gate = _dot(xb, wg_buf[slot].astype(BF16))
        upp = _dot(xb, wu_buf[slot].astype(BF16))
        hid = (gate * _sigmoid(gate)) * upp
        y_ref[...] = _dot(hid.astype(BF16), wd_buf[slot].astype(BF16))

    @pl.when(jnp.logical_not(used))
    def _():
        y_ref[...] = jnp.zeros_like(y_ref)


def _experts_call(blk_expert, n_used, xs, wg, wu, wd):
    nb = blk_expert.shape[0]
    d = wg.shape[1]
    first, next_e, wslot = _weight_schedule(blk_expert, n_used)
    blk = lambda i, *_: (i, 0)
    grid_spec = pltpu.PrefetchScalarGridSpec(
        num_scalar_prefetch=5,
        grid=(nb,),
        in_specs=[pl.BlockSpec((MOE_BM, d // 2), blk),
                  pl.BlockSpec(memory_space=pl.ANY),
                  pl.BlockSpec(memory_space=pl.ANY),
                  pl.BlockSpec(memory_space=pl.ANY)],
        out_specs=pl.BlockSpec((MOE_BM, d), blk),
        scratch_shapes=[pltpu.VMEM((2, d, D_EXPERT), F32), pltpu.VMEM((2, d, D_EXPERT), F32),
                        pltpu.VMEM((2, D_EXPERT, d), F32), pltpu.SemaphoreType.DMA((2,))],
    )
    return pl.pallas_call(
        _experts_body,
        grid_spec=grid_spec,
        out_shape=jax.ShapeDtypeStruct((nb * MOE_BM, d), F32),
        compiler_params=_cparams(),
        name="experts",
    )(blk_expert, n_used, first, next_e, wslot, xs, wg, wu, wd)


def _combine_body(x_ref, y0_ref, y1_ref, gate_ref, mod_ref, fg_ref, o_ref):
    tm, d = x_ref.shape
    y = gate_ref[:, 0:1] * y0_ref[...] + gate_ref[:, 1:2] * y1_ref[...]
    x = x_ref[...] + mod_ref[0:1, 5 * d:6 * d] * y
    ms = jnp.mean(x * x, axis=-1, keepdims=True)
    o_ref[...] = x * lax.rsqrt(ms + NORM_EPS) * fg_ref[...]


def _combine_call(x_mid, ypairs, gcol, mod, fg):
    s, d = x_mid.shape
    tm = TM_PROJ
    nt = s // tm
    const = lambda j: (0, 0)
    return pl.pallas_call(
        _combine_body,
        grid=(nt,),
        in_specs=[pl.BlockSpec((tm, d), lambda j: (j, 0)),
                  pl.BlockSpec((tm, d), lambda j: (j, 0)),
                  pl.BlockSpec((tm, d), lambda j: (j + nt, 0)),
                  pl.BlockSpec((tm, 2), lambda j: (j, 0)),
                  pl.BlockSpec(mod.shape, const),
                  pl.BlockSpec((1, d), const)],
        out_specs=pl.BlockSpec((tm, d), lambda j: (j, 0)),
        out_shape=jax.ShapeDtypeStruct((s, d), F32),
        compiler_params=_cparams(),
        name="combine",
    )(x_mid, ypairs, ypairs, gcol, mod, fg)


def _blockdiag2(wf, wb):
    z = jnp.zeros_like(wf)
    return jnp.concatenate([jnp.concatenate([wf, z], axis=1), jnp.concatenate([z, wb], axis=1)], axis=0)


def _routing_plan(eids, n_tok):
    m = 2 * n_tok
    experts = jnp.arange(N_EXPERTS, dtype=jnp.int32)
    flat_e = eids[:2].reshape(m)
    counts = jnp.sum((flat_e[:, None] == experts[None, :]).astype(jnp.int32), axis=0)
    pcounts = (counts + MOE_BM - 1) // MOE_BM * MOE_BM
    n_blocks = (m + N_EXPERTS * (MOE_BM - 1) + MOE_BM - 1) // MOE_BM
    n_slots = n_blocks * MOE_BM
    pad_ends = jnp.cumsum(pcounts - counts)
    pad = jnp.arange(n_slots - m, dtype=jnp.int32)
    pad_e = jnp.sum((pad[:, None] >= pad_ends[None, :]).astype(jnp.int32), axis=1)
    ids = jnp.arange(n_slots, dtype=jnp.int32)
    slot_e, slot_id = lax.sort((jnp.concatenate([flat_e, pad_e]), ids), num_keys=1, is_stable=True)
    src_tok = slot_id % n_tok
    _, slot_of_id = lax.sort((slot_id, ids), num_keys=1)
    dest = slot_of_id[:m]
    blk_expert = jnp.minimum(slot_e[::MOE_BM], N_EXPERTS - 1)
    n_used = (jnp.sum(pcounts)[None] // MOE_BM).astype(jnp.int32)
    return src_tok, dest, blk_expert, n_used


def _weight_schedule(blk_expert, n_used):
    used = jnp.arange(blk_expert.shape[0], dtype=jnp.int32) < n_used[0]
    first = jnp.logical_and(used, jnp.concatenate([jnp.ones((1,), bool), blk_expert[1:] != blk_expert[:-1]]))
    none = jnp.int32(N_EXPERTS)
    later_first = lax.cummin(jnp.where(first, blk_expert, none), axis=0, reverse=True)
    next_e = jnp.concatenate([later_first[1:], none[None]])
    next_e = jnp.where(next_e == none, blk_expert, next_e)
    wslot = jnp.maximum(jnp.cumsum(first.astype(jnp.int32)) - 1, 0) & 1
    return first.astype(jnp.int32), next_e, wslot


def kernel(x, c, ctx, c_ctx, norm1_g, w_mod, b_mod, w_in, mu_shift, w0_f, w2_f, w0_b, w2_b, a0_f, a2_f, a0_b, a2_b, g2, k_k, k_a, r_k_f, r_k_b, gn_w, gn_b, w_pool, pool_scale, w_out, norm2_g, w_router_grp, b_router_grp, w_router_exp, b_router_exp, w_gate, w_up, w_down, final_g):
    b, s, d = x.shape
    n_ctx = ctx.shape[1]
    assert b == 1 and c.shape[0] == 1 and w_mod.shape[0] == 1
    assert s % TM_PROJ == 0 and n_ctx % CHUNK == 0 and n_ctx == TM_FEAT and d == 2 * D_RWKV
    x2 = x[0]
    rows = s // GRID_W

    quarter = d // 4
    freq = POS_THETA ** (-jnp.arange(quarter, dtype=F32) / quarter)
    rarg = jnp.arange(rows, dtype=F32)[:, None] * freq
    carg = jnp.arange(GRID_W, dtype=F32)[:, None] * freq
    rowtab = jnp.concatenate([jnp.sin(rarg), jnp.cos(rarg)], axis=-1)
    coltab = jnp.tile(jnp.concatenate([jnp.sin(carg), jnp.cos(carg)], axis=-1), (TM_PROJ // GRID_W, 1))

    cs = jnp.concatenate([c, c_ctx[None, :], jnp.zeros((6, d), F32)], axis=0)
    mod = _mod_call(cs, w_mod[0], b_mod)

    ctx_pad = jnp.concatenate([ctx[0], jnp.zeros((TM_PROJ - n_ctx, d), F32)], axis=0)
    par = jnp.concatenate([w0_f, w0_b, a0_f, a0_b, k_k, k_a, r_k_f, r_k_b], axis=0)
    outs = _front_call(x2, ctx_pad, rowtab, coltab, norm1_g, mod, w_in[0].astype(BF16), mu_shift, par,
                       _blockdiag2(w2_f[0], w2_b[0]).astype(BF16), _blockdiag2(a2_f[0], a2_b[0]).astype(BF16),
                       g2[0].astype(BF16))
    p, feats, gamf, gamb, bonus, g = outs[0], outs[1:10], outs[10], outs[11], outs[12], outs[13]
    yf, yb = _scan_call(feats, gamf.reshape(-1, 1, D_RWKV), gamb.reshape(-1, 1, D_RWKV),
                        s // CHUNK, n_ctx // CHUNK)

    gn = jnp.concatenate([gn_w, gn_b], axis=0)
    wr = jnp.concatenate([w_router_grp[0].T, w_router_exp[0].T,
                          jnp.zeros((ROUTER_ROWS - N_GROUPS - N_EXPERTS, d), F32)], axis=0)
    br = jnp.concatenate([b_router_grp[0], b_router_exp[0],
                          jnp.zeros((ROUTER_ROWS - N_GROUPS - N_EXPERTS,), F32)])[:, None]
    x_mid, hp, eids, gates = _mix_call(x2, rowtab, coltab, yf, yb, bonus, g, p, gn, w_pool[0].astype(BF16),
                                       pool_scale, w_out[0].astype(BF16), mod, norm2_g, wr, br)

    src_tok, dest, blk_expert, n_used = _routing_plan(eids, s)
    xs = _sc_gather_rows(hp, src_tok)
    yexp = _experts_call(blk_expert, n_used, xs, w_gate[0], w_up[0], w_down[0])
    ypairs = _sc_gather_rows(yexp, dest)
    out = _combine_call(x_mid, ypairs, gates[:2].T, mod, final_g[None, :])
    return out[None]
```

```python
import functools
import math

import jax
import jax.numpy as jnp
from jax import lax
from jax.experimental import pallas as pl
from jax.experimental.pallas import tpu as pltpu
from jax.experimental.pallas import tpu_sc as plsc

F32 = jnp.float32
BF16 = jnp.bfloat16

GRID_W = 64
SUBLANES, LANES = 8, 128
HEAD = 64
N_HEADS = 8
D_RWKV = HEAD * N_HEADS
D_POOL = 512
POOL_WINDOWS = (2, 4, 8, 16)
POOL_GW = 128
POOL_HALO = 8
D_SHIFT = 1920
N_GROUPS = 4
EXPERTS_PER_GROUP = 8
N_EXPERTS = 32
D_EXPERT = 512
NORM_EPS = 1e-6
GN_EPS = 64e-5
POS_THETA = 10000.0
DECAY_SCALE = 0.6065306597

CHUNK = 64
QUAD = 4 * HEAD
SCAN_CPS = 4
TM_PROJ = 512
TM_FEAT = 256
FEAT_SHIFT_BLOCKS = 3
MOE_BM = 512
EXPERT_IN_BUFS = 3
ROUTER_ROWS = 48
V7X_VMEM_LIMIT = 56 * 1024 * 1024
V7X_SC_CORES, V7X_SC_SUBCORES = 2, 16
SC_GATHER_BYTES = 128 * 1024


def _cparams(n_axes=1, vmem=V7X_VMEM_LIMIT):
    return pltpu.CompilerParams(dimension_semantics=("arbitrary",) * n_axes,
                                vmem_limit_bytes=vmem)


def _dot(a, b):
    return jnp.dot(a, b, preferred_element_type=F32)


def _dot_nt(a, b):
    return lax.dot_general(a, b, (((1,), (1,)), ((), ())), preferred_element_type=F32)


def _dot_tn(a, b):
    return lax.dot_general(a, b, (((0,), (0,)), ((), ())), preferred_element_type=F32)


def _split2(x):
    hi = x.astype(BF16)
    lo = (x - hi.astype(F32)).astype(BF16)
    return hi, lo


def _dot3(a, b):
    ah, al = _split2(a)
    bh, bl = _split2(b)
    return _dot(ah, bh) + (_dot(ah, bl) + _dot(al, bh))


def _sigmoid(x):
    return 0.5 * jnp.tanh(0.5 * x) + 0.5


def _head_ones():
    r = lax.broadcasted_iota(jnp.int32, (D_RWKV, D_RWKV), 0) >> 6
    c = lax.broadcasted_iota(jnp.int32, (D_RWKV, D_RWKV), 1) >> 6
    return (r == c).astype(BF16)


def _headsum(x, ones):
    hi, lo = _split2(x)
    both = _dot(jnp.concatenate([hi, lo], axis=0), ones)
    return both[:x.shape[0]] + both[x.shape[0]:]


def _mod_body(cs_ref, w_ref, b_ref, o_ref):
    a = cs_ref[...]
    a = a * _sigmoid(a)
    o_ref[...] = _dot3(a, w_ref[...]) + b_ref[...]


def _mod_call(cs, w_mod, b_mod):
    d, n = w_mod.shape
    tn = 512
    return pl.pallas_call(
        _mod_body,
        grid=(n // tn,),
        in_specs=[pl.BlockSpec((8, d), lambda j: (0, 0)),
                  pl.BlockSpec((d, tn), lambda j: (0, j)),
                  pl.BlockSpec((1, tn), lambda j: (0, j))],
        out_specs=pl.BlockSpec((8, tn), lambda j: (0, j)),
        out_shape=jax.ShapeDtypeStruct((8, n), F32),
        compiler_params=_cparams(),
        name="mod",
    )(cs, w_mod, b_mod)


def _pos_tile(rowtab_ref, coltab_ref):
    rt = rowtab_ref[...]
    rowpart = jnp.concatenate(
        [jnp.broadcast_to(rt[r:r + 1, :], (GRID_W, rt.shape[1])) for r in range(TM_PROJ // GRID_W)], axis=0)
    return jnp.concatenate([rowpart, coltab_ref[...]], axis=1)


def _features(u, prow, nrow, mu, par_ref, w2_ref, a2_ref, g2_ref, rows, chunk0, outs, consts):
    (af_ref, rf_ref, btf_ref, ktf_ref, ab_ref, rb_ref, btb_ref, ktb_ref, v_ref,
     gamf_ref, gamb_ref, bonus_ref, g_ref) = outs
    tm = u.shape[0]
    rid = lax.broadcasted_iota(jnp.int32, (SUBLANES, 1), 0)

    def shifted(lo, hi):
        uc, m = u[:, lo:hi], mu[:, lo:hi]
        up = pltpu.roll(uc, 1, 0)
        up = jnp.concatenate([jnp.where(rid == 0, prow[:, lo:hi], up[:SUBLANES]), up[SUBLANES:]], axis=0)
        dn = pltpu.roll(uc, tm - 1, 0)
        dn = jnp.concatenate(
            [dn[:-SUBLANES], jnp.where(rid == SUBLANES - 1, nrow[:, lo:hi], dn[-SUBLANES:])], axis=0)
        return (1.0 - m) * uc + (0.5 * m) * (up + dn)

    w0f, w0b, a0f, a0b = par_ref[0:1, :], par_ref[1:2, :], par_ref[2:3, :], par_ref[3:4, :]
    k_k, k_a, rkf, rkb = par_ref[4:5, :], par_ref[5:6, :], par_ref[6:7, :], par_ref[7:8, :]

    r = shifted(0, 512)
    yield
    k = shifted(512, 1024)
    yield
    v = shifted(1024, 1536)
    v_ref[rows, :] = v.astype(BF16)
    yield
    lora = shifted(1536, D_SHIFT)
    zw = _dot(jnp.tanh(lora[:, 0:128]).astype(BF16), w2_ref[...])
    za = _dot(lora[:, 128:256].astype(BF16), a2_ref[...])
    g_ref[rows, :] = _dot(_sigmoid(lora[:, 256:384]).astype(BF16), g2_ref[...])
    yield
    lw_f = -DECAY_SCALE * _sigmoid(w0f + zw[:, :512])
    lw_b = -DECAY_SCALE * _sigmoid(w0b + zw[:, 512:])
    a_f = _sigmoid(a0f + za[:, :512])
    a_b = _sigmoid(a0b + za[:, 512:])
    yield

    ones, earlier, later = consts
    kkr = k * k_k
    kk = kkr * lax.rsqrt(jnp.maximum(_headsum(kkr * kkr, ones), 1e-24))
    yield
    k_f = k * (1.0 + (a_f - 1.0) * k_a)
    k_b = k * (1.0 + (a_b - 1.0) * k_a)
    bonus_ref[rows, :] = _headsum(r * (k_f * rkf + k_b * rkb), ones) * v
    yield

    n_ch = tm // CHUNK

    def visited_before(m, lwd):
        l1, l2 = _split2(lwd)
        return _dot(m, l1) + _dot(m, l2)

    def emit(excl, lwd, a_d, k_d, last_row, a_ref, r_ref, bt_ref, kt_ref, gam_ref):
        cum = excl + lwd
        gam_ref[0, chunk0:chunk0 + n_ch, :] = jnp.exp(jnp.concatenate(
            [cum[c * CHUNK + last_row:c * CHUNK + last_row + 1, :] for c in range(n_ch)], axis=0))
        a_ref[rows, :] = (-kk * jnp.exp(excl)).astype(BF16)
        r_ref[rows, :] = (r * jnp.exp(cum)).astype(BF16)
        yield
        e_neg = jnp.exp(-cum)
        bt_ref[rows, :] = (kk * a_d * e_neg).astype(BF16)
        kt_ref[rows, :] = (k_d * e_neg).astype(BF16)
        yield

    excl_f = visited_before(earlier, lw_f)
    yield
    yield from emit(excl_f, lw_f, a_f, k_f, CHUNK - 1, af_ref, rf_ref, btf_ref, ktf_ref, gamf_ref)
    excl_b = visited_before(later, lw_b)
    yield
    yield from emit(excl_b, lw_b, a_b, k_b, 0, ab_ref, rb_ref, btb_ref, ktb_ref, gamb_ref)


def _front_body(x_ref, ctx_ref, rowtab_ref, coltab_ref, g1_ref, mod_ref, w_ref, mu_ref, par_ref, w2_ref, a2_ref,
                g2_ref, p_ref, af_ref, rf_ref, btf_ref, ktf_ref, ab_ref, rb_ref, btb_ref, ktb_ref, v_ref,
                gamf_ref, gamb_ref, bonus_ref, g_ref, u_ref, unew_ref, carry_ref, *, n_x_tiles):
    i = pl.program_id(0)
    d = x_ref.shape[1]
    d_in = w_ref.shape[1]
    half = TM_FEAT
    last = carry_ref.shape[0] - 1

    @pl.when(i == 0)
    def _():
        u_ref[...] = jnp.zeros_like(u_ref)
        carry_ref[...] = jnp.zeros_like(carry_ref)

    is_ctx = i >= n_x_tiles
    xin = jnp.where(is_ctx, ctx_ref[...], x_ref[...] + _pos_tile(rowtab_ref, coltab_ref))
    ms = jnp.mean(xin * xin, axis=-1, keepdims=True)
    xn = xin * lax.rsqrt(ms + NORM_EPS) * g1_ref[...]
    sh = jnp.where(is_ctx, mod_ref[1:2, 0:d], mod_ref[0:1, 0:d])
    sc = jnp.where(is_ctx, mod_ref[1:2, d:2 * d], mod_ref[0:1, d:2 * d])
    hb = (xn * (1.0 + sc) + sh).astype(BF16)
    pending = [(slice(r0, r0 + half), lo, min(lo + QUAD, d_in))
               for lo in range(0, d_in, QUAD) for r0 in range(0, hb.shape[0], half)]

    def project_next():
        if not pending:
            return
        rs, lo, hi = pending.pop(0)
        uc = _dot(hb[rs, :], w_ref[:, lo:hi])
        if lo < D_SHIFT:
            unew_ref[rs, lo:min(hi, D_SHIFT)] = uc[:, :min(hi, D_SHIFT) - lo]
        if hi > D_SHIFT:
            p_ref[rs, max(lo, D_SHIFT) - D_SHIFT:hi - D_SHIFT] = uc[:, max(lo, D_SHIFT) - lo:]

    outs = (af_ref, rf_ref, btf_ref, ktf_ref, ab_ref, rb_ref, btb_ref, ktb_ref, v_ref,
            gamf_ref, gamb_ref, bonus_ref, g_ref)
    mu = mu_ref[...]
    ctx_block = 2 * n_x_tiles
    blocks = ((2 * i - 3, carry_ref[SUBLANES:, :], carry_ref[SUBLANES - 1:SUBLANES, :], u_ref[0:1, :]),
              (2 * i - 2, u_ref[0:half, :], carry_ref[last:last + 1, :], u_ref[half:half + 1, :]))
    rr = lax.broadcasted_iota(jnp.int32, (half, half), 0)
    cc = lax.broadcasted_iota(jnp.int32, (half, half), 1)
    same = (rr >> 6) == (cc >> 6)
    consts = (_head_ones(), jnp.logical_and(same, cc < rr).astype(BF16),
              jnp.logical_and(same, cc > rr).astype(BF16))
    project_next()
    for hi, (blk, ub, prow, nrow) in enumerate(blocks):
        prev_ok = jnp.logical_and(blk != 0, blk != ctx_block)
        next_ok = jnp.logical_and(blk != ctx_block - 1, blk != ctx_block)
        for _ in _features(ub, jnp.where(prev_ok, prow, 0.0), jnp.where(next_ok, nrow, 0.0), mu, par_ref, w2_ref,
                           a2_ref, g2_ref, slice(hi * half, (hi + 1) * half), hi * (half // CHUNK), outs, consts):
            project_next()
    while pending:
        project_next()
    carry_ref[...] = u_ref[half - SUBLANES:, :]
    u_ref[...] = unew_ref[...]


def _front_call(x2, ctx_pad, rowtab, coltab, g1, mod, w_in_bf, mu, par, w2, a2, g2):
    s, d = x2.shape
    tm = TM_PROJ
    nx = s // tm
    d_in = w_in_bf.shape[1]
    n_steps = nx + 2
    xmap = lambda i: (jnp.minimum(i, nx - 1), 0)
    const = lambda i: (0, 0)
    step = lambda i: (i, 0)
    seq_bf = jax.ShapeDtypeStruct((n_steps * tm, D_RWKV), BF16)
    seq_f32 = jax.ShapeDtypeStruct((n_steps * tm, D_RWKV), F32)
    gam = jax.ShapeDtypeStruct((n_steps, tm // CHUNK, D_RWKV), F32)
    seq_spec = pl.BlockSpec((tm, D_RWKV), step)
    gam_spec = pl.BlockSpec((1, tm // CHUNK, D_RWKV), lambda i: (i, 0, 0))
    return pl.pallas_call(
        functools.partial(_front_body, n_x_tiles=nx),
        grid=(n_steps,),
        in_specs=[pl.BlockSpec((tm, d), xmap),
                  pl.BlockSpec((tm, d), const),
                  pl.BlockSpec((tm // GRID_W, d // 2), xmap),
                  pl.BlockSpec((tm, d // 2), const),
                  pl.BlockSpec((1, d), const),
                  pl.BlockSpec(mod.shape, const),
                  pl.BlockSpec((d, d_in), const),
                  pl.BlockSpec((1, D_SHIFT), const),
                  pl.BlockSpec((8, D_RWKV), const),
                  pl.BlockSpec(w2.shape, const),
                  pl.BlockSpec(a2.shape, const),
                  pl.BlockSpec(g2.shape, const)],
        out_specs=([pl.BlockSpec((tm, D_POOL), lambda i: (jnp.minimum(i, nx), 0))] + [seq_spec] * 9
                   + [gam_spec, gam_spec, seq_spec, seq_spec]),
        out_shape=([jax.ShapeDtypeStruct(((nx + 1) * tm, D_POOL), F32)] + [seq_bf] * 9
                   + [gam, gam, seq_f32, seq_f32]),
        scratch_shapes=[pltpu.VMEM((tm, D_SHIFT), F32), pltpu.VMEM((tm, D_SHIFT), F32),
                        pltpu.VMEM((TM_FEAT + SUBLANES, D_SHIFT), F32)],
        compiler_params=_cparams(),
        name="front",
    )(x2, ctx_pad, rowtab, coltab, g1, mod, w_in_bf, mu, par, w2, a2, g2)


def _blockdiag(x, head_of_lane):
    xb = x.astype(BF16)
    zero = jnp.zeros_like(xb)
    return jnp.concatenate([jnp.where(head_of_lane == h, xb, zero) for h in range(QUAD // HEAD)], axis=0)


def _unblock(x, head_of_lane):
    out = jnp.where(head_of_lane == 0, x[0:HEAD, :], 0.0)
    for h in range(1, QUAD // HEAD):
        out = jnp.where(head_of_lane == h, x[h * HEAD:(h + 1) * HEAD, :], out)
    return out


def _scan_chunks(chains):
    lane = lax.broadcasted_iota(jnp.int32, (HEAD, QUAD), 1)
    head_of_lane = lane >> 6
    s_idx = lane & (HEAD - 1)
    t_idx = lax.broadcasted_iota(jnp.int32, (HEAD, QUAD), 0)
    eye = s_idx == t_idx
    lower = s_idx < t_idx
    upper = s_idx > t_idx
    bd = functools.partial(_blockdiag, head_of_lane=head_of_lane)

    def same_block(shift):
        return (s_idx >> shift) == (t_idx >> shift)

    def each(fn):
        return [fn(c) for c in chains]

    def stage_masks(c):
        ll = _dot_nt(jnp.concatenate([c["a"], c["r"]], axis=0),
                     jnp.concatenate([bd(c["bt"]), bd(c["kt"])], axis=0))
        strict = upper if c["reverse"] else lower
        incl = jnp.logical_or(strict, eye)
        c["lab"] = jnp.where(strict, ll[:HEAD, :QUAD], 0.0)
        c["lak"] = jnp.where(strict, ll[:HEAD, QUAD:], 0.0)
        c["mrb"] = jnp.where(incl, ll[HEAD:, :QUAD], 0.0)
        c["mrk"] = jnp.where(incl, ll[HEAD:, QUAD:], 0.0)

    each(stage_masks)

    def stage_square(c):
        ld = jnp.where(same_block(4), c["lab"], 0.0)
        c["tm"] = jnp.where(eye, 1.0, 0.0) + ld
        c["xk"] = _dot(ld.astype(BF16), bd(ld))

    def stage_double(c):
        y = _dot(jnp.concatenate([c["tm"], c["xk"]], axis=0).astype(BF16), bd(c["xk"]))
        c["tm"] = c["tm"] + y[:HEAD]
        c["xk"] = y[HEAD:]

    def stage_last_power(c):
        c["tm"] = c["tm"] + _dot(c["tm"].astype(BF16), bd(c["xk"]))

    each(stage_square)
    each(stage_double)
    each(stage_double)
    each(stage_last_power)
    each(lambda c: c.update(vv=_dot(jnp.concatenate([c["lak"], c["mrk"]], axis=0).astype(BF16), bd(c["v"]))))
    for shift in (4, 5):
        joined = jnp.logical_and(same_block(shift + 1), jnp.logical_not(same_block(shift)))
        each(lambda c: c.update(te=_dot(c["tm"].astype(BF16), bd(jnp.where(joined, c["lab"], 0.0)))))
        each(lambda c: c.update(tm=c["tm"] + _dot(c["te"].astype(BF16), bd(c["tm"]))))

    each(lambda c: c.update(x1=_dot(c["tm"].astype(BF16),
                                    jnp.concatenate([bd(c["a"]), bd(c["vv"][:HEAD])], axis=1))))
    each(lambda c: c.update(x2=_dot(c["mrb"].astype(BF16),
                                    jnp.concatenate([bd(c["x1"][:, :QUAD]), bd(c["x1"][:, QUAD:])], axis=1))))

    def stage_state_terms(c):
        rhs = jnp.concatenate(
            [c["x1"].astype(BF16), jnp.concatenate([jnp.zeros((HEAD, QUAD), BF16), c["v"]], axis=1)], axis=0)
        hat = (jnp.concatenate([c["bt"], c["kt"]], axis=0).astype(F32) * c["gam"]).astype(BF16)
        pz = _dot_tn(hat, rhs)
        c["pq"] = _unblock(pz[:, :QUAD], head_of_lane) + jnp.where(eye, c["gam"], 0.0)
        c["zq"] = _unblock(pz[:, QUAD:], head_of_lane)
        c["qeff"] = c["r"].astype(F32) + c["x2"][:, :QUAD]
        c["yin"] = c["x2"][:, QUAD:] + c["vv"][HEAD:]

    each(stage_state_terms)

    outs, state = [], {}
    for c in chains:
        hst = state.get(c["key"], c["hst"])
        qp = _dot(jnp.concatenate([c["qeff"], c["pq"]], axis=0).astype(BF16), bd(hst))
        state[c["key"]] = qp[HEAD:] + c["zq"]
        outs.append(qp[:HEAD] + c["yin"])
    return outs, state


def _scan_body(af, rf, btf, ktf, vf, gamf, ab, rb, btb, ktb, vb, gamb, yf_ref, yb_ref, hf_ref, hb_ref):
    @pl.when(pl.program_id(0) == 0)
    def _():
        hf_ref[...] = jnp.zeros_like(hf_ref)
        hb_ref[...] = jnp.zeros_like(hb_ref)

    names = ("a", "r", "bt", "kt", "v")
    chains, sinks, state_refs = [], [], {}
    for ci in range(SCAN_CPS):
        for refs, y_ref, h_ref, reverse in (((af, rf, btf, ktf, vf, gamf), yf_ref, hf_ref, False),
                                            ((ab, rb, btb, ktb, vb, gamb), yb_ref, hb_ref, True)):
            chunk = SCAN_CPS - 1 - ci if reverse else ci
            rows = slice(chunk * CHUNK, (chunk + 1) * CHUNK)
            for q in range(D_RWKV // QUAD):
                sl = slice(q * QUAD, (q + 1) * QUAD)
                chain = {n: ref[rows, sl] for n, ref in zip(names, refs[:5])}
                chain.update(gam=refs[5][chunk, :, sl], hst=h_ref[:, sl], reverse=reverse, key=(reverse, q))
                chains.append(chain)
                sinks.append((y_ref, rows, sl))
                state_refs[(reverse, q)] = (h_ref, sl)
    outs, state = _scan_chunks(chains)
    for y, (y_ref, rows, sl) in zip(outs, sinks):
        y_ref[rows, sl] = y
    for key, (h_ref, sl) in state_refs.items():
        h_ref[:, sl] = state[key]


def _scan_call(feats, gamf, gamb, n_x_chunks, n_c_chunks):
    af, rf, btf, ktf, ab, rb, btb, ktb, v = feats
    assert n_x_chunks % SCAN_CPS == 0 and n_c_chunks % SCAN_CPS == 0
    nxb, ncb = n_x_chunks // SCAN_CPS, n_c_chunks // SCAN_CPS
    n = nxb + ncb
    assert SCAN_CPS * CHUNK == TM_FEAT
    fwd_blk = lambda i: jnp.where(i < ncb, nxb + i, i - ncb)
    bwd_blk = lambda i: n - 1 - i
    fwd_y = lambda i: (fwd_blk(i), 0)
    bwd_y = lambda i: (bwd_blk(i), 0)
    fwd = lambda i: (fwd_blk(i) + FEAT_SHIFT_BLOCKS, 0)
    bwd = lambda i: (bwd_blk(i) + FEAT_SHIFT_BLOCKS, 0)
    fwd3 = lambda i: (fwd_blk(i) + FEAT_SHIFT_BLOCKS, 0, 0)
    bwd3 = lambda i: (bwd_blk(i) + FEAT_SHIFT_BLOCKS, 0, 0)
    blk = (SCAN_CPS * CHUNK, D_RWKV)
    in_specs = ([pl.BlockSpec(blk, fwd)] * 5 + [pl.BlockSpec((SCAN_CPS, 1, D_RWKV), fwd3)]
                + [pl.BlockSpec(blk, bwd)] * 5 + [pl.BlockSpec((SCAN_CPS, 1, D_RWKV), bwd3)])
    y = jax.ShapeDtypeStruct((n * SCAN_CPS * CHUNK, D_RWKV), F32)
    return pl.pallas_call(
        _scan_body,
        grid=(n,),
        in_specs=in_specs,
        out_specs=[pl.BlockSpec(blk, fwd_y), pl.BlockSpec(blk, bwd_y)],
        out_shape=[y, y],
        scratch_shapes=[pltpu.VMEM((HEAD, D_RWKV), F32), pltpu.VMEM((HEAD, D_RWKV), F32)],
        compiler_params=_cparams(),
        name="scan",
    )(af, rf, btf, ktf, v, gamf, ab, rb, btb, ktb, v, gamb)


def _mix_body(x_ref, rowtab_ref, coltab_ref, yf_ref, yb_ref, bonus0_ref, bonus1_ref, g0_ref, g1_ref,
              pm_ref, pp_ref, pn_ref, gn_ref, wpool_ref, pscale_ref, wout_ref, mod_ref, g2n_ref, wr_ref, br_ref,
              o_ref, hp_ref, eid_ref, gate_ref, slab_ref, *, n_x_tiles, seq_len):
    j = pl.program_id(0)
    tm, d = x_ref.shape
    hl = POOL_HALO
    slab_ref[0:hl, :] = jnp.where(j != 0, pp_ref[...], 0.0)
    slab_ref[hl:hl + tm, :] = pm_ref[...]
    slab_ref[hl + tm:2 * hl + tm, :] = jnp.where(j != n_x_tiles - 1, pn_ref[...], 0.0)
    tglob = j * tm + lax.broadcasted_iota(jnp.int32, (tm, 1), 0)
    pooled = []
    for gi, win in enumerate(POOL_WINDOWS):
        cols = slice(gi * POOL_GW, (gi + 1) * POOL_GW)
        acc = slab_ref[hl - win // 2:hl - win // 2 + tm, cols]
        for off in range(-win // 2 + 1, win // 2):
            acc = acc + slab_ref[hl + off:hl + off + tm, cols]
        lo = jnp.maximum(tglob - win // 2, 0)
        hi = jnp.minimum(tglob - win // 2 + win, seq_len)
        cnt = (hi - lo).astype(F32)
        pg = acc / cnt - slab_ref[hl:hl + tm, cols]
        pooled.append(_dot(pg.astype(BF16), wpool_ref[gi]))
    pool = jnp.concatenate(pooled, axis=1) * pscale_ref[...]

    ones = _head_ones()
    y = yf_ref[...] + yb_ref[...]
    yc = y - _headsum(y, ones) * (1.0 / HEAD)
    var = _headsum(yc * yc, ones) * (1.0 / HEAD)
    bonus = jnp.concatenate([bonus0_ref[...], bonus1_ref[...]], axis=0)
    gate = jnp.concatenate([g0_ref[...], g1_ref[...]], axis=0)
    rw = (yc * lax.rsqrt(var + GN_EPS) * gn_ref[0:1, :] + gn_ref[1:2, :] + bonus) * gate
    mix = _dot(jnp.concatenate([rw, pool], axis=1).astype(BF16), wout_ref[...])
    gt_a = mod_ref[0:1, 2 * d:3 * d]
    x_mid = x_ref[...] + _pos_tile(rowtab_ref, coltab_ref) + gt_a * mix
    o_ref[...] = x_mid
    _route(x_mid, g2n_ref, mod_ref, wr_ref, br_ref, hp_ref, eid_ref, gate_ref)


def _mix_call(x2, rowtab, coltab, yf, yb, bonus, g, p, gn, w_pool_bf, pscale, w_out_bf, mod, g2n, wr, br):
    s, d = x2.shape
    tm = TM_PROJ
    nx = s // tm
    hb = tm // POOL_HALO
    const = lambda j: (0, 0)
    tok = lambda j: (j, 0)
    seq_spec = pl.BlockSpec((tm, D_RWKV), tok)
    assert tm == 2 * TM_FEAT
    feat_lo = pl.BlockSpec((TM_FEAT, D_RWKV), lambda j: (2 * j + FEAT_SHIFT_BLOCKS, 0))
    feat_hi = pl.BlockSpec((TM_FEAT, D_RWKV), lambda j: (2 * j + FEAT_SHIFT_BLOCKS + 1, 0))
    return pl.pallas_call(
        functools.partial(_mix_body, n_x_tiles=nx, seq_len=s),
        grid=(nx,),
        in_specs=[pl.BlockSpec((tm, d), tok),
                  pl.BlockSpec((tm // GRID_W, d // 2), tok),
                  pl.BlockSpec((tm, d // 2), const),
                  seq_spec, seq_spec, feat_lo, feat_hi, feat_lo, feat_hi, seq_spec,
                  pl.BlockSpec((POOL_HALO, D_POOL), lambda j: (jnp.maximum(j * hb - 1, 0), 0)),
                  pl.BlockSpec((POOL_HALO, D_POOL), lambda j: ((j + 1) * hb, 0)),
                  pl.BlockSpec((2, D_RWKV), const),
                  pl.BlockSpec(w_pool_bf.shape, lambda j: (0, 0, 0)),
                  pl.BlockSpec((1, D_POOL), const),
                  pl.BlockSpec(w_out_bf.shape, const),
                  pl.BlockSpec(mod.shape, const),
                  pl.BlockSpec((1, d), const),
                  pl.BlockSpec(wr.shape, const),
                  pl.BlockSpec(br.shape, const)],
        out_specs=[pl.BlockSpec((tm, d), tok),
                   pl.BlockSpec((tm, d // 2), tok),
                   pl.BlockSpec((8, tm), lambda j: (0, j)),
                   pl.BlockSpec((8, tm), lambda j: (0, j))],
        out_shape=[jax.ShapeDtypeStruct((s, d), F32),
                   jax.ShapeDtypeStruct((s, d // 2), jnp.uint32),
                   jax.ShapeDtypeStruct((8, s), jnp.int32),
                   jax.ShapeDtypeStruct((8, s), F32)],
        scratch_shapes=[pltpu.VMEM((tm + 2 * POOL_HALO, D_POOL), F32)],
        compiler_params=_cparams(),
        name="mix",
    )(x2, rowtab, coltab, yf, yb, bonus, bonus, g, g, p, p, p, gn, w_pool_bf, pscale, w_out_bf, mod, g2n, wr, br)


def _pack_bf16_halves(x):
    half = x.shape[1] // 2
    bits = lax.bitcast_convert_type(x.astype(F32), jnp.uint32)
    return (bits[:, :half] >> 16) | bits[:, half:]


def _unpack_bf16_halves(words):
    lo = lax.bitcast_convert_type(words << 16, F32)
    hi = lax.bitcast_convert_type(words & jnp.uint32(0xFFFF0000), F32)
    return jnp.concatenate([lo, hi], axis=1).astype(BF16)


def _route(x, g2_ref, mod_ref, wr_ref, br_ref, hp_ref, eid_ref, gate_ref):
    tm, d = x.shape
    ms = jnp.mean(x * x, axis=-1, keepdims=True)
    xn = x * lax.rsqrt(ms + NORM_EPS) * g2_ref[...]
    h = xn * (1.0 + mod_ref[0:1, 4 * d:5 * d]) + mod_ref[0:1, 3 * d:4 * d]

    wh, wl = _split2(wr_ref[...])
    hh, hl = _split2(h)
    hp_ref[...] = _pack_bf16_halves(hh)
    logits = _dot_nt(wh, hh) + (_dot_nt(wh, hl) + _dot_nt(wl, hh)) + br_ref[...]
    gl = logits[0:N_GROUPS, :]
    gmax = jnp.max(gl, axis=0, keepdims=True)
    gidx = lax.broadcasted_iota(jnp.int32, gl.shape, 0).astype(F32)
    grp = jnp.min(jnp.where(gl == gmax, gidx, float(N_GROUPS)), axis=0, keepdims=True)
    p_grp = 1.0 / jnp.sum(jnp.exp(gl - gmax), axis=0, keepdims=True)
    sel = jnp.zeros((EXPERTS_PER_GROUP, tm), F32)
    for gi in range(N_GROUPS):
        lo = N_GROUPS + gi * EXPERTS_PER_GROUP
        sel = jnp.where(grp == float(gi), logits[lo:lo + EXPERTS_PER_GROUP, :], sel)
    eidx = lax.broadcasted_iota(jnp.int32, sel.shape, 0).astype(F32)
    top1 = jnp.max(sel, axis=0, keepdims=True)
    i1 = jnp.min(jnp.where(sel == top1, eidx, float(EXPERTS_PER_GROUP)), axis=0, keepdims=True)
    sel2 = jnp.where(eidx == i1, -jnp.inf, sel)
    top2 = jnp.max(sel2, axis=0, keepdims=True)
    i2 = jnp.min(jnp.where(sel2 == top2, eidx, float(EXPERTS_PER_GROUP)), axis=0, keepdims=True)
    e2 = jnp.exp(top2 - top1)
    inv = 1.0 / (1.0 + e2)
    zf = jnp.zeros((6, tm), F32)
    gate_ref[...] = jnp.concatenate([p_grp * inv, p_grp * (e2 * inv), zf], axis=0)
    base = grp * float(EXPERTS_PER_GROUP)
    eid_ref[...] = jnp.concatenate([base + i1, base + i2, zf], axis=0).astype(jnp.int32)


def _sc_gather_rows(table, idx):
    n_out = idx.shape[0]
    n_workers = V7X_SC_CORES * V7X_SC_SUBCORES
    per_worker = n_out // n_workers
    chunk_rows = SC_GATHER_BYTES // (4 * math.prod(table.shape[1:]))
    n_chunks = per_worker // chunk_rows
    assert per_worker * n_workers == n_out and n_chunks * chunk_rows == per_worker and n_chunks % 2 == 0
    mesh = plsc.VectorSubcoreMesh(core_axis_name="c", subcore_axis_name="s",
                                  num_cores=V7X_SC_CORES, num_subcores=V7X_SC_SUBCORES)

    def body(table_hbm, idx_hbm, out_hbm, idx_v, rows0, rows1, gsem0, gsem1, wsem0, wsem1):
        worker = lax.axis_index("s") * V7X_SC_CORES + lax.axis_index("c")
        base = worker * per_worker
        pltpu.sync_copy(idx_hbm.at[pl.ds(base, per_worker)], idx_v)

        def gather(j, buf, sem):
            off = pl.multiple_of(j * chunk_rows, chunk_rows)
            return pltpu.make_async_copy(table_hbm.at[idx_v.at[pl.ds(off, chunk_rows)]], buf, sem)

        def write(j, buf, sem):
            off = pl.multiple_of(j * chunk_rows, chunk_rows)
            return pltpu.make_async_copy(buf, out_hbm.at[pl.ds(base + off, chunk_rows)], sem)

        gather(0, rows0, gsem0).start()

        @pl.loop(0, n_chunks, step=2)
        def _(j):
            gather(j, rows0, gsem0).wait()

            @pl.when(j >= 2)
            def _():
                write(j - 1, rows1, wsem1).wait()

            gather(j + 1, rows1, gsem1).start()
            write(j, rows0, wsem0).start()
            gather(j + 1, rows1, gsem1).wait()
            write(j, rows0, wsem0).wait()

            @pl.when(j + 2 < n_chunks)
            def _():
                gather(j + 2, rows0, gsem0).start()

            write(j + 1, rows1, wsem1).start()

        write(n_chunks - 1, rows1, wsem1).wait()

    rows = pltpu.VMEM((chunk_rows,) + table.shape[1:], table.dtype)
    return pl.kernel(
        body,
        out_type=jax.ShapeDtypeStruct((n_out,) + table.shape[1:], table.dtype),
        mesh=mesh,
        scratch_types=[pltpu.VMEM((per_worker,), jnp.int32), rows, rows,
                       pltpu.SemaphoreType.DMA, pltpu.SemaphoreType.DMA,
                       pltpu.SemaphoreType.DMA, pltpu.SemaphoreType.DMA],
        name="sc_gather",
    )(table, idx)


def _experts_body(blk_e_ref, n_used_ref, first_ref, next_e_ref, wslot_ref, xs_hbm, wg_hbm, wu_hbm, wd_hbm, y_hbm,
                  wg_buf, wu_buf, wd_buf, x_buf, y_buf, wsem_ref, xsem_ref, ysem_ref, *, n_blocks):
    i = pl.program_id(0)
    used = i < n_used_ref[0]
    slot = wslot_ref[i]
    ahead = EXPERT_IN_BUFS - 1

    def rows_in(block):
        buf = block % EXPERT_IN_BUFS
        return pltpu.make_async_copy(xs_hbm.at[pl.ds(block * MOE_BM, MOE_BM)], x_buf.at[buf], xsem_ref.at[buf])

    def rows_out(block):
        buf = block % 2
        return pltpu.make_async_copy(y_buf.at[buf], y_hbm.at[pl.ds(block * MOE_BM, MOE_BM)], ysem_ref.at[buf])

    @pl.when(i == 0)
    def _():
        for b in range(min(ahead, n_blocks)):
            rows_in(b).start()

    @pl.when(i + ahead < n_blocks)
    def _():
        rows_in(i + ahead).start()

    rows_in(i).wait()

    @pl.when(i >= 2)
    def _():
        rows_out(i - 2).wait()

    def weight_copies(expert, to_slot):
        return [pltpu.make_async_copy(hbm.at[expert], buf.at[to_slot], wsem_ref.at[to_slot])
                for hbm, buf in ((wg_hbm, wg_buf), (wu_hbm, wu_buf), (wd_hbm, wd_buf))]

    @pl.when(jnp.logical_and(i == 0, used))
    def _():
        for cp in weight_copies(blk_e_ref[0], 0):
            cp.start()

    @pl.when(jnp.logical_and(used, first_ref[i] == 1))
    def _():
        for cp in weight_copies(blk_e_ref[i], slot):
            cp.wait()

        @pl.when(next_e_ref[i] != blk_e_ref[i])
        def _():
            for cp in weight_copies(next_e_ref[i], 1 - slot):
                cp.start()

    y_now = y_buf.at[i % 2]

    @pl.when(used)
    def _():
        xb = _unpack_bf16_halves(x_buf[i % EXPERT_IN_BUFS])
        gate = _dot(xb, wg_buf[slot].astype(BF16))
        upp = _dot(xb, wu_buf[slot].astype(BF16))
        hid = (gate * _sigmoid(gate)) * upp
        y_now[...] = _dot(hid.astype(BF16), wd_buf[slot].astype(BF16))

    @pl.when(jnp.logical_not(used))
    def _():
        y_now[...] = jnp.zeros_like(y_now)

    rows_out(i).start()

    @pl.when(i == n_blocks - 1)
    def _():
        if n_blocks >= 2:
            rows_out(i - 1).wait()
        rows_out(i).wait()


def _experts_call(blk_expert, n_used, xs, wg, wu, wd):
    nb = blk_expert.shape[0]
    d = wg.shape[1]
    first, next_e, wslot = _weight_schedule(blk_expert, n_used)
    any_space = pl.BlockSpec(memory_space=pl.ANY)
    grid_spec = pltpu.PrefetchScalarGridSpec(
        num_scalar_prefetch=5,
        grid=(nb,),
        in_specs=[any_space, any_space, any_space, any_space],
        out_specs=any_space,
        scratch_shapes=[pltpu.VMEM((2, d, D_EXPERT), F32), pltpu.VMEM((2, d, D_EXPERT), F32),
                        pltpu.VMEM((2, D_EXPERT, d), F32),
                        pltpu.VMEM((EXPERT_IN_BUFS, MOE_BM, d // 2), jnp.uint32), pltpu.VMEM((2, MOE_BM, d), F32),
                        pltpu.SemaphoreType.DMA((2,)), pltpu.SemaphoreType.DMA((EXPERT_IN_BUFS,)),
                        pltpu.SemaphoreType.DMA((2,))],
    )
    return pl.pallas_call(
        functools.partial(_experts_body, n_blocks=nb),
        grid_spec=grid_spec,
        out_shape=jax.ShapeDtypeStruct((nb * MOE_BM, d), F32),
        compiler_params=_cparams(),
        name="experts",
    )(blk_expert, n_used, first, next_e, wslot, xs, wg, wu, wd)


def _combine_body(x_ref, y0_ref, y1_ref, gate_ref, mod_ref, fg_ref, o_ref):
    tm, d = x_ref.shape
    y = gate_ref[:, 0:1] * y0_ref[...] + gate_ref[:, 1:2] * y1_ref[...]
    x = x_ref[...] + mod_ref[0:1, 5 * d:6 * d] * y
    ms = jnp.mean(x * x, axis=-1, keepdims=True)
    o_ref[...] = x * lax.rsqrt(ms + NORM_EPS) * fg_ref[...]


def _combine_call(x_mid, ypairs, gcol, mod, fg):
    s, d = x_mid.shape
    tm = TM_PROJ
    nt = s // tm
    const = lambda j: (0, 0)
    return pl.pallas_call(
        _combine_body,
        grid=(nt,),
        in_specs=[pl.BlockSpec((tm, d), lambda j: (j, 0)),
                  pl.BlockSpec((tm, d), lambda j: (j, 0)),
                  pl.BlockSpec((tm, d), lambda j: (j + nt, 0)),
                  pl.BlockSpec((tm, 2), lambda j: (j, 0)),
                  pl.BlockSpec(mod.shape, const),
                  pl.BlockSpec((1, d), const)],
        out_specs=pl.BlockSpec((tm, d), lambda j: (j, 0)),
        out_shape=jax.ShapeDtypeStruct((s, d), F32),
        compiler_params=_cparams(),
        name="combine",
    )(x_mid, ypairs, ypairs, gcol, mod, fg)


def _blockdiag2(wf, wb):
    z = jnp.zeros_like(wf)
    return jnp.concatenate([jnp.concatenate([wf, z], axis=1), jnp.concatenate([z, wb], axis=1)], axis=0)


def _routing_plan(eids, n_tok):
    m = 2 * n_tok
    experts = jnp.arange(N_EXPERTS, dtype=jnp.int32)
    flat_e = eids[:2].reshape(m)
    counts = jnp.sum((flat_e[:, None] == experts[None, :]).astype(jnp.int32), axis=0)
    pcounts = (counts + MOE_BM - 1) // MOE_BM * MOE_BM
    n_blocks = (m + N_EXPERTS * (MOE_BM - 1) + MOE_BM - 1) // MOE_BM
    n_slots = n_blocks * MOE_BM
    pad_ends = jnp.cumsum(pcounts - counts)
    pad = jnp.arange(n_slots - m, dtype=jnp.int32)
    pad_e = jnp.sum((pad[:, None] >= pad_ends[None, :]).astype(jnp.int32), axis=1)
    ids = jnp.arange(n_slots, dtype=jnp.int32)
    slot_e, slot_id = lax.sort((jnp.concatenate([flat_e, pad_e]), ids), num_keys=1, is_stable=True)
    src_tok = slot_id % n_tok
    _, slot_of_id = lax.sort((slot_id, ids), num_keys=1)
    dest = slot_of_id[:m]
    blk_expert = jnp.minimum(slot_e[::MOE_BM], N_EXPERTS - 1)
    n_used = (jnp.sum(pcounts)[None] // MOE_BM).astype(jnp.int32)
    return src_tok, dest, blk_expert, n_used


def _weight_schedule(blk_expert, n_used):
    used = jnp.arange(blk_expert.shape[0], dtype=jnp.int32) < n_used[0]
    first = jnp.logical_and(used, jnp.concatenate([jnp.ones((1,), bool), blk_expert[1:] != blk_expert[:-1]]))
    none = jnp.int32(N_EXPERTS)
    later_first = lax.cummin(jnp.where(first, blk_expert, none), axis=0, reverse=True)
    next_e = jnp.concatenate([later_first[1:], none[None]])
    next_e = jnp.where(next_e == none, blk_expert, next_e)
    wslot = jnp.maximum(jnp.cumsum(first.astype(jnp.int32)) - 1, 0) & 1
    return first.astype(jnp.int32), next_e, wslot


def kernel(x, c, ctx, c_ctx, norm1_g, w_mod, b_mod, w_in, mu_shift, w0_f, w2_f, w0_b, w2_b, a0_f, a2_f, a0_b, a2_b, g2, k_k, k_a, r_k_f, r_k_b, gn_w, gn_b, w_pool, pool_scale, w_out, norm2_g, w_router_grp, b_router_grp, w_router_exp, b_router_exp, w_gate, w_up, w_down, final_g):
    b, s, d = x.shape
    n_ctx = ctx.shape[1]
    assert b == 1 and c.shape[0] == 1 and w_mod.shape[0] == 1
    assert s % TM_PROJ == 0 and n_ctx % CHUNK == 0 and n_ctx == TM_FEAT and d == 2 * D_RWKV
    x2 = x[0]
    rows = s // GRID_W

    quarter = d // 4
    freq = POS_THETA ** (-jnp.arange(quarter, dtype=F32) / quarter)
    rarg = jnp.arange(rows, dtype=F32)[:, None] * freq
    carg = jnp.arange(GRID_W, dtype=F32)[:, None] * freq
    rowtab = jnp.concatenate([jnp.sin(rarg), jnp.cos(rarg)], axis=-1)
    coltab = jnp.tile(jnp.concatenate([jnp.sin(carg), jnp.cos(carg)], axis=-1), (TM_PROJ // GRID_W, 1))

    cs = jnp.concatenate([c, c_ctx[None, :], jnp.zeros((6, d), F32)], axis=0)
    mod = _mod_call(cs, w_mod[0], b_mod)

    ctx_pad = jnp.concatenate([ctx[0], jnp.zeros((TM_PROJ - n_ctx, d), F32)], axis=0)
    par = jnp.concatenate([w0_f, w0_b, a0_f, a0_b, k_k, k_a, r_k_f, r_k_b], axis=0)
    outs = _front_call(x2, ctx_pad, rowtab, coltab, norm1_g, mod, w_in[0].astype(BF16), mu_shift, par,
                       _blockdiag2(w2_f[0], w2_b[0]).astype(BF16), _blockdiag2(a2_f[0], a2_b[0]).astype(BF16),
                       g2[0].astype(BF16))
    p, feats, gamf, gamb, bonus, g = outs[0], outs[1:10], outs[10], outs[11], outs[12], outs[13]
    yf, yb = _scan_call(feats, gamf.reshape(-1, 1, D_RWKV), gamb.reshape(-1, 1, D_RWKV),
                        s // CHUNK, n_ctx // CHUNK)

    gn = jnp.concatenate([gn_w, gn_b], axis=0)
    wr = jnp.concatenate([w_router_grp[0].T, w_router_exp[0].T,
                          jnp.zeros((ROUTER_ROWS - N_GROUPS - N_EXPERTS, d), F32)], axis=0)
    br = jnp.concatenate([b_router_grp[0], b_router_exp[0],
                          jnp.zeros((ROUTER_ROWS - N_GROUPS - N_EXPERTS,), F32)])[:, None]
    x_mid, hp, eids, gates = _mix_call(x2, rowtab, coltab, yf, yb, bonus, g, p, gn, w_pool[0].astype(BF16),
                                       pool_scale, w_out[0].astype(BF16), mod, norm2_g, wr, br)

    src_tok, dest, blk_expert, n_used = _routing_plan(eids, s)
    xs = _sc_gather_rows(hp, src_tok)
    yexp = _experts_call(blk_expert, n_used, xs, w_gate[0], w_up[0], w_down[0])
    ypairs = _sc_gather_rows(yexp, dest)
    out = _combine_call(x_mid, ypairs, gates[:2].T, mod, final_g[None, :])
    return out[None]
```

```python
import functools
import math

import jax
import jax.numpy as jnp
from jax import lax
from jax.experimental import pallas as pl
from jax.experimental.pallas import tpu as pltpu
from jax.experimental.pallas import tpu_sc as plsc

F32 = jnp.float32
BF16 = jnp.bfloat16

GRID_W = 64
SUBLANES, LANES = 8, 128
HEAD = 64
N_HEADS = 8
D_RWKV = HEAD * N_HEADS
D_POOL = 512
POOL_WINDOWS = (2, 4, 8, 16)
POOL_GW = 128
POOL_HALO = 8
D_SHIFT = 1920
N_GROUPS = 4
EXPERTS_PER_GROUP = 8
N_EXPERTS = 32
D_EXPERT = 512
NORM_EPS = 1e-6
GN_EPS = 64e-5
POS_THETA = 10000.0
DECAY_SCALE = 0.6065306597

CHUNK = 64
QUAD = 4 * HEAD
SCAN_CPS = 4
TM_PROJ = 512
TM_FEAT = 256
FEAT_SHIFT_BLOCKS = 3
MOE_BM = 512
EXPERT_IN_BUFS = 3
ROUTER_ROWS = 48
V7X_VMEM_LIMIT = 56 * 1024 * 1024
V7X_SC_CORES, V7X_SC_SUBCORES = 2, 16
SC_GATHER_BYTES = 128 * 1024


def _cparams(n_axes=1, vmem=V7X_VMEM_LIMIT):
    return pltpu.CompilerParams(dimension_semantics=("arbitrary",) * n_axes,
                                vmem_limit_bytes=vmem)


def _dot(a, b):
    return jnp.dot(a, b, preferred_element_type=F32)


def _dot_nt(a, b):
    return lax.dot_general(a, b, (((1,), (1,)), ((), ())), preferred_element_type=F32)


def _dot_tn(a, b):
    return lax.dot_general(a, b, (((0,), (0,)), ((), ())), preferred_element_type=F32)


def _split2(x):
    hi = x.astype(BF16)
    lo = (x - hi.astype(F32)).astype(BF16)
    return hi, lo


def _dot3(a, b):
    ah, al = _split2(a)
    bh, bl = _split2(b)
    return _dot(ah, bh) + (_dot(ah, bl) + _dot(al, bh))


def _sigmoid(x):
    return 0.5 * jnp.tanh(0.5 * x) + 0.5


def _head_ones():
    r = lax.broadcasted_iota(jnp.int32, (D_RWKV, D_RWKV), 0) >> 6
    c = lax.broadcasted_iota(jnp.int32, (D_RWKV, D_RWKV), 1) >> 6
    return (r == c).astype(BF16)


def _headsum(x, ones):
    hi, lo = _split2(x)
    both = _dot(jnp.concatenate([hi, lo], axis=0), ones)
    return both[:x.shape[0]] + both[x.shape[0]:]


def _mod_body(cs_ref, w_ref, b_ref, o_ref):
    a = cs_ref[...]
    a = a * _sigmoid(a)
    o_ref[...] = _dot3(a, w_ref[...]) + b_ref[...]


def _mod_call(cs, w_mod, b_mod):
    d, n = w_mod.shape
    tn = 512
    return pl.pallas_call(
        _mod_body,
        grid=(n // tn,),
        in_specs=[pl.BlockSpec((8, d), lambda j: (0, 0)),
                  pl.BlockSpec((d, tn), lambda j: (0, j)),
                  pl.BlockSpec((1, tn), lambda j: (0, j))],
        out_specs=pl.BlockSpec((8, tn), lambda j: (0, j)),
        out_shape=jax.ShapeDtypeStruct((8, n), F32),
        compiler_params=_cparams(),
        name="mod",
    )(cs, w_mod, b_mod)


def _pos_tile(rowtab_ref, coltab_ref):
    rt = rowtab_ref[...]
    rowpart = jnp.concatenate(
        [jnp.broadcast_to(rt[r:r + 1, :], (GRID_W, rt.shape[1])) for r in range(TM_PROJ // GRID_W)], axis=0)
    return jnp.concatenate([rowpart, coltab_ref[...]], axis=1)


def _features(u, prow, nrow, mu, par_ref, w2_ref, a2_ref, g2_ref, rows, chunk0, outs, consts):
    (af_ref, rf_ref, btf_ref, ktf_ref, ab_ref, rb_ref, btb_ref, ktb_ref, v_ref,
     gamf_ref, gamb_ref, bonus_ref, g_ref) = outs
    tm = u.shape[0]
    rid = lax.broadcasted_iota(jnp.int32, (SUBLANES, 1), 0)

    def shifted(lo, hi):
        uc, m = u[:, lo:hi], mu[:, lo:hi]
        up = pltpu.roll(uc, 1, 0)
        up = jnp.concatenate([jnp.where(rid == 0, prow[:, lo:hi], up[:SUBLANES]), up[SUBLANES:]], axis=0)
        dn = pltpu.roll(uc, tm - 1, 0)
        dn = jnp.concatenate(
            [dn[:-SUBLANES], jnp.where(rid == SUBLANES - 1, nrow[:, lo:hi], dn[-SUBLANES:])], axis=0)
        return (1.0 - m) * uc + (0.5 * m) * (up + dn)

    w0f, w0b, a0f, a0b = par_ref[0:1, :], par_ref[1:2, :], par_ref[2:3, :], par_ref[3:4, :]
    k_k, k_a, rkf, rkb = par_ref[4:5, :], par_ref[5:6, :], par_ref[6:7, :], par_ref[7:8, :]

    r = shifted(0, 512)
    yield
    k = shifted(512, 1024)
    yield
    v = shifted(1024, 1536)
    v_ref[rows, :] = v.astype(BF16)
    yield
    lora = shifted(1536, D_SHIFT)
    zw = _dot(jnp.tanh(lora[:, 0:128]).astype(BF16), w2_ref[...])
    za = _dot(lora[:, 128:256].astype(BF16), a2_ref[...])
    g_ref[rows, :] = _dot(_sigmoid(lora[:, 256:384]).astype(BF16), g2_ref[...])
    yield
    lw_f = -DECAY_SCALE * _sigmoid(w0f + zw[:, :512])
    lw_b = -DECAY_SCALE * _sigmoid(w0b + zw[:, 512:])
    a_f = _sigmoid(a0f + za[:, :512])
    a_b = _sigmoid(a0b + za[:, 512:])
    yield

    ones, earlier, later = consts
    kkr = k * k_k
    kk = kkr * lax.rsqrt(jnp.maximum(_headsum(kkr * kkr, ones), 1e-24))
    yield
    k_f = k * (1.0 + (a_f - 1.0) * k_a)
    k_b = k * (1.0 + (a_b - 1.0) * k_a)
    bonus_ref[rows, :] = _headsum(r * (k_f * rkf + k_b * rkb), ones) * v
    yield

    n_ch = tm // CHUNK

    def visited_before(m, lwd):
        l1, l2 = _split2(lwd)
        return _dot(m, l1) + _dot(m, l2)

    def emit(excl, lwd, a_d, k_d, last_row, a_ref, r_ref, bt_ref, kt_ref, gam_ref):
        cum = excl + lwd
        gam_ref[0, chunk0:chunk0 + n_ch, :] = jnp.exp(jnp.concatenate(
            [cum[c * CHUNK + last_row:c * CHUNK + last_row + 1, :] for c in range(n_ch)], axis=0))
        a_ref[rows, :] = (-kk * jnp.exp(excl)).astype(BF16)
        r_ref[rows, :] = (r * jnp.exp(cum)).astype(BF16)
        yield
        e_neg = jnp.exp(-cum)
        bt_ref[rows, :] = (kk * a_d * e_neg).astype(BF16)
        kt_ref[rows, :] = (k_d * e_neg).astype(BF16)
        yield

    excl_f = visited_before(earlier, lw_f)
    yield
    yield from emit(excl_f, lw_f, a_f, k_f, CHUNK - 1, af_ref, rf_ref, btf_ref, ktf_ref, gamf_ref)
    excl_b = visited_before(later, lw_b)
    yield
    yield from emit(excl_b, lw_b, a_b, k_b, 0, ab_ref, rb_ref, btb_ref, ktb_ref, gamb_ref)


def _front_body(x_ref, ctx_ref, rowtab_ref, coltab_ref, g1_ref, mod_ref, w_ref, mu_ref, par_ref, w2_ref, a2_ref,
                g2_ref, p_ref, af_ref, rf_ref, btf_ref, ktf_ref, ab_ref, rb_ref, btb_ref, ktb_ref, v_ref,
                gamf_ref, gamb_ref, bonus_ref, g_ref, u_ref, unew_ref, carry_ref, *, n_x_tiles):
    i = pl.program_id(0)
    d = x_ref.shape[1]
    d_in = w_ref.shape[1]
    half = TM_FEAT
    last = carry_ref.shape[0] - 1

    @pl.when(i == 0)
    def _():
        u_ref[...] = jnp.zeros_like(u_ref)
        carry_ref[...] = jnp.zeros_like(carry_ref)

    is_ctx = i >= n_x_tiles
    xin = jnp.where(is_ctx, ctx_ref[...], x_ref[...] + _pos_tile(rowtab_ref, coltab_ref))
    ms = jnp.mean(xin * xin, axis=-1, keepdims=True)
    xn = xin * lax.rsqrt(ms + NORM_EPS) * g1_ref[...]
    sh = jnp.where(is_ctx, mod_ref[1:2, 0:d], mod_ref[0:1, 0:d])
    sc = jnp.where(is_ctx, mod_ref[1:2, d:2 * d], mod_ref[0:1, d:2 * d])
    hb = (xn * (1.0 + sc) + sh).astype(BF16)
    pending = [(slice(r0, r0 + half), lo, min(lo + QUAD, d_in))
               for lo in range(0, d_in, QUAD) for r0 in range(0, hb.shape[0], half)]

    def project_next():
        if not pending:
            return
        rs, lo, hi = pending.pop(0)
        uc = _dot(hb[rs, :], w_ref[:, lo:hi])
        if lo < D_SHIFT:
            unew_ref[rs, lo:min(hi, D_SHIFT)] = uc[:, :min(hi, D_SHIFT) - lo]
        if hi > D_SHIFT:
            p_ref[rs, max(lo, D_SHIFT) - D_SHIFT:hi - D_SHIFT] = uc[:, max(lo, D_SHIFT) - lo:]

    outs = (af_ref, rf_ref, btf_ref, ktf_ref, ab_ref, rb_ref, btb_ref, ktb_ref, v_ref,
            gamf_ref, gamb_ref, bonus_ref, g_ref)
    mu = mu_ref[...]
    ctx_block = 2 * n_x_tiles
    blocks = ((2 * i - 3, carry_ref[SUBLANES:, :], carry_ref[SUBLANES - 1:SUBLANES, :], u_ref[0:1, :]),
              (2 * i - 2, u_ref[0:half, :], carry_ref[last:last + 1, :], u_ref[half:half + 1, :]))
    rr = lax.broadcasted_iota(jnp.int32, (half, half), 0)
    cc = lax.broadcasted_iota(jnp.int32, (half, half), 1)
    same = (rr >> 6) == (cc >> 6)
    consts = (_head_ones(), jnp.logical_and(same, cc < rr).astype(BF16),
              jnp.logical_and(same, cc > rr).astype(BF16))
    project_next()
    for hi, (blk, ub, prow, nrow) in enumerate(blocks):
        prev_ok = jnp.logical_and(blk != 0, blk != ctx_block)
        next_ok = jnp.logical_and(blk != ctx_block - 1, blk != ctx_block)
        for _ in _features(ub, jnp.where(prev_ok, prow, 0.0), jnp.where(next_ok, nrow, 0.0), mu, par_ref, w2_ref,
                           a2_ref, g2_ref, slice(hi * half, (hi + 1) * half), hi * (half // CHUNK), outs, consts):
            project_next()
    while pending:
        project_next()
    carry_ref[...] = u_ref[half - SUBLANES:, :]
    u_ref[...] = unew_ref[...]


def _front_call(x2, ctx_pad, rowtab, coltab, g1, mod, w_in_bf, mu, par, w2, a2, g2):
    s, d = x2.shape
    tm = TM_PROJ
    nx = s // tm
    d_in = w_in_bf.shape[1]
    n_steps = nx + 2
    xmap = lambda i: (jnp.minimum(i, nx - 1), 0)
    const = lambda i: (0, 0)
    step = lambda i: (i, 0)
    seq_bf = jax.ShapeDtypeStruct((n_steps * tm, D_RWKV), BF16)
    seq_f32 = jax.ShapeDtypeStruct((n_steps * tm, D_RWKV), F32)
    gam = jax.ShapeDtypeStruct((n_steps, tm // CHUNK, D_RWKV), F32)
    seq_spec = pl.BlockSpec((tm, D_RWKV), step)
    gam_spec = pl.BlockSpec((1, tm // CHUNK, D_RWKV), lambda i: (i, 0, 0))
    return pl.pallas_call(
        functools.partial(_front_body, n_x_tiles=nx),
        grid=(n_steps,),
        in_specs=[pl.BlockSpec((tm, d), xmap),
                  pl.BlockSpec((tm, d), const),
                  pl.BlockSpec((tm // GRID_W, d // 2), xmap),
                  pl.BlockSpec((tm, d // 2), const),
                  pl.BlockSpec((1, d), const),
                  pl.BlockSpec(mod.shape, const),
                  pl.BlockSpec((d, d_in), const),
                  pl.BlockSpec((1, D_SHIFT), const),
                  pl.BlockSpec((8, D_RWKV), const),
                  pl.BlockSpec(w2.shape, const),
                  pl.BlockSpec(a2.shape, const),
                  pl.BlockSpec(g2.shape, const)],
        out_specs=([pl.BlockSpec((tm, D_POOL), lambda i: (jnp.minimum(i, nx), 0))] + [seq_spec] * 9
                   + [gam_spec, gam_spec, seq_spec, seq_spec]),
        out_shape=([jax.ShapeDtypeStruct(((nx + 1) * tm, D_POOL), F32)] + [seq_bf] * 9
                   + [gam, gam, seq_f32, seq_f32]),
        scratch_shapes=[pltpu.VMEM((tm, D_SHIFT), F32), pltpu.VMEM((tm, D_SHIFT), F32),
                        pltpu.VMEM((TM_FEAT + SUBLANES, D_SHIFT), F32)],
        compiler_params=_cparams(),
        name="front",
    )(x2, ctx_pad, rowtab, coltab, g1, mod, w_in_bf, mu, par, w2, a2, g2)


def _blockdiag(x, head_of_lane):
    xb = x.astype(BF16)
    zero = jnp.zeros_like(xb)
    return jnp.concatenate([jnp.where(head_of_lane == h, xb, zero) for h in range(QUAD // HEAD)], axis=0)


def _unblock(x, head_of_lane):
    out = jnp.where(head_of_lane == 0, x[0:HEAD, :], 0.0)
    for h in range(1, QUAD // HEAD):
        out = jnp.where(head_of_lane == h, x[h * HEAD:(h + 1) * HEAD, :], out)
    return out


def _scan_chunks(chains):
    lane = lax.broadcasted_iota(jnp.int32, (HEAD, QUAD), 1)
    head_of_lane = lane >> 6
    s_idx = lane & (HEAD - 1)
    t_idx = lax.broadcasted_iota(jnp.int32, (HEAD, QUAD), 0)
    eye = s_idx == t_idx
    lower = s_idx < t_idx
    upper = s_idx > t_idx
    bd = functools.partial(_blockdiag, head_of_lane=head_of_lane)

    def same_block(shift):
        return (s_idx >> shift) == (t_idx >> shift)

    def each(fn):
        return [fn(c) for c in chains]

    def stage_masks(c):
        ll = _dot_nt(jnp.concatenate([c["a"], c["r"]], axis=0),
                     jnp.concatenate([bd(c["bt"]), bd(c["kt"])], axis=0))
        strict = upper if c["reverse"] else lower
        incl = jnp.logical_or(strict, eye)
        c["lab"] = jnp.where(strict, ll[:HEAD, :QUAD], 0.0)
        c["lak"] = jnp.where(strict, ll[:HEAD, QUAD:], 0.0)
        c["mrb"] = jnp.where(incl, ll[HEAD:, :QUAD], 0.0)
        c["mrk"] = jnp.where(incl, ll[HEAD:, QUAD:], 0.0)

    each(stage_masks)

    def stage_square(c):
        ld = jnp.where(same_block(4), c["lab"], 0.0)
        c["tm"] = jnp.where(eye, 1.0, 0.0) + ld
        c["xk"] = _dot(ld.astype(BF16), bd(ld))

    def stage_double(c):
        y = _dot(jnp.concatenate([c["tm"], c["xk"]], axis=0).astype(BF16), bd(c["xk"]))
        c["tm"] = c["tm"] + y[:HEAD]
        c["xk"] = y[HEAD:]

    def stage_last_power(c):
        c["tm"] = c["tm"] + _dot(c["tm"].astype(BF16), bd(c["xk"]))

    each(stage_square)
    each(stage_double)
    each(stage_double)
    each(stage_last_power)
    each(lambda c: c.update(vv=_dot(jnp.concatenate([c["lak"], c["mrk"]], axis=0).astype(BF16), bd(c["v"]))))
    for shift in (4, 5):
        joined = jnp.logical_and(same_block(shift + 1), jnp.logical_not(same_block(shift)))
        each(lambda c: c.update(te=_dot(c["tm"].astype(BF16), bd(jnp.where(joined, c["lab"], 0.0)))))
        each(lambda c: c.update(tm=c["tm"] + _dot(c["te"].astype(BF16), bd(c["tm"]))))

    each(lambda c: c.update(x1=_dot(c["tm"].astype(BF16),
                                    jnp.concatenate([bd(c["a"]), bd(c["vv"][:HEAD])], axis=1))))
    each(lambda c: c.update(x2=_dot(c["mrb"].astype(BF16),
                                    jnp.concatenate([bd(c["x1"][:, :QUAD]), bd(c["x1"][:, QUAD:])], axis=1))))

    def stage_state_terms(c):
        rhs = jnp.concatenate(
            [c["x1"].astype(BF16), jnp.concatenate([jnp.zeros((HEAD, QUAD), BF16), c["v"]], axis=1)], axis=0)
        hat = (jnp.concatenate([c["bt"], c["kt"]], axis=0).astype(F32) * c["gam"]).astype(BF16)
        pz = _dot_tn(hat, rhs)
        c["pq"] = _unblock(pz[:, :QUAD], head_of_lane) + jnp.where(eye, c["gam"], 0.0)
        c["zq"] = _unblock(pz[:, QUAD:], head_of_lane)
        c["qeff"] = c["r"].astype(F32) + c["x2"][:, :QUAD]
        c["yin"] = c["x2"][:, QUAD:] + c["vv"][HEAD:]

    each(stage_state_terms)

    outs, state = [], {}
    for c in chains:
        hst = state.get(c["key"], c["hst"])
        qp = _dot(jnp.concatenate([c["qeff"], c["pq"]], axis=0).astype(BF16), bd(hst))
        state[c["key"]] = qp[HEAD:] + c["zq"]
        outs.append(qp[:HEAD] + c["yin"])
    return outs, state


def _scan_body(af, rf, btf, ktf, vf, gamf, ab, rb, btb, ktb, vb, gamb, yf_ref, yb_ref, hf_ref, hb_ref):
    @pl.when(pl.program_id(0) == 0)
    def _():
        hf_ref[...] = jnp.zeros_like(hf_ref)
        hb_ref[...] = jnp.zeros_like(hb_ref)

    names = ("a", "r", "bt", "kt", "v")
    chains, sinks, state_refs = [], [], {}
    for ci in range(SCAN_CPS):
        for refs, y_ref, h_ref, reverse in (((af, rf, btf, ktf, vf, gamf), yf_ref, hf_ref, False),
                                            ((ab, rb, btb, ktb, vb, gamb), yb_ref, hb_ref, True)):
            chunk = SCAN_CPS - 1 - ci if reverse else ci
            rows = slice(chunk * CHUNK, (chunk + 1) * CHUNK)
            for q in range(D_RWKV // QUAD):
                sl = slice(q * QUAD, (q + 1) * QUAD)
                chain = {n: ref[rows, sl] for n, ref in zip(names, refs[:5])}
                chain.update(gam=refs[5][chunk, :, sl], hst=h_ref[:, sl], reverse=reverse, key=(reverse, q))
                chains.append(chain)
                sinks.append((y_ref, rows, sl))
                state_refs[(reverse, q)] = (h_ref, sl)
    outs, state = _scan_chunks(chains)
    for y, (y_ref, rows, sl) in zip(outs, sinks):
        y_ref[rows, sl] = y
    for key, (h_ref, sl) in state_refs.items():
        h_ref[:, sl] = state[key]


def _scan_call(feats, gamf, gamb, n_x_chunks, n_c_chunks):
    af, rf, btf, ktf, ab, rb, btb, ktb, v = feats
    assert n_x_chunks % SCAN_CPS == 0 and n_c_chunks % SCAN_CPS == 0
    nxb, ncb = n_x_chunks // SCAN_CPS, n_c_chunks // SCAN_CPS
    n = nxb + ncb
    assert SCAN_CPS * CHUNK == TM_FEAT
    fwd_blk = lambda i: jnp.where(i < ncb, nxb + i, i - ncb)
    bwd_blk = lambda i: n - 1 - i
    fwd_y = lambda i: (fwd_blk(i), 0)
    bwd_y = lambda i: (bwd_blk(i), 0)
    fwd = lambda i: (fwd_blk(i) + FEAT_SHIFT_BLOCKS, 0)
    bwd = lambda i: (bwd_blk(i) + FEAT_SHIFT_BLOCKS, 0)
    fwd3 = lambda i: (fwd_blk(i) + FEAT_SHIFT_BLOCKS, 0, 0)
    bwd3 = lambda i: (bwd_blk(i) + FEAT_SHIFT_BLOCKS, 0, 0)
    blk = (SCAN_CPS * CHUNK, D_RWKV)
    in_specs = ([pl.BlockSpec(blk, fwd)] * 5 + [pl.BlockSpec((SCAN_CPS, 1, D_RWKV), fwd3)]
                + [pl.BlockSpec(blk, bwd)] * 5 + [pl.BlockSpec((SCAN_CPS, 1, D_RWKV), bwd3)])
    y = jax.ShapeDtypeStruct((n * SCAN_CPS * CHUNK, D_RWKV), F32)
    return pl.pallas_call(
        _scan_body,
        grid=(n,),
        in_specs=in_specs,
        out_specs=[pl.BlockSpec(blk, fwd_y), pl.BlockSpec(blk, bwd_y)],
        out_shape=[y, y],
        scratch_shapes=[pltpu.VMEM((HEAD, D_RWKV), F32), pltpu.VMEM((HEAD, D_RWKV), F32)],
        compiler_params=_cparams(),
        name="scan",
    )(af, rf, btf, ktf, v, gamf, ab, rb, btb, ktb, v, gamb)


def _mix_body(x_ref, rowtab_ref, coltab_ref, yf_ref, yb_ref, bonus0_ref, bonus1_ref, g0_ref, g1_ref,
              pm_ref, pp_ref, pn_ref, gn_ref, wpool_ref, pscale_ref, wout_ref, mod_ref, g2n_ref, wr_ref, br_ref,
              o_ref, hp_ref, eid_ref, gate_ref, slab_ref, *, n_x_tiles, seq_len):
    j = pl.program_id(0)
    tm, d = x_ref.shape
    hl = POOL_HALO
    slab_ref[0:hl, :] = jnp.where(j != 0, pp_ref[...], 0.0)
    slab_ref[hl:hl + tm, :] = pm_ref[...]
    slab_ref[hl + tm:2 * hl + tm, :] = jnp.where(j != n_x_tiles - 1, pn_ref[...], 0.0)
    tglob = j * tm + lax.broadcasted_iota(jnp.int32, (tm, 1), 0)
    pooled = []
    for gi, win in enumerate(POOL_WINDOWS):
        cols = slice(gi * POOL_GW, (gi + 1) * POOL_GW)
        acc = slab_ref[hl - win // 2:hl - win // 2 + tm, cols]
        for off in range(-win // 2 + 1, win // 2):
            acc = acc + slab_ref[hl + off:hl + off + tm, cols]
        lo = jnp.maximum(tglob - win // 2, 0)
        hi = jnp.minimum(tglob - win // 2 + win, seq_len)
        cnt = (hi - lo).astype(F32)
        pg = acc / cnt - slab_ref[hl:hl + tm, cols]
        pooled.append(_dot(pg.astype(BF16), wpool_ref[gi]))
    pool = jnp.concatenate(pooled, axis=1) * pscale_ref[...]

    ones = _head_ones()
    y = yf_ref[...] + yb_ref[...]
    yc = y - _headsum(y, ones) * (1.0 / HEAD)
    var = _headsum(yc * yc, ones) * (1.0 / HEAD)
    bonus = jnp.concatenate([bonus0_ref[...], bonus1_ref[...]], axis=0)
    gate = jnp.concatenate([g0_ref[...], g1_ref[...]], axis=0)
    rw = (yc * lax.rsqrt(var + GN_EPS) * gn_ref[0:1, :] + gn_ref[1:2, :] + bonus) * gate
    mix = _dot(jnp.concatenate([rw, pool], axis=1).astype(BF16), wout_ref[...])
    gt_a = mod_ref[0:1, 2 * d:3 * d]
    x_mid = x_ref[...] + _pos_tile(rowtab_ref, coltab_ref) + gt_a * mix
    o_ref[...] = x_mid
    _route(x_mid, g2n_ref, mod_ref, wr_ref, br_ref, hp_ref, eid_ref, gate_ref)


def _mix_call(x2, rowtab, coltab, yf, yb, bonus, g, p, gn, w_pool_bf, pscale, w_out_bf, mod, g2n, wr, br):
    s, d = x2.shape
    tm = TM_PROJ
    nx = s // tm
    hb = tm // POOL_HALO
    const = lambda j: (0, 0)
    tok = lambda j: (j, 0)
    seq_spec = pl.BlockSpec((tm, D_RWKV), tok)
    assert tm == 2 * TM_FEAT
    feat_lo = pl.BlockSpec((TM_FEAT, D_RWKV), lambda j: (2 * j + FEAT_SHIFT_BLOCKS, 0))
    feat_hi = pl.BlockSpec((TM_FEAT, D_RWKV), lambda j: (2 * j + FEAT_SHIFT_BLOCKS + 1, 0))
    return pl.pallas_call(
        functools.partial(_mix_body, n_x_tiles=nx, seq_len=s),
        grid=(nx,),
        in_specs=[pl.BlockSpec((tm, d), tok),
                  pl.BlockSpec((tm // GRID_W, d // 2), tok),
                  pl.BlockSpec((tm, d // 2), const),
                  seq_spec, seq_spec, feat_lo, feat_hi, feat_lo, feat_hi, seq_spec,
                  pl.BlockSpec((POOL_HALO, D_POOL), lambda j: (jnp.maximum(j * hb - 1, 0), 0)),
                  pl.BlockSpec((POOL_HALO, D_POOL), lambda j: ((j + 1) * hb, 0)),
                  pl.BlockSpec((2, D_RWKV), const),
                  pl.BlockSpec(w_pool_bf.shape, lambda j: (0, 0, 0)),
                  pl.BlockSpec((1, D_POOL), const),
                  pl.BlockSpec(w_out_bf.shape, const),
                  pl.BlockSpec(mod.shape, const),
                  pl.BlockSpec((1, d), const),
                  pl.BlockSpec(wr.shape, const),
                  pl.BlockSpec(br.shape, const)],
        out_specs=[pl.BlockSpec((tm, d), tok),
                   pl.BlockSpec((tm, d // 2), tok),
                   pl.BlockSpec((8, tm), lambda j: (0, j)),
                   pl.BlockSpec((8, tm), lambda j: (0, j))],
        out_shape=[jax.ShapeDtypeStruct((s, d), F32),
                   jax.ShapeDtypeStruct((s, d // 2), jnp.uint32),
                   jax.ShapeDtypeStruct((8, s), jnp.int32),
                   jax.ShapeDtypeStruct((8, s), F32)],
        scratch_shapes=[pltpu.VMEM((tm + 2 * POOL_HALO, D_POOL), F32)],
        compiler_params=_cparams(),
        name="mix",
    )(x2, rowtab, coltab, yf, yb, bonus, bonus, g, g, p, p, p, gn, w_pool_bf, pscale, w_out_bf, mod, g2n, wr, br)


def _pack_bf16_halves(x):
    half = x.shape[1] // 2
    bits = lax.bitcast_convert_type(x.astype(F32), jnp.uint32)
    return (bits[:, :half] >> 16) | bits[:, half:]


def _unpack_bf16_halves(words):
    lo = lax.bitcast_convert_type(words << 16, F32)
    hi = lax.bitcast_convert_type(words & jnp.uint32(0xFFFF0000), F32)
    return jnp.concatenate([lo, hi], axis=1).astype(BF16)


def _route(x, g2_ref, mod_ref, wr_ref, br_ref, hp_ref, eid_ref, gate_ref):
    tm, d = x.shape
    ms = jnp.mean(x * x, axis=-1, keepdims=True)
    xn = x * lax.rsqrt(ms + NORM_EPS) * g2_ref[...]
    h = xn * (1.0 + mod_ref[0:1, 4 * d:5 * d]) + mod_ref[0:1, 3 * d:4 * d]

    wh, wl = _split2(wr_ref[...])
    hh, hl = _split2(h)
    hp_ref[...] = _pack_bf16_halves(hh)
    logits = _dot_nt(wh, hh) + (_dot_nt(wh, hl) + _dot_nt(wl, hh)) + br_ref[...]
    gl = logits[0:N_GROUPS, :]
    gmax = jnp.max(gl, axis=0, keepdims=True)
    gidx = lax.broadcasted_iota(jnp.int32, gl.shape, 0).astype(F32)
    grp = jnp.min(jnp.where(gl == gmax, gidx, float(N_GROUPS)), axis=0, keepdims=True)
    p_grp = 1.0 / jnp.sum(jnp.exp(gl - gmax), axis=0, keepdims=True)
    sel = jnp.zeros((EXPERTS_PER_GROUP, tm), F32)
    for gi in range(N_GROUPS):
        lo = N_GROUPS + gi * EXPERTS_PER_GROUP
        sel = jnp.where(grp == float(gi), logits[lo:lo + EXPERTS_PER_GROUP, :], sel)
    eidx = lax.broadcasted_iota(jnp.int32, sel.shape, 0).astype(F32)
    top1 = jnp.max(sel, axis=0, keepdims=True)
    i1 = jnp.min(jnp.where(sel == top1, eidx, float(EXPERTS_PER_GROUP)), axis=0, keepdims=True)
    sel2 = jnp.where(eidx == i1, -jnp.inf, sel)
    top2 = jnp.max(sel2, axis=0, keepdims=True)
    i2 = jnp.min(jnp.where(sel2 == top2, eidx, float(EXPERTS_PER_GROUP)), axis=0, keepdims=True)
    e2 = jnp.exp(top2 - top1)
    inv = 1.0 / (1.0 + e2)
    zf = jnp.zeros((6, tm), F32)
    gate_ref[...] = jnp.concatenate([p_grp * inv, p_grp * (e2 * inv), zf], axis=0)
    base = grp * float(EXPERTS_PER_GROUP)
    eid_ref[...] = jnp.concatenate([base + i1, base + i2, zf], axis=0).astype(jnp.int32)


def _sc_gather_rows(table, idx):
    n_out = idx.shape[0]
    n_workers = V7X_SC_CORES * V7X_SC_SUBCORES
    per_worker = n_out // n_workers
    chunk_rows = SC_GATHER_BYTES // (4 * math.prod(table.shape[1:]))
    n_chunks = per_worker // chunk_rows
    assert per_worker * n_workers == n_out and n_chunks * chunk_rows == per_worker and n_chunks % 2 == 0
    mesh = plsc.VectorSubcoreMesh(core_axis_name="c", subcore_axis_name="s",
                                  num_cores=V7X_SC_CORES, num_subcores=V7X_SC_SUBCORES)

    def body(table_hbm, idx_hbm, out_hbm, idx_v, rows0, rows1, gsem0, gsem1, wsem0, wsem1):
        worker = lax.axis_index("s") * V7X_SC_CORES + lax.axis_index("c")
        base = worker * per_worker
        pltpu.sync_copy(idx_hbm.at[pl.ds(base, per_worker)], idx_v)

        def gather(j, buf, sem):
            off = pl.multiple_of(j * chunk_rows, chunk_rows)
            return pltpu.make_async_copy(table_hbm.at[idx_v.at[pl.ds(off, chunk_rows)]], buf, sem)

        def write(j, buf, sem):
            off = pl.multiple_of(j * chunk_rows, chunk_rows)
            return pltpu.make_async_copy(buf, out_hbm.at[pl.ds(base + off, chunk_rows)], sem)

        gather(0, rows0, gsem0).start()

        @pl.loop(0, n_chunks, step=2)
        def _(j):
            gather(j, rows0, gsem0).wait()

            @pl.when(j >= 2)
            def _():
                write(j - 1, rows1, wsem1).wait()

            gather(j + 1, rows1, gsem1).start()
            write(j, rows0, wsem0).start()
            gather(j + 1, rows1, gsem1).wait()
            write(j, rows0, wsem0).wait()

            @pl.when(j + 2 < n_chunks)
            def _():
                gather(j + 2, rows0, gsem0).start()

            write(j + 1, rows1, wsem1).start()

        write(n_chunks - 1, rows1, wsem1).wait()

    rows = pltpu.VMEM((chunk_rows,) + table.shape[1:], table.dtype)
    return pl.kernel(
        body,
        out_type=jax.ShapeDtypeStruct((n_out,) + table.shape[1:], table.dtype),
        mesh=mesh,
        scratch_types=[pltpu.VMEM((per_worker,), jnp.int32), rows, rows,
                       pltpu.SemaphoreType.DMA, pltpu.SemaphoreType.DMA,
                       pltpu.SemaphoreType.DMA, pltpu.SemaphoreType.DMA],
        name="sc_gather",
    )(table, idx)


def _experts_body(blk_e_ref, n_used_ref, first_ref, next_e_ref, wslot_ref, xs_hbm, wg_hbm, wu_hbm, wd_hbm, y_hbm,
                  wg_buf, wu_buf, wd_buf, x_buf, y_buf, wsem_ref, xsem_ref, ysem_ref, *, n_blocks):
    i = pl.program_id(0)
    used = i < n_used_ref[0]
    slot = wslot_ref[i]
    ahead = EXPERT_IN_BUFS - 1

    def rows_in(block):
        buf = block % EXPERT_IN_BUFS
        return pltpu.make_async_copy(xs_hbm.at[pl.ds(block * MOE_BM, MOE_BM)], x_buf.at[buf], xsem_ref.at[buf])

    def rows_out(block):
        buf = block % 2
        return pltpu.make_async_copy(y_buf.at[buf], y_hbm.at[pl.ds(block * MOE_BM, MOE_BM)], ysem_ref.at[buf])

    @pl.when(i == 0)
    def _():
        for b in range(min(ahead, n_blocks)):
            @pl.when(b < n_used_ref[0])
            def _():
                rows_in(b).start()

    @pl.when(i + ahead < n_used_ref[0])
    def _():
        rows_in(i + ahead).start()

    @pl.when(used)
    def _():
        rows_in(i).wait()

    @pl.when(i >= 2)
    def _():
        rows_out(i - 2).wait()

    def weight_copies(expert, to_slot):
        return [pltpu.make_async_copy(hbm.at[expert], buf.at[to_slot], wsem_ref.at[to_slot])
                for hbm, buf in ((wg_hbm, wg_buf), (wu_hbm, wu_buf), (wd_hbm, wd_buf))]

    @pl.when(jnp.logical_and(i == 0, used))
    def _():
        for cp in weight_copies(blk_e_ref[0], 0):
            cp.start()

    @pl.when(jnp.logical_and(used, first_ref[i] == 1))
    def _():
        for cp in weight_copies(blk_e_ref[i], slot):
            cp.wait()

        @pl.when(next_e_ref[i] != blk_e_ref[i])
        def _():
            for cp in weight_copies(next_e_ref[i], 1 - slot):
                cp.start()

    y_now = y_buf.at[i % 2]

    @pl.when(used)
    def _():
        xb = _unpack_bf16_halves(x_buf[i % EXPERT_IN_BUFS])
        gate = _dot(xb, wg_buf[slot].astype(BF16))
        upp = _dot(xb, wu_buf[slot].astype(BF16))
        hid = (gate * _sigmoid(gate)) * upp
        y_now[...] = _dot(hid.astype(BF16), wd_buf[slot].astype(BF16))

    @pl.when(jnp.logical_not(used))
    def _():
        y_now[...] = jnp.zeros_like(y_now)

    rows_out(i).start()

    @pl.when(i == n_blocks - 1)
    def _():
        if n_blocks >= 2:
            rows_out(i - 1).wait()
        rows_out(i).wait()


def _experts_call(blk_expert, n_used, xs, wg, wu, wd):
    nb = blk_expert.shape[0]
    d = wg.shape[1]
    first, next_e, wslot = _weight_schedule(blk_expert, n_used)
    any_space = pl.BlockSpec(memory_space=pl.ANY)
    grid_spec = pltpu.PrefetchScalarGridSpec(
        num_scalar_prefetch=5,
        grid=(nb,),
        in_specs=[any_space, any_space, any_space, any_space],
        out_specs=any_space,
        scratch_shapes=[pltpu.VMEM((2, d, D_EXPERT), F32), pltpu.VMEM((2, d, D_EXPERT), F32),
                        pltpu.VMEM((2, D_EXPERT, d), F32),
                        pltpu.VMEM((EXPERT_IN_BUFS, MOE_BM, d // 2), jnp.uint32), pltpu.VMEM((2, MOE_BM, d), F32),
                        pltpu.SemaphoreType.DMA((2,)), pltpu.SemaphoreType.DMA((EXPERT_IN_BUFS,)),
                        pltpu.SemaphoreType.DMA((2,))],
    )
    return pl.pallas_call(
        functools.partial(_experts_body, n_blocks=nb),
        grid_spec=grid_spec,
        out_shape=jax.ShapeDtypeStruct((nb * MOE_BM, d), F32),
        compiler_params=_cparams(),
        name="experts",
    )(blk_expert, n_used, first, next_e, wslot, xs, wg, wu, wd)


def _combine_body(x_ref, y0_ref, y1_ref, gate_ref, mod_ref, fg_ref, o_ref):
    tm, d = x_ref.shape
    y = gate_ref[:, 0:1] * y0_ref[...] + gate_ref[:, 1:2] * y1_ref[...]
    x = x_ref[...] + mod_ref[0:1, 5 * d:6 * d] * y
    ms = jnp.mean(x * x, axis=-1, keepdims=True)
    o_ref[...] = x * lax.rsqrt(ms + NORM_EPS) * fg_ref[...]


def _combine_call(x_mid, ypairs, gcol, mod, fg):
    s, d = x_mid.shape
    tm = TM_PROJ
    nt = s // tm
    const = lambda j: (0, 0)
    return pl.pallas_call(
        _combine_body,
        grid=(nt,),
        in_specs=[pl.BlockSpec((tm, d), lambda j: (j, 0)),
                  pl.BlockSpec((tm, d), lambda j: (j, 0)),
                  pl.BlockSpec((tm, d), lambda j: (j + nt, 0)),
                  pl.BlockSpec((tm, 2), lambda j: (j, 0)),
                  pl.BlockSpec(mod.shape, const),
                  pl.BlockSpec((1, d), const)],
        out_specs=pl.BlockSpec((tm, d), lambda j: (j, 0)),
        out_shape=jax.ShapeDtypeStruct((s, d), F32),
        compiler_params=_cparams(),
        name="combine",
    )(x_mid, ypairs, ypairs, gcol, mod, fg)


def _blockdiag2(wf, wb):
    z = jnp.zeros_like(wf)
    return jnp.concatenate([jnp.concatenate([wf, z], axis=1), jnp.concatenate([z, wb], axis=1)], axis=0)


def _routing_plan(eids, n_tok):
    m = 2 * n_tok
    experts = jnp.arange(N_EXPERTS, dtype=jnp.int32)
    flat_e = eids[:2].reshape(m)
    counts = jnp.sum((flat_e[:, None] == experts[None, :]).astype(jnp.int32), axis=0)
    pcounts = (counts + MOE_BM - 1) // MOE_BM * MOE_BM
    n_blocks = (m + N_EXPERTS * (MOE_BM - 1) + MOE_BM - 1) // MOE_BM
    n_slots = n_blocks * MOE_BM
    pad_ends = jnp.cumsum(pcounts - counts)
    pad = jnp.arange(n_slots - m, dtype=jnp.int32)
    pad_e = jnp.sum((pad[:, None] >= pad_ends[None, :]).astype(jnp.int32), axis=1)
    ids = jnp.arange(n_slots, dtype=jnp.int32)
    slot_e, slot_id = lax.sort((jnp.concatenate([flat_e, pad_e]), ids), num_keys=1, is_stable=True)
    src_tok = slot_id % n_tok
    _, slot_of_id = lax.sort((slot_id, ids), num_keys=1)
    dest = slot_of_id[:m]
    blk_expert = jnp.minimum(slot_e[::MOE_BM], N_EXPERTS - 1)
    n_used = (jnp.sum(pcounts)[None] // MOE_BM).astype(jnp.int32)
    return src_tok, dest, blk_expert, n_used


def _weight_schedule(blk_expert, n_used):
    used = jnp.arange(blk_expert.shape[0], dtype=jnp.int32) < n_used[0]
    first = jnp.logical_and(used, jnp.concatenate([jnp.ones((1,), bool), blk_expert[1:] != blk_expert[:-1]]))
    none = jnp.int32(N_EXPERTS)
    later_first = lax.cummin(jnp.where(first, blk_expert, none), axis=0, reverse=True)
    next_e = jnp.concatenate([later_first[1:], none[None]])
    next_e = jnp.where(next_e == none, blk_expert, next_e)
    wslot = jnp.maximum(jnp.cumsum(first.astype(jnp.int32)) - 1, 0) & 1
    return first.astype(jnp.int32), next_e, wslot


def kernel(x, c, ctx, c_ctx, norm1_g, w_mod, b_mod, w_in, mu_shift, w0_f, w2_f, w0_b, w2_b, a0_f, a2_f, a0_b, a2_b, g2, k_k, k_a, r_k_f, r_k_b, gn_w, gn_b, w_pool, pool_scale, w_out, norm2_g, w_router_grp, b_router_grp, w_router_exp, b_router_exp, w_gate, w_up, w_down, final_g):
    b, s, d = x.shape
    n_ctx = ctx.shape[1]
    assert b == 1 and c.shape[0] == 1 and w_mod.shape[0] == 1
    assert s % TM_PROJ == 0 and n_ctx % CHUNK == 0 and n_ctx == TM_FEAT and d == 2 * D_RWKV
    x2 = x[0]
    rows = s // GRID_W

    quarter = d // 4
    freq = POS_THETA ** (-jnp.arange(quarter, dtype=F32) / quarter)
    rarg = jnp.arange(rows, dtype=F32)[:, None] * freq
    carg = jnp.arange(GRID_W, dtype=F32)[:, None] * freq
    rowtab = jnp.concatenate([jnp.sin(rarg), jnp.cos(rarg)], axis=-1)
    coltab = jnp.tile(jnp.concatenate([jnp.sin(carg), jnp.cos(carg)], axis=-1), (TM_PROJ // GRID_W, 1))

    cs = jnp.concatenate([c, c_ctx[None, :], jnp.zeros((6, d), F32)], axis=0)
    mod = _mod_call(cs, w_mod[0], b_mod)

    ctx_pad = jnp.concatenate([ctx[0], jnp.zeros((TM_PROJ - n_ctx, d), F32)], axis=0)
    par = jnp.concatenate([w0_f, w0_b, a0_f, a0_b, k_k, k_a, r_k_f, r_k_b], axis=0)
    outs = _front_call(x2, ctx_pad, rowtab, coltab, norm1_g, mod, w_in[0].astype(BF16), mu_shift, par,
                       _blockdiag2(w2_f[0], w2_b[0]).astype(BF16), _blockdiag2(a2_f[0], a2_b[0]).astype(BF16),
                       g2[0].astype(BF16))
    p, feats, gamf, gamb, bonus, g = outs[0], outs[1:10], outs[10], outs[11], outs[12], outs[13]
    yf, yb = _scan_call(feats, gamf.reshape(-1, 1, D_RWKV), gamb.reshape(-1, 1, D_RWKV),
                        s // CHUNK, n_ctx // CHUNK)

    gn = jnp.concatenate([gn_w, gn_b], axis=0)
    wr = jnp.concatenate([w_router_grp[0].T, w_router_exp[0].T,
                          jnp.zeros((ROUTER_ROWS - N_GROUPS - N_EXPERTS, d), F32)], axis=0)
    br = jnp.concatenate([b_router_grp[0], b_router_exp[0],
                          jnp.zeros((ROUTER_ROWS - N_GROUPS - N_EXPERTS,), F32)])[:, None]
    x_mid, hp, eids, gates = _mix_call(x2, rowtab, coltab, yf, yb, bonus, g, p, gn, w_pool[0].astype(BF16),
                                       pool_scale, w_out[0].astype(BF16), mod, norm2_g, wr, br)

    src_tok, dest, blk_expert, n_used = _routing_plan(eids, s)
    xs = _sc_gather_rows(hp, src_tok)
    yexp = _experts_call(blk_expert, n_used, xs, w_gate[0], w_up[0], w_down[0])
    ypairs = _sc_gather_rows(yexp, dest)
    out = _combine_call(x_mid, ypairs, gates[:2].T, mod, final_g[None, :])
    return out[None]
```
